```python
import jax, jax.numpy as jnp
from jax import lax
import numpy as np

D_MODEL = 1024
BATCH = 4
SEQ = 4096
DEPTH = 2

GRID_W = 64
CTX_LEN = 256
EPS = 1e-6
N_BRANCH = 4
BRANCH_W = D_MODEL // 4
D_FF = 4 * D_MODEL

POOL_GROUPS = 4
POOL_GC = BRANCH_W // POOL_GROUPS
POOL_WINDOWS = (2, 4, 8, 16)
POOL_W = POOL_GROUPS * POOL_GC

MLA_HEADS = 4
MLA_Q_RANK = D_MODEL // 4
MLA_KV_RANK = D_MODEL // 8
MLA_NOPE = 64
MLA_ROPE = 32
MLA_V = 64
MLA_QK = MLA_NOPE + MLA_ROPE
ROPE_BASE = 10000.0
Q_BLOCK = 128

NA_HEADS = 4
NA_HD = 64
NA_WIN_R = 8
NA_WIN_C = 16

HG_HEADS = 4
HG_DK = 64
HG_DV = 64
HG_CHUNK = 64
LB_EPS = 1e-6

KV_SIZES = (MLA_KV_RANK + MLA_ROPE, NA_HEADS * NA_HD, NA_HEADS * NA_HD,
            HG_HEADS * HG_DK, HG_HEADS * HG_DK, HG_HEADS * HG_DV)
Q_SIZES = (POOL_W, MLA_Q_RANK, NA_HEADS * NA_HD, HG_HEADS * HG_DK, HG_HEADS * HG_DV,
           N_BRANCH * D_MODEL)
KV_COLS = sum(KV_SIZES)
D_IN = KV_COLS + sum(Q_SIZES)

kernel_name = 'hybrid_dit_pool_mla_natten_hgrn2'


def rmsnorm(x, g):
    xf = x.astype(jnp.float32)
    y = xf * lax.rsqrt(jnp.mean(xf * xf, axis=-1, keepdims=True) + EPS)
    return (y * g.astype(jnp.float32)).astype(x.dtype)


def modulate(x, g, shift, scale):
    return rmsnorm(x, g) * (1 + scale) + shift


def split_cols(u, sizes):
    return jnp.split(u, [int(s) for s in np.cumsum(sizes)[:-1]], axis=-1)


def split_heads(u, n_heads, g=None):
    B, L, _ = u.shape
    h = u.reshape(B, L, n_heads, -1)
    if g is not None:
        h = rmsnorm(h, g)
    return h.transpose(0, 2, 1, 3)


def merge_heads(o):
    B, H, L, d = o.shape
    return o.transpose(0, 2, 1, 3).reshape(B, L, H * d)


def axial_rope_tables(n_tok):
    t = jnp.arange(n_tok)
    pos = jnp.stack([t // GRID_W, t % GRID_W], axis=-1).astype(jnp.float32)
    nf = MLA_ROPE // 4
    inv = ROPE_BASE ** (-jnp.arange(nf, dtype=jnp.float32) / nf)
    ang = pos[:, :, None] * inv
    return jnp.cos(ang), jnp.sin(ang)


def apply_axial_rope(x, cos, sin):
    sh = x.shape
    xs = x.reshape(sh[:-1] + (2, 2, MLA_ROPE // 4)).astype(jnp.float32)
    x1, x2 = xs[..., 0, :], xs[..., 1, :]
    out = jnp.stack([x1 * cos - x2 * sin, x2 * cos + x1 * sin], axis=-2)
    return out.reshape(sh).astype(x.dtype)


def rotate_tail(x, rope):
    return jnp.concatenate([x[..., :MLA_NOPE], apply_axial_rope(x[..., MLA_NOPE:], rope[0], rope[1])], axis=-1)


def mla_queries(qa, q_norm, w_qb, gq, rope):
    B, L, _ = qa.shape
    q = (rmsnorm(qa, q_norm) @ w_qb).reshape(B, L, MLA_HEADS, MLA_QK)
    q = rmsnorm(q, gq).transpose(0, 2, 1, 3)
    if rope is not None:
        q = rotate_tail(q, rope)
    return q


def mla_keys_values(kva, kv_norm, w_kvb, gk, rope):
    B, L, _ = kva.shape
    ckv, k_rope = kva[..., :MLA_KV_RANK], kva[..., MLA_KV_RANK:]
    kv = (rmsnorm(ckv, kv_norm) @ w_kvb).reshape(B, L, MLA_HEADS, MLA_NOPE + MLA_V)
    k = jnp.concatenate([kv[..., :MLA_NOPE],
                         jnp.broadcast_to(k_rope[:, :, None, :], (B, L, MLA_HEADS, MLA_ROPE))], axis=-1)
    k = rmsnorm(k, gk).transpose(0, 2, 1, 3)
    v = kv[..., MLA_NOPE:].transpose(0, 2, 1, 3)
    if rope is not None:
        k = rotate_tail(k, rope)
    return k, v


def dense_attention(q, k, v, scale):
    B, H, L, dq = q.shape
    nb = L // Q_BLOCK
    qb = q.reshape(B, H, nb, Q_BLOCK, dq).transpose(2, 0, 1, 3, 4)

    def one_block(qi):
        s = jnp.einsum('bhqd,bhkd->bhqk', qi, k).astype(jnp.float32) * scale
        pr = jax.nn.softmax(s, axis=-1).astype(v.dtype)
        return jnp.einsum('bhqk,bhkd->bhqd', pr, v)

    o = lax.map(one_block, qb)
    return o.transpose(1, 2, 0, 3, 4).reshape(B, H, L, v.shape[-1])


def na_tables(rows, rpb):
    wr = min(NA_WIN_R, rows)
    r = jnp.arange(rows)
    col = jnp.arange(GRID_W)
    rs = jnp.clip(r - NA_WIN_R // 2, 0, rows - wr)
    cs = jnp.clip(col - NA_WIN_C // 2, 0, GRID_W - NA_WIN_C)
    key_r = rs[:, None] + jnp.arange(wr)
    key_c = cs[:, None] + jnp.arange(NA_WIN_C)
    idx = (key_r[:, None, :, None] * GRID_W + key_c[None, :, None, :]).reshape(rows, GRID_W, wr * NA_WIN_C)
    dr = key_r - r[:, None] + NA_WIN_R - 1
    dc = key_c - col[:, None] + NA_WIN_C - 1
    bias = rpb[:, dr[:, None, :, None], dc[None, :, None, :]].reshape(NA_HEADS, rows, GRID_W, wr * NA_WIN_C)
    return idx, bias.astype(jnp.float32)


def na_attention(q, k, v, k_ctx, v_ctx, rpb):
    B, H, L, d = q.shape
    rows = L // GRID_W
    idx, bias = na_tables(rows, rpb)
    n_loc = idx.shape[-1]
    scale = NA_HD ** -0.5
    qr = q.reshape(B, H, rows, GRID_W, d).transpose(2, 0, 1, 3, 4)

    def one_row(args):
        qi, ii, bi = args
        k_loc = k[:, :, ii]
        v_loc = v[:, :, ii]
        s_loc = jnp.einsum('bhqd,bhqnd->bhqn', qi, k_loc).astype(jnp.float32) * scale + bi
        s_ctx = jnp.einsum('bhqd,bhkd->bhqk', qi, k_ctx).astype(jnp.float32) * scale
        pr = jax.nn.softmax(jnp.concatenate([s_loc, s_ctx], axis=-1), axis=-1).astype(v.dtype)
        return (jnp.einsum('bhqn,bhqnd->bhqd', pr[..., :n_loc], v_loc)
                + jnp.einsum('bhqk,bhkd->bhqd', pr[..., n_loc:], v_ctx))

    o = lax.map(one_row, (qr, idx, bias.transpose(1, 0, 2, 3)))
    return o.transpose(1, 2, 0, 3, 4).reshape(B, H, L, d)


def pool_mix(u, w_grp, scale):
    B, L, _ = u.shape
    uf = u.astype(jnp.float32)
    csum = jnp.concatenate([jnp.zeros((B, 1, POOL_W), jnp.float32), jnp.cumsum(uf, axis=1)], axis=1)
    t = jnp.arange(L)
    groups = []
    for gi, w in enumerate(POOL_WINDOWS):
        lo = jnp.clip(t - w // 2, 0, L)
        hi = jnp.clip(t - w // 2 + w, 0, L)
        cg = csum[..., gi * POOL_GC:(gi + 1) * POOL_GC]
        cnt = (hi - lo).astype(jnp.float32)[None, :, None]
        groups.append((cg[:, hi] - cg[:, lo]) / cnt - uf[..., gi * POOL_GC:(gi + 1) * POOL_GC])
    p = jnp.stack(groups, axis=2)
    y = jnp.einsum('blgc,gcd->blgd', p, w_grp.astype(jnp.float32)).reshape(B, L, POOL_W)
    return (y * scale.astype(jnp.float32)).astype(u.dtype)


def hgrn_gates(z, lb):
    logf = jnp.logaddexp(jax.nn.log_sigmoid(z), jnp.log(lb) + jax.nn.log_sigmoid(-z))
    k = (1 - lb) * jax.nn.sigmoid(-z)
    return logf, k


def hgrn_scan(q, k, logf, i, s0):
    B, H, L, _ = k.shape
    nc = L // HG_CHUNK

    def chunks(a):
        return a.reshape(B, H, nc, HG_CHUNK, a.shape[-1]).transpose(2, 0, 1, 3, 4)

    lower_tri = jnp.tril(jnp.ones((HG_CHUNK, HG_CHUNK), dtype=bool))[:, :, None]

    def advance(S, b, kc, ic):
        bl = b[:, :, -1]
        return jnp.exp(bl)[..., None] * S + jnp.einsum('bhsd,bhsv->bhdv', kc * jnp.exp(bl[:, :, None] - b), ic)

    if q is None:
        def state_step(S, xs):
            kc, fc, ic = xs
            return advance(S, jnp.cumsum(fc, axis=2), kc, ic), None
        S_fin, _ = lax.scan(state_step, s0, (chunks(k), chunks(logf), chunks(i)))
        return None, S_fin

    def step(S, xs):
        qc, kc, fc, ic = xs
        b = jnp.cumsum(fc, axis=2)
        diff = b[:, :, :, None, :] - b[:, :, None, :, :]
        dec = jnp.where(lower_tri, jnp.exp(jnp.minimum(diff, 0.0)), 0.0)
        a = jnp.einsum('bhtd,bhsd,bhtsd->bhts', qc, kc, dec)
        o = jnp.einsum('bhts,bhsv->bhtv', a, ic) + jnp.einsum('bhtd,bhdv->bhtv', qc * jnp.exp(b), S)
        return advance(S, b, kc, ic), o

    S_fin, o = lax.scan(step, s0, (chunks(q), chunks(k), chunks(logf), chunks(i)))
    return o.transpose(1, 2, 0, 3, 4).reshape(B, H, L, -1), S_fin


def hgrn_inputs(kv, lbF, lbB):
    zF = split_heads(kv[3], HG_HEADS).astype(jnp.float32)
    zB = split_heads(kv[4], HG_HEADS).astype(jnp.float32)
    fF, kF = hgrn_gates(zF, lbF)
    fB, kB = hgrn_gates(zB, lbB)
    i = split_heads(kv[5], HG_HEADS).astype(jnp.float32)
    return fF, kF, fB, kB, i


def hgrn_readout(o, g_cols, g_norm):
    y = merge_heads(rmsnorm(o, g_norm))
    return (y * jax.nn.sigmoid(g_cols.astype(jnp.float32))).astype(g_cols.dtype)


def flip(a):
    return jnp.flip(a, axis=2)


def merge_branches(branches, gate_cols, w_br, w_o):
    B, L, _ = gate_cols.shape
    gates = jax.nn.sigmoid(gate_cols.reshape(B, L, N_BRANCH, D_MODEL))
    y = jnp.stack(branches, axis=2)
    proj = jnp.einsum('blnw,nwd->blnd', y, w_br)
    return jnp.einsum('bld,de->ble', jnp.sum(gates * proj, axis=2), w_o)


def sqrelu_mlp(h, w1, w2):
    return jnp.square(jax.nn.relu(h @ w1)) @ w2


def setup_inputs(seed: int = 0) -> dict:
    key = jax.random.key(seed)
    ks = jax.random.split(key, 32)
    D = D_MODEL

    def nrm(k, shape, s):
        return jax.random.normal(k, shape, jnp.float32) * s

    def gain(k, shape):
        return 1.0 + 0.05 * jax.random.normal(k, shape, jnp.float32)

    return {
        'x': nrm(ks[0], (BATCH, SEQ, D), 1.0),
        'c': nrm(ks[1], (BATCH, D), 1.0),
        'ctx': nrm(ks[2], (BATCH, CTX_LEN, D), 1.0),
        'c_ctx': nrm(ks[3], (D,), 1.0),
        'ada_w': nrm(ks[4], (DEPTH, D, 6 * D), D ** -0.5),
        'ada_b': nrm(ks[5], (DEPTH, 6 * D), 0.02),
        'norm1': gain(ks[6], (DEPTH, D)),
        'norm2': gain(ks[7], (DEPTH, D)),
        'w_in': nrm(ks[8], (DEPTH, D, D_IN), D ** -0.5),
        'pool_w': nrm(ks[9], (DEPTH, POOL_GROUPS, POOL_GC, POOL_GC), POOL_GC ** -0.5),
        'pool_scale': gain(ks[10], (DEPTH, POOL_W)),
        'mla_q_norm': gain(ks[11], (DEPTH, MLA_Q_RANK)),
        'mla_wq_b': nrm(ks[12], (DEPTH, MLA_Q_RANK, MLA_HEADS * MLA_QK), MLA_Q_RANK ** -0.5),
        'mla_kv_norm': gain(ks[13], (DEPTH, MLA_KV_RANK)),
        'mla_wkv_b': nrm(ks[14], (DEPTH, MLA_KV_RANK, MLA_HEADS * (MLA_NOPE + MLA_V)), MLA_KV_RANK ** -0.5),
        'mla_gq': gain(ks[15], (DEPTH, MLA_QK)),
        'mla_gk': gain(ks[16], (DEPTH, MLA_QK)),
        'na_gq': gain(ks[17], (DEPTH, NA_HD)),
        'na_gk': gain(ks[18], (DEPTH, NA_HD)),
        'na_rpb': nrm(ks[19], (DEPTH, NA_HEADS, 2 * NA_WIN_R - 1, 2 * NA_WIN_C - 1), 0.1),
        'hg_lb': nrm(ks[20], (DEPTH + 1, 2, HG_HEADS * HG_DK), 0.5),
        'hg_norm': gain(ks[21], (DEPTH, HG_DV)),
        'w_branch': nrm(ks[22], (DEPTH, N_BRANCH, BRANCH_W, D), BRANCH_W ** -0.5),
        'w_out': nrm(ks[23], (DEPTH, D, D), D ** -0.5),
        'w_ff1': nrm(ks[24], (DEPTH, D, D_FF), D ** -0.5),
        'w_ff2': nrm(ks[25], (DEPTH, D_FF, D), D_FF ** -0.5),
    }


def reference(x, c, ctx, c_ctx, ada_w, ada_b, norm1, norm2, w_in, pool_w, pool_scale,
              mla_q_norm, mla_wq_b, mla_kv_norm, mla_wkv_b, mla_gq, mla_gk,
              na_gq, na_gk, na_rpb, hg_lb, hg_norm, w_branch, w_out, w_ff1, w_ff2):
    B, L, D = x.shape
    rope = axial_rope_tables(L)
    lb_p = jax.nn.softmax(hg_lb.astype(jnp.float32), axis=0)
    lb_all = jnp.clip(jnp.cumsum(lb_p, axis=0)[:DEPTH], LB_EPS, 1.0 - LB_EPS)
    s_lat = jax.nn.silu(c)
    s_ctx = jax.nn.silu(c_ctx)
    mla_scale = MLA_QK ** -0.5
    hg_qscale = HG_DK ** -0.5
    xl, xc = x, ctx
    for l in range(DEPTH):
        ctx_out = l < DEPTH - 1
        mod_l = (s_lat @ ada_w[l] + ada_b[l])[:, None, :]
        sh1, sc1, g1, sh2, sc2, g2 = jnp.split(mod_l, 6, axis=-1)
        n_mod = 6 if ctx_out else 2
        mods_c = jnp.split(s_ctx @ ada_w[l][:, :n_mod * D] + ada_b[l][:n_mod * D], n_mod)
        lbF = lb_all[l, 0].reshape(HG_HEADS, 1, HG_DK)
        lbB = lb_all[l, 1].reshape(HG_HEADS, 1, HG_DK)

        hl = modulate(xl, norm1[l], sh1, sc1)
        hc = modulate(xc, norm1[l], mods_c[0], mods_c[1])
        ul = hl @ w_in[l]
        kv_l = split_cols(ul[..., :KV_COLS], KV_SIZES)
        q_l = split_cols(ul[..., KV_COLS:], Q_SIZES)
        uc = hc @ (w_in[l] if ctx_out else w_in[l][:, :KV_COLS])
        kv_c = split_cols(uc[..., :KV_COLS], KV_SIZES)

        k_mla_c, v_mla_c = mla_keys_values(kv_c[0], mla_kv_norm[l], mla_wkv_b[l], mla_gk[l], None)
        k_na_c = split_heads(kv_c[1], NA_HEADS, na_gk[l])
        v_na_c = split_heads(kv_c[2], NA_HEADS)
        fFc, kFc, fBc, kBc, ic = hgrn_inputs(kv_c, lbF, lbB)
        zeros = jnp.zeros((B, HG_HEADS, HG_DK, HG_DV), jnp.float32)
        if ctx_out:
            q_c = split_cols(uc[..., KV_COLS:], Q_SIZES)
            qc_hg = split_heads(q_c[3], HG_HEADS).astype(jnp.float32) * hg_qscale
            oFc, SFc = hgrn_scan(qc_hg, kFc, fFc, ic, zeros)
            oBc, SBc = hgrn_scan(flip(qc_hg), flip(kBc), flip(fBc), flip(ic), zeros)
        else:
            _, SFc = hgrn_scan(None, kFc, fFc, ic, zeros)
            _, SBc = hgrn_scan(None, flip(kBc), flip(fBc), flip(ic), zeros)

        a_l = pool_mix(q_l[0], pool_w[l], pool_scale[l])
        q_mla = mla_queries(q_l[1], mla_q_norm[l], mla_wq_b[l], mla_gq[l], rope)
        k_mla, v_mla = mla_keys_values(kv_l[0], mla_kv_norm[l], mla_wkv_b[l], mla_gk[l], rope)
        b_l = merge_heads(dense_attention(q_mla, jnp.concatenate([k_mla, k_mla_c], axis=2),
                                          jnp.concatenate([v_mla, v_mla_c], axis=2), mla_scale))
        q_na = split_heads(q_l[2], NA_HEADS, na_gq[l])
        k_na = split_heads(kv_l[1], NA_HEADS, na_gk[l])
        v_na = split_heads(kv_l[2], NA_HEADS)
        c_l = merge_heads(na_attention(q_na, k_na, v_na, k_na_c, v_na_c, na_rpb[l]))
        fFl, kFl, fBl, kBl, il = hgrn_inputs(kv_l, lbF, lbB)
        ql_hg = split_heads(q_l[3], HG_HEADS).astype(jnp.float32) * hg_qscale
        oFl, _ = hgrn_scan(ql_hg, kFl, fFl, il, SFc)
        oBl, _ = hgrn_scan(flip(ql_hg), flip(kBl), flip(fBl), flip(il), SBc)
        d_l = hgrn_readout(oFl + flip(oBl), q_l[4], hg_norm[l])
        y_l = merge_branches([a_l, b_l, c_l, d_l], q_l[5], w_branch[l], w_out[l])

        if ctx_out:
            a_c = pool_mix(q_c[0], pool_w[l], pool_scale[l])
            q_mla_c = mla_queries(q_c[1], mla_q_norm[l], mla_wq_b[l], mla_gq[l], None)
            b_c = merge_heads(dense_attention(q_mla_c, k_mla_c, v_mla_c, mla_scale))
            q_na_c = split_heads(q_c[2], NA_HEADS, na_gq[l])
            c_c = merge_heads(dense_attention(q_na_c, k_na_c, v_na_c, NA_HD ** -0.5))
            d_c = hgrn_readout(oFc + flip(oBc), q_c[4], hg_norm[l])
            y_c = merge_branches([a_c, b_c, c_c, d_c], q_c[5], w_branch[l], w_out[l])
            xc = xc + mods_c[2] * y_c
            xc = xc + mods_c[5] * sqrelu_mlp(modulate(xc, norm2[l], mods_c[3], mods_c[4]), w_ff1[l], w_ff2[l])

        xl = xl + g1 * y_l
        xl = xl + g2 * sqrelu_mlp(modulate(xl, norm2[l], sh2, sc2), w_ff1[l], w_ff2[l])
    return xl
```

```python
import functools

import jax
import jax.numpy as jnp
import numpy as np
from jax import lax
from jax.experimental import pallas as pl
from jax.experimental.pallas import tpu as pltpu

D_MODEL = 1024
GRID_W = 64
EPS = 1e-6
N_BRANCH = 4
BRANCH_W = 256
D_FF = 4096
POOL_GC = 64
POOL_WINDOWS = (2, 4, 8, 16)
N_HEADS = 4
MLA_Q_RANK = 256
MLA_KV_RANK = 128
MLA_NOPE = 64
MLA_ROPE = 32
MLA_V = 64
MLA_QK = 96
ROPE_BASE = 10000.0
NA_HD = 64
NA_WIN_R = 8
NA_WIN_C = 16
HG_DK = 64
HG_CHUNK = 64
LB_EPS = 1e-6

MXU_DTYPE = jnp.bfloat16
LANE = 128
HEAD_PAD = 128
VMEM_LIMIT = 48 * 1024 * 1024

COL = dict(gates=0, pool=4096, qa=4352, q_na=4608, q_hg=4864, g_cols=5120,
           k_na=5376, v_na=5632, zF=5888, zB=6144, i_hg=6400, ckv=6656, kr=6784, kr_sw=6912)
D_IN_PAD = 7168
KV_START = 5376
N_TILE = 1792


def _cparams(sem):
    return pltpu.CompilerParams(dimension_semantics=sem, vmem_limit_bytes=VMEM_LIMIT)


def _bf(x):
    return x.astype(MXU_DTYPE)


def _dot(a, b):
    return jnp.dot(a, b, preferred_element_type=jnp.float32)


def _dot_nt(a, b):
    return lax.dot_general(a, b, (((1,), (1,)), ((), ())), preferred_element_type=jnp.float32)


def _split_dot(x, w):
    hi = _bf(x)
    lo = _bf(x - hi.astype(jnp.float32))
    return _dot(hi, w) + _dot(lo, w)


def _sigmoid(x):
    return 1.0 / (1.0 + jnp.exp(-x))


def _rms_scale(x):
    return lax.rsqrt(jnp.mean(x * x, axis=-1, keepdims=True) + EPS)


def _mods_kernel(c_ref, w_ref, b_ref, o_ref):
    c = c_ref[...]
    s = _bf(c * _sigmoid(c))
    o_ref[0] = _dot(s, _bf(w_ref[0])) + b_ref[0]


def _mods(cc, ada_w, ada_b):
    depth, d, n = ada_w.shape
    tn = 1536
    return pl.pallas_call(
        _mods_kernel,
        grid=(depth, n // tn),
        in_specs=[pl.BlockSpec((8, d), lambda l, j: (0, 0)),
                  pl.BlockSpec((1, d, tn), lambda l, j: (l, 0, j)),
                  pl.BlockSpec((1, 1, tn), lambda l, j: (l, 0, j))],
        out_specs=pl.BlockSpec((1, 8, tn), lambda l, j: (l, 0, j)),
        out_shape=jax.ShapeDtypeStruct((depth, 8, n), jnp.float32),
        compiler_params=_cparams(("arbitrary", "arbitrary")),
        name="mods",
    )(cc, ada_w, ada_b.reshape(depth, 1, n))


def _inproj_kernel(x_ref, g_ref, sh_ref, sc_ref, w_ref, o_ref, h_ref):
    @pl.when(pl.program_id(2) == 0)
    def _():
        x = x_ref[0]
        y = x * _rms_scale(x) * g_ref[...]
        h_ref[...] = _bf(y * (1.0 + sc_ref[0]) + sh_ref[0])

    o_ref[0] = _dot(h_ref[...], w_ref[...])


def _inproj(x, g, sh, sc, w, tm):
    b, t, d = x.shape
    n = w.shape[1]
    return pl.pallas_call(
        _inproj_kernel,
        grid=(b, t // tm, n // N_TILE),
        in_specs=[pl.BlockSpec((1, tm, d), lambda i, j, k: (i, j, 0)),
                  pl.BlockSpec((1, d), lambda i, j, k: (0, 0)),
                  pl.BlockSpec((1, 1, d), lambda i, j, k: (i, 0, 0)),
                  pl.BlockSpec((1, 1, d), lambda i, j, k: (i, 0, 0)),
                  pl.BlockSpec((d, N_TILE), lambda i, j, k: (0, k))],
        out_specs=pl.BlockSpec((1, tm, N_TILE), lambda i, j, k: (i, j, k)),
        out_shape=jax.ShapeDtypeStruct((b, t, n), jnp.float32),
        scratch_shapes=[pltpu.VMEM((tm, d), MXU_DTYPE)],
        compiler_params=_cparams(("parallel", "parallel", "arbitrary")),
        name="inproj",
    )(x, g, sh, sc, w)


def _mla_prep_kernel(qa_ref, ckv_ref, kr_ref, krsw_ref, cos_ref, sin_ref, qn_ref, wq_ref, kvn_ref, wkv_ref,
                     gq_ref, gqsw_ref, gk_ref, gksw_ref, q_ref, k_ref, v_ref, *, scale):
    cos = cos_ref[...]
    sin = sin_ref[...]
    inv_w = 1.0 / MLA_QK
    qa = qa_ref[0]
    qq = _dot(_bf(qa * _rms_scale(qa) * qn_ref[...]), wq_ref[...])
    ckv = ckv_ref[0]
    kv = _dot(_bf(ckv * _rms_scale(ckv) * kvn_ref[...]), wkv_ref[...])
    kr = kr_ref[0]
    kr_rot = krsw_ref[0] * gksw_ref[...] * sin
    gq_cos = gq_ref[...] * cos
    gq_sin = gqsw_ref[...] * sin
    gk_cos = gk_ref[...] * cos
    for h in range(N_HEADS):
        qp = qq[:, h * HEAD_PAD:(h + 1) * HEAD_PAD]
        qs = qq[:, (N_HEADS + h) * HEAD_PAD:(N_HEADS + h + 1) * HEAD_PAD]
        rs = lax.rsqrt(jnp.sum(qp * qp, axis=-1, keepdims=True) * inv_w + EPS)
        q_ref[0, h] = _bf((qp * gq_cos + qs * gq_sin) * (rs * scale))
        kp = kv[:, h * HEAD_PAD:(h + 1) * HEAD_PAD] + kr
        rs = lax.rsqrt(jnp.sum(kp * kp, axis=-1, keepdims=True) * inv_w + EPS)
        k_ref[0, h] = _bf((kp * gk_cos + kr_rot) * rs)
    v_ref[0] = _bf(kv[:, N_HEADS * HEAD_PAD:])


def _mla_prep(u, base, cos, sin, qn, wq, kvn, wkv, gq, gqsw, gk, gksw, tm):
    b, t, _ = u.shape

    def ublk(name, width):
        idx = (COL[name] - base) // width
        return pl.BlockSpec((1, tm, width), lambda i, j: (i, j, idx))

    def full(a):
        return pl.BlockSpec(a.shape, lambda i, j: (0,) * a.ndim)

    tab = pl.BlockSpec((tm, HEAD_PAD), lambda i, j: (j, 0))
    hd = pl.BlockSpec((1, N_HEADS, tm, HEAD_PAD), lambda i, j: (i, 0, j, 0))
    return pl.pallas_call(
        functools.partial(_mla_prep_kernel, scale=MLA_QK ** -0.5),
        grid=(b, t // tm),
        in_specs=[ublk("qa", 256), ublk("ckv", 128), ublk("kr", 128), ublk("kr_sw", 128), tab, tab,
                  full(qn), full(wq), full(kvn), full(wkv), full(gq), full(gqsw), full(gk), full(gksw)],
        out_specs=[hd, hd, pl.BlockSpec((1, tm, 256), lambda i, j: (i, j, 0))],
        out_shape=[jax.ShapeDtypeStruct((b, N_HEADS, t, HEAD_PAD), MXU_DTYPE),
                   jax.ShapeDtypeStruct((b, N_HEADS, t, HEAD_PAD), MXU_DTYPE),
                   jax.ShapeDtypeStruct((b, t, 256), MXU_DTYPE)],
        compiler_params=_cparams(("parallel", "parallel")),
        name="mla_prep",
    )(u, u, u, u, cos, sin, qn, wq, kvn, wkv, gq, gqsw, gk, gksw)


def _mla_kv_prep_kernel(ckv_ref, kr_ref, kvn_ref, wkv_ref, gk_ref, k_ref, v_ref):
    inv_w = 1.0 / MLA_QK
    ckv = ckv_ref[0]
    kv = _dot(_bf(ckv * _rms_scale(ckv) * kvn_ref[...]), wkv_ref[...])
    kr = kr_ref[0]
    for h in range(N_HEADS):
        kp = kv[:, h * HEAD_PAD:(h + 1) * HEAD_PAD] + kr
        rs = lax.rsqrt(jnp.sum(kp * kp, axis=-1, keepdims=True) * inv_w + EPS)
        k_ref[0, h] = _bf(kp * gk_ref[...] * rs)
    v_ref[0] = _bf(kv[:, N_HEADS * HEAD_PAD:])


def _mla_kv_prep(u, base, kvn, wkv, gk, tm):
    b, t, _ = u.shape

    def ublk(name, width):
        idx = (COL[name] - base) // width
        return pl.BlockSpec((1, tm, width), lambda i, j: (i, j, idx))

    def full(a):
        return pl.BlockSpec(a.shape, lambda i, j: (0,) * a.ndim)

    hd = pl.BlockSpec((1, N_HEADS, tm, HEAD_PAD), lambda i, j: (i, 0, j, 0))
    return pl.pallas_call(
        _mla_kv_prep_kernel,
        grid=(b, t // tm),
        in_specs=[ublk("ckv", 128), ublk("kr", 128), full(kvn), full(wkv), full(gk)],
        out_specs=[hd, pl.BlockSpec((1, tm, 256), lambda i, j: (i, j, 0))],
        out_shape=[jax.ShapeDtypeStruct((b, N_HEADS, t, HEAD_PAD), MXU_DTYPE),
                   jax.ShapeDtypeStruct((b, t, 256), MXU_DTYPE)],
        compiler_params=_cparams(("parallel", "parallel")),
        name="mla_kv_prep",
    )(u, u, kvn, wkv, gk)


def _attn_kernel(q_ref, k_ref, v_ref, o_ref, *, tk, n_chunks):
    tq = q_ref.shape[2]
    lane = lax.broadcasted_iota(jnp.int32, (tq, LANE), 1)
    outs = []
    for h in range(N_HEADS):
        q = q_ref[0, h]
        vcol = (h // 2) * LANE

        def body(c, carry):
            m, l, acc = carry
            start = pl.multiple_of(c * tk, tk)
            s = _dot_nt(q, k_ref[0, h, pl.ds(start, tk), :])
            m_new = jnp.maximum(m, jnp.max(s, axis=-1, keepdims=True))
            alpha = jnp.exp(m - m_new)
            p = jnp.exp(s - m_new)
            l = alpha * l + jnp.sum(p, axis=-1, keepdims=True)
            acc = alpha * acc + _dot(_bf(p), v_ref[0, pl.ds(start, tk), vcol:vcol + LANE])
            return m_new, l, acc

        init = (jnp.full((tq, 1), -jnp.inf, jnp.float32), jnp.zeros((tq, 1), jnp.float32),
                jnp.zeros((tq, LANE), jnp.float32))
        _, l, acc = lax.fori_loop(0, n_chunks, body, init)
        outs.append(acc / l)
    lo = jnp.where(lane < 64, outs[0], outs[1])
    hi = jnp.where(lane < 64, outs[2], outs[3])
    o_ref[0, :, 0:LANE] = lo
    o_ref[0, :, LANE:2 * LANE] = hi


def _attention(q, k, v, tq, tk):
    b, _, t_q, _ = q.shape
    t_k = k.shape[2]
    assert t_q % tq == 0 and t_k % tk == 0
    return pl.pallas_call(
        functools.partial(_attn_kernel, tk=tk, n_chunks=t_k // tk),
        grid=(b, t_q // tq),
        in_specs=[pl.BlockSpec((1, N_HEADS, tq, HEAD_PAD), lambda i, j: (i, 0, j, 0)),
                  pl.BlockSpec((1, N_HEADS, t_k, HEAD_PAD), lambda i, j: (i, 0, 0, 0)),
                  pl.BlockSpec((1, t_k, 256), lambda i, j: (i, 0, 0))],
        out_specs=pl.BlockSpec((1, tq, 256), lambda i, j: (i, j, 0)),
        out_shape=jax.ShapeDtypeStruct((b, t_q, 256), jnp.float32),
        compiler_params=_cparams(("parallel", "parallel")),
        name="attention",
    )(q, k, v)


def _merge_kernel(x_ref, a_ref, b_ref, c_ref, o_ref, gc_ref, g0_ref, g1_ref, g2_ref, g3_ref,
                  hgn_ref, e_ref, wbr_ref, wo_ref, gate_ref, out_ref):
    o = o_ref[0]
    ms = _split_dot(o * o, e_ref[...]) * (1.0 / HG_DK)
    d = o * lax.rsqrt(ms + EPS) * hgn_ref[...] * _sigmoid(gc_ref[0])
    acc = None
    for n, (y, g_ref) in enumerate(((a_ref[0], g0_ref), (b_ref[0], g1_ref), (c_ref[0], g2_ref), (d, g3_ref))):
        term = _sigmoid(g_ref[0]) * _dot(_bf(y), wbr_ref[n])
        acc = term if acc is None else acc + term
    out_ref[0] = x_ref[0] + gate_ref[0] * _dot(_bf(acc), wo_ref[...])


def _merge(x, a, bm, cn, o_hg, u, base, hgn, e_mat, w_br, w_o, gate, tm):
    b, t, d = x.shape

    def ublk(col, width):
        idx = (col - base) // width
        return pl.BlockSpec((1, tm, width), lambda i, j: (i, j, idx))

    def full(arr):
        return pl.BlockSpec(arr.shape, lambda i, j: (0,) * arr.ndim)

    tok = pl.BlockSpec((1, tm, 256), lambda i, j: (i, j, 0))
    xs = pl.BlockSpec((1, tm, d), lambda i, j: (i, j, 0))
    return pl.pallas_call(
        _merge_kernel,
        grid=(b, t // tm),
        in_specs=[xs, tok, tok, tok, tok, ublk(COL["g_cols"], 256)]
                 + [ublk(COL["gates"] + n * d, d) for n in range(N_BRANCH)]
                 + [full(hgn), full(e_mat), full(w_br), full(w_o),
                    pl.BlockSpec((1, 1, d), lambda i, j: (i, 0, 0))],
        out_specs=xs,
        out_shape=jax.ShapeDtypeStruct((b, t, d), jnp.float32),
        compiler_params=_cparams(("parallel", "parallel")),
        name="merge",
    )(x, a, bm, cn, o_hg, u, u, u, u, u, hgn, e_mat, w_br, w_o, gate)


def _ffn_kernel(x_ref, g_ref, sh_ref, sc_ref, w1_ref, w2_ref, gate_ref, o_ref, h_ref, acc_ref):
    k = pl.program_id(2)

    @pl.when(k == 0)
    def _():
        x = x_ref[0]
        y = x * _rms_scale(x) * g_ref[...]
        h_ref[...] = _bf(y * (1.0 + sc_ref[0]) + sh_ref[0])

    z = jnp.maximum(_dot(h_ref[...], w1_ref[...]), 0.0)
    part = _dot(_bf(z * z), w2_ref[...])

    @pl.when(k == 0)
    def _():
        acc_ref[...] = part

    @pl.when(k > 0)
    def _():
        acc_ref[...] += part

    @pl.when(k == pl.num_programs(2) - 1)
    def _():
        o_ref[0] = x_ref[0] + gate_ref[0] * acc_ref[...]


def _ffn(x, g, sh, sc, w1, w2, gate, tm, tf):
    b, t, d = x.shape
    f = w1.shape[1]
    xs = pl.BlockSpec((1, tm, d), lambda i, j, k: (i, j, 0))
    vec = pl.BlockSpec((1, 1, d), lambda i, j, k: (i, 0, 0))
    return pl.pallas_call(
        _ffn_kernel,
        grid=(b, t // tm, f // tf),
        in_specs=[xs, pl.BlockSpec((1, d), lambda i, j, k: (0, 0)), vec, vec,
                  pl.BlockSpec((d, tf), lambda i, j, k: (0, k)),
                  pl.BlockSpec((tf, d), lambda i, j, k: (k, 0)), vec],
        out_specs=xs,
        out_shape=jax.ShapeDtypeStruct((b, t, d), jnp.float32),
        scratch_shapes=[pltpu.VMEM((tm, d), MXU_DTYPE), pltpu.VMEM((tm, d), jnp.float32)],
        compiler_params=_cparams(("parallel", "parallel", "arbitrary")),
        name="ffn",
    )(x, g, sh, sc, w1, w2, gate)


def _rmsnorm(x, g):
    xf = x.astype(jnp.float32)
    return xf * lax.rsqrt(jnp.mean(xf * xf, axis=-1, keepdims=True) + EPS) * g


def _split_heads(u, g=None):
    b, l, _ = u.shape
    h = u.reshape(b, l, N_HEADS, -1)
    if g is not None:
        h = _rmsnorm(h, g)
    return h.transpose(0, 2, 1, 3)


def _merge_heads(o):
    b, h, l, d = o.shape
    return o.transpose(0, 2, 1, 3).reshape(b, l, h * d)


def _na_tables(rows, rpb):
    wr = min(NA_WIN_R, rows)
    r = jnp.arange(rows)
    col = jnp.arange(GRID_W)
    rs = jnp.clip(r - NA_WIN_R // 2, 0, rows - wr)
    cs = jnp.clip(col - NA_WIN_C // 2, 0, GRID_W - NA_WIN_C)
    key_r = rs[:, None] + jnp.arange(wr)
    key_c = cs[:, None] + jnp.arange(NA_WIN_C)
    idx = (key_r[:, None, :, None] * GRID_W + key_c[None, :, None, :]).reshape(rows, GRID_W, wr * NA_WIN_C)
    dr = key_r - r[:, None] + NA_WIN_R - 1
    dc = key_c - col[:, None] + NA_WIN_C - 1
    bias = rpb[:, dr[:, None, :, None], dc[None, :, None, :]].reshape(N_HEADS, rows, GRID_W, wr * NA_WIN_C)
    return idx, bias.astype(jnp.float32)


def _na_attention_jax(q, k, v, k_ctx, v_ctx, rpb):
    b, hh, l, d = q.shape
    rows = l // GRID_W
    idx, bias = _na_tables(rows, rpb)
    n_loc = idx.shape[-1]
    scale = NA_HD ** -0.5
    qr = q.reshape(b, hh, rows, GRID_W, d).transpose(2, 0, 1, 3, 4)

    def one_row(args):
        qi, ii, bi = args
        k_loc = k[:, :, ii]
        v_loc = v[:, :, ii]
        s_loc = jnp.einsum('bhqd,bhqnd->bhqn', qi, k_loc) * scale + bi
        s_ctx = jnp.einsum('bhqd,bhkd->bhqk', qi, k_ctx) * scale
        pr = jax.nn.softmax(jnp.concatenate([s_loc, s_ctx], axis=-1), axis=-1)
        return (jnp.einsum('bhqn,bhqnd->bhqd', pr[..., :n_loc], v_loc)
                + jnp.einsum('bhqk,bhkd->bhqd', pr[..., n_loc:], v_ctx))

    o = lax.map(one_row, (qr, idx, bias.transpose(1, 0, 2, 3)))
    return o.transpose(1, 2, 0, 3, 4).reshape(b, hh, l, d)


def _dense_attention_jax(q, k, v, scale):
    s = jnp.einsum('bhqd,bhkd->bhqk', q, k) * scale
    return jnp.einsum('bhqk,bhkd->bhqd', jax.nn.softmax(s, axis=-1), v)


def _pool_mix_jax(u, w_grp, scale):
    b, l, _ = u.shape
    csum = jnp.concatenate([jnp.zeros((b, 1, 256), jnp.float32), jnp.cumsum(u, axis=1)], axis=1)
    t = jnp.arange(l)
    groups = []
    for gi, w in enumerate(POOL_WINDOWS):
        lo = jnp.clip(t - w // 2, 0, l)
        hi = jnp.clip(t - w // 2 + w, 0, l)
        cg = csum[..., gi * POOL_GC:(gi + 1) * POOL_GC]
        cnt = (hi - lo).astype(jnp.float32)[None, :, None]
        groups.append((cg[:, hi] - cg[:, lo]) / cnt - u[..., gi * POOL_GC:(gi + 1) * POOL_GC])
    p = jnp.stack(groups, axis=2)
    y = jnp.einsum('blgc,gcd->blgd', p, w_grp).reshape(b, l, 256)
    return y * scale


def _hgrn_gates(z, lb):
    logf = jnp.logaddexp(jax.nn.log_sigmoid(z), jnp.log(lb) + jax.nn.log_sigmoid(-z))
    k = (1 - lb) * jax.nn.sigmoid(-z)
    return logf, k


def _hgrn_scan_jax(q, k, logf, i, s0):
    b, hh, l, _ = k.shape
    nc = l // HG_CHUNK

    def chunks(a):
        return a.reshape(b, hh, nc, HG_CHUNK, a.shape[-1]).transpose(2, 0, 1, 3, 4)

    lower_tri = jnp.tril(jnp.ones((HG_CHUNK, HG_CHUNK), dtype=bool))[:, :, None]

    def step(S, xs):
        qc, kc, fc, ic = xs
        bb = jnp.cumsum(fc, axis=2)
        diff = bb[:, :, :, None, :] - bb[:, :, None, :, :]
        dec = jnp.where(lower_tri, jnp.exp(jnp.minimum(diff, 0.0)), 0.0)
        a = jnp.einsum('bhtd,bhsd,bhtsd->bhts', qc, kc, dec)
        o = jnp.einsum('bhts,bhsv->bhtv', a, ic) + jnp.einsum('bhtd,bhdv->bhtv', qc * jnp.exp(bb), S)
        bl = bb[:, :, -1]
        S = jnp.exp(bl)[..., None] * S + jnp.einsum('bhsd,bhsv->bhdv', kc * jnp.exp(bl[:, :, None] - bb), ic)
        return S, o

    s_fin, o = lax.scan(step, s0, (chunks(q), chunks(k), chunks(logf), chunks(i)))
    return o.transpose(1, 2, 0, 3, 4).reshape(b, hh, l, -1), s_fin


def _prep_w_in(w):
    d = w.shape[0]
    o = dict(ckv=0, kr=128, k_na=160, v_na=416, zF=672, zB=928, i_hg=1184, pool=1440, qa=1696, q_na=1952,
             q_hg=2208, g_cols=2464, gates=2720)
    z = lambda n: jnp.zeros((d, n), w.dtype)
    k_rope = w[:, o["kr"]:o["kr"] + MLA_ROPE]
    perm = np.array([(t // 16) * 16 + ((t % 16) + 8) % 16 for t in range(MLA_ROPE)])
    parts = [w[:, o["gates"]:o["gates"] + 4096]]
    for name in ("pool", "qa", "q_na", "q_hg", "g_cols", "k_na", "v_na", "zF", "zB", "i_hg"):
        parts.append(w[:, o[name]:o[name] + 256])
    parts += [w[:, 0:128], z(64), k_rope, z(32), z(64), k_rope[:, perm], z(32), z(128)]
    return _bf(jnp.concatenate(parts, axis=1))


def _rope_perm():
    return np.array([(t // 16) * 16 + ((t % 16) + 8) % 16 for t in range(MLA_ROPE)])


def _prep_wq(w):
    r = w.shape[0]
    wh = w.reshape(r, N_HEADS, MLA_QK)
    pre = jnp.pad(wh, ((0, 0), (0, 0), (0, HEAD_PAD - MLA_QK)))
    sw = jnp.pad(wh[:, :, MLA_NOPE:][:, :, _rope_perm()], ((0, 0), (0, 0), (MLA_NOPE, HEAD_PAD - MLA_QK)))
    return _bf(jnp.concatenate([pre.reshape(r, -1), sw.reshape(r, -1)], axis=1))


def _prep_wkv(w):
    r = w.shape[0]
    wh = w.reshape(r, N_HEADS, MLA_NOPE + MLA_V)
    kpart = jnp.pad(wh[:, :, :MLA_NOPE], ((0, 0), (0, 0), (0, HEAD_PAD - MLA_NOPE)))
    return _bf(jnp.concatenate([kpart.reshape(r, -1), wh[:, :, MLA_NOPE:].reshape(r, -1)], axis=1))


def _pad_gain(g):
    plain = jnp.pad(g, (0, HEAD_PAD - MLA_QK))[None]
    sw = jnp.pad(g[MLA_NOPE:][_rope_perm()], (MLA_NOPE, HEAD_PAD - MLA_QK))[None]
    return plain, sw


def _rope_tables(n_tok):
    t = jnp.arange(n_tok)
    pos = jnp.stack([t // GRID_W, t % GRID_W], axis=-1).astype(jnp.float32)
    nf = MLA_ROPE // 4
    inv = ROPE_BASE ** (-jnp.arange(nf, dtype=jnp.float32) / nf)
    ang = pos[:, :, None] * inv
    cos, sin = jnp.cos(ang), jnp.sin(ang)
    cos_t = jnp.concatenate([cos, cos], axis=-1).reshape(n_tok, MLA_ROPE)
    sin_t = jnp.concatenate([-sin, sin], axis=-1).reshape(n_tok, MLA_ROPE)
    ones = jnp.ones((n_tok, MLA_NOPE), jnp.float32)
    zer = jnp.zeros((n_tok, HEAD_PAD - MLA_QK), jnp.float32)
    cos_p = jnp.concatenate([ones, cos_t, zer], axis=-1)
    sin_p = jnp.concatenate([0 * ones, sin_t, zer], axis=-1)
    return cos_p, sin_p


def _plain_tables(n_tok):
    cos_p = jnp.concatenate([jnp.ones((n_tok, MLA_QK), jnp.float32),
                             jnp.zeros((n_tok, HEAD_PAD - MLA_QK), jnp.float32)], axis=-1)
    return cos_p, jnp.zeros((n_tok, HEAD_PAD), jnp.float32)


def _head_indicator():
    i = np.arange(256) // 64
    return jnp.asarray((i[:, None] == i[None, :]).astype(np.float32), dtype=MXU_DTYPE)


def kernel(x, c, ctx, c_ctx, ada_w, ada_b, norm1, norm2, w_in, pool_w, pool_scale, mla_q_norm, mla_wq_b, mla_kv_norm, mla_wkv_b, mla_gq, mla_gk, na_gq, na_gk, na_rpb, hg_lb, hg_norm, w_branch, w_out, w_ff1, w_ff2):
    B, L, D = x.shape
    Tc = ctx.shape[1]
    depth = ada_w.shape[0]
    f32 = jnp.float32

    lb_p = jax.nn.softmax(hg_lb.astype(f32), axis=0)
    lb_all = jnp.clip(jnp.cumsum(lb_p, axis=0)[:depth], LB_EPS, 1.0 - LB_EPS)

    cc = jnp.concatenate([c, c_ctx[None], jnp.zeros((8 - B - 1, D), f32)], axis=0)
    mods = _mods(cc, ada_w, ada_b)

    cos_l, sin_l = _rope_tables(L)
    cos_c, sin_c = _plain_tables(Tc)
    e_mat = _head_indicator()
    hg_qscale = HG_DK ** -0.5

    xl, xc = x, ctx
    for l in range(depth):
        ctx_out = l < depth - 1
        ml = mods[l, :B].reshape(B, 1, 6, D)
        mc = jnp.broadcast_to(mods[l, B].reshape(1, 1, 6, D), (B, 1, 6, D))
        lmod = [ml[:, :, i] for i in range(6)]
        cmod = [mc[:, :, i] for i in range(6)]
        g1n = norm1[l][None]
        g2n = norm2[l][None]
        w_in_p = _prep_w_in(w_in[l])
        wq = _prep_wq(mla_wq_b[l])
        wkv = _prep_wkv(mla_wkv_b[l])
        gq, gqsw = _pad_gain(mla_gq[l])
        gk, gksw = _pad_gain(mla_gk[l])
        qn = mla_q_norm[l][None]
        kvn = mla_kv_norm[l][None]
        hgn = jnp.tile(hg_norm[l], N_HEADS)[None]
        w_br = _bf(w_branch[l])
        w_o = _bf(w_out[l])
        w1 = _bf(w_ff1[l])
        w2 = _bf(w_ff2[l])
        lbF = lb_all[l, 0].reshape(N_HEADS, 1, HG_DK)
        lbB = lb_all[l, 1].reshape(N_HEADS, 1, HG_DK)

        ul = _inproj(xl, g1n, lmod[0], lmod[1], w_in_p, min(1024, L))
        if ctx_out:
            uc, cbase = _inproj(xc, g1n, cmod[0], cmod[1], w_in_p, Tc), 0
        else:
            uc, cbase = _inproj(xc, g1n, cmod[0], cmod[1], w_in_p[:, KV_START:], Tc), KV_START

        def colsl(u, base, name, width=256):
            return u[..., COL[name] - base:COL[name] - base + width]

        q_mla, k_mla, v_mla = _mla_prep(ul, 0, cos_l, sin_l, qn, wq, kvn, wkv, gq, gqsw, gk, gksw, 512)
        if ctx_out:
            q_mla_c, k_mla_c, v_mla_c = _mla_prep(uc, cbase, cos_c, sin_c, qn, wq, kvn, wkv, gq, gqsw, gk, gksw, Tc)
        else:
            k_mla_c, v_mla_c = _mla_kv_prep(uc, cbase, kvn, wkv, gk, Tc)
        b_l = _attention(q_mla, jnp.concatenate([k_mla_c, k_mla], axis=2),
                         jnp.concatenate([v_mla_c, v_mla], axis=1), 256, 256)

        k_na_c = _split_heads(colsl(uc, cbase, "k_na"), na_gk[l])
        v_na_c = _split_heads(colsl(uc, cbase, "v_na"))
        q_na = _split_heads(colsl(ul, 0, "q_na"), na_gq[l])
        k_na = _split_heads(colsl(ul, 0, "k_na"), na_gk[l])
        v_na = _split_heads(colsl(ul, 0, "v_na"))
        c_l = _merge_heads(_na_attention_jax(q_na, k_na, v_na, k_na_c, v_na_c, na_rpb[l]))
        a_l = _pool_mix_jax(colsl(ul, 0, "pool"), pool_w[l], pool_scale[l])

        def hg_in(u, base):
            fF, kF = _hgrn_gates(_split_heads(colsl(u, base, "zF")), lbF)
            fB, kB = _hgrn_gates(_split_heads(colsl(u, base, "zB")), lbB)
            return fF, kF, fB, kB, _split_heads(colsl(u, base, "i_hg"))

        flip = lambda a: jnp.flip(a, axis=2)
        fFc, kFc, fBc, kBc, ic = hg_in(uc, cbase)
        zeros = jnp.zeros((B, N_HEADS, HG_DK, HG_DK), f32)
        qc_hg = (_split_heads(colsl(uc, cbase, "q_hg")) * hg_qscale) if ctx_out else jnp.zeros_like(ic)
        oFc, SFc = _hgrn_scan_jax(qc_hg, kFc, fFc, ic, zeros)
        oBc, SBc = _hgrn_scan_jax(flip(qc_hg), flip(kBc), flip(fBc), flip(ic), zeros)
        fFl, kFl, fBl, kBl, il = hg_in(ul, 0)
        ql_hg = _split_heads(colsl(ul, 0, "q_hg")) * hg_qscale
        oFl, _ = _hgrn_scan_jax(ql_hg, kFl, fFl, il, SFc)
        oBl, _ = _hgrn_scan_jax(flip(ql_hg), flip(kBl), flip(fBl), flip(il), SBc)
        o_l = _merge_heads(oFl + flip(oBl))

        xl_new = _merge(xl, a_l, b_l, c_l, o_l, ul, 0, hgn, e_mat, w_br, w_o, lmod[2], 256)

        if ctx_out:
            a_c = _pool_mix_jax(colsl(uc, cbase, "pool"), pool_w[l], pool_scale[l])
            b_c = _attention(q_mla_c, k_mla_c, v_mla_c, Tc, Tc)
            q_na_c = _split_heads(colsl(uc, cbase, "q_na"), na_gq[l])
            c_c = _merge_heads(_dense_attention_jax(q_na_c, k_na_c, v_na_c, NA_HD ** -0.5))
            o_c = _merge_heads(oFc + flip(oBc))
            xc = _merge(xc, a_c, b_c, c_c, o_c, uc, cbase, hgn, e_mat, w_br, w_o, cmod[2], Tc)
            xc = _ffn(xc, g2n, cmod[3], cmod[4], w1, w2, cmod[5], Tc, 1024)

        xl = _ffn(xl_new, g2n, lmod[3], lmod[4], w1, w2, lmod[5], 512, 1024)
    return xl
```

```python
import functools

import jax
import jax.numpy as jnp
import numpy as np
from jax import lax
from jax.experimental import pallas as pl
from jax.experimental.pallas import tpu as pltpu

D_MODEL = 1024
GRID_W = 64
EPS = 1e-6
N_BRANCH = 4
BRANCH_W = 256
POOL_GC = 64
N_HEADS = 4
MLA_NOPE = 64
MLA_ROPE = 32
MLA_V = 64
MLA_QK = 96
ROPE_BASE = 10000.0
NA_HD = 64
NA_WIN_R = 8
NA_WIN_C = 16
HG_DK = 64
LB_EPS = 1e-6

MXU_DTYPE = jnp.bfloat16
LANE = 128
HEAD_PAD = 128
VMEM_LIMIT = 48 * 1024 * 1024
HG_SUB = 16
HG_TILE = 256
NEG = -1e30

COL = dict(gates=0, pool=4096, qa=4352, q_na=4608, q_hg=4864, g_cols=5120,
           k_na=5376, v_na=5632, zF=5888, zB=6144, i_hg=6400, ckv=6656, kr=6784, kr_sw=6912)
D_IN_PAD = 7168
KV_START = 5376
N_TILE = 1792


def _cparams(sem):
    return pltpu.CompilerParams(dimension_semantics=sem, vmem_limit_bytes=VMEM_LIMIT)


def _bf(x):
    return x.astype(MXU_DTYPE)


def _dot(a, b):
    return jnp.dot(a, b, preferred_element_type=jnp.float32)


def _dot_nt(a, b):
    return lax.dot_general(a, b, (((1,), (1,)), ((), ())), preferred_element_type=jnp.float32)


def _dot_tn(a, b):
    return lax.dot_general(a, b, (((0,), (0,)), ((), ())), preferred_element_type=jnp.float32)


def _split2(x):
    hi = _bf(x)
    return hi, _bf(x - hi.astype(jnp.float32))


def _split_dot(x, w):
    hi, lo = _split2(x)
    return _dot(hi, w) + _dot(lo, w)


def _split3_dot_left(w, x):
    hi = _bf(x)
    r1 = x - hi.astype(jnp.float32)
    mid = _bf(r1)
    lo = _bf(r1 - mid.astype(jnp.float32))
    return _dot(w, hi) + _dot(w, mid) + _dot(w, lo)


def _sigmoid(x):
    return 1.0 / (1.0 + jnp.exp(-x))


def _rms_scale(x):
    return lax.rsqrt(jnp.mean(x * x, axis=-1, keepdims=True) + EPS)


def _full(arr, nargs):
    zeros = (0,) * arr.ndim
    return pl.BlockSpec(arr.shape, lambda *_: zeros)


def _ublk(tm, col, base, width):
    idx = (col - base) // width
    return pl.BlockSpec((1, tm, width), lambda i, j: (i, j, idx))


def _mods_kernel(c_ref, w_ref, b_ref, o_ref):
    c = c_ref[...]
    s = _bf(c * _sigmoid(c))
    o_ref[0] = _dot(s, _bf(w_ref[0])) + b_ref[0]


def _mods(cc, ada_w, ada_b):
    depth, d, n = ada_w.shape
    tn = 1536
    return pl.pallas_call(
        _mods_kernel,
        grid=(depth, n // tn),
        in_specs=[pl.BlockSpec((8, d), lambda l, j: (0, 0)),
                  pl.BlockSpec((1, d, tn), lambda l, j: (l, 0, j)),
                  pl.BlockSpec((1, 1, tn), lambda l, j: (l, 0, j))],
        out_specs=pl.BlockSpec((1, 8, tn), lambda l, j: (l, 0, j)),
        out_shape=jax.ShapeDtypeStruct((depth, 8, n), jnp.float32),
        compiler_params=_cparams(("arbitrary", "arbitrary")),
        name="mods",
    )(cc, ada_w, ada_b.reshape(depth, 1, n))


def _inproj_kernel(x_ref, g_ref, sh_ref, sc_ref, w_ref, o_ref, h_ref):
    @pl.when(pl.program_id(2) == 0)
    def _():
        x = x_ref[0]
        y = x * _rms_scale(x) * g_ref[...]
        h_ref[...] = _bf(y * (1.0 + sc_ref[0]) + sh_ref[0])

    o_ref[0] = _dot(h_ref[...], w_ref[...])


def _inproj(x, g, sh, sc, w, tm):
    b, t, d = x.shape
    n = w.shape[1]
    return pl.pallas_call(
        _inproj_kernel,
        grid=(b, t // tm, n // N_TILE),
        in_specs=[pl.BlockSpec((1, tm, d), lambda i, j, k: (i, j, 0)),
                  pl.BlockSpec((1, d), lambda i, j, k: (0, 0)),
                  pl.BlockSpec((1, 1, d), lambda i, j, k: (i, 0, 0)),
                  pl.BlockSpec((1, 1, d), lambda i, j, k: (i, 0, 0)),
                  pl.BlockSpec((d, N_TILE), lambda i, j, k: (0, k))],
        out_specs=pl.BlockSpec((1, tm, N_TILE), lambda i, j, k: (i, j, k)),
        out_shape=jax.ShapeDtypeStruct((b, t, n), jnp.float32),
        scratch_shapes=[pltpu.VMEM((tm, d), MXU_DTYPE)],
        compiler_params=_cparams(("parallel", "parallel", "arbitrary")),
        name="inproj",
    )(x, g, sh, sc, w)


def _mla_prep_kernel(qa_ref, ckv_ref, kr_ref, krsw_ref, cos_ref, sin_ref, qn_ref, wq_ref, kvn_ref, wkv_ref,
                     gq_ref, gqsw_ref, gk_ref, gksw_ref, q_ref, k_ref, v_ref, *, scale):
    cos = cos_ref[...]
    sin = sin_ref[...]
    inv_w = 1.0 / MLA_QK
    qa = qa_ref[0]
    qq = _dot(_bf(qa * _rms_scale(qa) * qn_ref[...]), wq_ref[...])
    ckv = ckv_ref[0]
    kv = _dot(_bf(ckv * _rms_scale(ckv) * kvn_ref[...]), wkv_ref[...])
    kr = kr_ref[0]
    kr_rot = krsw_ref[0] * gksw_ref[...] * sin
    gq_cos = gq_ref[...] * cos
    gq_sin = gqsw_ref[...] * sin
    gk_cos = gk_ref[...] * cos
    for h in range(N_HEADS):
        qp = qq[:, h * HEAD_PAD:(h + 1) * HEAD_PAD]
        qs = qq[:, (N_HEADS + h) * HEAD_PAD:(N_HEADS + h + 1) * HEAD_PAD]
        rs = lax.rsqrt(jnp.sum(qp * qp, axis=-1, keepdims=True) * inv_w + EPS)
        q_ref[0, h] = _bf((qp * gq_cos + qs * gq_sin) * (rs * scale))
        kp = kv[:, h * HEAD_PAD:(h + 1) * HEAD_PAD] + kr
        rs = lax.rsqrt(jnp.sum(kp * kp, axis=-1, keepdims=True) * inv_w + EPS)
        k_ref[0, h] = _bf((kp * gk_cos + kr_rot) * rs)
    v_ref[0] = _bf(kv[:, N_HEADS * HEAD_PAD:])


def _mla_prep(u, base, cos, sin, qn, wq, kvn, wkv, gq, gqsw, gk, gksw, tm):
    b, t, _ = u.shape
    tab = pl.BlockSpec((tm, HEAD_PAD), lambda i, j: (j, 0))
    hd = pl.BlockSpec((1, N_HEADS, tm, HEAD_PAD), lambda i, j: (i, 0, j, 0))
    consts = (qn, wq, kvn, wkv, gq, gqsw, gk, gksw)
    return pl.pallas_call(
        functools.partial(_mla_prep_kernel, scale=MLA_QK ** -0.5),
        grid=(b, t // tm),
        in_specs=[_ublk(tm, COL["qa"], base, 256), _ublk(tm, COL["ckv"], base, 128),
                  _ublk(tm, COL["kr"], base, 128), _ublk(tm, COL["kr_sw"], base, 128), tab, tab]
                 + [_full(a, 2) for a in consts],
        out_specs=[hd, hd, pl.BlockSpec((1, tm, 256), lambda i, j: (i, j, 0))],
        out_shape=[jax.ShapeDtypeStruct((b, N_HEADS, t, HEAD_PAD), MXU_DTYPE),
                   jax.ShapeDtypeStruct((b, N_HEADS, t, HEAD_PAD), MXU_DTYPE),
                   jax.ShapeDtypeStruct((b, t, 256), MXU_DTYPE)],
        compiler_params=_cparams(("parallel", "parallel")),
        name="mla_prep",
    )(u, u, u, u, cos, sin, *consts)


def _mla_kv_prep_kernel(ckv_ref, kr_ref, kvn_ref, wkv_ref, gk_ref, k_ref, v_ref):
    inv_w = 1.0 / MLA_QK
    ckv = ckv_ref[0]
    kv = _dot(_bf(ckv * _rms_scale(ckv) * kvn_ref[...]), wkv_ref[...])
    kr = kr_ref[0]
    for h in range(N_HEADS):
        kp = kv[:, h * HEAD_PAD:(h + 1) * HEAD_PAD] + kr
        rs = lax.rsqrt(jnp.sum(kp * kp, axis=-1, keepdims=True) * inv_w + EPS)
        k_ref[0, h] = _bf(kp * gk_ref[...] * rs)
    v_ref[0] = _bf(kv[:, N_HEADS * HEAD_PAD:])


def _mla_kv_prep(u, base, kvn, wkv, gk, tm):
    b, t, _ = u.shape
    hd = pl.BlockSpec((1, N_HEADS, tm, HEAD_PAD), lambda i, j: (i, 0, j, 0))
    return pl.pallas_call(
        _mla_kv_prep_kernel,
        grid=(b, t // tm),
        in_specs=[_ublk(tm, COL["ckv"], base, 128), _ublk(tm, COL["kr"], base, 128),
                  _full(kvn, 2), _full(wkv, 2), _full(gk, 2)],
        out_specs=[hd, pl.BlockSpec((1, tm, 256), lambda i, j: (i, j, 0))],
        out_shape=[jax.ShapeDtypeStruct((b, N_HEADS, t, HEAD_PAD), MXU_DTYPE),
                   jax.ShapeDtypeStruct((b, t, 256), MXU_DTYPE)],
        compiler_params=_cparams(("parallel", "parallel")),
        name="mla_kv_prep",
    )(u, u, kvn, wkv, gk)


def _attn_kernel(q_ref, k_ref, v_ref, o_ref, *, tk, n_chunks):
    tq = q_ref.shape[2]
    lane = lax.broadcasted_iota(jnp.int32, (tq, LANE), 1)
    outs = []
    for h in range(N_HEADS):
        q = q_ref[0, h]
        vcol = (h // 2) * LANE

        def body(c, carry):
            m, l, acc = carry
            start = pl.multiple_of(c * tk, tk)
            s = _dot_nt(q, k_ref[0, h, pl.ds(start, tk), :])
            m_new = jnp.maximum(m, jnp.max(s, axis=-1, keepdims=True))
            alpha = jnp.exp(m - m_new)
            p = jnp.exp(s - m_new)
            l = alpha * l + jnp.sum(p, axis=-1, keepdims=True)
            acc = alpha * acc + _dot(_bf(p), v_ref[0, pl.ds(start, tk), vcol:vcol + LANE])
            return m_new, l, acc

        init = (jnp.full((tq, 1), -jnp.inf, jnp.float32), jnp.zeros((tq, 1), jnp.float32),
                jnp.zeros((tq, LANE), jnp.float32))
        _, l, acc = lax.fori_loop(0, n_chunks, body, init)
        outs.append(acc / l)
    o_ref[0, :, 0:LANE] = jnp.where(lane < 64, outs[0], outs[1])
    o_ref[0, :, LANE:2 * LANE] = jnp.where(lane < 64, outs[2], outs[3])


def _attention(q, k, v, tq, tk):
    b, _, t_q, _ = q.shape
    t_k = k.shape[2]
    assert t_q % tq == 0 and t_k % tk == 0
    return pl.pallas_call(
        functools.partial(_attn_kernel, tk=tk, n_chunks=t_k // tk),
        grid=(b, t_q // tq),
        in_specs=[pl.BlockSpec((1, N_HEADS, tq, HEAD_PAD), lambda i, j: (i, 0, j, 0)),
                  pl.BlockSpec((1, N_HEADS, t_k, HEAD_PAD), lambda i, j: (i, 0, 0, 0)),
                  pl.BlockSpec((1, t_k, 256), lambda i, j: (i, 0, 0))],
        out_specs=pl.BlockSpec((1, tq, 256), lambda i, j: (i, j, 0)),
        out_shape=jax.ShapeDtypeStruct((b, t_q, 256), jnp.float32),
        compiler_params=_cparams(("parallel", "parallel")),
        name="attention",
    )(q, k, v)


def _pool_kernel(prev_ref, cur_ref, next_ref, w_ref, sc_ref, o_ref, ext_ref, *, tm, n_tok):
    j = pl.program_id(1)
    cur = cur_ref[0]
    ext_ref[8:8 + tm, :] = cur
    ext_ref[0:8, :] = jnp.where(j > 0, prev_ref[0], 0.0)
    ext_ref[8 + tm:16 + tm, :] = jnp.where(j < pl.num_programs(1) - 1, next_ref[0], 0.0)

    def window_sums(c0, widths):
        sh = lambda k: ext_ref[8 + k:8 + k + tm, c0:c0 + LANE]
        acc = sh(-1) + sh(0)
        out = {2: acc}
        for w in (4, 8, 16):
            if w > max(widths):
                break
            for k in list(range(-w // 2, -w // 4)) + list(range(w // 4, w // 2)):
                acc = acc + sh(k)
            out[w] = acc
        return [out[w] for w in widths]

    lane = lax.broadcasted_iota(jnp.int32, (tm, LANE), 1)
    t = j * tm + lax.broadcasted_iota(jnp.int32, (tm, LANE), 0)
    halves = []
    for c0, (wa, wb) in ((0, (2, 4)), (LANE, (8, 16))):
        sa, sb = window_sums(c0, (wa, wb))
        first = lane < POOL_GC
        s = jnp.where(first, sa, sb)
        half = jnp.where(first, wa // 2, wb // 2)
        cnt = jnp.minimum(t + half, n_tok) - jnp.maximum(t - half, 0)
        halves.append(s / cnt.astype(jnp.float32) - cur[:, c0:c0 + LANE])
    p = jnp.concatenate(halves, axis=-1)
    o_ref[0] = _dot(_bf(p), w_ref[...]) * sc_ref[...]


def _pool(u, base, w_bd, scale, tm):
    b, t, _ = u.shape
    cidx = (COL["pool"] - base) // 256
    rb = tm // 8
    last = t // 8 - 1
    return pl.pallas_call(
        functools.partial(_pool_kernel, tm=tm, n_tok=t),
        grid=(b, t // tm),
        in_specs=[pl.BlockSpec((1, 8, 256), lambda i, j: (i, jnp.maximum(j * rb - 1, 0), cidx)),
                  pl.BlockSpec((1, tm, 256), lambda i, j: (i, j, cidx)),
                  pl.BlockSpec((1, 8, 256), lambda i, j: (i, jnp.minimum((j + 1) * rb, last), cidx)),
                  _full(w_bd, 2), _full(scale, 2)],
        out_specs=pl.BlockSpec((1, tm, 256), lambda i, j: (i, j, 0)),
        out_shape=jax.ShapeDtypeStruct((b, t, 256), jnp.float32),
        scratch_shapes=[pltpu.VMEM((tm + 16, 256), jnp.float32)],
        compiler_params=_cparams(("parallel", "parallel")),
        name="pool",
    )(u, u, u, w_bd, scale)


def _head_rms(x, e):
    return lax.rsqrt(_split_dot(x * x, e) * (1.0 / NA_HD) + EPS)


def _na_prep_kernel(q_ref, k_ref, v_ref, gq_ref, gk_ref, e_ref, qo_ref, ko_ref, vo_ref):
    e = e_ref[...]
    q = q_ref[0]
    qo_ref[0] = _bf(q * _head_rms(q, e) * gq_ref[...])
    k = k_ref[0]
    ko_ref[0] = _bf(k * _head_rms(k, e) * gk_ref[...])
    vo_ref[0] = _bf(v_ref[0])


def _na_prep(uq, qcol, u, base, gq, gk, e_mat, tm):
    b, t, _ = u.shape
    tok = pl.BlockSpec((1, tm, 256), lambda i, j: (i, j, 0))
    return pl.pallas_call(
        _na_prep_kernel,
        grid=(b, t // tm),
        in_specs=[pl.BlockSpec((1, tm, 256), lambda i, j: (i, j, qcol)),
                  _ublk(tm, COL["k_na"], base, 256), _ublk(tm, COL["v_na"], base, 256),
                  _full(gq, 2), _full(gk, 2), _full(e_mat, 2)],
        out_specs=[tok, tok, tok],
        out_shape=[jax.ShapeDtypeStruct((b, t, 256), MXU_DTYPE)] * 3,
        compiler_params=_cparams(("parallel", "parallel")),
        name="na_prep",
    )(uq, u, u, gq, gk, e_mat)


def _stack_heads(q):
    qf = q.astype(jnp.float32)
    head = lax.broadcasted_iota(jnp.int32, qf.shape, 1) // NA_HD
    return _bf(jnp.concatenate([jnp.where(head == h, qf, 0.0) for h in range(N_HEADS)], axis=0))


def _unstack_heads(res):
    n = res.shape[0] // N_HEADS
    head = lax.broadcasted_iota(jnp.int32, (n, res.shape[1]), 1) // NA_HD
    out = res[0:n]
    for h in range(1, N_HEADS):
        out = jnp.where(head == h, res[h * n:(h + 1) * n], out)
    return out


def _softmax_pv(scores, values):
    m = functools.reduce(jnp.maximum, [jnp.max(s, axis=-1, keepdims=True) for s in scores])
    ps = [jnp.exp(s - m) for s in scores]
    l = functools.reduce(jnp.add, [jnp.sum(p, axis=-1, keepdims=True) for p in ps])
    acc = functools.reduce(jnp.add, [_dot(_bf(p), v) for p, v in zip(ps, values)])
    return acc / l


def _na_kernel(q_ref, k_ref, v_ref, kc_ref, vc_ref, tbl_ref, o_ref, *, rows, rb):
    jb = pl.program_id(1)
    kc = kc_ref[0]
    vc = vc_ref[0]
    n_loc = NA_WIN_R * GRID_W
    for rr in range(rb):
        r = jb * rb + rr
        rs = jnp.clip(r - NA_WIN_R // 2, 0, rows - NA_WIN_R)
        start = pl.multiple_of(rs * GRID_W, GRID_W)
        qm = _stack_heads(q_ref[0, rr * GRID_W:(rr + 1) * GRID_W, :])
        s_loc = _dot_nt(qm, k_ref[0, pl.ds(start, n_loc), :]) + tbl_ref[r - rs]
        s_ctx = _dot_nt(qm, kc)
        res = _softmax_pv([s_loc, s_ctx], [v_ref[0, pl.ds(start, n_loc), :], vc])
        o_ref[0, rr * GRID_W:(rr + 1) * GRID_W, :] = _unstack_heads(res)


def _na_attention(q, k, v, kc, vc, tbl, rb):
    b, t, _ = q.shape
    tc = kc.shape[1]
    rows = t // GRID_W
    assert rows >= NA_WIN_R and rows % rb == 0
    seq = pl.BlockSpec((1, t, 256), lambda i, j: (i, 0, 0))
    cseq = pl.BlockSpec((1, tc, 256), lambda i, j: (i, 0, 0))
    return pl.pallas_call(
        functools.partial(_na_kernel, rows=rows, rb=rb),
        grid=(b, rows // rb),
        in_specs=[pl.BlockSpec((1, rb * GRID_W, 256), lambda i, j: (i, j, 0)), seq, seq, cseq, cseq,
                  _full(tbl, 2)],
        out_specs=pl.BlockSpec((1, rb * GRID_W, 256), lambda i, j: (i, j, 0)),
        out_shape=jax.ShapeDtypeStruct((b, t, 256), jnp.float32),
        compiler_params=_cparams(("parallel", "parallel")),
        name="na_attention",
    )(q, k, v, kc, vc, tbl)


def _ctx_na_kernel(q_ref, kc_ref, vc_ref, o_ref):
    kc = kc_ref[0]
    vc = vc_ref[0]
    for rr in range(q_ref.shape[1] // GRID_W):
        qm = _stack_heads(q_ref[0, rr * GRID_W:(rr + 1) * GRID_W, :])
        res = _softmax_pv([_dot_nt(qm, kc)], [vc])
        o_ref[0, rr * GRID_W:(rr + 1) * GRID_W, :] = _unstack_heads(res)


def _ctx_na_attention(q, kc, vc):
    b, tc, _ = q.shape
    cseq = pl.BlockSpec((1, tc, 256), lambda i: (i, 0, 0))
    return pl.pallas_call(
        _ctx_na_kernel,
        grid=(b,),
        in_specs=[cseq, cseq, cseq],
        out_specs=cseq,
        out_shape=jax.ShapeDtypeStruct((b, tc, 256), jnp.float32),
        compiler_params=_cparams(("parallel",)),
        name="ctx_na_attention",
    )(q, kc, vc)


def _hg_gates(z, lb):
    e = jnp.exp(-jnp.abs(z))
    r = 1.0 / (1.0 + e)
    pos = z >= 0
    sig = jnp.where(pos, r, e * r)
    sig_neg = jnp.where(pos, e * r, r)
    return jnp.log(lb + (1.0 - lb) * sig), (1.0 - lb) * sig_neg


def _hg_prep_kernel(zf_ref, zb_ref, i_ref, q_ref, lbf_ref, lbb_ref, tl_ref, tu_ref, ta_ref, sel_ref, e_ref,
                    qf_o, kf_o, qb_o, kb_o, i_o, df_o, db_o, oi_o, kf_pad, bf_pad, kb_pad, bb_pad, i_pad,
                    *, tm, qscale):
    pad = HG_SUB
    q = q_ref[0] * qscale
    xi = i_ref[0]
    i_o[0] = _bf(xi)
    logf_f, kg_f = _hg_gates(zf_ref[0], lbf_ref[...])
    logf_b, kg_b = _hg_gates(zb_ref[0], lbb_ref[...])
    b_f = _split3_dot_left(tl_ref[...], logf_f)
    b_b = _split3_dot_left(tu_ref[...], logf_b)
    tot_f = _split3_dot_left(ta_ref[...], logf_f)
    tot_b = _split3_dot_left(ta_ref[...], logf_b)
    qf_o[0] = _bf(q * jnp.exp(b_f))
    kf_o[0] = _bf(kg_f * jnp.exp(tot_f - b_f))
    qb_o[0] = _bf(q * jnp.exp(b_b))
    kb_o[0] = _bf(kg_b * jnp.exp(tot_b - b_b))
    df_o[0] = jnp.exp(_split3_dot_left(sel_ref[...], logf_f))
    db_o[0] = jnp.exp(_split3_dot_left(sel_ref[...], logf_b))

    zeros = jnp.zeros((pad, 256), jnp.float32)
    for ref, val in ((kf_pad, kg_f), (bf_pad, b_f), (kb_pad, kg_b), (bb_pad, b_b), (i_pad, xi)):
        ref[0:pad, :] = zeros
        ref[pad + tm:2 * pad + tm, :] = zeros
        ref[pad:pad + tm, :] = val
    e = e_ref[...]
    pos = lax.broadcasted_iota(jnp.int32, (tm, 256), 0) % HG_SUB
    acc = _dot(_bf(q * (kg_f + kg_b)), e) * xi
    for delta in range(1, HG_SUB):
        lo = pad - delta
        x = q * kf_pad[lo:lo + tm, :] * jnp.exp(jnp.minimum(b_f - bf_pad[lo:lo + tm, :], 0.0))
        x = jnp.where(pos >= delta, x, 0.0)
        acc = acc + _dot(_bf(x), e) * i_pad[lo:lo + tm, :]
        hi = pad + delta
        x = q * kb_pad[hi:hi + tm, :] * jnp.exp(jnp.minimum(b_b - bb_pad[hi:hi + tm, :], 0.0))
        x = jnp.where(pos + delta < HG_SUB, x, 0.0)
        acc = acc + _dot(_bf(x), e) * i_pad[hi:hi + tm, :]
    oi_o[0] = acc


def _hg_prep(uq, qcol, u, base, lbf, lbb, consts, e_mat):
    b, t, _ = u.shape
    tm = HG_TILE
    nsub = tm // HG_SUB
    tok = pl.BlockSpec((1, tm, 256), lambda i, j: (i, j, 0))
    dec = pl.BlockSpec((1, nsub, 256), lambda i, j: (i, j, 0))
    tl, tu, ta, sel = consts
    pad_shape = pltpu.VMEM((tm + 2 * HG_SUB, 256), jnp.float32)
    return pl.pallas_call(
        functools.partial(_hg_prep_kernel, tm=tm, qscale=HG_DK ** -0.5),
        grid=(b, t // tm),
        in_specs=[_ublk(tm, COL["zF"], base, 256), _ublk(tm, COL["zB"], base, 256),
                  _ublk(tm, COL["i_hg"], base, 256),
                  pl.BlockSpec((1, tm, 256), lambda i, j: (i, j, qcol)),
                  _full(lbf, 2), _full(lbb, 2), _full(tl, 2), _full(tu, 2), _full(ta, 2), _full(sel, 2),
                  _full(e_mat, 2)],
        out_specs=[tok, tok, tok, tok, tok, dec, dec, tok],
        out_shape=[jax.ShapeDtypeStruct((b, t, 256), MXU_DTYPE)] * 5
                  + [jax.ShapeDtypeStruct((b, t // HG_SUB, 256), jnp.float32)] * 2
                  + [jax.ShapeDtypeStruct((b, t, 256), jnp.float32)],
        scratch_shapes=[pad_shape] * 5,
        compiler_params=_cparams(("parallel", "parallel")),
        name="hg_prep",
    )(u, u, u, uq, lbf, lbb, tl, tu, ta, sel, e_mat)


def _hg_scan_kernel(qf_ref, kf_ref, if_ref, df_ref, qb_ref, kb_ref, ib_ref, db_ref, s0f_ref, s0b_ref, mask_ref,
                    of_ref, ob_ref, sf_out, sb_out, sf, sb, *, nsub):
    j = pl.program_id(1)

    @pl.when(j == 0)
    def _():
        sf[...] = s0f_ref[0]
        sb[...] = s0b_ref[0]

    mask = mask_ref[...]

    def step(state, q_ref, k_ref, i_ref, d_ref, o_ref, c):
        rows = slice(c * HG_SUB, (c + 1) * HG_SUB)
        s_old = state[...]
        o_ref[0, rows, :] = _dot_nt(q_ref[0, rows, :], _bf(s_old))
        upd = _dot_tn(i_ref[0, rows, :], k_ref[0, rows, :])
        state[...] = s_old * d_ref[0, c:c + 1, :] + upd * mask

    for c in range(nsub):
        step(sf, qf_ref, kf_ref, if_ref, df_ref, of_ref, c)
        step(sb, qb_ref, kb_ref, ib_ref, db_ref, ob_ref, nsub - 1 - c)

    @pl.when(j == pl.num_programs(1) - 1)
    def _():
        sf_out[0] = sf[...]
        sb_out[0] = sb[...]


def _hg_scan(qf, kf, qb, kb, xi, df, db, s0f, s0b, mask):
    b, t, _ = qf.shape
    tm = HG_TILE
    n = t // tm
    nsub = tm // HG_SUB
    fwd = pl.BlockSpec((1, tm, 256), lambda i, j: (i, j, 0))
    bwd = pl.BlockSpec((1, tm, 256), lambda i, j: (i, n - 1 - j, 0))
    dfwd = pl.BlockSpec((1, nsub, 256), lambda i, j: (i, j, 0))
    dbwd = pl.BlockSpec((1, nsub, 256), lambda i, j: (i, n - 1 - j, 0))
    st = pl.BlockSpec((1, 256, 256), lambda i, j: (i, 0, 0))
    return pl.pallas_call(
        functools.partial(_hg_scan_kernel, nsub=nsub),
        grid=(b, n),
        in_specs=[fwd, fwd, fwd, dfwd, bwd, bwd, bwd, dbwd, st, st, _full(mask, 2)],
        out_specs=[fwd, bwd, st, st],
        out_shape=[jax.ShapeDtypeStruct((b, t, 256), jnp.float32)] * 2
                  + [jax.ShapeDtypeStruct((b, 256, 256), jnp.float32)] * 2,
        scratch_shapes=[pltpu.VMEM((256, 256), jnp.float32)] * 2,
        compiler_params=_cparams(("parallel", "arbitrary")),
        name="hg_scan",
    )(qf, kf, xi, df, qb, kb, xi, db, s0f, s0b, mask)


def _merge_kernel(x_ref, a_ref, b_ref, c_ref, of_ref, ob_ref, oi_ref, gc_ref, g0_ref, g1_ref, g2_ref, g3_ref,
                  hgn_ref, e_ref, wbr_ref, wo_ref, gate_ref, out_ref):
    o = of_ref[0] + ob_ref[0] + oi_ref[0]
    ms = _split_dot(o * o, e_ref[...]) * (1.0 / HG_DK)
    d = o * lax.rsqrt(ms + EPS) * hgn_ref[...] * _sigmoid(gc_ref[0])
    acc = None
    for n, (y, g_ref) in enumerate(((a_ref[0], g0_ref), (b_ref[0], g1_ref), (c_ref[0], g2_ref), (d, g3_ref))):
        term = _sigmoid(g_ref[0]) * _dot(_bf(y), wbr_ref[n])
        acc = term if acc is None else acc + term
    out_ref[0] = x_ref[0] + gate_ref[0] * _dot(_bf(acc), wo_ref[...])


def _merge(x, a, bm, cn, o_f, o_b, o_i, u, base, hgn, e_mat, w_br, w_o, gate, tm):
    b, t, d = x.shape
    tok = pl.BlockSpec((1, tm, 256), lambda i, j: (i, j, 0))
    xs = pl.BlockSpec((1, tm, d), lambda i, j: (i, j, 0))
    return pl.pallas_call(
        _merge_kernel,
        grid=(b, t // tm),
        in_specs=[xs, tok, tok, tok, tok, tok, tok, _ublk(tm, COL["g_cols"], base, 256)]
                 + [_ublk(tm, COL["gates"] + n * d, base, d) for n in range(N_BRANCH)]
                 + [_full(hgn, 2), _full(e_mat, 2), _full(w_br, 2), _full(w_o, 2),
                    pl.BlockSpec((1, 1, d), lambda i, j: (i, 0, 0))],
        out_specs=xs,
        out_shape=jax.ShapeDtypeStruct((b, t, d), jnp.float32),
        compiler_params=_cparams(("parallel", "parallel")),
        name="merge",
    )(x, a, bm, cn, o_f, o_b, o_i, u, u, u, u, u, hgn, e_mat, w_br, w_o, gate)


def _ffn_kernel(x_ref, g_ref, sh_ref, sc_ref, w1_ref, w2_ref, gate_ref, o_ref, h_ref, acc_ref):
    k = pl.program_id(2)

    @pl.when(k == 0)
    def _():
        x = x_ref[0]
        y = x * _rms_scale(x) * g_ref[...]
        h_ref[...] = _bf(y * (1.0 + sc_ref[0]) + sh_ref[0])

    z = jnp.maximum(_dot(h_ref[...], w1_ref[...]), 0.0)
    part = _dot(_bf(z * z), w2_ref[...])

    @pl.when(k == 0)
    def _():
        acc_ref[...] = part

    @pl.when(k > 0)
    def _():
        acc_ref[...] += part

    @pl.when(k == pl.num_programs(2) - 1)
    def _():
        o_ref[0] = x_ref[0] + gate_ref[0] * acc_ref[...]


def _ffn(x, g, sh, sc, w1, w2, gate, tm, tf):
    b, t, d = x.shape
    f = w1.shape[1]
    xs = pl.BlockSpec((1, tm, d), lambda i, j, k: (i, j, 0))
    vec = pl.BlockSpec((1, 1, d), lambda i, j, k: (i, 0, 0))
    return pl.pallas_call(
        _ffn_kernel,
        grid=(b, t // tm, f // tf),
        in_specs=[xs, pl.BlockSpec((1, d), lambda i, j, k: (0, 0)), vec, vec,
                  pl.BlockSpec((d, tf), lambda i, j, k: (0, k)),
                  pl.BlockSpec((tf, d), lambda i, j, k: (k, 0)), vec],
        out_specs=xs,
        out_shape=jax.ShapeDtypeStruct((b, t, d), jnp.float32),
        scratch_shapes=[pltpu.VMEM((tm, d), MXU_DTYPE), pltpu.VMEM((tm, d), jnp.float32)],
        compiler_params=_cparams(("parallel", "parallel", "arbitrary")),
        name="ffn",
    )(x, g, sh, sc, w1, w2, gate)


def _rope_perm():
    return np.array([(t // 16) * 16 + ((t % 16) + 8) % 16 for t in range(MLA_ROPE)])


def _prep_w_in(w):
    d = w.shape[0]
    o = dict(ckv=0, kr=128, k_na=160, v_na=416, zF=672, zB=928, i_hg=1184, pool=1440, qa=1696, q_na=1952,
             q_hg=2208, g_cols=2464, gates=2720)
    z = lambda n: jnp.zeros((d, n), w.dtype)
    k_rope = w[:, o["kr"]:o["kr"] + MLA_ROPE]
    parts = [w[:, o["gates"]:o["gates"] + 4096]]
    for name in ("pool", "qa", "q_na", "q_hg", "g_cols", "k_na", "v_na", "zF", "zB", "i_hg"):
        parts.append(w[:, o[name]:o[name] + 256])
    parts += [w[:, 0:128], z(64), k_rope, z(32), z(64), k_rope[:, _rope_perm()], z(32), z(128)]
    return _bf(jnp.concatenate(parts, axis=1))


def _prep_wq(w):
    r = w.shape[0]
    wh = w.reshape(r, N_HEADS, MLA_QK)
    pre = jnp.pad(wh, ((0, 0), (0, 0), (0, HEAD_PAD - MLA_QK)))
    sw = jnp.pad(wh[:, :, MLA_NOPE:][:, :, _rope_perm()], ((0, 0), (0, 0), (MLA_NOPE, HEAD_PAD - MLA_QK)))
    return _bf(jnp.concatenate([pre.reshape(r, -1), sw.reshape(r, -1)], axis=1))


def _prep_wkv(w):
    r = w.shape[0]
    wh = w.reshape(r, N_HEADS, MLA_NOPE + MLA_V)
    kpart = jnp.pad(wh[:, :, :MLA_NOPE], ((0, 0), (0, 0), (0, HEAD_PAD - MLA_NOPE)))
    return _bf(jnp.concatenate([kpart.reshape(r, -1), wh[:, :, MLA_NOPE:].reshape(r, -1)], axis=1))


def _pad_gain(g):
    plain = jnp.pad(g, (0, HEAD_PAD - MLA_QK))[None]
    sw = jnp.pad(g[MLA_NOPE:][_rope_perm()], (MLA_NOPE, HEAD_PAD - MLA_QK))[None]
    return plain, sw


def _rope_tables(n_tok):
    t = jnp.arange(n_tok)
    pos = jnp.stack([t // GRID_W, t % GRID_W], axis=-1).astype(jnp.float32)
    nf = MLA_ROPE // 4
    inv = ROPE_BASE ** (-jnp.arange(nf, dtype=jnp.float32) / nf)
    ang = pos[:, :, None] * inv
    cos, sin = jnp.cos(ang), jnp.sin(ang)
    cos_t = jnp.concatenate([cos, cos], axis=-1).reshape(n_tok, MLA_ROPE)
    sin_t = jnp.concatenate([-sin, sin], axis=-1).reshape(n_tok, MLA_ROPE)
    ones = jnp.ones((n_tok, MLA_NOPE), jnp.float32)
    zer = jnp.zeros((n_tok, HEAD_PAD - MLA_QK), jnp.float32)
    cos_p = jnp.concatenate([ones, cos_t, zer], axis=-1)
    sin_p = jnp.concatenate([0 * ones, sin_t, zer], axis=-1)
    return cos_p, sin_p


def _plain_tables(n_tok):
    cos_p = jnp.concatenate([jnp.ones((n_tok, MLA_QK), jnp.float32),
                             jnp.zeros((n_tok, HEAD_PAD - MLA_QK), jnp.float32)], axis=-1)
    return cos_p, jnp.zeros((n_tok, HEAD_PAD), jnp.float32)


def _head_indicator(dtype):
    i = np.arange(256) // 64
    return jnp.asarray((i[:, None] == i[None, :]).astype(np.float32), dtype=dtype)


def _na_bias_table(rpb):
    p = np.arange(NA_WIN_R)[:, None]
    j = np.arange(NA_WIN_R)[None, :]
    dr = j + NA_WIN_R - 1 - p
    qc = np.arange(GRID_W)[:, None]
    kc = np.arange(GRID_W)[None, :]
    cs = np.clip(qc - NA_WIN_C // 2, 0, GRID_W - NA_WIN_C)
    inwin = (kc >= cs) & (kc < cs + NA_WIN_C)
    dc = np.clip(kc - qc + NA_WIN_C - 1, 0, 2 * NA_WIN_C - 2)
    b = rpb.astype(jnp.float32)[:, dr[:, :, None, None], dc[None, None, :, :]]
    b = jnp.where(inwin[None, None, None], b, NEG)
    return b.transpose(1, 0, 3, 2, 4).reshape(NA_WIN_R, N_HEADS * GRID_W, NA_WIN_R * GRID_W)


def _hg_consts():
    t = np.arange(HG_TILE)
    same = (t[:, None] // HG_SUB) == (t[None, :] // HG_SUB)
    tl = same & (t[None, :] <= t[:, None])
    tu = same & (t[None, :] >= t[:, None])
    sel = (np.arange(HG_TILE // HG_SUB)[:, None] == (t[None, :] // HG_SUB))
    f = lambda m: jnp.asarray(m.astype(np.float32), dtype=MXU_DTYPE)
    return f(tl), f(tu), f(same), f(sel)


def _block_diag(w):
    g, c, d = w.shape
    out = jnp.zeros((g * c, g * d), w.dtype)
    for i in range(g):
        out = out.at[i * c:(i + 1) * c, i * d:(i + 1) * d].set(w[i])
    return out


def kernel(x, c, ctx, c_ctx, ada_w, ada_b, norm1, norm2, w_in, pool_w, pool_scale, mla_q_norm, mla_wq_b, mla_kv_norm, mla_wkv_b, mla_gq, mla_gk, na_gq, na_gk, na_rpb, hg_lb, hg_norm, w_branch, w_out, w_ff1, w_ff2):
    B, L, D = x.shape
    Tc = ctx.shape[1]
    depth = ada_w.shape[0]
    f32 = jnp.float32
    assert L % HG_TILE == 0 and Tc % HG_TILE == 0 and B < 8

    lb_p = jax.nn.softmax(hg_lb.astype(f32), axis=0)
    lb_all = jnp.clip(jnp.cumsum(lb_p, axis=0)[:depth], LB_EPS, 1.0 - LB_EPS)

    cc = jnp.concatenate([c, c_ctx[None], jnp.zeros((8 - B - 1, D), f32)], axis=0)
    mods = _mods(cc, ada_w, ada_b)

    cos_l, sin_l = _rope_tables(L)
    cos_c, sin_c = _plain_tables(Tc)
    e_mat = _head_indicator(MXU_DTYPE)
    hmask = _head_indicator(f32)
    hg_consts = _hg_consts()
    zero_state = jnp.zeros((B, 256, 256), f32)
    zero_q = jnp.zeros((B, Tc, 256), f32)
    tl = min(512, L)

    xl, xc = x, ctx
    for l in range(depth):
        ctx_out = l < depth - 1
        ml = mods[l, :B].reshape(B, 1, 6, D)
        mc = jnp.broadcast_to(mods[l, B].reshape(1, 1, 6, D), (B, 1, 6, D))
        lmod = [ml[:, :, i] for i in range(6)]
        cmod = [mc[:, :, i] for i in range(6)]
        g1n = norm1[l][None]
        g2n = norm2[l][None]
        w_in_p = _prep_w_in(w_in[l])
        wq = _prep_wq(mla_wq_b[l])
        wkv = _prep_wkv(mla_wkv_b[l])
        gq, gqsw = _pad_gain(mla_gq[l])
        gk, gksw = _pad_gain(mla_gk[l])
        qn = mla_q_norm[l][None]
        kvn = mla_kv_norm[l][None]
        na_q_gain = (jnp.tile(na_gq[l], N_HEADS) * NA_HD ** -0.5)[None]
        na_k_gain = jnp.tile(na_gk[l], N_HEADS)[None]
        na_tbl = _na_bias_table(na_rpb[l])
        pool_bd = _bf(_block_diag(pool_w[l]))
        pool_sc = pool_scale[l][None]
        hgn = jnp.tile(hg_norm[l], N_HEADS)[None]
        lbF = lb_all[l, 0][None]
        lbB = lb_all[l, 1][None]
        w_br = _bf(w_branch[l])
        w_o = _bf(w_out[l])
        w1 = _bf(w_ff1[l])
        w2 = _bf(w_ff2[l])

        ul = _inproj(xl, g1n, lmod[0], lmod[1], w_in_p, min(1024, L))
        if ctx_out:
            uc, cbase = _inproj(xc, g1n, cmod[0], cmod[1], w_in_p, Tc), 0
        else:
            uc, cbase = _inproj(xc, g1n, cmod[0], cmod[1], w_in_p[:, KV_START:], Tc), KV_START

        q_mla, k_mla, v_mla = _mla_prep(ul, 0, cos_l, sin_l, qn, wq, kvn, wkv, gq, gqsw, gk, gksw, tl)
        if ctx_out:
            q_mla_c, k_mla_c, v_mla_c = _mla_prep(uc, cbase, cos_c, sin_c, qn, wq, kvn, wkv, gq, gqsw, gk, gksw, Tc)
        else:
            k_mla_c, v_mla_c = _mla_kv_prep(uc, cbase, kvn, wkv, gk, Tc)
        b_l = _attention(q_mla, jnp.concatenate([k_mla_c, k_mla], axis=2),
                         jnp.concatenate([v_mla_c, v_mla], axis=1), 256, 256)

        q_na, k_na, v_na = _na_prep(ul, COL["q_na"] // 256, ul, 0, na_q_gain, na_k_gain, e_mat, tl)
        if ctx_out:
            q_na_c, k_na_c, v_na_c = _na_prep(uc, COL["q_na"] // 256, uc, cbase, na_q_gain, na_k_gain, e_mat, Tc)
        else:
            _, k_na_c, v_na_c = _na_prep(zero_q, 0, uc, cbase, na_q_gain, na_k_gain, e_mat, Tc)
        c_l = _na_attention(q_na, k_na, v_na, k_na_c, v_na_c, na_tbl, 4)

        a_l = _pool(ul, 0, pool_bd, pool_sc, tl)

        if ctx_out:
            hc = _hg_prep(uc, COL["q_hg"] // 256, uc, cbase, lbF, lbB, hg_consts, e_mat)
        else:
            hc = _hg_prep(zero_q, 0, uc, cbase, lbF, lbB, hg_consts, e_mat)
        oFc, oBc, sFc, sBc = _hg_scan(hc[0], hc[1], hc[2], hc[3], hc[4], hc[5], hc[6], zero_state, zero_state, hmask)
        hl = _hg_prep(ul, COL["q_hg"] // 256, ul, 0, lbF, lbB, hg_consts, e_mat)
        oFl, oBl, _, _ = _hg_scan(hl[0], hl[1], hl[2], hl[3], hl[4], hl[5], hl[6], sFc, sBc, hmask)

        xl_new = _merge(xl, a_l, b_l, c_l, oFl, oBl, hl[7], ul, 0, hgn, e_mat, w_br, w_o, lmod[2], 256)

        if ctx_out:
            a_c = _pool(uc, cbase, pool_bd, pool_sc, Tc)
            b_c = _attention(q_mla_c, k_mla_c, v_mla_c, Tc, Tc)
            c_c = _ctx_na_attention(q_na_c, k_na_c, v_na_c)
            xc = _merge(xc, a_c, b_c, c_c, oFc, oBc, hc[7], uc, cbase, hgn, e_mat, w_br, w_o, cmod[2], Tc)
            xc = _ffn(xc, g2n, cmod[3], cmod[4], w1, w2, cmod[5], Tc, 1024)

        xl = _ffn(xl_new, g2n, lmod[3], lmod[4], w1, w2, lmod[5], tl, 1024)
    return xl
```

```python
import functools

import jax
import jax.numpy as jnp
import numpy as np
from jax import lax
from jax.experimental import pallas as pl
from jax.experimental.pallas import tpu as pltpu

D_MODEL = 1024
GRID_W = 64
EPS = 1e-6
N_BRANCH = 4
BRANCH_W = 256
POOL_GC = 64
N_HEADS = 4
MLA_NOPE = 64
MLA_ROPE = 32
MLA_V = 64
MLA_QK = 96
ROPE_BASE = 10000.0
NA_HD = 64
NA_WIN_R = 8
NA_WIN_C = 16
HG_DK = 64
LB_EPS = 1e-6

MXU_DTYPE = jnp.bfloat16
LANE = 128
HEAD_PAD = 128
VMEM_LIMIT = 48 * 1024 * 1024
HG_SUB = 16
HG_TILE = 256
NEG = -1e30

COL = dict(gates=0, pool=4096, qa=4352, q_na=4608, q_hg=4864, g_cols=5120,
           k_na=5376, v_na=5632, zF=5888, zB=6144, i_hg=6400, ckv=6656, kr=6784, kr_sw=6912)
D_IN_PAD = 7168
KV_START = 5376
N_TILE = 1792


def _cparams(sem):
    return pltpu.CompilerParams(dimension_semantics=sem, vmem_limit_bytes=VMEM_LIMIT)


def _bf(x):
    return x.astype(MXU_DTYPE)


def _dot(a, b):
    return jnp.dot(a, b, preferred_element_type=jnp.float32)


def _dot_nt(a, b):
    return lax.dot_general(a, b, (((1,), (1,)), ((), ())), preferred_element_type=jnp.float32)


def _dot_tn(a, b):
    return lax.dot_general(a, b, (((0,), (0,)), ((), ())), preferred_element_type=jnp.float32)


def _split2(x):
    hi = _bf(x)
    return hi, _bf(x - hi.astype(jnp.float32))


def _split_dot(x, w):
    hi, lo = _split2(x)
    return _dot(hi, w) + _dot(lo, w)


def _split3_dot_left(w, x):
    hi = _bf(x)
    r1 = x - hi.astype(jnp.float32)
    mid = _bf(r1)
    lo = _bf(r1 - mid.astype(jnp.float32))
    return _dot(w, hi) + _dot(w, mid) + _dot(w, lo)


def _sigmoid(x):
    return 1.0 / (1.0 + jnp.exp(-x))


def _rms_scale(x):
    return lax.rsqrt(jnp.mean(x * x, axis=-1, keepdims=True) + EPS)


def _full(arr, nargs):
    zeros = (0,) * arr.ndim
    return pl.BlockSpec(arr.shape, lambda *_: zeros)


def _ublk(tm, col, base, width):
    idx = (col - base) // width
    return pl.BlockSpec((1, tm, width), lambda i, j: (i, j, idx))


def _mods_kernel(c_ref, w_ref, b_ref, o_ref):
    c = c_ref[...]
    s = _bf(c * _sigmoid(c))
    o_ref[0] = _dot(s, _bf(w_ref[0])) + b_ref[0]


def _mods(cc, ada_w, ada_b):
    depth, d, n = ada_w.shape
    tn = 1536
    return pl.pallas_call(
        _mods_kernel,
        grid=(depth, n // tn),
        in_specs=[pl.BlockSpec((8, d), lambda l, j: (0, 0)),
                  pl.BlockSpec((1, d, tn), lambda l, j: (l, 0, j)),
                  pl.BlockSpec((1, 1, tn), lambda l, j: (l, 0, j))],
        out_specs=pl.BlockSpec((1, 8, tn), lambda l, j: (l, 0, j)),
        out_shape=jax.ShapeDtypeStruct((depth, 8, n), jnp.float32),
        compiler_params=_cparams(("arbitrary", "arbitrary")),
        name="mods",
    )(cc, ada_w, ada_b.reshape(depth, 1, n))


def _inproj_kernel(x_ref, g_ref, sh_ref, sc_ref, w_ref, o_ref, h_ref):
    @pl.when(pl.program_id(2) == 0)
    def _():
        x = x_ref[0]
        y = x * _rms_scale(x) * g_ref[...]
        h_ref[...] = _bf(y * (1.0 + sc_ref[0]) + sh_ref[0])

    o_ref[0] = _dot(h_ref[...], w_ref[...])


def _inproj(x, g, sh, sc, w, tm):
    b, t, d = x.shape
    n = w.shape[1]
    return pl.pallas_call(
        _inproj_kernel,
        grid=(b, t // tm, n // N_TILE),
        in_specs=[pl.BlockSpec((1, tm, d), lambda i, j, k: (i, j, 0)),
                  pl.BlockSpec((1, d), lambda i, j, k: (0, 0)),
                  pl.BlockSpec((1, 1, d), lambda i, j, k: (i, 0, 0)),
                  pl.BlockSpec((1, 1, d), lambda i, j, k: (i, 0, 0)),
                  pl.BlockSpec((d, N_TILE), lambda i, j, k: (0, k))],
        out_specs=pl.BlockSpec((1, tm, N_TILE), lambda i, j, k: (i, j, k)),
        out_shape=jax.ShapeDtypeStruct((b, t, n), jnp.float32),
        scratch_shapes=[pltpu.VMEM((tm, d), MXU_DTYPE)],
        compiler_params=_cparams(("parallel", "parallel", "arbitrary")),
        name="inproj",
    )(x, g, sh, sc, w)


def _mla_prep_kernel(qa_ref, ckv_ref, kr_ref, krsw_ref, cos_ref, sin_ref, qn_ref, wq_ref, kvn_ref, wkv_ref,
                     gq_ref, gqsw_ref, gk_ref, gksw_ref, q_ref, k_ref, vt_ref, *, scale):
    cos = cos_ref[...]
    sin = sin_ref[...]
    inv_w = 1.0 / MLA_QK
    qa = qa_ref[0]
    qq = _dot(_bf(qa * _rms_scale(qa) * qn_ref[...]), wq_ref[...])
    ckv = ckv_ref[0]
    kv = _dot(_bf(ckv * _rms_scale(ckv) * kvn_ref[...]), wkv_ref[...])
    kr = kr_ref[0]
    kr_rot = krsw_ref[0] * gksw_ref[...] * sin
    gq_cos = gq_ref[...] * cos
    gq_sin = gqsw_ref[...] * sin
    gk_cos = gk_ref[...] * cos
    for h in range(N_HEADS):
        qp = qq[:, h * HEAD_PAD:(h + 1) * HEAD_PAD]
        qs = qq[:, (N_HEADS + h) * HEAD_PAD:(N_HEADS + h + 1) * HEAD_PAD]
        rs = lax.rsqrt(jnp.sum(qp * qp, axis=-1, keepdims=True) * inv_w + EPS)
        q_ref[0, h] = _bf((qp * gq_cos + qs * gq_sin) * (rs * scale))
        kp = kv[:, h * HEAD_PAD:(h + 1) * HEAD_PAD] + kr
        rs = lax.rsqrt(jnp.sum(kp * kp, axis=-1, keepdims=True) * inv_w + EPS)
        k_ref[0, h] = _bf((kp * gk_cos + kr_rot) * rs)
    vt_ref[0] = _bf(kv[:, N_HEADS * HEAD_PAD:].T)


def _mla_prep(u, base, cos, sin, qn, wq, kvn, wkv, gq, gqsw, gk, gksw, tm):
    b, t, _ = u.shape
    tab = pl.BlockSpec((tm, HEAD_PAD), lambda i, j: (j, 0))
    hd = pl.BlockSpec((1, N_HEADS, tm, HEAD_PAD), lambda i, j: (i, 0, j, 0))
    consts = (qn, wq, kvn, wkv, gq, gqsw, gk, gksw)
    return pl.pallas_call(
        functools.partial(_mla_prep_kernel, scale=MLA_QK ** -0.5),
        grid=(b, t // tm),
        in_specs=[_ublk(tm, COL["qa"], base, 256), _ublk(tm, COL["ckv"], base, 128),
                  _ublk(tm, COL["kr"], base, 128), _ublk(tm, COL["kr_sw"], base, 128), tab, tab]
                 + [_full(a, 2) for a in consts],
        out_specs=[hd, hd, pl.BlockSpec((1, 256, tm), lambda i, j: (i, 0, j))],
        out_shape=[jax.ShapeDtypeStruct((b, N_HEADS, t, HEAD_PAD), MXU_DTYPE),
                   jax.ShapeDtypeStruct((b, N_HEADS, t, HEAD_PAD), MXU_DTYPE),
                   jax.ShapeDtypeStruct((b, 256, t), MXU_DTYPE)],
        compiler_params=_cparams(("parallel", "parallel")),
        name="mla_prep",
    )(u, u, u, u, cos, sin, *consts)


def _mla_kv_prep_kernel(ckv_ref, kr_ref, kvn_ref, wkv_ref, gk_ref, k_ref, vt_ref):
    inv_w = 1.0 / MLA_QK
    ckv = ckv_ref[0]
    kv = _dot(_bf(ckv * _rms_scale(ckv) * kvn_ref[...]), wkv_ref[...])
    kr = kr_ref[0]
    for h in range(N_HEADS):
        kp = kv[:, h * HEAD_PAD:(h + 1) * HEAD_PAD] + kr
        rs = lax.rsqrt(jnp.sum(kp * kp, axis=-1, keepdims=True) * inv_w + EPS)
        k_ref[0, h] = _bf(kp * gk_ref[...] * rs)
    vt_ref[0] = _bf(kv[:, N_HEADS * HEAD_PAD:].T)


def _mla_kv_prep(u, base, kvn, wkv, gk, tm):
    b, t, _ = u.shape
    hd = pl.BlockSpec((1, N_HEADS, tm, HEAD_PAD), lambda i, j: (i, 0, j, 0))
    return pl.pallas_call(
        _mla_kv_prep_kernel,
        grid=(b, t // tm),
        in_specs=[_ublk(tm, COL["ckv"], base, 128), _ublk(tm, COL["kr"], base, 128),
                  _full(kvn, 2), _full(wkv, 2), _full(gk, 2)],
        out_specs=[hd, pl.BlockSpec((1, 256, tm), lambda i, j: (i, 0, j))],
        out_shape=[jax.ShapeDtypeStruct((b, N_HEADS, t, HEAD_PAD), MXU_DTYPE),
                   jax.ShapeDtypeStruct((b, 256, t), MXU_DTYPE)],
        compiler_params=_cparams(("parallel", "parallel")),
        name="mla_kv_prep",
    )(u, u, kvn, wkv, gk)


def _attn_kernel(q_ref, k_ref, vt_ref, o_ref, *, tk, n_chunks):
    tq = q_ref.shape[2]
    qs = [q_ref[0, h] for h in range(N_HEADS)]

    def body(c, carry):
        start = pl.multiple_of(c * tk, tk)
        scores = [_dot_nt(k_ref[0, h, pl.ds(start, tk), :], qs[h]) for h in range(N_HEADS)]
        new = []
        for h in range(N_HEADS):
            m, l, acc = carry[h]
            s = scores[h]
            m_new = jnp.maximum(m, jnp.max(s, axis=0, keepdims=True))
            alpha = jnp.exp(m - m_new)
            p = jnp.exp(s - m_new)
            l = alpha * l + jnp.sum(p, axis=0, keepdims=True)
            vt = vt_ref[0, h * MLA_V:(h + 1) * MLA_V, pl.ds(start, tk)]
            new.append((m_new, l, alpha * acc + _dot(vt, _bf(p))))
        return tuple(new)

    init = tuple((jnp.full((1, tq), -jnp.inf, jnp.float32), jnp.zeros((1, tq), jnp.float32),
                  jnp.zeros((MLA_V, tq), jnp.float32)) for _ in range(N_HEADS))
    res = lax.fori_loop(0, n_chunks, body, init)
    o_t = jnp.concatenate([acc / l for _, l, acc in res], axis=0)
    o_ref[0] = o_t.T


def _attention(q, k, vt, tq, tk):
    b, _, t_q, _ = q.shape
    t_k = k.shape[2]
    assert t_q % tq == 0 and t_k % tk == 0
    return pl.pallas_call(
        functools.partial(_attn_kernel, tk=tk, n_chunks=t_k // tk),
        grid=(b, t_q // tq),
        in_specs=[pl.BlockSpec((1, N_HEADS, tq, HEAD_PAD), lambda i, j: (i, 0, j, 0)),
                  pl.BlockSpec((1, N_HEADS, t_k, HEAD_PAD), lambda i, j: (i, 0, 0, 0)),
                  pl.BlockSpec((1, 256, t_k), lambda i, j: (i, 0, 0))],
        out_specs=pl.BlockSpec((1, tq, 256), lambda i, j: (i, j, 0)),
        out_shape=jax.ShapeDtypeStruct((b, t_q, 256), jnp.float32),
        compiler_params=_cparams(("parallel", "parallel")),
        name="attention",
    )(q, k, vt)


def _pool_kernel(prev_ref, cur_ref, next_ref, w_ref, sc_ref, o_ref, ext_ref, *, tm, n_tok):
    j = pl.program_id(1)
    cur = cur_ref[0]
    ext_ref[8:8 + tm, :] = cur
    ext_ref[0:8, :] = jnp.where(j > 0, prev_ref[0], 0.0)
    ext_ref[8 + tm:16 + tm, :] = jnp.where(j < pl.num_programs(1) - 1, next_ref[0], 0.0)

    def window_sums(c0, widths):
        sh = lambda k: ext_ref[8 + k:8 + k + tm, c0:c0 + LANE]
        acc = sh(-1) + sh(0)
        out = {2: acc}
        for w in (4, 8, 16):
            if w > max(widths):
                break
            for k in list(range(-w // 2, -w // 4)) + list(range(w // 4, w // 2)):
                acc = acc + sh(k)
            out[w] = acc
        return [out[w] for w in widths]

    lane = lax.broadcasted_iota(jnp.int32, (tm, LANE), 1)
    t = j * tm + lax.broadcasted_iota(jnp.int32, (tm, LANE), 0)
    halves = []
    for c0, (wa, wb) in ((0, (2, 4)), (LANE, (8, 16))):
        sa, sb = window_sums(c0, (wa, wb))
        first = lane < POOL_GC
        s = jnp.where(first, sa, sb)
        half = jnp.where(first, wa // 2, wb // 2)
        cnt = jnp.minimum(t + half, n_tok) - jnp.maximum(t - half, 0)
        halves.append(s / cnt.astype(jnp.float32) - cur[:, c0:c0 + LANE])
    p = jnp.concatenate(halves, axis=-1)
    o_ref[0] = _dot(_bf(p), w_ref[...]) * sc_ref[...]


def _pool(u, base, w_bd, scale, tm):
    b, t, _ = u.shape
    cidx = (COL["pool"] - base) // 256
    rb = tm // 8
    last = t // 8 - 1
    return pl.pallas_call(
        functools.partial(_pool_kernel, tm=tm, n_tok=t),
        grid=(b, t // tm),
        in_specs=[pl.BlockSpec((1, 8, 256), lambda i, j: (i, jnp.maximum(j * rb - 1, 0), cidx)),
                  pl.BlockSpec((1, tm, 256), lambda i, j: (i, j, cidx)),
                  pl.BlockSpec((1, 8, 256), lambda i, j: (i, jnp.minimum((j + 1) * rb, last), cidx)),
                  _full(w_bd, 2), _full(scale, 2)],
        out_specs=pl.BlockSpec((1, tm, 256), lambda i, j: (i, j, 0)),
        out_shape=jax.ShapeDtypeStruct((b, t, 256), jnp.float32),
        scratch_shapes=[pltpu.VMEM((tm + 16, 256), jnp.float32)],
        compiler_params=_cparams(("parallel", "parallel")),
        name="pool",
    )(u, u, u, w_bd, scale)


def _head_rms(x, e):
    return lax.rsqrt(_split_dot(x * x, e) * (1.0 / NA_HD) + EPS)


def _na_prep_kernel(q_ref, k_ref, v_ref, gq_ref, gk_ref, e_ref, qo_ref, ko_ref, vo_ref):
    e = e_ref[...]
    q = q_ref[0]
    qo_ref[0] = _bf(q * _head_rms(q, e) * gq_ref[...])
    k = k_ref[0]
    ko_ref[0] = _bf(k * _head_rms(k, e) * gk_ref[...])
    vo_ref[0] = _bf(v_ref[0])


def _na_prep(uq, qcol, u, base, gq, gk, e_mat, tm):
    b, t, _ = u.shape
    tok = pl.BlockSpec((1, tm, 256), lambda i, j: (i, j, 0))
    return pl.pallas_call(
        _na_prep_kernel,
        grid=(b, t // tm),
        in_specs=[pl.BlockSpec((1, tm, 256), lambda i, j: (i, j, qcol)),
                  _ublk(tm, COL["k_na"], base, 256), _ublk(tm, COL["v_na"], base, 256),
                  _full(gq, 2), _full(gk, 2), _full(e_mat, 2)],
        out_specs=[tok, tok, tok],
        out_shape=[jax.ShapeDtypeStruct((b, t, 256), MXU_DTYPE)] * 3,
        compiler_params=_cparams(("parallel", "parallel")),
        name="na_prep",
    )(uq, u, u, gq, gk, e_mat)


def _stack_heads(q):
    qf = q.astype(jnp.float32)
    head = lax.broadcasted_iota(jnp.int32, qf.shape, 1) // NA_HD
    return _bf(jnp.concatenate([jnp.where(head == h, qf, 0.0) for h in range(N_HEADS)], axis=0))


def _unstack_heads(res):
    n = res.shape[0] // N_HEADS
    head = lax.broadcasted_iota(jnp.int32, (n, res.shape[1]), 1) // NA_HD
    out = res[0:n]
    for h in range(1, N_HEADS):
        out = jnp.where(head == h, res[h * n:(h + 1) * n], out)
    return out


def _softmax_pv(scores, values):
    m = functools.reduce(jnp.maximum, [jnp.max(s, axis=-1, keepdims=True) for s in scores])
    ps = [jnp.exp(s - m) for s in scores]
    l = functools.reduce(jnp.add, [jnp.sum(p, axis=-1, keepdims=True) for p in ps])
    acc = functools.reduce(jnp.add, [_dot(_bf(p), v) for p, v in zip(ps, values)])
    return acc / l


def _na_kernel(q_ref, k_ref, v_ref, kc_ref, vc_ref, tbl_ref, o_ref, *, rows, rb):
    jb = pl.program_id(1)
    kc = kc_ref[0]
    vc = vc_ref[0]
    n_loc = NA_WIN_R * GRID_W
    for rr in range(rb):
        r = jb * rb + rr
        rs = jnp.clip(r - NA_WIN_R // 2, 0, rows - NA_WIN_R)
        start = pl.multiple_of(rs * GRID_W, GRID_W)
        qm = _stack_heads(q_ref[0, rr * GRID_W:(rr + 1) * GRID_W, :])
        s_loc = _dot_nt(qm, k_ref[0, pl.ds(start, n_loc), :]) + tbl_ref[r - rs]
        s_ctx = _dot_nt(qm, kc)
        res = _softmax_pv([s_loc, s_ctx], [v_ref[0, pl.ds(start, n_loc), :], vc])
        o_ref[0, rr * GRID_W:(rr + 1) * GRID_W, :] = _unstack_heads(res)


def _na_attention(q, k, v, kc, vc, tbl, rb):
    b, t, _ = q.shape
    tc = kc.shape[1]
    rows = t // GRID_W
    assert rows >= NA_WIN_R and rows % rb == 0
    seq = pl.BlockSpec((1, t, 256), lambda i, j: (i, 0, 0))
    cseq = pl.BlockSpec((1, tc, 256), lambda i, j: (i, 0, 0))
    return pl.pallas_call(
        functools.partial(_na_kernel, rows=rows, rb=rb),
        grid=(b, rows // rb),
        in_specs=[pl.BlockSpec((1, rb * GRID_W, 256), lambda i, j: (i, j, 0)), seq, seq, cseq, cseq,
                  _full(tbl, 2)],
        out_specs=pl.BlockSpec((1, rb * GRID_W, 256), lambda i, j: (i, j, 0)),
        out_shape=jax.ShapeDtypeStruct((b, t, 256), jnp.float32),
        compiler_params=_cparams(("parallel", "parallel")),
        name="na_attention",
    )(q, k, v, kc, vc, tbl)


def _ctx_na_kernel(q_ref, kc_ref, vc_ref, o_ref):
    kc = kc_ref[0]
    vc = vc_ref[0]
    for rr in range(q_ref.shape[1] // GRID_W):
        qm = _stack_heads(q_ref[0, rr * GRID_W:(rr + 1) * GRID_W, :])
        res = _softmax_pv([_dot_nt(qm, kc)], [vc])
        o_ref[0, rr * GRID_W:(rr + 1) * GRID_W, :] = _unstack_heads(res)


def _ctx_na_attention(q, kc, vc):
    b, tc, _ = q.shape
    cseq = pl.BlockSpec((1, tc, 256), lambda i: (i, 0, 0))
    return pl.pallas_call(
        _ctx_na_kernel,
        grid=(b,),
        in_specs=[cseq, cseq, cseq],
        out_specs=cseq,
        out_shape=jax.ShapeDtypeStruct((b, tc, 256), jnp.float32),
        compiler_params=_cparams(("parallel",)),
        name="ctx_na_attention",
    )(q, kc, vc)


def _hg_gates(z, lb):
    e = jnp.exp(-jnp.abs(z))
    r = 1.0 / (1.0 + e)
    pos = z >= 0
    sig = jnp.where(pos, r, e * r)
    sig_neg = jnp.where(pos, e * r, r)
    return jnp.log(lb + (1.0 - lb) * sig), (1.0 - lb) * sig_neg


def _hg_prep_kernel(zf_ref, zb_ref, i_ref, q_ref, lbf_ref, lbb_ref, tl_ref, tu_ref, ta_ref, sel_ref, e_ref,
                    qf_o, kf_o, qb_o, kb_o, i_o, df_o, db_o, oi_o, kf_pad, bf_pad, kb_pad, bb_pad, i_pad,
                    *, tm, qscale):
    pad = HG_SUB
    q = q_ref[0] * qscale
    xi = i_ref[0]
    i_o[0] = _bf(xi)
    logf_f, kg_f = _hg_gates(zf_ref[0], lbf_ref[...])
    logf_b, kg_b = _hg_gates(zb_ref[0], lbb_ref[...])
    b_f = _split3_dot_left(tl_ref[...], logf_f)
    b_b = _split3_dot_left(tu_ref[...], logf_b)
    tot_f = _split3_dot_left(ta_ref[...], logf_f)
    tot_b = _split3_dot_left(ta_ref[...], logf_b)
    qf_o[0] = _bf(q * jnp.exp(b_f))
    kf_o[0] = _bf(kg_f * jnp.exp(tot_f - b_f))
    qb_o[0] = _bf(q * jnp.exp(b_b))
    kb_o[0] = _bf(kg_b * jnp.exp(tot_b - b_b))
    df_o[0] = jnp.exp(_split3_dot_left(sel_ref[...], logf_f))
    db_o[0] = jnp.exp(_split3_dot_left(sel_ref[...], logf_b))

    zeros = jnp.zeros((pad, 256), jnp.float32)
    for ref, val in ((kf_pad, kg_f), (bf_pad, b_f), (kb_pad, kg_b), (bb_pad, b_b), (i_pad, xi)):
        ref[0:pad, :] = zeros
        ref[pad + tm:2 * pad + tm, :] = zeros
        ref[pad:pad + tm, :] = val
    e = e_ref[...]
    pos = lax.broadcasted_iota(jnp.int32, (tm, 256), 0) % HG_SUB
    acc = _dot(_bf(q * (kg_f + kg_b)), e) * xi
    for delta in range(1, HG_SUB):
        lo = pad - delta
        x = q * kf_pad[lo:lo + tm, :] * jnp.exp(jnp.minimum(b_f - bf_pad[lo:lo + tm, :], 0.0))
        x = jnp.where(pos >= delta, x, 0.0)
        acc = acc + _dot(_bf(x), e) * i_pad[lo:lo + tm, :]
        hi = pad + delta
        x = q * kb_pad[hi:hi + tm, :] * jnp.exp(jnp.minimum(b_b - bb_pad[hi:hi + tm, :], 0.0))
        x = jnp.where(pos + delta < HG_SUB, x, 0.0)
        acc = acc + _dot(_bf(x), e) * i_pad[hi:hi + tm, :]
    oi_o[0] = acc


def _hg_prep(uq, qcol, u, base, lbf, lbb, consts, e_mat):
    b, t, _ = u.shape
    tm = HG_TILE
    nsub = tm // HG_SUB
    tok = pl.BlockSpec((1, tm, 256), lambda i, j: (i, j, 0))
    dec = pl.BlockSpec((1, nsub, 256), lambda i, j: (i, j, 0))
    tl, tu, ta, sel = consts
    pad_shape = pltpu.VMEM((tm + 2 * HG_SUB, 256), jnp.float32)
    return pl.pallas_call(
        functools.partial(_hg_prep_kernel, tm=tm, qscale=HG_DK ** -0.5),
        grid=(b, t // tm),
        in_specs=[_ublk(tm, COL["zF"], base, 256), _ublk(tm, COL["zB"], base, 256),
                  _ublk(tm, COL["i_hg"], base, 256),
                  pl.BlockSpec((1, tm, 256), lambda i, j: (i, j, qcol)),
                  _full(lbf, 2), _full(lbb, 2), _full(tl, 2), _full(tu, 2), _full(ta, 2), _full(sel, 2),
                  _full(e_mat, 2)],
        out_specs=[tok, tok, tok, tok, tok, dec, dec, tok],
        out_shape=[jax.ShapeDtypeStruct((b, t, 256), MXU_DTYPE)] * 5
                  + [jax.ShapeDtypeStruct((b, t // HG_SUB, 256), jnp.float32)] * 2
                  + [jax.ShapeDtypeStruct((b, t, 256), jnp.float32)],
        scratch_shapes=[pad_shape] * 5,
        compiler_params=_cparams(("parallel", "parallel")),
        name="hg_prep",
    )(u, u, u, uq, lbf, lbb, tl, tu, ta, sel, e_mat)


def _hg_scan_kernel(qf_ref, kf_ref, if_ref, df_ref, qb_ref, kb_ref, ib_ref, db_ref, s0f_ref, s0b_ref, mask_ref,
                    of_ref, ob_ref, sf_out, sb_out, sf, sb, *, nsub):
    j = pl.program_id(1)

    @pl.when(j == 0)
    def _():
        sf[...] = s0f_ref[0]
        sb[...] = s0b_ref[0]

    mask = mask_ref[...]

    def step(state, q_ref, k_ref, i_ref, d_ref, o_ref, c):
        rows = slice(c * HG_SUB, (c + 1) * HG_SUB)
        s_old = state[...]
        o_ref[0, rows, :] = _dot_nt(q_ref[0, rows, :], _bf(s_old))
        upd = _dot_tn(i_ref[0, rows, :], k_ref[0, rows, :])
        state[...] = s_old * d_ref[0, c:c + 1, :] + upd * mask

    for c in range(nsub):
        step(sf, qf_ref, kf_ref, if_ref, df_ref, of_ref, c)
        step(sb, qb_ref, kb_ref, ib_ref, db_ref, ob_ref, nsub - 1 - c)

    @pl.when(j == pl.num_programs(1) - 1)
    def _():
        sf_out[0] = sf[...]
        sb_out[0] = sb[...]


def _hg_scan(qf, kf, qb, kb, xi, df, db, s0f, s0b, mask):
    b, t, _ = qf.shape
    tm = HG_TILE
    n = t // tm
    nsub = tm // HG_SUB
    fwd = pl.BlockSpec((1, tm, 256), lambda i, j: (i, j, 0))
    bwd = pl.BlockSpec((1, tm, 256), lambda i, j: (i, n - 1 - j, 0))
    dfwd = pl.BlockSpec((1, nsub, 256), lambda i, j: (i, j, 0))
    dbwd = pl.BlockSpec((1, nsub, 256), lambda i, j: (i, n - 1 - j, 0))
    st = pl.BlockSpec((1, 256, 256), lambda i, j: (i, 0, 0))
    return pl.pallas_call(
        functools.partial(_hg_scan_kernel, nsub=nsub),
        grid=(b, n),
        in_specs=[fwd, fwd, fwd, dfwd, bwd, bwd, bwd, dbwd, st, st, _full(mask, 2)],
        out_specs=[fwd, bwd, st, st],
        out_shape=[jax.ShapeDtypeStruct((b, t, 256), jnp.float32)] * 2
                  + [jax.ShapeDtypeStruct((b, 256, 256), jnp.float32)] * 2,
        scratch_shapes=[pltpu.VMEM((256, 256), jnp.float32)] * 2,
        compiler_params=_cparams(("parallel", "arbitrary")),
        name="hg_scan",
    )(qf, kf, xi, df, qb, kb, xi, db, s0f, s0b, mask)


def _merge_kernel(x_ref, a_ref, b_ref, c_ref, of_ref, ob_ref, oi_ref, gc_ref, g0_ref, g1_ref, g2_ref, g3_ref,
                  hgn_ref, e_ref, wbr_ref, wo_ref, gate_ref, out_ref):
    o = of_ref[0] + ob_ref[0] + oi_ref[0]
    ms = _split_dot(o * o, e_ref[...]) * (1.0 / HG_DK)
    d = o * lax.rsqrt(ms + EPS) * hgn_ref[...] * _sigmoid(gc_ref[0])
    acc = None
    for n, (y, g_ref) in enumerate(((a_ref[0], g0_ref), (b_ref[0], g1_ref), (c_ref[0], g2_ref), (d, g3_ref))):
        term = _sigmoid(g_ref[0]) * _dot(_bf(y), wbr_ref[n])
        acc = term if acc is None else acc + term
    out_ref[0] = x_ref[0] + gate_ref[0] * _dot(_bf(acc), wo_ref[...])


def _merge(x, a, bm, cn, o_f, o_b, o_i, u, base, hgn, e_mat, w_br, w_o, gate, tm):
    b, t, d = x.shape
    tok = pl.BlockSpec((1, tm, 256), lambda i, j: (i, j, 0))
    xs = pl.BlockSpec((1, tm, d), lambda i, j: (i, j, 0))
    return pl.pallas_call(
        _merge_kernel,
        grid=(b, t // tm),
        in_specs=[xs, tok, tok, tok, tok, tok, tok, _ublk(tm, COL["g_cols"], base, 256)]
                 + [_ublk(tm, COL["gates"] + n * d, base, d) for n in range(N_BRANCH)]
                 + [_full(hgn, 2), _full(e_mat, 2), _full(w_br, 2), _full(w_o, 2),
                    pl.BlockSpec((1, 1, d), lambda i, j: (i, 0, 0))],
        out_specs=xs,
        out_shape=jax.ShapeDtypeStruct((b, t, d), jnp.float32),
        compiler_params=_cparams(("parallel", "parallel")),
        name="merge",
    )(x, a, bm, cn, o_f, o_b, o_i, u, u, u, u, u, hgn, e_mat, w_br, w_o, gate)


def _ffn_kernel(x_ref, g_ref, sh_ref, sc_ref, w1_ref, w2_ref, gate_ref, o_ref, h_ref, acc_ref):
    k = pl.program_id(2)

    @pl.when(k == 0)
    def _():
        x = x_ref[0]
        y = x * _rms_scale(x) * g_ref[...]
        h_ref[...] = _bf(y * (1.0 + sc_ref[0]) + sh_ref[0])

    z = jnp.maximum(_dot(h_ref[...], w1_ref[...]), 0.0)
    part = _dot(_bf(z * z), w2_ref[...])

    @pl.when(k == 0)
    def _():
        acc_ref[...] = part

    @pl.when(k > 0)
    def _():
        acc_ref[...] += part

    @pl.when(k == pl.num_programs(2) - 1)
    def _():
        o_ref[0] = x_ref[0] + gate_ref[0] * acc_ref[...]


def _ffn(x, g, sh, sc, w1, w2, gate, tm, tf):
    b, t, d = x.shape
    f = w1.shape[1]
    xs = pl.BlockSpec((1, tm, d), lambda i, j, k: (i, j, 0))
    vec = pl.BlockSpec((1, 1, d), lambda i, j, k: (i, 0, 0))
    return pl.pallas_call(
        _ffn_kernel,
        grid=(b, t // tm, f // tf),
        in_specs=[xs, pl.BlockSpec((1, d), lambda i, j, k: (0, 0)), vec, vec,
                  pl.BlockSpec((d, tf), lambda i, j, k: (0, k)),
                  pl.BlockSpec((tf, d), lambda i, j, k: (k, 0)), vec],
        out_specs=xs,
        out_shape=jax.ShapeDtypeStruct((b, t, d), jnp.float32),
        scratch_shapes=[pltpu.VMEM((tm, d), MXU_DTYPE), pltpu.VMEM((tm, d), jnp.float32)],
        compiler_params=_cparams(("parallel", "parallel", "arbitrary")),
        name="ffn",
    )(x, g, sh, sc, w1, w2, gate)


def _rope_perm():
    return np.array([(t // 16) * 16 + ((t % 16) + 8) % 16 for t in range(MLA_ROPE)])


def _prep_w_in(w):
    d = w.shape[0]
    o = dict(ckv=0, kr=128, k_na=160, v_na=416, zF=672, zB=928, i_hg=1184, pool=1440, qa=1696, q_na=1952,
             q_hg=2208, g_cols=2464, gates=2720)
    z = lambda n: jnp.zeros((d, n), w.dtype)
    k_rope = w[:, o["kr"]:o["kr"] + MLA_ROPE]
    parts = [w[:, o["gates"]:o["gates"] + 4096]]
    for name in ("pool", "qa", "q_na", "q_hg", "g_cols", "k_na", "v_na", "zF", "zB", "i_hg"):
        parts.append(w[:, o[name]:o[name] + 256])
    parts += [w[:, 0:128], z(64), k_rope, z(32), z(64), k_rope[:, _rope_perm()], z(32), z(128)]
    return _bf(jnp.concatenate(parts, axis=1))


def _prep_wq(w):
    r = w.shape[0]
    wh = w.reshape(r, N_HEADS, MLA_QK)
    pre = jnp.pad(wh, ((0, 0), (0, 0), (0, HEAD_PAD - MLA_QK)))
    sw = jnp.pad(wh[:, :, MLA_NOPE:][:, :, _rope_perm()], ((0, 0), (0, 0), (MLA_NOPE, HEAD_PAD - MLA_QK)))
    return _bf(jnp.concatenate([pre.reshape(r, -1), sw.reshape(r, -1)], axis=1))


def _prep_wkv(w):
    r = w.shape[0]
    wh = w.reshape(r, N_HEADS, MLA_NOPE + MLA_V)
    kpart = jnp.pad(wh[:, :, :MLA_NOPE], ((0, 0), (0, 0), (0, HEAD_PAD - MLA_NOPE)))
    return _bf(jnp.concatenate([kpart.reshape(r, -1), wh[:, :, MLA_NOPE:].reshape(r, -1)], axis=1))


def _pad_gain(g):
    plain = jnp.pad(g, (0, HEAD_PAD - MLA_QK))[None]
    sw = jnp.pad(g[MLA_NOPE:][_rope_perm()], (MLA_NOPE, HEAD_PAD - MLA_QK))[None]
    return plain, sw


def _rope_tables(n_tok):
    t = jnp.arange(n_tok)
    pos = jnp.stack([t // GRID_W, t % GRID_W], axis=-1).astype(jnp.float32)
    nf = MLA_ROPE // 4
    inv = ROPE_BASE ** (-jnp.arange(nf, dtype=jnp.float32) / nf)
    ang = pos[:, :, None] * inv
    cos, sin = jnp.cos(ang), jnp.sin(ang)
    cos_t = jnp.concatenate([cos, cos], axis=-1).reshape(n_tok, MLA_ROPE)
    sin_t = jnp.concatenate([-sin, sin], axis=-1).reshape(n_tok, MLA_ROPE)
    ones = jnp.ones((n_tok, MLA_NOPE), jnp.float32)
    zer = jnp.zeros((n_tok, HEAD_PAD - MLA_QK), jnp.float32)
    cos_p = jnp.concatenate([ones, cos_t, zer], axis=-1)
    sin_p = jnp.concatenate([0 * ones, sin_t, zer], axis=-1)
    return cos_p, sin_p


def _plain_tables(n_tok):
    cos_p = jnp.concatenate([jnp.ones((n_tok, MLA_QK), jnp.float32),
                             jnp.zeros((n_tok, HEAD_PAD - MLA_QK), jnp.float32)], axis=-1)
    return cos_p, jnp.zeros((n_tok, HEAD_PAD), jnp.float32)


def _head_indicator(dtype):
    i = np.arange(256) // 64
    return jnp.asarray((i[:, None] == i[None, :]).astype(np.float32), dtype=dtype)


def _na_bias_table(rpb):
    qc = np.arange(GRID_W)[:, None]
    kc = np.arange(GRID_W)[None, :]
    cs = np.clip(qc - NA_WIN_C // 2, 0, GRID_W - NA_WIN_C)
    inwin = (kc >= cs) & (kc < cs + NA_WIN_C)
    dc = np.clip(kc - qc + NA_WIN_C - 1, 0, 2 * NA_WIN_C - 2)
    t = jnp.where(inwin[None, None], rpb.astype(jnp.float32)[:, :, dc], NEG)
    tbl = jnp.stack([t[:, NA_WIN_R - 1 - p:2 * NA_WIN_R - 1 - p] for p in range(NA_WIN_R)], axis=0)
    return tbl.transpose(0, 1, 3, 2, 4).reshape(NA_WIN_R, N_HEADS * GRID_W, NA_WIN_R * GRID_W)


def _hg_consts():
    t = np.arange(HG_TILE)
    same = (t[:, None] // HG_SUB) == (t[None, :] // HG_SUB)
    tl = same & (t[None, :] <= t[:, None])
    tu = same & (t[None, :] >= t[:, None])
    sel = (np.arange(HG_TILE // HG_SUB)[:, None] == (t[None, :] // HG_SUB))
    f = lambda m: jnp.asarray(m.astype(np.float32), dtype=MXU_DTYPE)
    return f(tl), f(tu), f(same), f(sel)


def _block_diag(w):
    g, c, d = w.shape
    out = jnp.zeros((g * c, g * d), w.dtype)
    for i in range(g):
        out = out.at[i * c:(i + 1) * c, i * d:(i + 1) * d].set(w[i])
    return out


def kernel(x, c, ctx, c_ctx, ada_w, ada_b, norm1, norm2, w_in, pool_w, pool_scale, mla_q_norm, mla_wq_b, mla_kv_norm, mla_wkv_b, mla_gq, mla_gk, na_gq, na_gk, na_rpb, hg_lb, hg_norm, w_branch, w_out, w_ff1, w_ff2):
    B, L, D = x.shape
    Tc = ctx.shape[1]
    depth = ada_w.shape[0]
    f32 = jnp.float32
    assert L % HG_TILE == 0 and Tc % HG_TILE == 0 and B < 8

    lb_p = jax.nn.softmax(hg_lb.astype(f32), axis=0)
    lb_all = jnp.clip(jnp.cumsum(lb_p, axis=0)[:depth], LB_EPS, 1.0 - LB_EPS)

    cc = jnp.concatenate([c, c_ctx[None], jnp.zeros((8 - B - 1, D), f32)], axis=0)
    mods = _mods(cc, ada_w, ada_b)

    cos_l, sin_l = _rope_tables(L)
    cos_c, sin_c = _plain_tables(Tc)
    e_mat = _head_indicator(MXU_DTYPE)
    hmask = _head_indicator(f32)
    hg_consts = _hg_consts()
    zero_state = jnp.zeros((B, 256, 256), f32)
    zero_q = jnp.zeros((B, Tc, 256), f32)
    tl = min(512, L)

    xl, xc = x, ctx
    for l in range(depth):
        ctx_out = l < depth - 1
        ml = mods[l, :B].reshape(B, 1, 6, D)
        mc = jnp.broadcast_to(mods[l, B].reshape(1, 1, 6, D), (B, 1, 6, D))
        lmod = [ml[:, :, i] for i in range(6)]
        cmod = [mc[:, :, i] for i in range(6)]
        g1n = norm1[l][None]
        g2n = norm2[l][None]
        w_in_p = _prep_w_in(w_in[l])
        wq = _prep_wq(mla_wq_b[l])
        wkv = _prep_wkv(mla_wkv_b[l])
        gq, gqsw = _pad_gain(mla_gq[l])
        gk, gksw = _pad_gain(mla_gk[l])
        qn = mla_q_norm[l][None]
        kvn = mla_kv_norm[l][None]
        na_q_gain = (jnp.tile(na_gq[l], N_HEADS) * NA_HD ** -0.5)[None]
        na_k_gain = jnp.tile(na_gk[l], N_HEADS)[None]
        na_tbl = _na_bias_table(na_rpb[l])
        pool_bd = _bf(_block_diag(pool_w[l]))
        pool_sc = pool_scale[l][None]
        hgn = jnp.tile(hg_norm[l], N_HEADS)[None]
        lbF = lb_all[l, 0][None]
        lbB = lb_all[l, 1][None]
        w_br = _bf(w_branch[l])
        w_o = _bf(w_out[l])
        w1 = _bf(w_ff1[l])
        w2 = _bf(w_ff2[l])

        ul = _inproj(xl, g1n, lmod[0], lmod[1], w_in_p, min(1024, L))
        if ctx_out:
            uc, cbase = _inproj(xc, g1n, cmod[0], cmod[1], w_in_p, Tc), 0
        else:
            uc, cbase = _inproj(xc, g1n, cmod[0], cmod[1], w_in_p[:, KV_START:], Tc), KV_START

        q_mla, k_mla, v_mla = _mla_prep(ul, 0, cos_l, sin_l, qn, wq, kvn, wkv, gq, gqsw, gk, gksw, tl)
        if ctx_out:
            q_mla_c, k_mla_c, v_mla_c = _mla_prep(uc, cbase, cos_c, sin_c, qn, wq, kvn, wkv, gq, gqsw, gk, gksw, Tc)
        else:
            k_mla_c, v_mla_c = _mla_kv_prep(uc, cbase, kvn, wkv, gk, Tc)
        b_l = _attention(q_mla, jnp.concatenate([k_mla_c, k_mla], axis=2),
                         jnp.concatenate([v_mla_c, v_mla], axis=2), 256, 256)

        q_na, k_na, v_na = _na_prep(ul, COL["q_na"] // 256, ul, 0, na_q_gain, na_k_gain, e_mat, tl)
        if ctx_out:
            q_na_c, k_na_c, v_na_c = _na_prep(uc, COL["q_na"] // 256, uc, cbase, na_q_gain, na_k_gain, e_mat, Tc)
        else:
            _, k_na_c, v_na_c = _na_prep(zero_q, 0, uc, cbase, na_q_gain, na_k_gain, e_mat, Tc)
        c_l = _na_attention(q_na, k_na, v_na, k_na_c, v_na_c, na_tbl, 4)

        a_l = _pool(ul, 0, pool_bd, pool_sc, tl)

        if ctx_out:
            hc = _hg_prep(uc, COL["q_hg"] // 256, uc, cbase, lbF, lbB, hg_consts, e_mat)
        else:
            hc = _hg_prep(zero_q, 0, uc, cbase, lbF, lbB, hg_consts, e_mat)
        oFc, oBc, sFc, sBc = _hg_scan(hc[0], hc[1], hc[2], hc[3], hc[4], hc[5], hc[6], zero_state, zero_state, hmask)
        hl = _hg_prep(ul, COL["q_hg"] // 256, ul, 0, lbF, lbB, hg_consts, e_mat)
        oFl, oBl, _, _ = _hg_scan(hl[0], hl[1], hl[2], hl[3], hl[4], hl[5], hl[6], sFc, sBc, hmask)

        xl_new = _merge(xl, a_l, b_l, c_l, oFl, oBl, hl[7], ul, 0, hgn, e_mat, w_br, w_o, lmod[2], 256)

        if ctx_out:
            a_c = _pool(uc, cbase, pool_bd, pool_sc, Tc)
            b_c = _attention(q_mla_c, k_mla_c, v_mla_c, Tc, Tc)
            c_c = _ctx_na_attention(q_na_c, k_na_c, v_na_c)
            xc = _merge(xc, a_c, b_c, c_c, oFc, oBc, hc[7], uc, cbase, hgn, e_mat, w_br, w_o, cmod[2], Tc)
            xc = _ffn(xc, g2n, cmod[3], cmod[4], w1, w2, cmod[5], Tc, 1024)

        xl = _ffn(xl_new, g2n, lmod[3], lmod[4], w1, w2, lmod[5], tl, 1024)
    return xl
```

```python
import functools

import jax
import jax.numpy as jnp
import numpy as np
from jax import lax
from jax.experimental import pallas as pl
from jax.experimental.pallas import tpu as pltpu

D_MODEL = 1024
GRID_W = 64
EPS = 1e-6
N_BRANCH = 4
BRANCH_W = 256
POOL_GC = 64
N_HEADS = 4
MLA_NOPE = 64
MLA_ROPE = 32
MLA_V = 64
MLA_QK = 96
ROPE_BASE = 10000.0
NA_HD = 64
NA_WIN_R = 8
NA_WIN_C = 16
HG_DK = 64
LB_EPS = 1e-6

MXU_DTYPE = jnp.bfloat16
LANE = 128
HEAD_PAD = 128
VMEM_LIMIT = 48 * 1024 * 1024
HG_SUB = 16
HG_TILE = 256
NEG = -1e30

COL = dict(gates=0, pool=4096, qa=4352, q_na=4608, q_hg=4864, g_cols=5120,
           k_na=5376, v_na=5632, zF=5888, zB=6144, i_hg=6400, ckv=6656, kr=6784, kr_sw=6912)
D_IN_PAD = 7168
KV_START = 5376
N_TILE = 1792


def _cparams(sem):
    return pltpu.CompilerParams(dimension_semantics=sem, vmem_limit_bytes=VMEM_LIMIT)


def _bf(x):
    return x.astype(MXU_DTYPE)


def _dot(a, b):
    return jnp.dot(a, b, preferred_element_type=jnp.float32)


def _dot_nt(a, b):
    return lax.dot_general(a, b, (((1,), (1,)), ((), ())), preferred_element_type=jnp.float32)


def _dot_tn(a, b):
    return lax.dot_general(a, b, (((0,), (0,)), ((), ())), preferred_element_type=jnp.float32)


def _split2(x):
    hi = _bf(x)
    return hi, _bf(x - hi.astype(jnp.float32))


def _split_dot(x, w):
    hi, lo = _split2(x)
    return _dot(hi, w) + _dot(lo, w)


def _split3_dot_left(w, x):
    hi = _bf(x)
    r1 = x - hi.astype(jnp.float32)
    mid = _bf(r1)
    lo = _bf(r1 - mid.astype(jnp.float32))
    return _dot(w, hi) + _dot(w, mid) + _dot(w, lo)


def _sigmoid(x):
    return 0.5 * jnp.tanh(0.5 * x) + 0.5


def _rms_scale(x):
    return lax.rsqrt(jnp.mean(x * x, axis=-1, keepdims=True) + EPS)


def _full(arr, nargs):
    zeros = (0,) * arr.ndim
    return pl.BlockSpec(arr.shape, lambda *_: zeros)


def _ublk(tm, col, base, width):
    idx = (col - base) // width
    return pl.BlockSpec((1, tm, width), lambda i, j: (i, j, idx))


def _mods_kernel(c_ref, w_ref, b_ref, o_ref):
    c = c_ref[...]
    s = _bf(c * _sigmoid(c))
    o_ref[0] = _dot(s, _bf(w_ref[0])) + b_ref[0]


def _mods(cc, ada_w, ada_b):
    depth, d, n = ada_w.shape
    tn = 1536
    return pl.pallas_call(
        _mods_kernel,
        grid=(depth, n // tn),
        in_specs=[pl.BlockSpec((8, d), lambda l, j: (0, 0)),
                  pl.BlockSpec((1, d, tn), lambda l, j: (l, 0, j)),
                  pl.BlockSpec((1, 1, tn), lambda l, j: (l, 0, j))],
        out_specs=pl.BlockSpec((1, 8, tn), lambda l, j: (l, 0, j)),
        out_shape=jax.ShapeDtypeStruct((depth, 8, n), jnp.float32),
        compiler_params=_cparams(("arbitrary", "arbitrary")),
        name="mods",
    )(cc, ada_w, ada_b.reshape(depth, 1, n))


def _inproj_kernel(x_ref, g_ref, sh_ref, sc_ref, w_ref, o_ref, h_ref):
    @pl.when(pl.program_id(2) == 0)
    def _():
        x = x_ref[0]
        y = x * _rms_scale(x) * g_ref[...]
        h_ref[...] = _bf(y * (1.0 + sc_ref[0]) + sh_ref[0])

    o_ref[0] = _dot(h_ref[...], w_ref[...])


def _inproj(x, g, sh, sc, w, tm):
    b, t, d = x.shape
    n = w.shape[1]
    return pl.pallas_call(
        _inproj_kernel,
        grid=(b, t // tm, n // N_TILE),
        in_specs=[pl.BlockSpec((1, tm, d), lambda i, j, k: (i, j, 0)),
                  pl.BlockSpec((1, d), lambda i, j, k: (0, 0)),
                  pl.BlockSpec((1, 1, d), lambda i, j, k: (i, 0, 0)),
                  pl.BlockSpec((1, 1, d), lambda i, j, k: (i, 0, 0)),
                  pl.BlockSpec((d, N_TILE), lambda i, j, k: (0, k))],
        out_specs=pl.BlockSpec((1, tm, N_TILE), lambda i, j, k: (i, j, k)),
        out_shape=jax.ShapeDtypeStruct((b, t, n), jnp.float32),
        scratch_shapes=[pltpu.VMEM((tm, d), MXU_DTYPE)],
        compiler_params=_cparams(("parallel", "parallel", "arbitrary")),
        name="inproj",
    )(x, g, sh, sc, w)


def _mla_prep_kernel(qa_ref, ckv_ref, kr_ref, krsw_ref, cos_ref, sin_ref, qn_ref, wq_ref, kvn_ref, wkv_ref,
                     gq_ref, gqsw_ref, gk_ref, gksw_ref, q_ref, k_ref, vt_ref, *, scale):
    cos = cos_ref[...]
    sin = sin_ref[...]
    inv_w = 1.0 / MLA_QK
    qa = qa_ref[0]
    qq = _dot(_bf(qa * _rms_scale(qa) * qn_ref[...]), wq_ref[...])
    ckv = ckv_ref[0]
    kv = _dot(_bf(ckv * _rms_scale(ckv) * kvn_ref[...]), wkv_ref[...])
    kr = kr_ref[0]
    kr_rot = krsw_ref[0] * gksw_ref[...] * sin
    gq_cos = gq_ref[...] * cos
    gq_sin = gqsw_ref[...] * sin
    gk_cos = gk_ref[...] * cos
    for h in range(N_HEADS):
        qp = qq[:, h * HEAD_PAD:(h + 1) * HEAD_PAD]
        qs = qq[:, (N_HEADS + h) * HEAD_PAD:(N_HEADS + h + 1) * HEAD_PAD]
        rs = lax.rsqrt(jnp.sum(qp * qp, axis=-1, keepdims=True) * inv_w + EPS)
        q_ref[0, h] = _bf((qp * gq_cos + qs * gq_sin) * (rs * scale))
        kp = kv[:, h * HEAD_PAD:(h + 1) * HEAD_PAD] + kr
        rs = lax.rsqrt(jnp.sum(kp * kp, axis=-1, keepdims=True) * inv_w + EPS)
        k_ref[0, h] = _bf((kp * gk_cos + kr_rot) * rs)
    vt_ref[0] = _bf(kv[:, N_HEADS * HEAD_PAD:].T)


def _mla_prep(u, base, cos, sin, qn, wq, kvn, wkv, gq, gqsw, gk, gksw, tm):
    b, t, _ = u.shape
    tab = pl.BlockSpec((tm, HEAD_PAD), lambda i, j: (j, 0))
    hd = pl.BlockSpec((1, N_HEADS, tm, HEAD_PAD), lambda i, j: (i, 0, j, 0))
    consts = (qn, wq, kvn, wkv, gq, gqsw, gk, gksw)
    return pl.pallas_call(
        functools.partial(_mla_prep_kernel, scale=MLA_QK ** -0.5 * float(np.log2(np.e))),
        grid=(b, t // tm),
        in_specs=[_ublk(tm, COL["qa"], base, 256), _ublk(tm, COL["ckv"], base, 128),
                  _ublk(tm, COL["kr"], base, 128), _ublk(tm, COL["kr_sw"], base, 128), tab, tab]
                 + [_full(a, 2) for a in consts],
        out_specs=[hd, hd, pl.BlockSpec((1, 256, tm), lambda i, j: (i, 0, j))],
        out_shape=[jax.ShapeDtypeStruct((b, N_HEADS, t, HEAD_PAD), MXU_DTYPE),
                   jax.ShapeDtypeStruct((b, N_HEADS, t, HEAD_PAD), MXU_DTYPE),
                   jax.ShapeDtypeStruct((b, 256, t), MXU_DTYPE)],
        compiler_params=_cparams(("parallel", "parallel")),
        name="mla_prep",
    )(u, u, u, u, cos, sin, *consts)


def _mla_kv_prep_kernel(ckv_ref, kr_ref, kvn_ref, wkv_ref, gk_ref, k_ref, vt_ref):
    inv_w = 1.0 / MLA_QK
    ckv = ckv_ref[0]
    kv = _dot(_bf(ckv * _rms_scale(ckv) * kvn_ref[...]), wkv_ref[...])
    kr = kr_ref[0]
    for h in range(N_HEADS):
        kp = kv[:, h * HEAD_PAD:(h + 1) * HEAD_PAD] + kr
        rs = lax.rsqrt(jnp.sum(kp * kp, axis=-1, keepdims=True) * inv_w + EPS)
        k_ref[0, h] = _bf(kp * gk_ref[...] * rs)
    vt_ref[0] = _bf(kv[:, N_HEADS * HEAD_PAD:].T)


def _mla_kv_prep(u, base, kvn, wkv, gk, tm):
    b, t, _ = u.shape
    hd = pl.BlockSpec((1, N_HEADS, tm, HEAD_PAD), lambda i, j: (i, 0, j, 0))
    return pl.pallas_call(
        _mla_kv_prep_kernel,
        grid=(b, t // tm),
        in_specs=[_ublk(tm, COL["ckv"], base, 128), _ublk(tm, COL["kr"], base, 128),
                  _full(kvn, 2), _full(wkv, 2), _full(gk, 2)],
        out_specs=[hd, pl.BlockSpec((1, 256, tm), lambda i, j: (i, 0, j))],
        out_shape=[jax.ShapeDtypeStruct((b, N_HEADS, t, HEAD_PAD), MXU_DTYPE),
                   jax.ShapeDtypeStruct((b, 256, t), MXU_DTYPE)],
        compiler_params=_cparams(("parallel", "parallel")),
        name="mla_kv_prep",
    )(u, u, kvn, wkv, gk)


def _attn_kernel(q_ref, k_ref, vt_ref, o_ref, sa_ref, sb_ref, *, tk, n_chunks):
    tq = q_ref.shape[2]
    qs = [q_ref[0, h] for h in range(N_HEADS)]

    def produce(c, s_ref):
        start = pl.multiple_of(c * tk, tk)
        for h in range(N_HEADS):
            s_ref[h] = _dot_nt(k_ref[0, h, pl.ds(start, tk), :], qs[h])

    def consume(c, s_ref, carry):
        start = pl.multiple_of(c * tk, tk)
        new = []
        for h in range(N_HEADS):
            m, l, acc = carry[h]
            s = s_ref[h]
            m_new = jnp.maximum(m, jnp.max(s, axis=0, keepdims=True))
            alpha = jnp.exp2(m - m_new)
            p = jnp.exp2(s - m_new)
            l = alpha * l + jnp.sum(p, axis=0, keepdims=True)
            vt = vt_ref[0, h * MLA_V:(h + 1) * MLA_V, pl.ds(start, tk)]
            new.append((m_new, l, alpha * acc + _dot(vt, _bf(p))))
        return tuple(new)

    def body(i, carry):
        c = 2 * i
        produce(c + 1, sb_ref)
        carry = consume(c, sa_ref, carry)
        produce(c + 2, sa_ref)
        return consume(c + 1, sb_ref, carry)

    init = tuple((jnp.full((1, tq), -jnp.inf, jnp.float32), jnp.zeros((1, tq), jnp.float32),
                  jnp.zeros((MLA_V, tq), jnp.float32)) for _ in range(N_HEADS))
    produce(0, sa_ref)
    n_pairs = (n_chunks - 1) // 2
    carry = lax.fori_loop(0, n_pairs, body, init)
    c = 2 * n_pairs
    if n_chunks - c == 2:
        produce(c + 1, sb_ref)
        carry = consume(c, sa_ref, carry)
        carry = consume(c + 1, sb_ref, carry)
    else:
        carry = consume(c, sa_ref, carry)
    o_t = jnp.concatenate([acc / l for _, l, acc in carry], axis=0)
    o_ref[0] = o_t.T


def _attention(q, k, vt, tq, tk):
    b, _, t_q, _ = q.shape
    t_k = k.shape[2]
    assert t_q % tq == 0 and t_k % tk == 0
    return pl.pallas_call(
        functools.partial(_attn_kernel, tk=tk, n_chunks=t_k // tk),
        grid=(b, t_q // tq),
        in_specs=[pl.BlockSpec((1, N_HEADS, tq, HEAD_PAD), lambda i, j: (i, 0, j, 0)),
                  pl.BlockSpec((1, N_HEADS, t_k, HEAD_PAD), lambda i, j: (i, 0, 0, 0)),
                  pl.BlockSpec((1, 256, t_k), lambda i, j: (i, 0, 0))],
        out_specs=pl.BlockSpec((1, tq, 256), lambda i, j: (i, j, 0)),
        out_shape=jax.ShapeDtypeStruct((b, t_q, 256), jnp.float32),
        scratch_shapes=[pltpu.VMEM((N_HEADS, tk, tq), jnp.float32)] * 2,
        compiler_params=_cparams(("parallel", "parallel")),
        name="attention",
    )(q, k, vt)


def _pool_kernel(prev_ref, cur_ref, next_ref, w_ref, sc_ref, o_ref, ext_ref, *, tm, n_tok):
    j = pl.program_id(1)
    cur = cur_ref[0]
    ext_ref[8:8 + tm, :] = cur
    ext_ref[0:8, :] = jnp.where(j > 0, prev_ref[0], 0.0)
    ext_ref[8 + tm:16 + tm, :] = jnp.where(j < pl.num_programs(1) - 1, next_ref[0], 0.0)

    def window_sums(c0, widths):
        sh = lambda k: ext_ref[8 + k:8 + k + tm, c0:c0 + LANE]
        acc = sh(-1) + sh(0)
        out = {2: acc}
        for w in (4, 8, 16):
            if w > max(widths):
                break
            for k in list(range(-w // 2, -w // 4)) + list(range(w // 4, w // 2)):
                acc = acc + sh(k)
            out[w] = acc
        return [out[w] for w in widths]

    lane = lax.broadcasted_iota(jnp.int32, (tm, LANE), 1)
    t = j * tm + lax.broadcasted_iota(jnp.int32, (tm, LANE), 0)
    halves = []
    for c0, (wa, wb) in ((0, (2, 4)), (LANE, (8, 16))):
        sa, sb = window_sums(c0, (wa, wb))
        first = lane < POOL_GC
        s = jnp.where(first, sa, sb)
        half = jnp.where(first, wa // 2, wb // 2)
        cnt = jnp.minimum(t + half, n_tok) - jnp.maximum(t - half, 0)
        halves.append(s / cnt.astype(jnp.float32) - cur[:, c0:c0 + LANE])
    p = jnp.concatenate(halves, axis=-1)
    o_ref[0] = _dot(_bf(p), w_ref[...]) * sc_ref[...]


def _pool(u, base, w_bd, scale, tm):
    b, t, _ = u.shape
    cidx = (COL["pool"] - base) // 256
    rb = tm // 8
    last = t // 8 - 1
    return pl.pallas_call(
        functools.partial(_pool_kernel, tm=tm, n_tok=t),
        grid=(b, t // tm),
        in_specs=[pl.BlockSpec((1, 8, 256), lambda i, j: (i, jnp.maximum(j * rb - 1, 0), cidx)),
                  pl.BlockSpec((1, tm, 256), lambda i, j: (i, j, cidx)),
                  pl.BlockSpec((1, 8, 256), lambda i, j: (i, jnp.minimum((j + 1) * rb, last), cidx)),
                  _full(w_bd, 2), _full(scale, 2)],
        out_specs=pl.BlockSpec((1, tm, 256), lambda i, j: (i, j, 0)),
        out_shape=jax.ShapeDtypeStruct((b, t, 256), jnp.float32),
        scratch_shapes=[pltpu.VMEM((tm + 16, 256), jnp.float32)],
        compiler_params=_cparams(("parallel", "parallel")),
        name="pool",
    )(u, u, u, w_bd, scale)


def _head_rms(x, e):
    return lax.rsqrt(_split_dot(x * x, e) * (1.0 / NA_HD) + EPS)


def _na_prep_kernel(q_ref, k_ref, v_ref, gq_ref, gk_ref, e_ref, qo_ref, ko_ref, vo_ref):
    e = e_ref[...]
    q = q_ref[0]
    qo_ref[0] = _bf(q * _head_rms(q, e) * gq_ref[...])
    k = k_ref[0]
    ko_ref[0] = _bf(k * _head_rms(k, e) * gk_ref[...])
    vo_ref[0] = _bf(v_ref[0])


def _na_prep(uq, qcol, u, base, gq, gk, e_mat, tm):
    b, t, _ = u.shape
    tok = pl.BlockSpec((1, tm, 256), lambda i, j: (i, j, 0))
    return pl.pallas_call(
        _na_prep_kernel,
        grid=(b, t // tm),
        in_specs=[pl.BlockSpec((1, tm, 256), lambda i, j: (i, j, qcol)),
                  _ublk(tm, COL["k_na"], base, 256), _ublk(tm, COL["v_na"], base, 256),
                  _full(gq, 2), _full(gk, 2), _full(e_mat, 2)],
        out_specs=[tok, tok, tok],
        out_shape=[jax.ShapeDtypeStruct((b, t, 256), MXU_DTYPE)] * 3,
        compiler_params=_cparams(("parallel", "parallel")),
        name="na_prep",
    )(uq, u, u, gq, gk, e_mat)


def _stack_heads(q):
    qf = q.astype(jnp.float32)
    head = lax.broadcasted_iota(jnp.int32, qf.shape, 1) // NA_HD
    return _bf(jnp.concatenate([jnp.where(head == h, qf, 0.0) for h in range(N_HEADS)], axis=0))


def _unstack_heads(res):
    n = res.shape[0] // N_HEADS
    head = lax.broadcasted_iota(jnp.int32, (n, res.shape[1]), 1) // NA_HD
    out = res[0:n]
    for h in range(1, N_HEADS):
        out = jnp.where(head == h, res[h * n:(h + 1) * n], out)
    return out


def _softmax_pv(scores, values):
    m = functools.reduce(jnp.maximum, [jnp.max(s, axis=-1, keepdims=True) for s in scores])
    ps = [jnp.exp(s - m) for s in scores]
    l = functools.reduce(jnp.add, [jnp.sum(p, axis=-1, keepdims=True) for p in ps])
    acc = functools.reduce(jnp.add, [_dot(_bf(p), v) for p, v in zip(ps, values)])
    return acc / l


def _na_kernel(q_ref, k_ref, v_ref, kc_ref, vc_ref, tbl_ref, o_ref, *, rows, rb):
    jb = pl.program_id(1)
    kc = kc_ref[0]
    vc = vc_ref[0]
    n_loc = NA_WIN_R * GRID_W
    for rr in range(rb):
        r = jb * rb + rr
        rs = jnp.clip(r - NA_WIN_R // 2, 0, rows - NA_WIN_R)
        start = pl.multiple_of(rs * GRID_W, GRID_W)
        qm = _stack_heads(q_ref[0, rr * GRID_W:(rr + 1) * GRID_W, :])
        s_loc = _dot_nt(qm, k_ref[0, pl.ds(start, n_loc), :]) + tbl_ref[r - rs]
        s_ctx = _dot_nt(qm, kc)
        res = _softmax_pv([s_loc, s_ctx], [v_ref[0, pl.ds(start, n_loc), :], vc])
        o_ref[0, rr * GRID_W:(rr + 1) * GRID_W, :] = _unstack_heads(res)


def _na_attention(q, k, v, kc, vc, tbl, rb):
    b, t, _ = q.shape
    tc = kc.shape[1]
    rows = t // GRID_W
    assert rows >= NA_WIN_R and rows % rb == 0
    seq = pl.BlockSpec((1, t, 256), lambda i, j: (i, 0, 0))
    cseq = pl.BlockSpec((1, tc, 256), lambda i, j: (i, 0, 0))
    return pl.pallas_call(
        functools.partial(_na_kernel, rows=rows, rb=rb),
        grid=(b, rows // rb),
        in_specs=[pl.BlockSpec((1, rb * GRID_W, 256), lambda i, j: (i, j, 0)), seq, seq, cseq, cseq,
                  _full(tbl, 2)],
        out_specs=pl.BlockSpec((1, rb * GRID_W, 256), lambda i, j: (i, j, 0)),
        out_shape=jax.ShapeDtypeStruct((b, t, 256), jnp.float32),
        compiler_params=_cparams(("parallel", "parallel")),
        name="na_attention",
    )(q, k, v, kc, vc, tbl)


def _ctx_na_kernel(q_ref, kc_ref, vc_ref, o_ref):
    kc = kc_ref[0]
    vc = vc_ref[0]
    for rr in range(q_ref.shape[1] // GRID_W):
        qm = _stack_heads(q_ref[0, rr * GRID_W:(rr + 1) * GRID_W, :])
        res = _softmax_pv([_dot_nt(qm, kc)], [vc])
        o_ref[0, rr * GRID_W:(rr + 1) * GRID_W, :] = _unstack_heads(res)


def _ctx_na_attention(q, kc, vc):
    b, tc, _ = q.shape
    cseq = pl.BlockSpec((1, tc, 256), lambda i: (i, 0, 0))
    return pl.pallas_call(
        _ctx_na_kernel,
        grid=(b,),
        in_specs=[cseq, cseq, cseq],
        out_specs=cseq,
        out_shape=jax.ShapeDtypeStruct((b, tc, 256), jnp.float32),
        compiler_params=_cparams(("parallel",)),
        name="ctx_na_attention",
    )(q, kc, vc)


def _hg_gates(z, lb):
    e = jnp.exp(-jnp.abs(z))
    r = 1.0 / (1.0 + e)
    pos = z >= 0
    sig = jnp.where(pos, r, e * r)
    sig_neg = jnp.where(pos, e * r, r)
    return jnp.log(lb + (1.0 - lb) * sig), (1.0 - lb) * sig_neg


def _hg_prep_kernel(zf_ref, zb_ref, i_ref, q_ref, lbf_ref, lbb_ref, tl_ref, tu_ref, ta_ref, sel_ref, e_ref,
                    qf_o, kf_o, qb_o, kb_o, i_o, df_o, db_o, oi_o, nat_ref, pm_ref, acc_ref,
                    *, tm, qscale):
    nch = tm // HG_SUB
    q = q_ref[0] * qscale
    xi = i_ref[0]
    i_o[0] = _bf(xi)
    logf_f, kg_f = _hg_gates(zf_ref[0], lbf_ref[...])
    logf_b, kg_b = _hg_gates(zb_ref[0], lbb_ref[...])
    b_f = _split3_dot_left(tl_ref[...], logf_f)
    b_b = _split3_dot_left(tu_ref[...], logf_b)
    tot_f = _split3_dot_left(ta_ref[...], logf_f)
    tot_b = _split3_dot_left(ta_ref[...], logf_b)
    qf_o[0] = _bf(q * jnp.exp(b_f))
    kf_o[0] = _bf(kg_f * jnp.exp(tot_f - b_f))
    qb_o[0] = _bf(q * jnp.exp(b_b))
    kb_o[0] = _bf(kg_b * jnp.exp(tot_b - b_b))
    df_o[0] = jnp.exp(_split3_dot_left(sel_ref[...], logf_f))
    db_o[0] = jnp.exp(_split3_dot_left(sel_ref[...], logf_b))

    for a, val in enumerate((q, kg_f, b_f, kg_b, b_b, xi)):
        for half in range(2):
            nat_ref[a, half] = val[:, half * LANE:(half + 1) * LANE]
            for p in range(HG_SUB):
                pm_ref[a, half, p * nch:(p + 1) * nch, :] = nat_ref[a, half, pl.ds(p, nch, stride=HG_SUB), :]
    e = e_ref[...]
    for half in range(2):
        qp, kfp, bfp, kbp, bbp, ip = [pm_ref[a, half] for a in range(6)]
        acc_ref[...] = _dot(_bf(qp * (kfp + kbp)), e) * ip
        for delta in range(1, HG_SUB):
            off = delta * nch
            n = tm - off
            x = qp[off:] * kfp[:n] * jnp.exp(bfp[off:] - bfp[:n])
            acc_ref[off:, :] += _dot(_bf(x), e) * ip[:n]
            x = qp[:n] * kbp[off:] * jnp.exp(bbp[:n] - bbp[off:])
            acc_ref[:n, :] += _dot(_bf(x), e) * ip[off:]
        for p in range(HG_SUB):
            nat_ref[0, half, pl.ds(p, nch, stride=HG_SUB), :] = acc_ref[p * nch:(p + 1) * nch, :]
        oi_o[0, :, half * LANE:(half + 1) * LANE] = nat_ref[0, half]


def _hg_prep(uq, qcol, u, base, lbf, lbb, consts, e_mat):
    b, t, _ = u.shape
    tm = HG_TILE
    nsub = tm // HG_SUB
    tok = pl.BlockSpec((1, tm, 256), lambda i, j: (i, j, 0))
    dec = pl.BlockSpec((1, nsub, 256), lambda i, j: (i, j, 0))
    tl, tu, ta, sel = consts
    e_half = e_mat[:LANE, :LANE]
    regroup = pltpu.VMEM((6, 2, tm, LANE), jnp.float32)
    return pl.pallas_call(
        functools.partial(_hg_prep_kernel, tm=tm, qscale=HG_DK ** -0.5),
        grid=(b, t // tm),
        in_specs=[_ublk(tm, COL["zF"], base, 256), _ublk(tm, COL["zB"], base, 256),
                  _ublk(tm, COL["i_hg"], base, 256),
                  pl.BlockSpec((1, tm, 256), lambda i, j: (i, j, qcol)),
                  _full(lbf, 2), _full(lbb, 2), _full(tl, 2), _full(tu, 2), _full(ta, 2), _full(sel, 2),
                  _full(e_half, 2)],
        out_specs=[tok, tok, tok, tok, tok, dec, dec, tok],
        out_shape=[jax.ShapeDtypeStruct((b, t, 256), MXU_DTYPE)] * 5
                  + [jax.ShapeDtypeStruct((b, t // HG_SUB, 256), jnp.float32)] * 2
                  + [jax.ShapeDtypeStruct((b, t, 256), jnp.float32)],
        scratch_shapes=[regroup, regroup, pltpu.VMEM((tm, LANE), jnp.float32)],
        compiler_params=_cparams(("parallel", "parallel")),
        name="hg_prep",
    )(u, u, u, uq, lbf, lbb, tl, tu, ta, sel, e_half)


def _hg_scan_kernel(qf_ref, kf_ref, if_ref, df_ref, qb_ref, kb_ref, ib_ref, db_ref, s0f_ref, s0b_ref,
                    of_ref, ob_ref, sf_out, sb_out, sf, sb, *, nsub):
    j = pl.program_id(1)

    @pl.when(j == 0)
    def _():
        sf[...] = s0f_ref[0]
        sb[...] = s0b_ref[0]

    first = lax.broadcasted_iota(jnp.int32, (HG_SUB, LANE), 1) < HG_DK

    def step(state, q_ref, k_ref, i_ref, d_ref, o_ref, c):
        rows = slice(c * HG_SUB, (c + 1) * HG_SUB)
        for p in range(2):
            cols = slice(p * LANE, (p + 1) * LANE)
            s_old = state[p]
            q = q_ref[0, rows, cols].astype(jnp.float32)
            q2 = _bf(jnp.concatenate([jnp.where(first, q, 0.0), jnp.where(first, 0.0, q)], axis=0))
            r = _dot_nt(q2, _bf(s_old))
            o_ref[0, rows, cols] = jnp.where(first, r[:HG_SUB], r[HG_SUB:])
            upd = _dot_tn(i_ref[0, rows, cols], k_ref[0, rows, cols])
            state[p] = s_old * d_ref[0, c:c + 1, cols] + upd

    for c in range(nsub):
        step(sf, qf_ref, kf_ref, if_ref, df_ref, of_ref, c)
        step(sb, qb_ref, kb_ref, ib_ref, db_ref, ob_ref, nsub - 1 - c)

    @pl.when(j == pl.num_programs(1) - 1)
    def _():
        sf_out[0] = sf[...]
        sb_out[0] = sb[...]


def _hg_scan(qf, kf, qb, kb, xi, df, db, s0f, s0b):
    b, t, _ = qf.shape
    tm = HG_TILE
    n = t // tm
    nsub = tm // HG_SUB
    fwd = pl.BlockSpec((1, tm, 256), lambda i, j: (i, j, 0))
    bwd = pl.BlockSpec((1, tm, 256), lambda i, j: (i, n - 1 - j, 0))
    dfwd = pl.BlockSpec((1, nsub, 256), lambda i, j: (i, j, 0))
    dbwd = pl.BlockSpec((1, nsub, 256), lambda i, j: (i, n - 1 - j, 0))
    st = pl.BlockSpec((1, 2, LANE, LANE), lambda i, j: (i, 0, 0, 0))
    return pl.pallas_call(
        functools.partial(_hg_scan_kernel, nsub=nsub),
        grid=(b, n),
        in_specs=[fwd, fwd, fwd, dfwd, bwd, bwd, bwd, dbwd, st, st],
        out_specs=[fwd, bwd, st, st],
        out_shape=[jax.ShapeDtypeStruct((b, t, 256), jnp.float32)] * 2
                  + [jax.ShapeDtypeStruct((b, 2, LANE, LANE), jnp.float32)] * 2,
        scratch_shapes=[pltpu.VMEM((2, LANE, LANE), jnp.float32)] * 2,
        compiler_params=_cparams(("parallel", "arbitrary")),
        name="hg_scan",
    )(qf, kf, xi, df, qb, kb, xi, db, s0f, s0b)


def _merge_kernel(x_ref, a_ref, b_ref, c_ref, of_ref, ob_ref, oi_ref, gc_ref, g0_ref, g1_ref, g2_ref, g3_ref,
                  hgn_ref, e_ref, wbr_ref, wo_ref, gate_ref, out_ref):
    o = of_ref[0] + ob_ref[0] + oi_ref[0]
    ms = _split_dot(o * o, e_ref[...]) * (1.0 / HG_DK)
    d = o * lax.rsqrt(ms + EPS) * hgn_ref[...] * _sigmoid(gc_ref[0])
    acc = None
    for n, (y, g_ref) in enumerate(((a_ref[0], g0_ref), (b_ref[0], g1_ref), (c_ref[0], g2_ref), (d, g3_ref))):
        term = _sigmoid(g_ref[0]) * _dot(_bf(y), wbr_ref[n])
        acc = term if acc is None else acc + term
    out_ref[0] = x_ref[0] + gate_ref[0] * _dot(_bf(acc), wo_ref[...])


def _merge(x, a, bm, cn, o_f, o_b, o_i, u, base, hgn, e_mat, w_br, w_o, gate, tm):
    b, t, d = x.shape
    tok = pl.BlockSpec((1, tm, 256), lambda i, j: (i, j, 0))
    xs = pl.BlockSpec((1, tm, d), lambda i, j: (i, j, 0))
    return pl.pallas_call(
        _merge_kernel,
        grid=(b, t // tm),
        in_specs=[xs, tok, tok, tok, tok, tok, tok, _ublk(tm, COL["g_cols"], base, 256)]
                 + [_ublk(tm, COL["gates"] + n * d, base, d) for n in range(N_BRANCH)]
                 + [_full(hgn, 2), _full(e_mat, 2), _full(w_br, 2), _full(w_o, 2),
                    pl.BlockSpec((1, 1, d), lambda i, j: (i, 0, 0))],
        out_specs=xs,
        out_shape=jax.ShapeDtypeStruct((b, t, d), jnp.float32),
        compiler_params=_cparams(("parallel", "parallel")),
        name="merge",
    )(x, a, bm, cn, o_f, o_b, o_i, u, u, u, u, u, hgn, e_mat, w_br, w_o, gate)


def _ffn_kernel(x_ref, g_ref, sh_ref, sc_ref, w1_ref, w2_ref, gate_ref, o_ref, h_ref, acc_ref):
    k = pl.program_id(2)

    @pl.when(k == 0)
    def _():
        x = x_ref[0]
        y = x * _rms_scale(x) * g_ref[...]
        h_ref[...] = _bf(y * (1.0 + sc_ref[0]) + sh_ref[0])

    z = jnp.maximum(_dot(h_ref[...], w1_ref[...]), 0.0)
    part = _dot(_bf(z * z), w2_ref[...])

    @pl.when(k == 0)
    def _():
        acc_ref[...] = part

    @pl.when(k > 0)
    def _():
        acc_ref[...] += part

    @pl.when(k == pl.num_programs(2) - 1)
    def _():
        o_ref[0] = x_ref[0] + gate_ref[0] * acc_ref[...]


def _ffn(x, g, sh, sc, w1, w2, gate, tm, tf):
    b, t, d = x.shape
    f = w1.shape[1]
    xs = pl.BlockSpec((1, tm, d), lambda i, j, k: (i, j, 0))
    vec = pl.BlockSpec((1, 1, d), lambda i, j, k: (i, 0, 0))
    return pl.pallas_call(
        _ffn_kernel,
        grid=(b, t // tm, f // tf),
        in_specs=[xs, pl.BlockSpec((1, d), lambda i, j, k: (0, 0)), vec, vec,
                  pl.BlockSpec((d, tf), lambda i, j, k: (0, k)),
                  pl.BlockSpec((tf, d), lambda i, j, k: (k, 0)), vec],
        out_specs=xs,
        out_shape=jax.ShapeDtypeStruct((b, t, d), jnp.float32),
        scratch_shapes=[pltpu.VMEM((tm, d), MXU_DTYPE), pltpu.VMEM((tm, d), jnp.float32)],
        compiler_params=_cparams(("parallel", "parallel", "arbitrary")),
        name="ffn",
    )(x, g, sh, sc, w1, w2, gate)


def _rope_perm():
    return np.array([(t // 16) * 16 + ((t % 16) + 8) % 16 for t in range(MLA_ROPE)])


def _prep_w_in(w):
    d = w.shape[0]
    o = dict(ckv=0, kr=128, k_na=160, v_na=416, zF=672, zB=928, i_hg=1184, pool=1440, qa=1696, q_na=1952,
             q_hg=2208, g_cols=2464, gates=2720)
    z = lambda n: jnp.zeros((d, n), w.dtype)
    k_rope = w[:, o["kr"]:o["kr"] + MLA_ROPE]
    parts = [w[:, o["gates"]:o["gates"] + 4096]]
    for name in ("pool", "qa", "q_na", "q_hg", "g_cols", "k_na", "v_na", "zF", "zB", "i_hg"):
        parts.append(w[:, o[name]:o[name] + 256])
    parts += [w[:, 0:128], z(64), k_rope, z(32), z(64), k_rope[:, _rope_perm()], z(32), z(128)]
    return _bf(jnp.concatenate(parts, axis=1))


def _prep_wq(w):
    r = w.shape[0]
    wh = w.reshape(r, N_HEADS, MLA_QK)
    pre = jnp.pad(wh, ((0, 0), (0, 0), (0, HEAD_PAD - MLA_QK)))
    sw = jnp.pad(wh[:, :, MLA_NOPE:][:, :, _rope_perm()], ((0, 0), (0, 0), (MLA_NOPE, HEAD_PAD - MLA_QK)))
    return _bf(jnp.concatenate([pre.reshape(r, -1), sw.reshape(r, -1)], axis=1))


def _prep_wkv(w):
    r = w.shape[0]
    wh = w.reshape(r, N_HEADS, MLA_NOPE + MLA_V)
    kpart = jnp.pad(wh[:, :, :MLA_NOPE], ((0, 0), (0, 0), (0, HEAD_PAD - MLA_NOPE)))
    return _bf(jnp.concatenate([kpart.reshape(r, -1), wh[:, :, MLA_NOPE:].reshape(r, -1)], axis=1))


def _pad_gain(g):
    plain = jnp.pad(g, (0, HEAD_PAD - MLA_QK))[None]
    sw = jnp.pad(g[MLA_NOPE:][_rope_perm()], (MLA_NOPE, HEAD_PAD - MLA_QK))[None]
    return plain, sw


def _rope_tables(n_tok):
    t = jnp.arange(n_tok)
    pos = jnp.stack([t // GRID_W, t % GRID_W], axis=-1).astype(jnp.float32)
    nf = MLA_ROPE // 4
    inv = ROPE_BASE ** (-jnp.arange(nf, dtype=jnp.float32) / nf)
    ang = pos[:, :, None] * inv
    cos, sin = jnp.cos(ang), jnp.sin(ang)
    cos_t = jnp.concatenate([cos, cos], axis=-1).reshape(n_tok, MLA_ROPE)
    sin_t = jnp.concatenate([-sin, sin], axis=-1).reshape(n_tok, MLA_ROPE)
    ones = jnp.ones((n_tok, MLA_NOPE), jnp.float32)
    zer = jnp.zeros((n_tok, HEAD_PAD - MLA_QK), jnp.float32)
    cos_p = jnp.concatenate([ones, cos_t, zer], axis=-1)
    sin_p = jnp.concatenate([0 * ones, sin_t, zer], axis=-1)
    return cos_p, sin_p


def _plain_tables(n_tok):
    cos_p = jnp.concatenate([jnp.ones((n_tok, MLA_QK), jnp.float32),
                             jnp.zeros((n_tok, HEAD_PAD - MLA_QK), jnp.float32)], axis=-1)
    return cos_p, jnp.zeros((n_tok, HEAD_PAD), jnp.float32)


def _head_indicator(dtype):
    i = np.arange(256) // 64
    return jnp.asarray((i[:, None] == i[None, :]).astype(np.float32), dtype=dtype)


def _na_bias_table(rpb):
    qc = np.arange(GRID_W)[:, None]
    kc = np.arange(GRID_W)[None, :]
    cs = np.clip(qc - NA_WIN_C // 2, 0, GRID_W - NA_WIN_C)
    inwin = (kc >= cs) & (kc < cs + NA_WIN_C)
    dc = np.clip(kc - qc + NA_WIN_C - 1, 0, 2 * NA_WIN_C - 2)
    t = jnp.where(inwin[None, None], rpb.astype(jnp.float32)[:, :, dc], NEG)
    tbl = jnp.stack([t[:, NA_WIN_R - 1 - p:2 * NA_WIN_R - 1 - p] for p in range(NA_WIN_R)], axis=0)
    return tbl.transpose(0, 1, 3, 2, 4).reshape(NA_WIN_R, N_HEADS * GRID_W, NA_WIN_R * GRID_W)


def _hg_consts():
    t = np.arange(HG_TILE)
    same = (t[:, None] // HG_SUB) == (t[None, :] // HG_SUB)
    tl = same & (t[None, :] <= t[:, None])
    tu = same & (t[None, :] >= t[:, None])
    sel = (np.arange(HG_TILE // HG_SUB)[:, None] == (t[None, :] // HG_SUB))
    f = lambda m: jnp.asarray(m.astype(np.float32), dtype=MXU_DTYPE)
    return f(tl), f(tu), f(same), f(sel)


def _block_diag(w):
    g, c, d = w.shape
    out = jnp.zeros((g * c, g * d), w.dtype)
    for i in range(g):
        out = out.at[i * c:(i + 1) * c, i * d:(i + 1) * d].set(w[i])
    return out


def kernel(x, c, ctx, c_ctx, ada_w, ada_b, norm1, norm2, w_in, pool_w, pool_scale, mla_q_norm, mla_wq_b, mla_kv_norm, mla_wkv_b, mla_gq, mla_gk, na_gq, na_gk, na_rpb, hg_lb, hg_norm, w_branch, w_out, w_ff1, w_ff2):
    B, L, D = x.shape
    Tc = ctx.shape[1]
    depth = ada_w.shape[0]
    f32 = jnp.float32
    assert L % HG_TILE == 0 and Tc % HG_TILE == 0 and B < 8

    lb_p = jax.nn.softmax(hg_lb.astype(f32), axis=0)
    lb_all = jnp.clip(jnp.cumsum(lb_p, axis=0)[:depth], LB_EPS, 1.0 - LB_EPS)

    cc = jnp.concatenate([c, c_ctx[None], jnp.zeros((8 - B - 1, D), f32)], axis=0)
    mods = _mods(cc, ada_w, ada_b)

    cos_l, sin_l = _rope_tables(L)
    cos_c, sin_c = _plain_tables(Tc)
    e_mat = _head_indicator(MXU_DTYPE)
    hg_consts = _hg_consts()
    zero_state = jnp.zeros((B, 2, LANE, LANE), f32)
    zero_q = jnp.zeros((B, Tc, 256), f32)
    tl = min(512, L)

    xl, xc = x, ctx
    for l in range(depth):
        ctx_out = l < depth - 1
        ml = mods[l, :B].reshape(B, 1, 6, D)
        mc = jnp.broadcast_to(mods[l, B].reshape(1, 1, 6, D), (B, 1, 6, D))
        lmod = [ml[:, :, i] for i in range(6)]
        cmod = [mc[:, :, i] for i in range(6)]
        g1n = norm1[l][None]
        g2n = norm2[l][None]
        w_in_p = _prep_w_in(w_in[l])
        wq = _prep_wq(mla_wq_b[l])
        wkv = _prep_wkv(mla_wkv_b[l])
        gq, gqsw = _pad_gain(mla_gq[l])
        gk, gksw = _pad_gain(mla_gk[l])
        qn = mla_q_norm[l][None]
        kvn = mla_kv_norm[l][None]
        na_q_gain = (jnp.tile(na_gq[l], N_HEADS) * NA_HD ** -0.5)[None]
        na_k_gain = jnp.tile(na_gk[l], N_HEADS)[None]
        na_tbl = _na_bias_table(na_rpb[l])
        pool_bd = _bf(_block_diag(pool_w[l]))
        pool_sc = pool_scale[l][None]
        hgn = jnp.tile(hg_norm[l], N_HEADS)[None]
        lbF = lb_all[l, 0][None]
        lbB = lb_all[l, 1][None]
        w_br = _bf(w_branch[l])
        w_o = _bf(w_out[l])
        w1 = _bf(w_ff1[l])
        w2 = _bf(w_ff2[l])

        ul = _inproj(xl, g1n, lmod[0], lmod[1], w_in_p, min(1024, L))
        if ctx_out:
            uc, cbase = _inproj(xc, g1n, cmod[0], cmod[1], w_in_p, Tc), 0
        else:
            uc, cbase = _inproj(xc, g1n, cmod[0], cmod[1], w_in_p[:, KV_START:], Tc), KV_START

        q_mla, k_mla, v_mla = _mla_prep(ul, 0, cos_l, sin_l, qn, wq, kvn, wkv, gq, gqsw, gk, gksw, tl)
        if ctx_out:
            q_mla_c, k_mla_c, v_mla_c = _mla_prep(uc, cbase, cos_c, sin_c, qn, wq, kvn, wkv, gq, gqsw, gk, gksw, Tc)
        else:
            k_mla_c, v_mla_c = _mla_kv_prep(uc, cbase, kvn, wkv, gk, Tc)
        b_l = _attention(q_mla, jnp.concatenate([k_mla_c, k_mla], axis=2),
                         jnp.concatenate([v_mla_c, v_mla], axis=2), 256, 256)

        q_na, k_na, v_na = _na_prep(ul, COL["q_na"] // 256, ul, 0, na_q_gain, na_k_gain, e_mat, tl)
        if ctx_out:
            q_na_c, k_na_c, v_na_c = _na_prep(uc, COL["q_na"] // 256, uc, cbase, na_q_gain, na_k_gain, e_mat, Tc)
        else:
            _, k_na_c, v_na_c = _na_prep(zero_q, 0, uc, cbase, na_q_gain, na_k_gain, e_mat, Tc)
        c_l = _na_attention(q_na, k_na, v_na, k_na_c, v_na_c, na_tbl, 4)

        a_l = _pool(ul, 0, pool_bd, pool_sc, tl)

        if ctx_out:
            hc = _hg_prep(uc, COL["q_hg"] // 256, uc, cbase, lbF, lbB, hg_consts, e_mat)
        else:
            hc = _hg_prep(zero_q, 0, uc, cbase, lbF, lbB, hg_consts, e_mat)
        oFc, oBc, sFc, sBc = _hg_scan(hc[0], hc[1], hc[2], hc[3], hc[4], hc[5], hc[6], zero_state, zero_state)
        hl = _hg_prep(ul, COL["q_hg"] // 256, ul, 0, lbF, lbB, hg_consts, e_mat)
        oFl, oBl, _, _ = _hg_scan(hl[0], hl[1], hl[2], hl[3], hl[4], hl[5], hl[6], sFc, sBc)

        xl_new = _merge(xl, a_l, b_l, c_l, oFl, oBl, hl[7], ul, 0, hgn, e_mat, w_br, w_o, lmod[2], 256)

        if ctx_out:
            a_c = _pool(uc, cbase, pool_bd, pool_sc, Tc)
            b_c = _attention(q_mla_c, k_mla_c, v_mla_c, Tc, Tc)
            c_c = _ctx_na_attention(q_na_c, k_na_c, v_na_c)
            xc = _merge(xc, a_c, b_c, c_c, oFc, oBc, hc[7], uc, cbase, hgn, e_mat, w_br, w_o, cmod[2], Tc)
            xc = _ffn(xc, g2n, cmod[3], cmod[4], w1, w2, cmod[5], Tc, 1024)

        xl = _ffn(xl_new, g2n, lmod[3], lmod[4], w1, w2, lmod[5], tl, 1024)
    return xl
```

```python
import functools

import jax
import jax.numpy as jnp
import numpy as np
from jax import lax
from jax.experimental import pallas as pl
from jax.experimental.pallas import tpu as pltpu

D_MODEL = 1024
GRID_W = 64
EPS = 1e-6
N_BRANCH = 4
BRANCH_W = 256
POOL_GC = 64
N_HEADS = 4
MLA_NOPE = 64
MLA_ROPE = 32
MLA_V = 64
MLA_QK = 96
ROPE_BASE = 10000.0
NA_HD = 64
NA_WIN_R = 8
NA_WIN_C = 16
HG_DK = 64
LB_EPS = 1e-6

MXU_DTYPE = jnp.bfloat16
LANE = 128
HEAD_PAD = 128
VMEM_LIMIT = 48 * 1024 * 1024
HG_SUB = 16
HG_TILE = 256
NEG = -1e30

COL = dict(pool=0, qa=256, q_na=512, q_hg=768, g_cols=1024,
           k_na=1280, v_na=1536, zF=1792, zB=2048, i_hg=2304, ckv=2560, kr=2688, kr_sw=2816)
D_MIX_PAD = 3072
KV_START = 1280
GATE_TILE = 2048
MIX_TILE = 1536
KV_TILE = D_MIX_PAD - KV_START


def _cparams(sem):
    return pltpu.CompilerParams(dimension_semantics=sem, vmem_limit_bytes=VMEM_LIMIT)


def _bf(x):
    return x.astype(MXU_DTYPE)


def _dot(a, b):
    return jnp.dot(a, b, preferred_element_type=jnp.float32)


def _dot_nt(a, b):
    return lax.dot_general(a, b, (((1,), (1,)), ((), ())), preferred_element_type=jnp.float32)


def _dot_tn(a, b):
    return lax.dot_general(a, b, (((0,), (0,)), ((), ())), preferred_element_type=jnp.float32)


def _split2(x):
    hi = _bf(x)
    return hi, _bf(x - hi.astype(jnp.float32))


def _split_dot(x, w):
    hi, lo = _split2(x)
    return _dot(hi, w) + _dot(lo, w)


def _split3_dot_left(w, x):
    hi = _bf(x)
    r1 = x - hi.astype(jnp.float32)
    mid = _bf(r1)
    lo = _bf(r1 - mid.astype(jnp.float32))
    return _dot(w, hi) + _dot(w, mid) + _dot(w, lo)


def _sigmoid(x):
    return 0.5 * jnp.tanh(0.5 * x) + 0.5


def _rms_scale(x):
    return lax.rsqrt(jnp.mean(x * x, axis=-1, keepdims=True) + EPS)


def _full(arr, nargs):
    zeros = (0,) * arr.ndim
    return pl.BlockSpec(arr.shape, lambda *_: zeros)


def _ublk(tm, col, base, width):
    idx = (col - base) // width
    return pl.BlockSpec((1, tm, width), lambda i, j: (i, j, idx))


def _mods_kernel(c_ref, w_ref, b_ref, o_ref):
    c = c_ref[...]
    s = _bf(c * _sigmoid(c))
    o_ref[0] = _dot(s, _bf(w_ref[0])) + b_ref[0]


def _mods(cc, ada_w, ada_b):
    depth, d, n = ada_w.shape
    tn = 1536
    return pl.pallas_call(
        _mods_kernel,
        grid=(depth, n // tn),
        in_specs=[pl.BlockSpec((8, d), lambda l, j: (0, 0)),
                  pl.BlockSpec((1, d, tn), lambda l, j: (l, 0, j)),
                  pl.BlockSpec((1, 1, tn), lambda l, j: (l, 0, j))],
        out_specs=pl.BlockSpec((1, 8, tn), lambda l, j: (l, 0, j)),
        out_shape=jax.ShapeDtypeStruct((depth, 8, n), jnp.float32),
        compiler_params=_cparams(("arbitrary", "arbitrary")),
        name="mods",
    )(cc, ada_w, ada_b.reshape(depth, 1, n))


def _inproj_kernel(x_ref, g_ref, sh_ref, sc_ref, w_ref, o_ref, h_ref):
    @pl.when(pl.program_id(2) == 0)
    def _():
        x = x_ref[0]
        y = x * _rms_scale(x) * g_ref[...]
        h_ref[...] = _bf(y * (1.0 + sc_ref[0]) + sh_ref[0])

    o_ref[0] = _dot(h_ref[...], w_ref[...]).astype(o_ref.dtype)


def _mod_spec(mods, idx, d):
    if mods.shape[0] == 1:
        return pl.BlockSpec((1, 1, d), lambda i, *_: (0, 0, idx))
    return pl.BlockSpec((1, 1, d), lambda i, *_: (i, 0, idx))


def _inproj(x, g, mods, w, tm, tn, out_dtype):
    b, t, d = x.shape
    n = w.shape[1]
    assert n % tn == 0
    return pl.pallas_call(
        _inproj_kernel,
        grid=(b, t // tm, n // tn),
        in_specs=[pl.BlockSpec((1, tm, d), lambda i, j, k: (i, j, 0)),
                  pl.BlockSpec((1, d), lambda i, j, k: (0, 0)),
                  _mod_spec(mods, 0, d), _mod_spec(mods, 1, d),
                  pl.BlockSpec((d, tn), lambda i, j, k: (0, k))],
        out_specs=pl.BlockSpec((1, tm, tn), lambda i, j, k: (i, j, k)),
        out_shape=jax.ShapeDtypeStruct((b, t, n), out_dtype),
        scratch_shapes=[pltpu.VMEM((tm, d), MXU_DTYPE)],
        compiler_params=_cparams(("parallel", "parallel", "arbitrary")),
        name="inproj",
    )(x, g, mods, mods, w)


def _mla_prep_kernel(qa_ref, ckv_ref, kr_ref, krsw_ref, cos_ref, sin_ref, qn_ref, wq_ref, kvn_ref, wkv_ref,
                     gq_ref, gqsw_ref, gk_ref, gksw_ref, q_ref, k_ref, vt_ref, *, scale):
    cos = cos_ref[...]
    sin = sin_ref[...]
    inv_w = 1.0 / MLA_QK
    qa = qa_ref[0]
    qq = _dot(_bf(qa * _rms_scale(qa) * qn_ref[...]), wq_ref[...])
    ckv = ckv_ref[0]
    kv = _dot(_bf(ckv * _rms_scale(ckv) * kvn_ref[...]), wkv_ref[...])
    kr = kr_ref[0]
    kr_rot = krsw_ref[0] * gksw_ref[...] * sin
    gq_cos = gq_ref[...] * cos
    gq_sin = gqsw_ref[...] * sin
    gk_cos = gk_ref[...] * cos
    for h in range(N_HEADS):
        qp = qq[:, h * HEAD_PAD:(h + 1) * HEAD_PAD]
        qs = qq[:, (N_HEADS + h) * HEAD_PAD:(N_HEADS + h + 1) * HEAD_PAD]
        rs = lax.rsqrt(jnp.sum(qp * qp, axis=-1, keepdims=True) * inv_w + EPS)
        q_ref[0, h] = _bf((qp * gq_cos + qs * gq_sin) * (rs * scale))
        kp = kv[:, h * HEAD_PAD:(h + 1) * HEAD_PAD] + kr
        rs = lax.rsqrt(jnp.sum(kp * kp, axis=-1, keepdims=True) * inv_w + EPS)
        k_ref[0, h] = _bf((kp * gk_cos + kr_rot) * rs)
    vt_ref[0] = _bf(kv[:, N_HEADS * HEAD_PAD:].T)


def _mla_prep(u, base, cos, sin, qn, wq, kvn, wkv, gq, gqsw, gk, gksw, tm):
    b, t, _ = u.shape
    tab = pl.BlockSpec((tm, HEAD_PAD), lambda i, j: (j, 0))
    hd = pl.BlockSpec((1, N_HEADS, tm, HEAD_PAD), lambda i, j: (i, 0, j, 0))
    consts = (qn, wq, kvn, wkv, gq, gqsw, gk, gksw)
    return pl.pallas_call(
        functools.partial(_mla_prep_kernel, scale=MLA_QK ** -0.5 * float(np.log2(np.e))),
        grid=(b, t // tm),
        in_specs=[_ublk(tm, COL["qa"], base, 256), _ublk(tm, COL["ckv"], base, 128),
                  _ublk(tm, COL["kr"], base, 128), _ublk(tm, COL["kr_sw"], base, 128), tab, tab]
                 + [_full(a, 2) for a in consts],
        out_specs=[hd, hd, pl.BlockSpec((1, 256, tm), lambda i, j: (i, 0, j))],
        out_shape=[jax.ShapeDtypeStruct((b, N_HEADS, t, HEAD_PAD), MXU_DTYPE),
                   jax.ShapeDtypeStruct((b, N_HEADS, t, HEAD_PAD), MXU_DTYPE),
                   jax.ShapeDtypeStruct((b, 256, t), MXU_DTYPE)],
        compiler_params=_cparams(("parallel", "parallel")),
        name="mla_prep",
    )(u, u, u, u, cos, sin, *consts)


def _mla_kv_prep_kernel(ckv_ref, kr_ref, kvn_ref, wkv_ref, gk_ref, k_ref, vt_ref):
    inv_w = 1.0 / MLA_QK
    ckv = ckv_ref[0]
    kv = _dot(_bf(ckv * _rms_scale(ckv) * kvn_ref[...]), wkv_ref[...])
    kr = kr_ref[0]
    for h in range(N_HEADS):
        kp = kv[:, h * HEAD_PAD:(h + 1) * HEAD_PAD] + kr
        rs = lax.rsqrt(jnp.sum(kp * kp, axis=-1, keepdims=True) * inv_w + EPS)
        k_ref[0, h] = _bf(kp * gk_ref[...] * rs)
    vt_ref[0] = _bf(kv[:, N_HEADS * HEAD_PAD:].T)


def _mla_kv_prep(u, base, kvn, wkv, gk, tm):
    b, t, _ = u.shape
    hd = pl.BlockSpec((1, N_HEADS, tm, HEAD_PAD), lambda i, j: (i, 0, j, 0))
    return pl.pallas_call(
        _mla_kv_prep_kernel,
        grid=(b, t // tm),
        in_specs=[_ublk(tm, COL["ckv"], base, 128), _ublk(tm, COL["kr"], base, 128),
                  _full(kvn, 2), _full(wkv, 2), _full(gk, 2)],
        out_specs=[hd, pl.BlockSpec((1, 256, tm), lambda i, j: (i, 0, j))],
        out_shape=[jax.ShapeDtypeStruct((b, N_HEADS, t, HEAD_PAD), MXU_DTYPE),
                   jax.ShapeDtypeStruct((b, 256, t), MXU_DTYPE)],
        compiler_params=_cparams(("parallel", "parallel")),
        name="mla_kv_prep",
    )(u, u, kvn, wkv, gk)


def _attn_kernel(*refs, tk, n_ctx, n_lat):
    if n_lat:
        q_ref, kc_ref, vtc_ref, kl_ref, vtl_ref, o_ref, sa_ref, sb_ref = refs
    else:
        q_ref, kc_ref, vtc_ref, o_ref, sa_ref, sb_ref = refs
    tq = q_ref.shape[2]
    qs = [q_ref[0, h] for h in range(N_HEADS)]

    def produce(k_ref, c, s_ref):
        start = pl.multiple_of(c * tk, tk)
        for h in range(N_HEADS):
            s_ref[h] = _dot_nt(k_ref[0, h, pl.ds(start, tk), :], qs[h])

    def consume(vt_ref, c, s_ref, carry):
        start = pl.multiple_of(c * tk, tk)
        new = []
        for h in range(N_HEADS):
            m, l, acc = carry[h]
            s = s_ref[h]
            m_new = jnp.maximum(m, jnp.max(s, axis=0, keepdims=True))
            alpha = jnp.exp2(m - m_new)
            p = jnp.exp2(s - m_new)
            l = alpha * l + jnp.sum(p, axis=0, keepdims=True)
            vt = vt_ref[0, h * MLA_V:(h + 1) * MLA_V, pl.ds(start, tk)]
            new.append((m_new, l, alpha * acc + _dot(vt, _bf(p))))
        return tuple(new)

    carry = tuple((jnp.full((1, tq), -jnp.inf, jnp.float32), jnp.zeros((1, tq), jnp.float32),
                   jnp.zeros((MLA_V, tq), jnp.float32)) for _ in range(N_HEADS))
    cur, nxt = sa_ref, sb_ref
    produce(kc_ref, 0, cur)
    for c in range(n_ctx):
        if c + 1 < n_ctx:
            produce(kc_ref, c + 1, nxt)
        elif n_lat:
            produce(kl_ref, 0, nxt)
        carry = consume(vtc_ref, c, cur, carry)
        cur, nxt = nxt, cur
    if n_lat:
        def body(i, carry):
            c = 2 * i
            produce(kl_ref, c + 1, nxt)
            carry = consume(vtl_ref, c, cur, carry)
            produce(kl_ref, c + 2, cur)
            return consume(vtl_ref, c + 1, nxt, carry)

        n_pairs = (n_lat - 1) // 2
        carry = lax.fori_loop(0, n_pairs, body, carry)
        c = 2 * n_pairs
        if n_lat - c == 2:
            produce(kl_ref, c + 1, nxt)
            carry = consume(vtl_ref, c, cur, carry)
            carry = consume(vtl_ref, c + 1, nxt, carry)
        else:
            carry = consume(vtl_ref, c, cur, carry)
    o_t = jnp.concatenate([acc / l for _, l, acc in carry], axis=0)
    o_ref[0] = o_t.T


def _attention(q, kc, vtc, kl, vtl, tq, tk):
    b, _, t_q, _ = q.shape
    t_c = kc.shape[2]
    t_l = 0 if kl is None else kl.shape[2]
    assert t_q % tq == 0 and t_c % tk == 0 and t_l % tk == 0
    kv_specs = [pl.BlockSpec((1, N_HEADS, t_c, HEAD_PAD), lambda i, j: (i, 0, 0, 0)),
                pl.BlockSpec((1, 256, t_c), lambda i, j: (i, 0, 0))]
    args = [q, kc, vtc]
    if t_l:
        kv_specs += [pl.BlockSpec((1, N_HEADS, t_l, HEAD_PAD), lambda i, j: (i, 0, 0, 0)),
                     pl.BlockSpec((1, 256, t_l), lambda i, j: (i, 0, 0))]
        args += [kl, vtl]
    return pl.pallas_call(
        functools.partial(_attn_kernel, tk=tk, n_ctx=t_c // tk, n_lat=t_l // tk),
        grid=(b, t_q // tq),
        in_specs=[pl.BlockSpec((1, N_HEADS, tq, HEAD_PAD), lambda i, j: (i, 0, j, 0))] + kv_specs,
        out_specs=pl.BlockSpec((1, tq, 256), lambda i, j: (i, j, 0)),
        out_shape=jax.ShapeDtypeStruct((b, t_q, 256), jnp.float32),
        scratch_shapes=[pltpu.VMEM((N_HEADS, tk, tq), jnp.float32)] * 2,
        compiler_params=_cparams(("parallel", "parallel")),
        name="attention",
    )(*args)


def _pool_kernel(prev_ref, cur_ref, next_ref, w_ref, sc_ref, o_ref, ext_ref, *, tm, n_tok):
    j = pl.program_id(1)
    cur = cur_ref[0]
    ext_ref[8:8 + tm, :] = cur
    ext_ref[0:8, :] = jnp.where(j > 0, prev_ref[0], 0.0)
    ext_ref[8 + tm:16 + tm, :] = jnp.where(j < pl.num_programs(1) - 1, next_ref[0], 0.0)

    def window_sums(c0, widths):
        sh = lambda k: ext_ref[8 + k:8 + k + tm, c0:c0 + LANE]
        acc = sh(-1) + sh(0)
        out = {2: acc}
        for w in (4, 8, 16):
            if w > max(widths):
                break
            for k in list(range(-w // 2, -w // 4)) + list(range(w // 4, w // 2)):
                acc = acc + sh(k)
            out[w] = acc
        return [out[w] for w in widths]

    lane = lax.broadcasted_iota(jnp.int32, (tm, LANE), 1)
    t = j * tm + lax.broadcasted_iota(jnp.int32, (tm, LANE), 0)
    halves = []
    for c0, (wa, wb) in ((0, (2, 4)), (LANE, (8, 16))):
        sa, sb = window_sums(c0, (wa, wb))
        first = lane < POOL_GC
        s = jnp.where(first, sa, sb)
        half = jnp.where(first, wa // 2, wb // 2)
        cnt = jnp.minimum(t + half, n_tok) - jnp.maximum(t - half, 0)
        halves.append(s / cnt.astype(jnp.float32) - cur[:, c0:c0 + LANE])
    p = jnp.concatenate(halves, axis=-1)
    o_ref[0] = _dot(_bf(p), w_ref[...]) * sc_ref[...]


def _pool(u, base, w_bd, scale, tm):
    b, t, _ = u.shape
    cidx = (COL["pool"] - base) // 256
    rb = tm // 8
    last = t // 8 - 1
    return pl.pallas_call(
        functools.partial(_pool_kernel, tm=tm, n_tok=t),
        grid=(b, t // tm),
        in_specs=[pl.BlockSpec((1, 8, 256), lambda i, j: (i, jnp.maximum(j * rb - 1, 0), cidx)),
                  pl.BlockSpec((1, tm, 256), lambda i, j: (i, j, cidx)),
                  pl.BlockSpec((1, 8, 256), lambda i, j: (i, jnp.minimum((j + 1) * rb, last), cidx)),
                  _full(w_bd, 2), _full(scale, 2)],
        out_specs=pl.BlockSpec((1, tm, 256), lambda i, j: (i, j, 0)),
        out_shape=jax.ShapeDtypeStruct((b, t, 256), jnp.float32),
        scratch_shapes=[pltpu.VMEM((tm + 16, 256), jnp.float32)],
        compiler_params=_cparams(("parallel", "parallel")),
        name="pool",
    )(u, u, u, w_bd, scale)


def _head_rms(x, e):
    return lax.rsqrt(_split_dot(x * x, e) * (1.0 / NA_HD) + EPS)


def _na_prep_kernel(q_ref, k_ref, v_ref, gq_ref, gk_ref, e_ref, qo_ref, ko_ref, vo_ref):
    e = e_ref[...]
    q = q_ref[0]
    qo_ref[0] = _bf(q * _head_rms(q, e) * gq_ref[...])
    k = k_ref[0]
    ko_ref[0] = _bf(k * _head_rms(k, e) * gk_ref[...])
    vo_ref[0] = _bf(v_ref[0])


def _na_prep(uq, qcol, u, base, gq, gk, e_mat, tm):
    b, t, _ = u.shape
    tok = pl.BlockSpec((1, tm, 256), lambda i, j: (i, j, 0))
    return pl.pallas_call(
        _na_prep_kernel,
        grid=(b, t // tm),
        in_specs=[pl.BlockSpec((1, tm, 256), lambda i, j: (i, j, qcol)),
                  _ublk(tm, COL["k_na"], base, 256), _ublk(tm, COL["v_na"], base, 256),
                  _full(gq, 2), _full(gk, 2), _full(e_mat, 2)],
        out_specs=[tok, tok, tok],
        out_shape=[jax.ShapeDtypeStruct((b, t, 256), MXU_DTYPE)] * 3,
        compiler_params=_cparams(("parallel", "parallel")),
        name="na_prep",
    )(uq, u, u, gq, gk, e_mat)


def _stack_heads(q):
    qf = q.astype(jnp.float32)
    head = lax.broadcasted_iota(jnp.int32, qf.shape, 1) // NA_HD
    return _bf(jnp.concatenate([jnp.where(head == h, qf, 0.0) for h in range(N_HEADS)], axis=0))


def _unstack_heads(res):
    n = res.shape[0] // N_HEADS
    head = lax.broadcasted_iota(jnp.int32, (n, res.shape[1]), 1) // NA_HD
    out = res[0:n]
    for h in range(1, N_HEADS):
        out = jnp.where(head == h, res[h * n:(h + 1) * n], out)
    return out


def _softmax_pv(scores, values):
    m = functools.reduce(jnp.maximum, [jnp.max(s, axis=-1, keepdims=True) for s in scores])
    ps = [jnp.exp(s - m) for s in scores]
    l = functools.reduce(jnp.add, [jnp.sum(p, axis=-1, keepdims=True) for p in ps])
    acc = functools.reduce(jnp.add, [_dot(_bf(p), v) for p, v in zip(ps, values)])
    return acc / l


def _na_kernel(q_ref, k_ref, v_ref, kc_ref, vc_ref, tbl_ref, o_ref, *, rows, rb):
    jb = pl.program_id(1)
    kc = kc_ref[0]
    vc = vc_ref[0]
    n_loc = NA_WIN_R * GRID_W
    for rr in range(rb):
        r = jb * rb + rr
        rs = jnp.clip(r - NA_WIN_R // 2, 0, rows - NA_WIN_R)
        start = pl.multiple_of(rs * GRID_W, GRID_W)
        qm = _stack_heads(q_ref[0, rr * GRID_W:(rr + 1) * GRID_W, :])
        s_loc = _dot_nt(qm, k_ref[0, pl.ds(start, n_loc), :]) + tbl_ref[r - rs]
        s_ctx = _dot_nt(qm, kc)
        res = _softmax_pv([s_loc, s_ctx], [v_ref[0, pl.ds(start, n_loc), :], vc])
        o_ref[0, rr * GRID_W:(rr + 1) * GRID_W, :] = _unstack_heads(res)


def _na_attention(q, k, v, kc, vc, tbl, rb):
    b, t, _ = q.shape
    tc = kc.shape[1]
    rows = t // GRID_W
    assert rows >= NA_WIN_R and rows % rb == 0
    seq = pl.BlockSpec((1, t, 256), lambda i, j: (i, 0, 0))
    cseq = pl.BlockSpec((1, tc, 256), lambda i, j: (i, 0, 0))
    return pl.pallas_call(
        functools.partial(_na_kernel, rows=rows, rb=rb),
        grid=(b, rows // rb),
        in_specs=[pl.BlockSpec((1, rb * GRID_W, 256), lambda i, j: (i, j, 0)), seq, seq, cseq, cseq,
                  _full(tbl, 2)],
        out_specs=pl.BlockSpec((1, rb * GRID_W, 256), lambda i, j: (i, j, 0)),
        out_shape=jax.ShapeDtypeStruct((b, t, 256), jnp.float32),
        compiler_params=_cparams(("parallel", "parallel")),
        name="na_attention",
    )(q, k, v, kc, vc, tbl)


def _ctx_na_kernel(q_ref, kc_ref, vc_ref, o_ref):
    kc = kc_ref[0]
    vc = vc_ref[0]
    for rr in range(q_ref.shape[1] // GRID_W):
        qm = _stack_heads(q_ref[0, rr * GRID_W:(rr + 1) * GRID_W, :])
        res = _softmax_pv([_dot_nt(qm, kc)], [vc])
        o_ref[0, rr * GRID_W:(rr + 1) * GRID_W, :] = _unstack_heads(res)


def _ctx_na_attention(q, kc, vc):
    b, tc, _ = q.shape
    cseq = pl.BlockSpec((1, tc, 256), lambda i: (i, 0, 0))
    return pl.pallas_call(
        _ctx_na_kernel,
        grid=(b,),
        in_specs=[cseq, cseq, cseq],
        out_specs=cseq,
        out_shape=jax.ShapeDtypeStruct((b, tc, 256), jnp.float32),
        compiler_params=_cparams(("parallel",)),
        name="ctx_na_attention",
    )(q, kc, vc)


def _hg_gates(z, lb):
    e = jnp.exp(-jnp.abs(z))
    r = 1.0 / (1.0 + e)
    pos = z >= 0
    sig = jnp.where(pos, r, e * r)
    sig_neg = jnp.where(pos, e * r, r)
    return jnp.log(lb + (1.0 - lb) * sig), (1.0 - lb) * sig_neg


def _hg_prep_kernel(zf_ref, zb_ref, i_ref, q_ref, lbf_ref, lbb_ref, tl_ref, tu_ref, ta_ref, sel_ref, e_ref,
                    qf_o, kf_o, qb_o, kb_o, i_o, df_o, db_o, oi_o, nat_ref, pm_ref, acc_ref,
                    *, tm, qscale):
    nch = tm // HG_SUB
    q = q_ref[0] * qscale
    xi = i_ref[0]
    i_o[0] = _bf(xi)
    logf_f, kg_f = _hg_gates(zf_ref[0], lbf_ref[...])
    logf_b, kg_b = _hg_gates(zb_ref[0], lbb_ref[...])
    b_f = _split3_dot_left(tl_ref[...], logf_f)
    b_b = _split3_dot_left(tu_ref[...], logf_b)
    tot_f = _split3_dot_left(ta_ref[...], logf_f)
    tot_b = _split3_dot_left(ta_ref[...], logf_b)
    qf_o[0] = _bf(q * jnp.exp(b_f))
    kf_o[0] = _bf(kg_f * jnp.exp(tot_f - b_f))
    qb_o[0] = _bf(q * jnp.exp(b_b))
    kb_o[0] = _bf(kg_b * jnp.exp(tot_b - b_b))
    df_o[0] = jnp.exp(_split3_dot_left(sel_ref[...], logf_f))
    db_o[0] = jnp.exp(_split3_dot_left(sel_ref[...], logf_b))

    for a, val in enumerate((q, kg_f, b_f, kg_b, b_b, xi)):
        for half in range(2):
            nat_ref[a, half] = val[:, half * LANE:(half + 1) * LANE]
            for p in range(HG_SUB):
                pm_ref[a, half, p * nch:(p + 1) * nch, :] = nat_ref[a, half, pl.ds(p, nch, stride=HG_SUB), :]
    e = e_ref[...]
    for half in range(2):
        qp, kfp, bfp, kbp, bbp, ip = [pm_ref[a, half] for a in range(6)]
        acc_ref[...] = _dot(_bf(qp * (kfp + kbp)), e) * ip
        for delta in range(1, HG_SUB):
            off = delta * nch
            n = tm - off
            x = qp[off:] * kfp[:n] * jnp.exp(bfp[off:] - bfp[:n])
            acc_ref[off:, :] += _dot(_bf(x), e) * ip[:n]
            x = qp[:n] * kbp[off:] * jnp.exp(bbp[:n] - bbp[off:])
            acc_ref[:n, :] += _dot(_bf(x), e) * ip[off:]
        for p in range(HG_SUB):
            nat_ref[0, half, pl.ds(p, nch, stride=HG_SUB), :] = acc_ref[p * nch:(p + 1) * nch, :]
        oi_o[0, :, half * LANE:(half + 1) * LANE] = nat_ref[0, half]


def _hg_prep(uq, qcol, u, base, lbf, lbb, consts, e_mat):
    b, t, _ = u.shape
    tm = HG_TILE
    nsub = tm // HG_SUB
    tok = pl.BlockSpec((1, tm, 256), lambda i, j: (i, j, 0))
    dec = pl.BlockSpec((1, nsub, 256), lambda i, j: (i, j, 0))
    tl, tu, ta, sel = consts
    e_half = e_mat[:LANE, :LANE]
    regroup = pltpu.VMEM((6, 2, tm, LANE), jnp.float32)
    return pl.pallas_call(
        functools.partial(_hg_prep_kernel, tm=tm, qscale=HG_DK ** -0.5),
        grid=(b, t // tm),
        in_specs=[_ublk(tm, COL["zF"], base, 256), _ublk(tm, COL["zB"], base, 256),
                  _ublk(tm, COL["i_hg"], base, 256),
                  pl.BlockSpec((1, tm, 256), lambda i, j: (i, j, qcol)),
                  _full(lbf, 2), _full(lbb, 2), _full(tl, 2), _full(tu, 2), _full(ta, 2), _full(sel, 2),
                  _full(e_half, 2)],
        out_specs=[tok, tok, tok, tok, tok, dec, dec, tok],
        out_shape=[jax.ShapeDtypeStruct((b, t, 256), MXU_DTYPE)] * 5
                  + [jax.ShapeDtypeStruct((b, t // HG_SUB, 256), jnp.float32)] * 2
                  + [jax.ShapeDtypeStruct((b, t, 256), jnp.float32)],
        scratch_shapes=[regroup, regroup, pltpu.VMEM((tm, LANE), jnp.float32)],
        compiler_params=_cparams(("parallel", "parallel")),
        name="hg_prep",
    )(u, u, u, uq, lbf, lbb, tl, tu, ta, sel, e_half)


def _hg_scan_kernel(qf_ref, kf_ref, if_ref, df_ref, qb_ref, kb_ref, ib_ref, db_ref, s0f_ref, s0b_ref,
                    of_ref, ob_ref, sf_out, sb_out, sf, sb, *, nsub, nb):
    j = pl.program_id(0)

    @pl.when(j == 0)
    def _():
        sf[...] = s0f_ref[...]
        sb[...] = s0b_ref[...]

    first = lax.broadcasted_iota(jnp.int32, (HG_SUB, LANE), 1) < HG_DK

    def step(state, q_ref, k_ref, i_ref, d_ref, o_ref, bi, c):
        rows = slice(c * HG_SUB, (c + 1) * HG_SUB)
        for p in range(2):
            cols = slice(p * LANE, (p + 1) * LANE)
            s_old = state[bi, p]
            q = q_ref[bi, rows, cols].astype(jnp.float32)
            q2 = _bf(jnp.concatenate([jnp.where(first, q, 0.0), jnp.where(first, 0.0, q)], axis=0))
            r = _dot_nt(q2, _bf(s_old))
            o_ref[bi, rows, cols] = jnp.where(first, r[:HG_SUB], r[HG_SUB:])
            upd = _dot_tn(i_ref[bi, rows, cols], k_ref[bi, rows, cols])
            state[bi, p] = s_old * d_ref[bi, c:c + 1, cols] + upd

    for c in range(nsub):
        for bi in range(nb):
            step(sf, qf_ref, kf_ref, if_ref, df_ref, of_ref, bi, c)
            step(sb, qb_ref, kb_ref, ib_ref, db_ref, ob_ref, bi, nsub - 1 - c)

    @pl.when(j == pl.num_programs(0) - 1)
    def _():
        sf_out[...] = sf[...]
        sb_out[...] = sb[...]


def _hg_scan(qf, kf, qb, kb, xi, df, db, s0f, s0b):
    b, t, _ = qf.shape
    tm = HG_TILE
    n = t // tm
    nsub = tm // HG_SUB
    fwd = pl.BlockSpec((b, tm, 256), lambda j: (0, j, 0))
    bwd = pl.BlockSpec((b, tm, 256), lambda j: (0, n - 1 - j, 0))
    dfwd = pl.BlockSpec((b, nsub, 256), lambda j: (0, j, 0))
    dbwd = pl.BlockSpec((b, nsub, 256), lambda j: (0, n - 1 - j, 0))
    st = pl.BlockSpec((b, 2, LANE, LANE), lambda j: (0, 0, 0, 0))
    return pl.pallas_call(
        functools.partial(_hg_scan_kernel, nsub=nsub, nb=b),
        grid=(n,),
        in_specs=[fwd, fwd, fwd, dfwd, bwd, bwd, bwd, dbwd, st, st],
        out_specs=[fwd, bwd, st, st],
        out_shape=[jax.ShapeDtypeStruct((b, t, 256), jnp.float32)] * 2
                  + [jax.ShapeDtypeStruct((b, 2, LANE, LANE), jnp.float32)] * 2,
        scratch_shapes=[pltpu.VMEM((b, 2, LANE, LANE), jnp.float32)] * 2,
        compiler_params=_cparams(("arbitrary",)),
        name="hg_scan",
    )(qf, kf, xi, df, qb, kb, xi, db, s0f, s0b)


def _merge_kernel(x_ref, a_ref, b_ref, c_ref, of_ref, ob_ref, oi_ref, gc_ref, g0_ref, g1_ref, g2_ref, g3_ref,
                  hgn_ref, e_ref, wbr_ref, wo_ref, gate_ref, out_ref):
    o = of_ref[0] + ob_ref[0] + oi_ref[0]
    ms = _split_dot(o * o, e_ref[...]) * (1.0 / HG_DK)
    d = o * lax.rsqrt(ms + EPS) * hgn_ref[...] * _sigmoid(gc_ref[0])
    acc = None
    for n, (y, g_ref) in enumerate(((a_ref[0], g0_ref), (b_ref[0], g1_ref), (c_ref[0], g2_ref), (d, g3_ref))):
        term = _sigmoid(g_ref[0].astype(jnp.float32)) * _dot(_bf(y), wbr_ref[n])
        acc = term if acc is None else acc + term
    out_ref[0] = x_ref[0] + gate_ref[0] * _dot(_bf(acc), wo_ref[...])


def _merge(x, a, bm, cn, o_f, o_b, o_i, u, ug, hgn, e_mat, w_br, w_o, mods, tm):
    b, t, d = x.shape
    tok = pl.BlockSpec((1, tm, 256), lambda i, j: (i, j, 0))
    xs = pl.BlockSpec((1, tm, d), lambda i, j: (i, j, 0))
    return pl.pallas_call(
        _merge_kernel,
        grid=(b, t // tm),
        in_specs=[xs, tok, tok, tok, tok, tok, tok, _ublk(tm, COL["g_cols"], 0, 256)]
                 + [_ublk(tm, n * d, 0, d) for n in range(N_BRANCH)]
                 + [_full(hgn, 2), _full(e_mat, 2), _full(w_br, 2), _full(w_o, 2), _mod_spec(mods, 2, d)],
        out_specs=xs,
        out_shape=jax.ShapeDtypeStruct((b, t, d), jnp.float32),
        compiler_params=_cparams(("parallel", "parallel")),
        name="merge",
    )(x, a, bm, cn, o_f, o_b, o_i, u, ug, ug, ug, ug, hgn, e_mat, w_br, w_o, mods)


def _ffn_kernel(x_ref, g_ref, sh_ref, sc_ref, w1_ref, w2_ref, gate_ref, o_ref, h_ref, acc_ref):
    k = pl.program_id(2)

    @pl.when(k == 0)
    def _():
        x = x_ref[0]
        y = x * _rms_scale(x) * g_ref[...]
        h_ref[...] = _bf(y * (1.0 + sc_ref[0]) + sh_ref[0])

    z = jnp.maximum(_dot(h_ref[...], w1_ref[...]), 0.0)
    part = _dot(_bf(z * z), w2_ref[...])

    @pl.when(k == 0)
    def _():
        acc_ref[...] = part

    @pl.when(k > 0)
    def _():
        acc_ref[...] += part

    @pl.when(k == pl.num_programs(2) - 1)
    def _():
        o_ref[0] = x_ref[0] + gate_ref[0] * acc_ref[...]


def _ffn(x, g, mods, w1, w2, tm, tf):
    b, t, d = x.shape
    f = w1.shape[1]
    xs = pl.BlockSpec((1, tm, d), lambda i, j, k: (i, j, 0))
    return pl.pallas_call(
        _ffn_kernel,
        grid=(b, t // tm, f // tf),
        in_specs=[xs, pl.BlockSpec((1, d), lambda i, j, k: (0, 0)), _mod_spec(mods, 3, d), _mod_spec(mods, 4, d),
                  pl.BlockSpec((d, tf), lambda i, j, k: (0, k)),
                  pl.BlockSpec((tf, d), lambda i, j, k: (k, 0)), _mod_spec(mods, 5, d)],
        out_specs=xs,
        out_shape=jax.ShapeDtypeStruct((b, t, d), jnp.float32),
        scratch_shapes=[pltpu.VMEM((tm, d), MXU_DTYPE), pltpu.VMEM((tm, d), jnp.float32)],
        compiler_params=_cparams(("parallel", "parallel", "arbitrary")),
        name="ffn",
    )(x, g, mods, mods, w1, w2, mods)


def _rope_perm():
    return np.array([(t // 16) * 16 + ((t % 16) + 8) % 16 for t in range(MLA_ROPE)])


def _prep_w_in(w):
    d = w.shape[0]
    o = dict(ckv=0, kr=128, k_na=160, v_na=416, zF=672, zB=928, i_hg=1184, pool=1440, qa=1696, q_na=1952,
             q_hg=2208, g_cols=2464, gates=2720)
    z = lambda n: jnp.zeros((d, n), w.dtype)
    k_rope = w[:, o["kr"]:o["kr"] + MLA_ROPE]
    parts = [w[:, o[name]:o[name] + 256]
             for name in ("pool", "qa", "q_na", "q_hg", "g_cols", "k_na", "v_na", "zF", "zB", "i_hg")]
    parts += [w[:, 0:128], z(64), k_rope, z(32), z(64), k_rope[:, _rope_perm()], z(32), z(128)]
    return _bf(w[:, o["gates"]:]), _bf(jnp.concatenate(parts, axis=1))


def _prep_wq(w):
    r = w.shape[0]
    wh = w.reshape(r, N_HEADS, MLA_QK)
    pre = jnp.pad(wh, ((0, 0), (0, 0), (0, HEAD_PAD - MLA_QK)))
    sw = jnp.pad(wh[:, :, MLA_NOPE:][:, :, _rope_perm()], ((0, 0), (0, 0), (MLA_NOPE, HEAD_PAD - MLA_QK)))
    return _bf(jnp.concatenate([pre.reshape(r, -1), sw.reshape(r, -1)], axis=1))


def _prep_wkv(w):
    r = w.shape[0]
    wh = w.reshape(r, N_HEADS, MLA_NOPE + MLA_V)
    kpart = jnp.pad(wh[:, :, :MLA_NOPE], ((0, 0), (0, 0), (0, HEAD_PAD - MLA_NOPE)))
    return _bf(jnp.concatenate([kpart.reshape(r, -1), wh[:, :, MLA_NOPE:].reshape(r, -1)], axis=1))


def _pad_gain(g):
    plain = jnp.pad(g, (0, HEAD_PAD - MLA_QK))[None]
    sw = jnp.pad(g[MLA_NOPE:][_rope_perm()], (MLA_NOPE, HEAD_PAD - MLA_QK))[None]
    return plain, sw


def _rope_tables(n_tok):
    t = jnp.arange(n_tok)
    pos = jnp.stack([t // GRID_W, t % GRID_W], axis=-1).astype(jnp.float32)
    nf = MLA_ROPE // 4
    inv = ROPE_BASE ** (-jnp.arange(nf, dtype=jnp.float32) / nf)
    ang = pos[:, :, None] * inv
    cos, sin = jnp.cos(ang), jnp.sin(ang)
    cos_t = jnp.concatenate([cos, cos], axis=-1).reshape(n_tok, MLA_ROPE)
    sin_t = jnp.concatenate([-sin, sin], axis=-1).reshape(n_tok, MLA_ROPE)
    ones = jnp.ones((n_tok, MLA_NOPE), jnp.float32)
    zer = jnp.zeros((n_tok, HEAD_PAD - MLA_QK), jnp.float32)
    cos_p = jnp.concatenate([ones, cos_t, zer], axis=-1)
    sin_p = jnp.concatenate([0 * ones, sin_t, zer], axis=-1)
    return cos_p, sin_p


def _plain_tables(n_tok):
    cos_p = jnp.concatenate([jnp.ones((n_tok, MLA_QK), jnp.float32),
                             jnp.zeros((n_tok, HEAD_PAD - MLA_QK), jnp.float32)], axis=-1)
    return cos_p, jnp.zeros((n_tok, HEAD_PAD), jnp.float32)


def _head_indicator(dtype):
    i = np.arange(256) // 64
    return jnp.asarray((i[:, None] == i[None, :]).astype(np.float32), dtype=dtype)


def _na_bias_table(rpb):
    qc = np.arange(GRID_W)[:, None]
    kc = np.arange(GRID_W)[None, :]
    cs = np.clip(qc - NA_WIN_C // 2, 0, GRID_W - NA_WIN_C)
    inwin = (kc >= cs) & (kc < cs + NA_WIN_C)
    dc = np.clip(kc - qc + NA_WIN_C - 1, 0, 2 * NA_WIN_C - 2)
    onehot = (dc[:, :, None] == np.arange(2 * NA_WIN_C - 1)).astype(np.float32)
    t = jnp.einsum('hrd,qkd->hrqk', rpb.astype(jnp.float32), onehot, precision=lax.Precision.HIGHEST)
    t = jnp.where(inwin[None, None], t, NEG)
    tbl = jnp.stack([t[:, NA_WIN_R - 1 - p:2 * NA_WIN_R - 1 - p] for p in range(NA_WIN_R)], axis=0)
    return tbl.transpose(0, 1, 3, 2, 4).reshape(NA_WIN_R, N_HEADS * GRID_W, NA_WIN_R * GRID_W)


def _hg_consts():
    t = np.arange(HG_TILE)
    same = (t[:, None] // HG_SUB) == (t[None, :] // HG_SUB)
    tl = same & (t[None, :] <= t[:, None])
    tu = same & (t[None, :] >= t[:, None])
    sel = (np.arange(HG_TILE // HG_SUB)[:, None] == (t[None, :] // HG_SUB))
    f = lambda m: jnp.asarray(m.astype(np.float32), dtype=MXU_DTYPE)
    return f(tl), f(tu), f(same), f(sel)


def _block_diag(w):
    g, c, d = w.shape
    out = jnp.zeros((g * c, g * d), w.dtype)
    for i in range(g):
        out = out.at[i * c:(i + 1) * c, i * d:(i + 1) * d].set(w[i])
    return out


def kernel(x, c, ctx, c_ctx, ada_w, ada_b, norm1, norm2, w_in, pool_w, pool_scale, mla_q_norm, mla_wq_b, mla_kv_norm, mla_wkv_b, mla_gq, mla_gk, na_gq, na_gk, na_rpb, hg_lb, hg_norm, w_branch, w_out, w_ff1, w_ff2):
    B, L, D = x.shape
    Tc = ctx.shape[1]
    depth = ada_w.shape[0]
    f32 = jnp.float32
    assert L % HG_TILE == 0 and Tc % HG_TILE == 0 and B < 8

    lb_p = jax.nn.softmax(hg_lb.astype(f32), axis=0)
    lb_all = jnp.clip(jnp.cumsum(lb_p, axis=0)[:depth], LB_EPS, 1.0 - LB_EPS)

    cc = jnp.concatenate([c, c_ctx[None], jnp.zeros((8 - B - 1, D), f32)], axis=0)
    mods = _mods(cc, ada_w, ada_b)

    cos_l, sin_l = _rope_tables(L)
    cos_c, sin_c = _plain_tables(Tc)
    e_mat = _head_indicator(MXU_DTYPE)
    hg_consts = _hg_consts()
    zero_state = jnp.zeros((B, 2, LANE, LANE), f32)
    zero_q = jnp.zeros((B, Tc, 256), f32)
    tl = min(512, L)
    t_big = min(1024, L)

    xl, xc = x, ctx
    for l in range(depth):
        ctx_out = l < depth - 1
        mods_l = mods[l, :B].reshape(B, 1, 6 * D)
        mods_c = mods[l, B].reshape(1, 1, 6 * D)
        g1n = norm1[l][None]
        g2n = norm2[l][None]
        w_gate, w_mix = _prep_w_in(w_in[l])
        wq = _prep_wq(mla_wq_b[l])
        wkv = _prep_wkv(mla_wkv_b[l])
        gq, gqsw = _pad_gain(mla_gq[l])
        gk, gksw = _pad_gain(mla_gk[l])
        qn = mla_q_norm[l][None]
        kvn = mla_kv_norm[l][None]
        na_q_gain = (jnp.tile(na_gq[l], N_HEADS) * NA_HD ** -0.5)[None]
        na_k_gain = jnp.tile(na_gk[l], N_HEADS)[None]
        na_tbl = _na_bias_table(na_rpb[l])
        pool_bd = _bf(_block_diag(pool_w[l]))
        pool_sc = pool_scale[l][None]
        hgn = jnp.tile(hg_norm[l], N_HEADS)[None]
        lbF = lb_all[l, 0][None]
        lbB = lb_all[l, 1][None]
        w_br = _bf(w_branch[l])
        w_o = _bf(w_out[l])
        w1 = _bf(w_ff1[l])
        w2 = _bf(w_ff2[l])

        ul = _inproj(xl, g1n, mods_l, w_mix, t_big, MIX_TILE, f32)
        ugl = _inproj(xl, g1n, mods_l, w_gate, t_big, GATE_TILE, MXU_DTYPE)
        if ctx_out:
            uc, cbase = _inproj(xc, g1n, mods_c, w_mix, Tc, MIX_TILE, f32), 0
            ugc = _inproj(xc, g1n, mods_c, w_gate, Tc, GATE_TILE, MXU_DTYPE)
        else:
            uc, cbase = _inproj(xc, g1n, mods_c, w_mix[:, KV_START:], Tc, KV_TILE, f32), KV_START

        q_mla, k_mla, v_mla = _mla_prep(ul, 0, cos_l, sin_l, qn, wq, kvn, wkv, gq, gqsw, gk, gksw, tl)
        if ctx_out:
            q_mla_c, k_mla_c, v_mla_c = _mla_prep(uc, cbase, cos_c, sin_c, qn, wq, kvn, wkv, gq, gqsw, gk, gksw, Tc)
        else:
            k_mla_c, v_mla_c = _mla_kv_prep(uc, cbase, kvn, wkv, gk, Tc)
        b_l = _attention(q_mla, k_mla_c, v_mla_c, k_mla, v_mla, 256, 256)

        q_na, k_na, v_na = _na_prep(ul, COL["q_na"] // 256, ul, 0, na_q_gain, na_k_gain, e_mat, tl)
        if ctx_out:
            q_na_c, k_na_c, v_na_c = _na_prep(uc, COL["q_na"] // 256, uc, cbase, na_q_gain, na_k_gain, e_mat, Tc)
        else:
            _, k_na_c, v_na_c = _na_prep(zero_q, 0, uc, cbase, na_q_gain, na_k_gain, e_mat, Tc)
        c_l = _na_attention(q_na, k_na, v_na, k_na_c, v_na_c, na_tbl, 4)

        a_l = _pool(ul, 0, pool_bd, pool_sc, tl)

        if ctx_out:
            hc = _hg_prep(uc, COL["q_hg"] // 256, uc, cbase, lbF, lbB, hg_consts, e_mat)
        else:
            hc = _hg_prep(zero_q, 0, uc, cbase, lbF, lbB, hg_consts, e_mat)
        oFc, oBc, sFc, sBc = _hg_scan(hc[0], hc[1], hc[2], hc[3], hc[4], hc[5], hc[6], zero_state, zero_state)
        hl = _hg_prep(ul, COL["q_hg"] // 256, ul, 0, lbF, lbB, hg_consts, e_mat)
        oFl, oBl, _, _ = _hg_scan(hl[0], hl[1], hl[2], hl[3], hl[4], hl[5], hl[6], sFc, sBc)

        xl_new = _merge(xl, a_l, b_l, c_l, oFl, oBl, hl[7], ul, ugl, hgn, e_mat, w_br, w_o, mods_l, 256)

        if ctx_out:
            a_c = _pool(uc, cbase, pool_bd, pool_sc, Tc)
            b_c = _attention(q_mla_c, k_mla_c, v_mla_c, None, None, Tc, Tc)
            c_c = _ctx_na_attention(q_na_c, k_na_c, v_na_c)
            xc = _merge(xc, a_c, b_c, c_c, oFc, oBc, hc[7], uc, ugc, hgn, e_mat, w_br, w_o, mods_c, Tc)
            xc = _ffn(xc, g2n, mods_c, w1, w2, Tc, 1024)

        xl = _ffn(xl_new, g2n, mods_l, w1, w2, t_big, 1024)
    return xl
```

```python
import functools

import jax
import jax.numpy as jnp
import numpy as np
from jax import lax
from jax.experimental import pallas as pl
from jax.experimental.pallas import tpu as pltpu

D_MODEL = 1024
GRID_W = 64
EPS = 1e-6
N_BRANCH = 4
BRANCH_W = 256
POOL_GC = 64
N_HEADS = 4
MLA_NOPE = 64
MLA_ROPE = 32
MLA_V = 64
MLA_QK = 96
ROPE_BASE = 10000.0
NA_HD = 64
NA_WIN_R = 8
NA_WIN_C = 16
HG_DK = 64
LB_EPS = 1e-6

MXU_DTYPE = jnp.bfloat16
LANE = 128
HEAD_PAD = 128
ONES_ROWS = 16
VMEM_LIMIT = 48 * 1024 * 1024
HG_SUB = 16
HG_TILE = 256
NEG = -1e30

COL = dict(pool=0, qa=256, q_na=512, q_hg=768, g_cols=1024,
           k_na=1280, v_na=1536, zF=1792, zB=2048, i_hg=2304, ckv=2560, kr=2688, kr_sw=2816)
D_MIX_PAD = 3072
KV_START = 1280
MIX_TILE = 1536
KV_TILE = D_MIX_PAD - KV_START


def _cparams(sem):
    return pltpu.CompilerParams(dimension_semantics=sem, vmem_limit_bytes=VMEM_LIMIT)


def _bf(x):
    return x.astype(MXU_DTYPE)


def _dot(a, b):
    return jnp.dot(a, b, preferred_element_type=jnp.float32)


def _dot_nt(a, b):
    return lax.dot_general(a, b, (((1,), (1,)), ((), ())), preferred_element_type=jnp.float32)


def _dot_tn(a, b):
    return lax.dot_general(a, b, (((0,), (0,)), ((), ())), preferred_element_type=jnp.float32)


def _split2(x):
    hi = _bf(x)
    return hi, _bf(x - hi.astype(jnp.float32))


def _split_dot(x, w):
    hi, lo = _split2(x)
    return _dot(hi, w) + _dot(lo, w)


def _split3_dot_left(w, x):
    hi = _bf(x)
    r1 = x - hi.astype(jnp.float32)
    mid = _bf(r1)
    lo = _bf(r1 - mid.astype(jnp.float32))
    return _dot(w, hi) + _dot(w, mid) + _dot(w, lo)


def _sigmoid(x):
    return 0.5 * jnp.tanh(0.5 * x) + 0.5


def _rms_scale(x):
    return lax.rsqrt(jnp.mean(x * x, axis=-1, keepdims=True) + EPS)


def _full(arr, nargs):
    zeros = (0,) * arr.ndim
    return pl.BlockSpec(arr.shape, lambda *_: zeros)


def _ublk(tm, col, base, width):
    idx = (col - base) // width
    return pl.BlockSpec((1, tm, width), lambda i, j: (i, j, idx))


def _mods_kernel(c_ref, w_ref, b_ref, o_ref):
    c = c_ref[...]
    s = _bf(c * _sigmoid(c))
    o_ref[0] = _dot(s, _bf(w_ref[0])) + b_ref[0]


def _mods(cc, ada_w, ada_b):
    depth, d, n = ada_w.shape
    tn = 1536
    return pl.pallas_call(
        _mods_kernel,
        grid=(depth, n // tn),
        in_specs=[pl.BlockSpec((8, d), lambda l, j: (0, 0)),
                  pl.BlockSpec((1, d, tn), lambda l, j: (l, 0, j)),
                  pl.BlockSpec((1, 1, tn), lambda l, j: (l, 0, j))],
        out_specs=pl.BlockSpec((1, 8, tn), lambda l, j: (l, 0, j)),
        out_shape=jax.ShapeDtypeStruct((depth, 8, n), jnp.float32),
        compiler_params=_cparams(("arbitrary", "arbitrary")),
        name="mods",
    )(cc, ada_w, ada_b.reshape(depth, 1, n))


def _inproj_kernel(x_ref, g_ref, sh_ref, sc_ref, w_ref, o_ref, h_ref):
    @pl.when(pl.program_id(2) == 0)
    def _():
        x = x_ref[0]
        y = x * _rms_scale(x) * g_ref[...]
        h_ref[...] = _bf(y * (1.0 + sc_ref[0]) + sh_ref[0])

    o_ref[0] = _dot(h_ref[...], w_ref[...]).astype(o_ref.dtype)


def _mod_spec(mods, idx, d):
    if mods.shape[0] == 1:
        return pl.BlockSpec((1, 1, d), lambda i, *_: (0, 0, idx))
    return pl.BlockSpec((1, 1, d), lambda i, *_: (i, 0, idx))


def _inproj(x, g, mods, w, tm, tn, out_dtype):
    b, t, d = x.shape
    n = w.shape[1]
    assert n % tn == 0
    return pl.pallas_call(
        _inproj_kernel,
        grid=(b, t // tm, n // tn),
        in_specs=[pl.BlockSpec((1, tm, d), lambda i, j, k: (i, j, 0)),
                  pl.BlockSpec((1, d), lambda i, j, k: (0, 0)),
                  _mod_spec(mods, 0, d), _mod_spec(mods, 1, d),
                  pl.BlockSpec((d, tn), lambda i, j, k: (0, k))],
        out_specs=pl.BlockSpec((1, tm, tn), lambda i, j, k: (i, j, k)),
        out_shape=jax.ShapeDtypeStruct((b, t, n), out_dtype),
        scratch_shapes=[pltpu.VMEM((tm, d), MXU_DTYPE)],
        compiler_params=_cparams(("parallel", "parallel", "arbitrary")),
        name="inproj",
    )(x, g, mods, mods, w)


def _mla_prep_kernel(qa_ref, ckv_ref, kr_ref, krsw_ref, cos_ref, sin_ref, qn_ref, wq_ref, kvn_ref, wkv_ref,
                     gq_ref, gqsw_ref, gk_ref, gksw_ref, q_ref, k_ref, vt_ref, *, scale):
    cos = cos_ref[...]
    sin = sin_ref[...]
    inv_w = 1.0 / MLA_QK
    qa = qa_ref[0]
    qq = _dot(_bf(qa * _rms_scale(qa) * qn_ref[...]), wq_ref[...])
    ckv = ckv_ref[0]
    kv = _dot(_bf(ckv * _rms_scale(ckv) * kvn_ref[...]), wkv_ref[...])
    kr = kr_ref[0]
    kr_rot = krsw_ref[0] * gksw_ref[...] * sin
    gq_cos = gq_ref[...] * cos
    gq_sin = gqsw_ref[...] * sin
    gk_cos = gk_ref[...] * cos
    for h in range(N_HEADS):
        qp = qq[:, h * HEAD_PAD:(h + 1) * HEAD_PAD]
        qs = qq[:, (N_HEADS + h) * HEAD_PAD:(N_HEADS + h + 1) * HEAD_PAD]
        rs = lax.rsqrt(jnp.sum(qp * qp, axis=-1, keepdims=True) * inv_w + EPS)
        q_ref[0, h] = _bf((qp * gq_cos + qs * gq_sin) * (rs * scale))
        kp = kv[:, h * HEAD_PAD:(h + 1) * HEAD_PAD] + kr
        rs = lax.rsqrt(jnp.sum(kp * kp, axis=-1, keepdims=True) * inv_w + EPS)
        k_ref[0, h] = _bf((kp * gk_cos + kr_rot) * rs)
    vt_ref[0] = _bf(kv[:, N_HEADS * HEAD_PAD:].T)


def _mla_prep(u, base, cos, sin, qn, wq, kvn, wkv, gq, gqsw, gk, gksw, tm):
    b, t, _ = u.shape
    tab = pl.BlockSpec((tm, HEAD_PAD), lambda i, j: (j, 0))
    hd = pl.BlockSpec((1, N_HEADS, tm, HEAD_PAD), lambda i, j: (i, 0, j, 0))
    consts = (qn, wq, kvn, wkv, gq, gqsw, gk, gksw)
    return pl.pallas_call(
        functools.partial(_mla_prep_kernel, scale=MLA_QK ** -0.5 * float(np.log2(np.e))),
        grid=(b, t // tm),
        in_specs=[_ublk(tm, COL["qa"], base, 256), _ublk(tm, COL["ckv"], base, 128),
                  _ublk(tm, COL["kr"], base, 128), _ublk(tm, COL["kr_sw"], base, 128), tab, tab]
                 + [_full(a, 2) for a in consts],
        out_specs=[hd, hd, pl.BlockSpec((1, 256, tm), lambda i, j: (i, 0, j))],
        out_shape=[jax.ShapeDtypeStruct((b, N_HEADS, t, HEAD_PAD), MXU_DTYPE),
                   jax.ShapeDtypeStruct((b, N_HEADS, t, HEAD_PAD), MXU_DTYPE),
                   jax.ShapeDtypeStruct((b, 256, t), MXU_DTYPE)],
        compiler_params=_cparams(("parallel", "parallel")),
        name="mla_prep",
    )(u, u, u, u, cos, sin, *consts)


def _mla_kv_prep_kernel(ckv_ref, kr_ref, kvn_ref, wkv_ref, gk_ref, k_ref, vt_ref):
    inv_w = 1.0 / MLA_QK
    ckv = ckv_ref[0]
    kv = _dot(_bf(ckv * _rms_scale(ckv) * kvn_ref[...]), wkv_ref[...])
    kr = kr_ref[0]
    for h in range(N_HEADS):
        kp = kv[:, h * HEAD_PAD:(h + 1) * HEAD_PAD] + kr
        rs = lax.rsqrt(jnp.sum(kp * kp, axis=-1, keepdims=True) * inv_w + EPS)
        k_ref[0, h] = _bf(kp * gk_ref[...] * rs)
    vt_ref[0] = _bf(kv[:, N_HEADS * HEAD_PAD:].T)


def _mla_kv_prep(u, base, kvn, wkv, gk, tm):
    b, t, _ = u.shape
    hd = pl.BlockSpec((1, N_HEADS, tm, HEAD_PAD), lambda i, j: (i, 0, j, 0))
    return pl.pallas_call(
        _mla_kv_prep_kernel,
        grid=(b, t // tm),
        in_specs=[_ublk(tm, COL["ckv"], base, 128), _ublk(tm, COL["kr"], base, 128),
                  _full(kvn, 2), _full(wkv, 2), _full(gk, 2)],
        out_specs=[hd, pl.BlockSpec((1, 256, tm), lambda i, j: (i, 0, j))],
        out_shape=[jax.ShapeDtypeStruct((b, N_HEADS, t, HEAD_PAD), MXU_DTYPE),
                   jax.ShapeDtypeStruct((b, 256, t), MXU_DTYPE)],
        compiler_params=_cparams(("parallel", "parallel")),
        name="mla_kv_prep",
    )(u, u, kvn, wkv, gk)


def _attn_kernel(*refs, tk, n_ctx, n_lat):
    if n_lat:
        q_ref, kc_ref, vtc_ref, kl_ref, vtl_ref, o_ref, sa_ref, sb_ref = refs
    else:
        q_ref, kc_ref, vtc_ref, o_ref, sa_ref, sb_ref = refs
    tq = q_ref.shape[2]
    qs = [q_ref[0, h] for h in range(N_HEADS)]

    def produce(k_ref, c, s_ref):
        start = pl.multiple_of(c * tk, tk)
        for h in range(N_HEADS):
            s_ref[h] = _dot_nt(k_ref[0, h, pl.ds(start, tk), :], qs[h])

    def consume(vt_ref, c, s_ref, carry):
        start = pl.multiple_of(c * tk, tk)
        new = []
        for h in range(N_HEADS):
            m, acc = carry[h]
            s = s_ref[h]
            m_new = jnp.maximum(m, jnp.max(s, axis=0, keepdims=True))
            alpha = jnp.exp2(m - m_new)
            p = _bf(jnp.exp2(s - m_new))
            vt = jnp.concatenate([vt_ref[0, h * MLA_V:(h + 1) * MLA_V, pl.ds(start, tk)], ones], axis=0)
            new.append((m_new, alpha * acc + _dot(vt, p)))
        return tuple(new)

    ones = jnp.ones((ONES_ROWS, tk), MXU_DTYPE)
    carry = tuple((jnp.full((1, tq), -jnp.inf, jnp.float32), jnp.zeros((MLA_V + ONES_ROWS, tq), jnp.float32))
                  for _ in range(N_HEADS))
    cur, nxt = sa_ref, sb_ref
    produce(kc_ref, 0, cur)
    for c in range(n_ctx):
        if c + 1 < n_ctx:
            produce(kc_ref, c + 1, nxt)
        elif n_lat:
            produce(kl_ref, 0, nxt)
        carry = consume(vtc_ref, c, cur, carry)
        cur, nxt = nxt, cur
    if n_lat:
        def body(i, carry):
            c = 2 * i
            produce(kl_ref, c + 1, nxt)
            carry = consume(vtl_ref, c, cur, carry)
            produce(kl_ref, c + 2, cur)
            return consume(vtl_ref, c + 1, nxt, carry)

        n_pairs = (n_lat - 1) // 2
        carry = lax.fori_loop(0, n_pairs, body, carry)
        c = 2 * n_pairs
        if n_lat - c == 2:
            produce(kl_ref, c + 1, nxt)
            carry = consume(vtl_ref, c, cur, carry)
            carry = consume(vtl_ref, c + 1, nxt, carry)
        else:
            carry = consume(vtl_ref, c, cur, carry)
    o_t = jnp.concatenate([acc[:MLA_V] / acc[MLA_V:MLA_V + 1] for _, acc in carry], axis=0)
    o_ref[0] = o_t.T


def _attention(q, kc, vtc, kl, vtl, tq, tk):
    b, _, t_q, _ = q.shape
    t_c = kc.shape[2]
    t_l = 0 if kl is None else kl.shape[2]
    assert t_q % tq == 0 and t_c % tk == 0 and t_l % tk == 0
    kv_specs = [pl.BlockSpec((1, N_HEADS, t_c, HEAD_PAD), lambda i, j: (i, 0, 0, 0)),
                pl.BlockSpec((1, 256, t_c), lambda i, j: (i, 0, 0))]
    args = [q, kc, vtc]
    if t_l:
        kv_specs += [pl.BlockSpec((1, N_HEADS, t_l, HEAD_PAD), lambda i, j: (i, 0, 0, 0)),
                     pl.BlockSpec((1, 256, t_l), lambda i, j: (i, 0, 0))]
        args += [kl, vtl]
    return pl.pallas_call(
        functools.partial(_attn_kernel, tk=tk, n_ctx=t_c // tk, n_lat=t_l // tk),
        grid=(b, t_q // tq),
        in_specs=[pl.BlockSpec((1, N_HEADS, tq, HEAD_PAD), lambda i, j: (i, 0, j, 0))] + kv_specs,
        out_specs=pl.BlockSpec((1, tq, 256), lambda i, j: (i, j, 0)),
        out_shape=jax.ShapeDtypeStruct((b, t_q, 256), jnp.float32),
        scratch_shapes=[pltpu.VMEM((N_HEADS, tk, tq), jnp.float32)] * 2,
        compiler_params=_cparams(("parallel", "parallel")),
        name="attention",
    )(*args)


def _pool_kernel(prev_ref, cur_ref, next_ref, w_ref, sc_ref, o_ref, ext_ref, *, tm, n_tok):
    j = pl.program_id(1)
    cur = cur_ref[0]
    ext_ref[8:8 + tm, :] = cur
    ext_ref[0:8, :] = jnp.where(j > 0, prev_ref[0], 0.0)
    ext_ref[8 + tm:16 + tm, :] = jnp.where(j < pl.num_programs(1) - 1, next_ref[0], 0.0)

    def window_sums(c0, widths):
        sh = lambda k: ext_ref[8 + k:8 + k + tm, c0:c0 + LANE]
        acc = sh(-1) + sh(0)
        out = {2: acc}
        for w in (4, 8, 16):
            if w > max(widths):
                break
            for k in list(range(-w // 2, -w // 4)) + list(range(w // 4, w // 2)):
                acc = acc + sh(k)
            out[w] = acc
        return [out[w] for w in widths]

    lane = lax.broadcasted_iota(jnp.int32, (tm, LANE), 1)
    t = j * tm + lax.broadcasted_iota(jnp.int32, (tm, LANE), 0)
    halves = []
    for c0, (wa, wb) in ((0, (2, 4)), (LANE, (8, 16))):
        sa, sb = window_sums(c0, (wa, wb))
        first = lane < POOL_GC
        s = jnp.where(first, sa, sb)
        half = jnp.where(first, wa // 2, wb // 2)
        cnt = jnp.minimum(t + half, n_tok) - jnp.maximum(t - half, 0)
        halves.append(s / cnt.astype(jnp.float32) - cur[:, c0:c0 + LANE])
    p = jnp.concatenate(halves, axis=-1)
    o_ref[0] = _dot(_bf(p), w_ref[...]) * sc_ref[...]


def _pool(u, base, w_bd, scale, tm):
    b, t, _ = u.shape
    cidx = (COL["pool"] - base) // 256
    rb = tm // 8
    last = t // 8 - 1
    return pl.pallas_call(
        functools.partial(_pool_kernel, tm=tm, n_tok=t),
        grid=(b, t // tm),
        in_specs=[pl.BlockSpec((1, 8, 256), lambda i, j: (i, jnp.maximum(j * rb - 1, 0), cidx)),
                  pl.BlockSpec((1, tm, 256), lambda i, j: (i, j, cidx)),
                  pl.BlockSpec((1, 8, 256), lambda i, j: (i, jnp.minimum((j + 1) * rb, last), cidx)),
                  _full(w_bd, 2), _full(scale, 2)],
        out_specs=pl.BlockSpec((1, tm, 256), lambda i, j: (i, j, 0)),
        out_shape=jax.ShapeDtypeStruct((b, t, 256), jnp.float32),
        scratch_shapes=[pltpu.VMEM((tm + 16, 256), jnp.float32)],
        compiler_params=_cparams(("parallel", "parallel")),
        name="pool",
    )(u, u, u, w_bd, scale)


def _head_rms(x, e):
    return lax.rsqrt(_split_dot(x * x, e) * (1.0 / NA_HD) + EPS)


def _na_prep_kernel(q_ref, k_ref, v_ref, gq_ref, gk_ref, e_ref, qo_ref, ko_ref, vo_ref):
    e = e_ref[...]
    q = q_ref[0]
    qo_ref[0] = _bf(q * _head_rms(q, e) * gq_ref[...])
    k = k_ref[0]
    ko_ref[0] = _bf(k * _head_rms(k, e) * gk_ref[...])
    vo_ref[0] = _bf(v_ref[0])


def _na_prep(uq, qcol, u, base, gq, gk, e_mat, tm):
    b, t, _ = u.shape
    tok = pl.BlockSpec((1, tm, 256), lambda i, j: (i, j, 0))
    return pl.pallas_call(
        _na_prep_kernel,
        grid=(b, t // tm),
        in_specs=[pl.BlockSpec((1, tm, 256), lambda i, j: (i, j, qcol)),
                  _ublk(tm, COL["k_na"], base, 256), _ublk(tm, COL["v_na"], base, 256),
                  _full(gq, 2), _full(gk, 2), _full(e_mat, 2)],
        out_specs=[tok, tok, tok],
        out_shape=[jax.ShapeDtypeStruct((b, t, 256), MXU_DTYPE)] * 3,
        compiler_params=_cparams(("parallel", "parallel")),
        name="na_prep",
    )(uq, u, u, gq, gk, e_mat)


def _stack_heads(q):
    qf = q.astype(jnp.float32)
    head = lax.broadcasted_iota(jnp.int32, qf.shape, 1) // NA_HD
    return _bf(jnp.concatenate([jnp.where(head == h, qf, 0.0) for h in range(N_HEADS)], axis=0))


def _unstack_heads(res):
    n = res.shape[0] // N_HEADS
    head = lax.broadcasted_iota(jnp.int32, (n, res.shape[1]), 1) // NA_HD
    out = res[0:n]
    for h in range(1, N_HEADS):
        out = jnp.where(head == h, res[h * n:(h + 1) * n], out)
    return out


def _softmax_pv(scores, values):
    m = functools.reduce(jnp.maximum, [jnp.max(s, axis=-1, keepdims=True) for s in scores])
    ps = [jnp.exp(s - m) for s in scores]
    l = functools.reduce(jnp.add, [jnp.sum(p, axis=-1, keepdims=True) for p in ps])
    acc = functools.reduce(jnp.add, [_dot(_bf(p), v) for p, v in zip(ps, values)])
    return acc / l


def _na_kernel(q_ref, k_ref, v_ref, kc_ref, vc_ref, tbl_ref, o_ref, *, rows, rb):
    jb = pl.program_id(1)
    kc = kc_ref[0]
    vc = vc_ref[0]
    n_loc = NA_WIN_R * GRID_W
    for rr in range(rb):
        r = jb * rb + rr
        rs = jnp.clip(r - NA_WIN_R // 2, 0, rows - NA_WIN_R)
        start = pl.multiple_of(rs * GRID_W, GRID_W)
        qm = _stack_heads(q_ref[0, rr * GRID_W:(rr + 1) * GRID_W, :])
        s_loc = _dot_nt(qm, k_ref[0, pl.ds(start, n_loc), :]) + tbl_ref[r - rs]
        s_ctx = _dot_nt(qm, kc)
        res = _softmax_pv([s_loc, s_ctx], [v_ref[0, pl.ds(start, n_loc), :], vc])
        o_ref[0, rr * GRID_W:(rr + 1) * GRID_W, :] = _unstack_heads(res)


def _na_attention(q, k, v, kc, vc, tbl, rb):
    b, t, _ = q.shape
    tc = kc.shape[1]
    rows = t // GRID_W
    assert rows >= NA_WIN_R and rows % rb == 0
    seq = pl.BlockSpec((1, t, 256), lambda i, j: (i, 0, 0))
    cseq = pl.BlockSpec((1, tc, 256), lambda i, j: (i, 0, 0))
    return pl.pallas_call(
        functools.partial(_na_kernel, rows=rows, rb=rb),
        grid=(b, rows // rb),
        in_specs=[pl.BlockSpec((1, rb * GRID_W, 256), lambda i, j: (i, j, 0)), seq, seq, cseq, cseq,
                  _full(tbl, 2)],
        out_specs=pl.BlockSpec((1, rb * GRID_W, 256), lambda i, j: (i, j, 0)),
        out_shape=jax.ShapeDtypeStruct((b, t, 256), jnp.float32),
        compiler_params=_cparams(("parallel", "parallel")),
        name="na_attention",
    )(q, k, v, kc, vc, tbl)


def _ctx_na_kernel(q_ref, kc_ref, vc_ref, o_ref):
    kc = kc_ref[0]
    vc = vc_ref[0]
    for rr in range(q_ref.shape[1] // GRID_W):
        qm = _stack_heads(q_ref[0, rr * GRID_W:(rr + 1) * GRID_W, :])
        res = _softmax_pv([_dot_nt(qm, kc)], [vc])
        o_ref[0, rr * GRID_W:(rr + 1) * GRID_W, :] = _unstack_heads(res)


def _ctx_na_attention(q, kc, vc):
    b, tc, _ = q.shape
    cseq = pl.BlockSpec((1, tc, 256), lambda i: (i, 0, 0))
    return pl.pallas_call(
        _ctx_na_kernel,
        grid=(b,),
        in_specs=[cseq, cseq, cseq],
        out_specs=cseq,
        out_shape=jax.ShapeDtypeStruct((b, tc, 256), jnp.float32),
        compiler_params=_cparams(("parallel",)),
        name="ctx_na_attention",
    )(q, kc, vc)


def _hg_gates(z, lb):
    e = jnp.exp(-jnp.abs(z))
    r = 1.0 / (1.0 + e)
    pos = z >= 0
    sig = jnp.where(pos, r, e * r)
    sig_neg = jnp.where(pos, e * r, r)
    return jnp.log(lb + (1.0 - lb) * sig), (1.0 - lb) * sig_neg


def _hg_prep_kernel(zf_ref, zb_ref, i_ref, q_ref, lbf_ref, lbb_ref, tl_ref, tu_ref, ta_ref, sel_ref, e_ref,
                    qf_o, kf_o, qb_o, kb_o, i_o, df_o, db_o, oi_o, nat_ref, pm_ref, acc_ref,
                    *, tm, qscale):
    nch = tm // HG_SUB
    q = q_ref[0] * qscale
    xi = i_ref[0]
    i_o[0] = _bf(xi)
    logf_f, kg_f = _hg_gates(zf_ref[0], lbf_ref[...])
    logf_b, kg_b = _hg_gates(zb_ref[0], lbb_ref[...])
    b_f = _split3_dot_left(tl_ref[...], logf_f)
    b_b = _split3_dot_left(tu_ref[...], logf_b)
    tot_f = _split3_dot_left(ta_ref[...], logf_f)
    tot_b = _split3_dot_left(ta_ref[...], logf_b)
    qf_o[0] = _bf(q * jnp.exp(b_f))
    kf_o[0] = _bf(kg_f * jnp.exp(tot_f - b_f))
    qb_o[0] = _bf(q * jnp.exp(b_b))
    kb_o[0] = _bf(kg_b * jnp.exp(tot_b - b_b))
    df_o[0] = jnp.exp(_split3_dot_left(sel_ref[...], logf_f))
    db_o[0] = jnp.exp(_split3_dot_left(sel_ref[...], logf_b))

    for a, val in enumerate((q, kg_f, b_f, kg_b, b_b, xi)):
        for half in range(2):
            nat_ref[a, half] = val[:, half * LANE:(half + 1) * LANE]
            for p in range(HG_SUB):
                pm_ref[a, half, p * nch:(p + 1) * nch, :] = nat_ref[a, half, pl.ds(p, nch, stride=HG_SUB), :]
    e = e_ref[...]
    for half in range(2):
        qp, kfp, bfp, kbp, bbp, ip = [pm_ref[a, half] for a in range(6)]
        acc_ref[...] = _dot(_bf(qp * (kfp + kbp)), e) * ip
        for delta in range(1, HG_SUB):
            off = delta * nch
            n = tm - off
            x = qp[off:] * kfp[:n] * jnp.exp(bfp[off:] - bfp[:n])
            acc_ref[off:, :] += _dot(_bf(x), e) * ip[:n]
            x = qp[:n] * kbp[off:] * jnp.exp(bbp[:n] - bbp[off:])
            acc_ref[:n, :] += _dot(_bf(x), e) * ip[off:]
        for p in range(HG_SUB):
            nat_ref[0, half, pl.ds(p, nch, stride=HG_SUB), :] = acc_ref[p * nch:(p + 1) * nch, :]
        oi_o[0, :, half * LANE:(half + 1) * LANE] = nat_ref[0, half]


def _hg_prep(uq, qcol, u, base, lbf, lbb, consts, e_mat):
    b, t, _ = u.shape
    tm = HG_TILE
    nsub = tm // HG_SUB
    tok = pl.BlockSpec((1, tm, 256), lambda i, j: (i, j, 0))
    dec = pl.BlockSpec((1, nsub, 256), lambda i, j: (i, j, 0))
    tl, tu, ta, sel = consts
    e_half = e_mat[:LANE, :LANE]
    regroup = pltpu.VMEM((6, 2, tm, LANE), jnp.float32)
    return pl.pallas_call(
        functools.partial(_hg_prep_kernel, tm=tm, qscale=HG_DK ** -0.5),
        grid=(b, t // tm),
        in_specs=[_ublk(tm, COL["zF"], base, 256), _ublk(tm, COL["zB"], base, 256),
                  _ublk(tm, COL["i_hg"], base, 256),
                  pl.BlockSpec((1, tm, 256), lambda i, j: (i, j, qcol)),
                  _full(lbf, 2), _full(lbb, 2), _full(tl, 2), _full(tu, 2), _full(ta, 2), _full(sel, 2),
                  _full(e_half, 2)],
        out_specs=[tok, tok, tok, tok, tok, dec, dec, tok],
        out_shape=[jax.ShapeDtypeStruct((b, t, 256), MXU_DTYPE)] * 5
                  + [jax.ShapeDtypeStruct((b, t // HG_SUB, 256), jnp.float32)] * 2
                  + [jax.ShapeDtypeStruct((b, t, 256), jnp.float32)],
        scratch_shapes=[regroup, regroup, pltpu.VMEM((tm, LANE), jnp.float32)],
        compiler_params=_cparams(("parallel", "parallel")),
        name="hg_prep",
    )(u, u, u, uq, lbf, lbb, tl, tu, ta, sel, e_half)


def _hg_scan_kernel(qf_ref, kf_ref, if_ref, df_ref, qb_ref, kb_ref, ib_ref, db_ref, s0f_ref, s0b_ref,
                    of_ref, ob_ref, sf_out, sb_out, sf, sb, *, nsub, nb):
    j = pl.program_id(0)

    @pl.when(j == 0)
    def _():
        sf[...] = s0f_ref[...]
        sb[...] = s0b_ref[...]

    first = lax.broadcasted_iota(jnp.int32, (HG_SUB, LANE), 1) < HG_DK

    def step(state, q_ref, k_ref, i_ref, d_ref, o_ref, bi, c):
        rows = slice(c * HG_SUB, (c + 1) * HG_SUB)
        for p in range(2):
            cols = slice(p * LANE, (p + 1) * LANE)
            s_old = state[bi, p]
            q = q_ref[bi, rows, cols].astype(jnp.float32)
            q2 = _bf(jnp.concatenate([jnp.where(first, q, 0.0), jnp.where(first, 0.0, q)], axis=0))
            r = _dot_nt(q2, _bf(s_old))
            o_ref[bi, rows, cols] = jnp.where(first, r[:HG_SUB], r[HG_SUB:])
            upd = _dot_tn(i_ref[bi, rows, cols], k_ref[bi, rows, cols])
            state[bi, p] = s_old * d_ref[bi, c:c + 1, cols] + upd

    for c in range(nsub):
        for bi in range(nb):
            step(sf, qf_ref, kf_ref, if_ref, df_ref, of_ref, bi, c)
            step(sb, qb_ref, kb_ref, ib_ref, db_ref, ob_ref, bi, nsub - 1 - c)

    @pl.when(j == pl.num_programs(0) - 1)
    def _():
        sf_out[...] = sf[...]
        sb_out[...] = sb[...]


def _hg_scan(qf, kf, qb, kb, xi, df, db, s0f, s0b):
    b, t, _ = qf.shape
    tm = HG_TILE
    n = t // tm
    nsub = tm // HG_SUB
    fwd = pl.BlockSpec((b, tm, 256), lambda j: (0, j, 0))
    bwd = pl.BlockSpec((b, tm, 256), lambda j: (0, n - 1 - j, 0))
    dfwd = pl.BlockSpec((b, nsub, 256), lambda j: (0, j, 0))
    dbwd = pl.BlockSpec((b, nsub, 256), lambda j: (0, n - 1 - j, 0))
    st = pl.BlockSpec((b, 2, LANE, LANE), lambda j: (0, 0, 0, 0))
    return pl.pallas_call(
        functools.partial(_hg_scan_kernel, nsub=nsub, nb=b),
        grid=(n,),
        in_specs=[fwd, fwd, fwd, dfwd, bwd, bwd, bwd, dbwd, st, st],
        out_specs=[fwd, bwd, st, st],
        out_shape=[jax.ShapeDtypeStruct((b, t, 256), jnp.float32)] * 2
                  + [jax.ShapeDtypeStruct((b, 2, LANE, LANE), jnp.float32)] * 2,
        scratch_shapes=[pltpu.VMEM((b, 2, LANE, LANE), jnp.float32)] * 2,
        compiler_params=_cparams(("arbitrary",)),
        name="hg_scan",
    )(qf, kf, xi, df, qb, kb, xi, db, s0f, s0b)


def _merge_kernel(x_ref, a_ref, b_ref, c_ref, of_ref, ob_ref, oi_ref, gc_ref, g_ref, sh_ref, sc_ref, wg_ref,
                  hgn_ref, e_ref, wbr_ref, wo_ref, gate_ref, out_ref):
    d_model = x_ref.shape[2]
    x = x_ref[0]
    h = _bf((x * _rms_scale(x) * g_ref[...]) * (1.0 + sc_ref[0]) + sh_ref[0])
    o = of_ref[0] + ob_ref[0] + oi_ref[0]
    ms = _split_dot(o * o, e_ref[...]) * (1.0 / HG_DK)
    d = o * lax.rsqrt(ms + EPS) * hgn_ref[...] * _sigmoid(gc_ref[0])
    acc = None
    for n, y in enumerate((a_ref[0], b_ref[0], c_ref[0], d)):
        gate = _sigmoid(_dot(h, wg_ref[:, n * d_model:(n + 1) * d_model]))
        term = gate * _dot(_bf(y), wbr_ref[n])
        acc = term if acc is None else acc + term
    out_ref[0] = x + gate_ref[0] * _dot(_bf(acc), wo_ref[...])


def _merge(x, a, bm, cn, o_f, o_b, o_i, u, g1n, w_gate, hgn, e_mat, w_br, w_o, mods, tm):
    b, t, d = x.shape
    tok = pl.BlockSpec((1, tm, 256), lambda i, j: (i, j, 0))
    xs = pl.BlockSpec((1, tm, d), lambda i, j: (i, j, 0))
    return pl.pallas_call(
        _merge_kernel,
        grid=(b, t // tm),
        in_specs=[xs, tok, tok, tok, tok, tok, tok, _ublk(tm, COL["g_cols"], 0, 256),
                  _full(g1n, 2), _mod_spec(mods, 0, d), _mod_spec(mods, 1, d), _full(w_gate, 2),
                  _full(hgn, 2), _full(e_mat, 2), _full(w_br, 2), _full(w_o, 2), _mod_spec(mods, 2, d)],
        out_specs=xs,
        out_shape=jax.ShapeDtypeStruct((b, t, d), jnp.float32),
        compiler_params=_cparams(("parallel", "parallel")),
        name="merge",
    )(x, a, bm, cn, o_f, o_b, o_i, u, g1n, mods, mods, w_gate, hgn, e_mat, w_br, w_o, mods)


def _ffn_kernel(x_ref, g_ref, sh_ref, sc_ref, w1_ref, w2_ref, gate_ref, o_ref, h_ref, acc_ref):
    k = pl.program_id(2)

    @pl.when(k == 0)
    def _():
        x = x_ref[0]
        y = x * _rms_scale(x) * g_ref[...]
        h_ref[...] = _bf(y * (1.0 + sc_ref[0]) + sh_ref[0])

    z = jnp.maximum(_dot(h_ref[...], w1_ref[...]), 0.0)
    part = _dot(_bf(z * z), w2_ref[...])

    @pl.when(k == 0)
    def _():
        acc_ref[...] = part

    @pl.when(k > 0)
    def _():
        acc_ref[...] += part

    @pl.when(k == pl.num_programs(2) - 1)
    def _():
        o_ref[0] = x_ref[0] + gate_ref[0] * acc_ref[...]


def _ffn(x, g, mods, w1, w2, tm, tf):
    b, t, d = x.shape
    f = w1.shape[1]
    xs = pl.BlockSpec((1, tm, d), lambda i, j, k: (i, j, 0))
    return pl.pallas_call(
        _ffn_kernel,
        grid=(b, t // tm, f // tf),
        in_specs=[xs, pl.BlockSpec((1, d), lambda i, j, k: (0, 0)), _mod_spec(mods, 3, d), _mod_spec(mods, 4, d),
                  pl.BlockSpec((d, tf), lambda i, j, k: (0, k)),
                  pl.BlockSpec((tf, d), lambda i, j, k: (k, 0)), _mod_spec(mods, 5, d)],
        out_specs=xs,
        out_shape=jax.ShapeDtypeStruct((b, t, d), jnp.float32),
        scratch_shapes=[pltpu.VMEM((tm, d), MXU_DTYPE), pltpu.VMEM((tm, d), jnp.float32)],
        compiler_params=_cparams(("parallel", "parallel", "arbitrary")),
        name="ffn",
    )(x, g, mods, mods, w1, w2, mods)


def _rope_perm():
    return np.array([(t // 16) * 16 + ((t % 16) + 8) % 16 for t in range(MLA_ROPE)])


def _prep_w_in(w):
    d = w.shape[0]
    o = dict(ckv=0, kr=128, k_na=160, v_na=416, zF=672, zB=928, i_hg=1184, pool=1440, qa=1696, q_na=1952,
             q_hg=2208, g_cols=2464, gates=2720)
    z = lambda n: jnp.zeros((d, n), w.dtype)
    k_rope = w[:, o["kr"]:o["kr"] + MLA_ROPE]
    parts = [w[:, o[name]:o[name] + 256]
             for name in ("pool", "qa", "q_na", "q_hg", "g_cols", "k_na", "v_na", "zF", "zB", "i_hg")]
    parts += [w[:, 0:128], z(64), k_rope, z(32), z(64), k_rope[:, _rope_perm()], z(32), z(128)]
    return _bf(w[:, o["gates"]:]), _bf(jnp.concatenate(parts, axis=1))


def _prep_wq(w):
    r = w.shape[0]
    wh = w.reshape(r, N_HEADS, MLA_QK)
    pre = jnp.pad(wh, ((0, 0), (0, 0), (0, HEAD_PAD - MLA_QK)))
    sw = jnp.pad(wh[:, :, MLA_NOPE:][:, :, _rope_perm()], ((0, 0), (0, 0), (MLA_NOPE, HEAD_PAD - MLA_QK)))
    return _bf(jnp.concatenate([pre.reshape(r, -1), sw.reshape(r, -1)], axis=1))


def _prep_wkv(w):
    r = w.shape[0]
    wh = w.reshape(r, N_HEADS, MLA_NOPE + MLA_V)
    kpart = jnp.pad(wh[:, :, :MLA_NOPE], ((0, 0), (0, 0), (0, HEAD_PAD - MLA_NOPE)))
    return _bf(jnp.concatenate([kpart.reshape(r, -1), wh[:, :, MLA_NOPE:].reshape(r, -1)], axis=1))


def _pad_gain(g):
    plain = jnp.pad(g, (0, HEAD_PAD - MLA_QK))[None]
    sw = jnp.pad(g[MLA_NOPE:][_rope_perm()], (MLA_NOPE, HEAD_PAD - MLA_QK))[None]
    return plain, sw


def _rope_tables(n_tok):
    t = jnp.arange(n_tok)
    pos = jnp.stack([t // GRID_W, t % GRID_W], axis=-1).astype(jnp.float32)
    nf = MLA_ROPE // 4
    inv = ROPE_BASE ** (-jnp.arange(nf, dtype=jnp.float32) / nf)
    ang = pos[:, :, None] * inv
    cos, sin = jnp.cos(ang), jnp.sin(ang)
    cos_t = jnp.concatenate([cos, cos], axis=-1).reshape(n_tok, MLA_ROPE)
    sin_t = jnp.concatenate([-sin, sin], axis=-1).reshape(n_tok, MLA_ROPE)
    ones = jnp.ones((n_tok, MLA_NOPE), jnp.float32)
    zer = jnp.zeros((n_tok, HEAD_PAD - MLA_QK), jnp.float32)
    cos_p = jnp.concatenate([ones, cos_t, zer], axis=-1)
    sin_p = jnp.concatenate([0 * ones, sin_t, zer], axis=-1)
    return cos_p, sin_p


def _plain_tables(n_tok):
    cos_p = jnp.concatenate([jnp.ones((n_tok, MLA_QK), jnp.float32),
                             jnp.zeros((n_tok, HEAD_PAD - MLA_QK), jnp.float32)], axis=-1)
    return cos_p, jnp.zeros((n_tok, HEAD_PAD), jnp.float32)


def _head_indicator(dtype):
    i = np.arange(256) // 64
    return jnp.asarray((i[:, None] == i[None, :]).astype(np.float32), dtype=dtype)


def _na_bias_table(rpb):
    qc = np.arange(GRID_W)[:, None]
    kc = np.arange(GRID_W)[None, :]
    cs = np.clip(qc - NA_WIN_C // 2, 0, GRID_W - NA_WIN_C)
    inwin = (kc >= cs) & (kc < cs + NA_WIN_C)
    dc = np.clip(kc - qc + NA_WIN_C - 1, 0, 2 * NA_WIN_C - 2)
    onehot = (dc[:, :, None] == np.arange(2 * NA_WIN_C - 1)).astype(np.float32)
    t = jnp.einsum('hrd,qkd->hrqk', rpb.astype(jnp.float32), onehot, precision=lax.Precision.HIGHEST)
    t = jnp.where(inwin[None, None], t, NEG)
    tbl = jnp.stack([t[:, NA_WIN_R - 1 - p:2 * NA_WIN_R - 1 - p] for p in range(NA_WIN_R)], axis=0)
    return tbl.transpose(0, 1, 3, 2, 4).reshape(NA_WIN_R, N_HEADS * GRID_W, NA_WIN_R * GRID_W)


def _hg_consts():
    t = np.arange(HG_TILE)
    same = (t[:, None] // HG_SUB) == (t[None, :] // HG_SUB)
    tl = same & (t[None, :] <= t[:, None])
    tu = same & (t[None, :] >= t[:, None])
    sel = (np.arange(HG_TILE // HG_SUB)[:, None] == (t[None, :] // HG_SUB))
    f = lambda m: jnp.asarray(m.astype(np.float32), dtype=MXU_DTYPE)
    return f(tl), f(tu), f(same), f(sel)


def _block_diag(w):
    g, c, d = w.shape
    out = jnp.zeros((g * c, g * d), w.dtype)
    for i in range(g):
        out = out.at[i * c:(i + 1) * c, i * d:(i + 1) * d].set(w[i])
    return out


def kernel(x, c, ctx, c_ctx, ada_w, ada_b, norm1, norm2, w_in, pool_w, pool_scale, mla_q_norm, mla_wq_b, mla_kv_norm, mla_wkv_b, mla_gq, mla_gk, na_gq, na_gk, na_rpb, hg_lb, hg_norm, w_branch, w_out, w_ff1, w_ff2):
    B, L, D = x.shape
    Tc = ctx.shape[1]
    depth = ada_w.shape[0]
    f32 = jnp.float32
    assert L % HG_TILE == 0 and Tc % HG_TILE == 0 and B < 8

    lb_p = jax.nn.softmax(hg_lb.astype(f32), axis=0)
    lb_all = jnp.clip(jnp.cumsum(lb_p, axis=0)[:depth], LB_EPS, 1.0 - LB_EPS)

    cc = jnp.concatenate([c, c_ctx[None], jnp.zeros((8 - B - 1, D), f32)], axis=0)
    mods = _mods(cc, ada_w, ada_b)

    cos_l, sin_l = _rope_tables(L)
    cos_c, sin_c = _plain_tables(Tc)
    e_mat = _head_indicator(MXU_DTYPE)
    hg_consts = _hg_consts()
    zero_state = jnp.zeros((B, 2, LANE, LANE), f32)
    zero_q = jnp.zeros((B, Tc, 256), f32)
    tl = min(512, L)
    t_big = min(1024, L)

    xl, xc = x, ctx
    for l in range(depth):
        ctx_out = l < depth - 1
        mods_l = mods[l, :B].reshape(B, 1, 6 * D)
        mods_c = mods[l, B].reshape(1, 1, 6 * D)
        g1n = norm1[l][None]
        g2n = norm2[l][None]
        w_gate, w_mix = _prep_w_in(w_in[l])
        wq = _prep_wq(mla_wq_b[l])
        wkv = _prep_wkv(mla_wkv_b[l])
        gq, gqsw = _pad_gain(mla_gq[l])
        gk, gksw = _pad_gain(mla_gk[l])
        qn = mla_q_norm[l][None]
        kvn = mla_kv_norm[l][None]
        na_q_gain = (jnp.tile(na_gq[l], N_HEADS) * NA_HD ** -0.5)[None]
        na_k_gain = jnp.tile(na_gk[l], N_HEADS)[None]
        na_tbl = _na_bias_table(na_rpb[l])
        pool_bd = _bf(_block_diag(pool_w[l]))
        pool_sc = pool_scale[l][None]
        hgn = jnp.tile(hg_norm[l], N_HEADS)[None]
        lbF = lb_all[l, 0][None]
        lbB = lb_all[l, 1][None]
        w_br = _bf(w_branch[l])
        w_o = _bf(w_out[l])
        w1 = _bf(w_ff1[l])
        w2 = _bf(w_ff2[l])

        ul = _inproj(xl, g1n, mods_l, w_mix, tl, D_MIX_PAD, f32)
        if ctx_out:
            uc, cbase = _inproj(xc, g1n, mods_c, w_mix, Tc, MIX_TILE, f32), 0
        else:
            uc, cbase = _inproj(xc, g1n, mods_c, w_mix[:, KV_START:], Tc, KV_TILE, f32), KV_START

        q_mla, k_mla, v_mla = _mla_prep(ul, 0, cos_l, sin_l, qn, wq, kvn, wkv, gq, gqsw, gk, gksw, tl)
        if ctx_out:
            q_mla_c, k_mla_c, v_mla_c = _mla_prep(uc, cbase, cos_c, sin_c, qn, wq, kvn, wkv, gq, gqsw, gk, gksw, Tc)
        else:
            k_mla_c, v_mla_c = _mla_kv_prep(uc, cbase, kvn, wkv, gk, Tc)
        b_l = _attention(q_mla, k_mla_c, v_mla_c, k_mla, v_mla, 256, 256)

        q_na, k_na, v_na = _na_prep(ul, COL["q_na"] // 256, ul, 0, na_q_gain, na_k_gain, e_mat, tl)
        if ctx_out:
            q_na_c, k_na_c, v_na_c = _na_prep(uc, COL["q_na"] // 256, uc, cbase, na_q_gain, na_k_gain, e_mat, Tc)
        else:
            _, k_na_c, v_na_c = _na_prep(zero_q, 0, uc, cbase, na_q_gain, na_k_gain, e_mat, Tc)
        c_l = _na_attention(q_na, k_na, v_na, k_na_c, v_na_c, na_tbl, 4)

        a_l = _pool(ul, 0, pool_bd, pool_sc, tl)

        if ctx_out:
            hc = _hg_prep(uc, COL["q_hg"] // 256, uc, cbase, lbF, lbB, hg_consts, e_mat)
        else:
            hc = _hg_prep(zero_q, 0, uc, cbase, lbF, lbB, hg_consts, e_mat)
        oFc, oBc, sFc, sBc = _hg_scan(hc[0], hc[1], hc[2], hc[3], hc[4], hc[5], hc[6], zero_state, zero_state)
        hl = _hg_prep(ul, COL["q_hg"] // 256, ul, 0, lbF, lbB, hg_consts, e_mat)
        oFl, oBl, _, _ = _hg_scan(hl[0], hl[1], hl[2], hl[3], hl[4], hl[5], hl[6], sFc, sBc)

        xl_new = _merge(xl, a_l, b_l, c_l, oFl, oBl, hl[7], ul, g1n, w_gate, hgn, e_mat, w_br, w_o, mods_l, tl)

        if ctx_out:
            a_c = _pool(uc, cbase, pool_bd, pool_sc, Tc)
            b_c = _attention(q_mla_c, k_mla_c, v_mla_c, None, None, Tc, Tc)
            c_c = _ctx_na_attention(q_na_c, k_na_c, v_na_c)
            xc = _merge(xc, a_c, b_c, c_c, oFc, oBc, hc[7], uc, g1n, w_gate, hgn, e_mat, w_br, w_o, mods_c, Tc)
            xc = _ffn(xc, g2n, mods_c, w1, w2, Tc, 1024)

        xl = _ffn(xl_new, g2n, mods_l, w1, w2, tl, 2048)
    return xl
```

```python
import functools

import jax
import jax.numpy as jnp
import numpy as np
from jax import lax
from jax.experimental import pallas as pl
from jax.experimental.pallas import tpu as pltpu

D_MODEL = 1024
GRID_W = 64
EPS = 1e-6
N_BRANCH = 4
BRANCH_W = 256
POOL_GC = 64
N_HEADS = 4
MLA_NOPE = 64
MLA_ROPE = 32
MLA_V = 64
MLA_QK = 96
ROPE_BASE = 10000.0
NA_HD = 64
NA_WIN_R = 8
NA_WIN_C = 16
HG_DK = 64
LB_EPS = 1e-6

MXU_DTYPE = jnp.bfloat16
LANE = 128
HEAD_PAD = 128
ONES_ROWS = 16
VMEM_LIMIT = 48 * 1024 * 1024
HG_SUB = 16
HG_TILE = 256
HG_PREP_TILE = 256
NEG = -1e30

COL = dict(pool=0, qa=256, q_na=512, q_hg=768, g_cols=1024,
           k_na=1280, v_na=1536, zF=1792, zB=2048, i_hg=2304, ckv=2560, kr=2688, kr_sw=2816)
D_MIX_PAD = 3072
KV_START = 1280
MIX_TILE = 1536
KV_TILE = D_MIX_PAD - KV_START


def _cparams(sem):
    return pltpu.CompilerParams(dimension_semantics=sem, vmem_limit_bytes=VMEM_LIMIT)


def _bf(x):
    return x.astype(MXU_DTYPE)


def _dot(a, b):
    return jnp.dot(a, b, preferred_element_type=jnp.float32)


def _dot_nt(a, b):
    return lax.dot_general(a, b, (((1,), (1,)), ((), ())), preferred_element_type=jnp.float32)


def _dot_tn(a, b):
    return lax.dot_general(a, b, (((0,), (0,)), ((), ())), preferred_element_type=jnp.float32)


def _split2(x):
    hi = _bf(x)
    return hi, _bf(x - hi.astype(jnp.float32))


def _split_dot(x, w):
    hi, lo = _split2(x)
    return _dot(hi, w) + _dot(lo, w)


def _split3_dot_left(w, x):
    hi = _bf(x)
    r1 = x - hi.astype(jnp.float32)
    mid = _bf(r1)
    lo = _bf(r1 - mid.astype(jnp.float32))
    return _dot(w, hi) + _dot(w, mid) + _dot(w, lo)


def _sigmoid(x):
    return 0.5 * jnp.tanh(0.5 * x) + 0.5


def _rms_scale(x):
    return lax.rsqrt(jnp.mean(x * x, axis=-1, keepdims=True) + EPS)


def _full(arr, nargs):
    zeros = (0,) * arr.ndim
    return pl.BlockSpec(arr.shape, lambda *_: zeros)


def _ublk(tm, col, base, width):
    idx = (col - base) // width
    return pl.BlockSpec((1, tm, width), lambda i, j: (i, j, idx))


def _mods_kernel(c_ref, w_ref, b_ref, o_ref):
    c = c_ref[...]
    s = _bf(c * _sigmoid(c))
    o_ref[0] = _dot(s, _bf(w_ref[0])) + b_ref[0]


def _mods(cc, ada_w, ada_b):
    depth, d, n = ada_w.shape
    tn = 1536
    return pl.pallas_call(
        _mods_kernel,
        grid=(depth, n // tn),
        in_specs=[pl.BlockSpec((8, d), lambda l, j: (0, 0)),
                  pl.BlockSpec((1, d, tn), lambda l, j: (l, 0, j)),
                  pl.BlockSpec((1, 1, tn), lambda l, j: (l, 0, j))],
        out_specs=pl.BlockSpec((1, 8, tn), lambda l, j: (l, 0, j)),
        out_shape=jax.ShapeDtypeStruct((depth, 8, n), jnp.float32),
        compiler_params=_cparams(("arbitrary", "arbitrary")),
        name="mods",
    )(cc, ada_w, ada_b.reshape(depth, 1, n))


def _inproj_kernel(x_ref, g_ref, sh_ref, sc_ref, w_ref, o_ref, h_ref):
    @pl.when(pl.program_id(2) == 0)
    def _():
        x = x_ref[0]
        y = x * _rms_scale(x) * g_ref[...]
        h_ref[...] = _bf(y * (1.0 + sc_ref[0]) + sh_ref[0])

    o_ref[0] = _dot(h_ref[...], w_ref[...]).astype(o_ref.dtype)


def _mod_spec(mods, idx, d):
    if mods.shape[0] == 1:
        return pl.BlockSpec((1, 1, d), lambda i, *_: (0, 0, idx))
    return pl.BlockSpec((1, 1, d), lambda i, *_: (i, 0, idx))


def _inproj(x, g, mods, w, tm, tn, out_dtype):
    b, t, d = x.shape
    n = w.shape[1]
    assert n % tn == 0
    return pl.pallas_call(
        _inproj_kernel,
        grid=(b, t // tm, n // tn),
        in_specs=[pl.BlockSpec((1, tm, d), lambda i, j, k: (i, j, 0)),
                  pl.BlockSpec((1, d), lambda i, j, k: (0, 0)),
                  _mod_spec(mods, 0, d), _mod_spec(mods, 1, d),
                  pl.BlockSpec((d, tn), lambda i, j, k: (0, k))],
        out_specs=pl.BlockSpec((1, tm, tn), lambda i, j, k: (i, j, k)),
        out_shape=jax.ShapeDtypeStruct((b, t, n), out_dtype),
        scratch_shapes=[pltpu.VMEM((tm, d), MXU_DTYPE)],
        compiler_params=_cparams(("parallel", "parallel", "arbitrary")),
        name="inproj",
    )(x, g, mods, mods, w)


def _mla_prep_kernel(qa_ref, ckv_ref, kr_ref, krsw_ref, cos_ref, sin_ref, qn_ref, wq_ref, kvn_ref, wkv_ref,
                     gq_ref, gqsw_ref, gk_ref, gksw_ref, q_ref, k_ref, vt_ref, *, scale):
    cos = cos_ref[...]
    sin = sin_ref[...]
    inv_w = 1.0 / MLA_QK
    qa = qa_ref[0]
    qq = _dot(_bf(qa * _rms_scale(qa) * qn_ref[...]), wq_ref[...])
    ckv = ckv_ref[0]
    kv = _dot(_bf(ckv * _rms_scale(ckv) * kvn_ref[...]), wkv_ref[...])
    kr = kr_ref[0]
    kr_rot = krsw_ref[0] * gksw_ref[...] * sin
    gq_cos = gq_ref[...] * cos
    gq_sin = gqsw_ref[...] * sin
    gk_cos = gk_ref[...] * cos
    for h in range(N_HEADS):
        qp = qq[:, h * HEAD_PAD:(h + 1) * HEAD_PAD]
        qs = qq[:, (N_HEADS + h) * HEAD_PAD:(N_HEADS + h + 1) * HEAD_PAD]
        rs = lax.rsqrt(jnp.sum(qp * qp, axis=-1, keepdims=True) * inv_w + EPS)
        q_ref[0, h] = _bf((qp * gq_cos + qs * gq_sin) * (rs * scale))
        kp = kv[:, h * HEAD_PAD:(h + 1) * HEAD_PAD] + kr
        rs = lax.rsqrt(jnp.sum(kp * kp, axis=-1, keepdims=True) * inv_w + EPS)
        k_ref[0, h] = _bf((kp * gk_cos + kr_rot) * rs)
    vt_ref[0] = _bf(kv[:, N_HEADS * HEAD_PAD:].T)


def _mla_prep(u, base, cos, sin, qn, wq, kvn, wkv, gq, gqsw, gk, gksw, tm):
    b, t, _ = u.shape
    tab = pl.BlockSpec((tm, HEAD_PAD), lambda i, j: (j, 0))
    hd = pl.BlockSpec((1, N_HEADS, tm, HEAD_PAD), lambda i, j: (i, 0, j, 0))
    consts = (qn, wq, kvn, wkv, gq, gqsw, gk, gksw)
    return pl.pallas_call(
        functools.partial(_mla_prep_kernel, scale=MLA_QK ** -0.5 * float(np.log2(np.e))),
        grid=(b, t // tm),
        in_specs=[_ublk(tm, COL["qa"], base, 256), _ublk(tm, COL["ckv"], base, 128),
                  _ublk(tm, COL["kr"], base, 128), _ublk(tm, COL["kr_sw"], base, 128), tab, tab]
                 + [_full(a, 2) for a in consts],
        out_specs=[hd, hd, pl.BlockSpec((1, 256, tm), lambda i, j: (i, 0, j))],
        out_shape=[jax.ShapeDtypeStruct((b, N_HEADS, t, HEAD_PAD), MXU_DTYPE),
                   jax.ShapeDtypeStruct((b, N_HEADS, t, HEAD_PAD), MXU_DTYPE),
                   jax.ShapeDtypeStruct((b, 256, t), MXU_DTYPE)],
        compiler_params=_cparams(("parallel", "parallel")),
        name="mla_prep",
    )(u, u, u, u, cos, sin, *consts)


def _mla_kv_prep_kernel(ckv_ref, kr_ref, kvn_ref, wkv_ref, gk_ref, k_ref, vt_ref):
    inv_w = 1.0 / MLA_QK
    ckv = ckv_ref[0]
    kv = _dot(_bf(ckv * _rms_scale(ckv) * kvn_ref[...]), wkv_ref[...])
    kr = kr_ref[0]
    for h in range(N_HEADS):
        kp = kv[:, h * HEAD_PAD:(h + 1) * HEAD_PAD] + kr
        rs = lax.rsqrt(jnp.sum(kp * kp, axis=-1, keepdims=True) * inv_w + EPS)
        k_ref[0, h] = _bf(kp * gk_ref[...] * rs)
    vt_ref[0] = _bf(kv[:, N_HEADS * HEAD_PAD:].T)


def _mla_kv_prep(u, base, kvn, wkv, gk, tm):
    b, t, _ = u.shape
    hd = pl.BlockSpec((1, N_HEADS, tm, HEAD_PAD), lambda i, j: (i, 0, j, 0))
    return pl.pallas_call(
        _mla_kv_prep_kernel,
        grid=(b, t // tm),
        in_specs=[_ublk(tm, COL["ckv"], base, 128), _ublk(tm, COL["kr"], base, 128),
                  _full(kvn, 2), _full(wkv, 2), _full(gk, 2)],
        out_specs=[hd, pl.BlockSpec((1, 256, tm), lambda i, j: (i, 0, j))],
        out_shape=[jax.ShapeDtypeStruct((b, N_HEADS, t, HEAD_PAD), MXU_DTYPE),
                   jax.ShapeDtypeStruct((b, 256, t), MXU_DTYPE)],
        compiler_params=_cparams(("parallel", "parallel")),
        name="mla_kv_prep",
    )(u, u, kvn, wkv, gk)


def _attn_kernel(*refs, tk, n_ctx, n_lat):
    if n_lat:
        q_ref, kc_ref, vtc_ref, kl_ref, vtl_ref, o_ref, sa_ref, sb_ref = refs
    else:
        q_ref, kc_ref, vtc_ref, o_ref, sa_ref, sb_ref = refs
    tq = q_ref.shape[2]
    qs = [q_ref[0, h] for h in range(N_HEADS)]

    def produce(k_ref, c, s_ref):
        start = pl.multiple_of(c * tk, tk)
        for h in range(N_HEADS):
            s_ref[h] = _dot_nt(k_ref[0, h, pl.ds(start, tk), :], qs[h])

    def consume(vt_ref, c, s_ref, carry):
        start = pl.multiple_of(c * tk, tk)
        new = []
        for h in range(N_HEADS):
            m, acc = carry[h]
            s = s_ref[h]
            m_new = jnp.maximum(m, jnp.max(s, axis=0, keepdims=True))
            alpha = jnp.exp2(m - m_new)
            p = _bf(jnp.exp2(s - m_new))
            vt = jnp.concatenate([vt_ref[0, h * MLA_V:(h + 1) * MLA_V, pl.ds(start, tk)], ones], axis=0)
            new.append((m_new, alpha * acc + _dot(vt, p)))
        return tuple(new)

    ones = jnp.ones((ONES_ROWS, tk), MXU_DTYPE)
    carry = tuple((jnp.full((1, tq), -jnp.inf, jnp.float32), jnp.zeros((MLA_V + ONES_ROWS, tq), jnp.float32))
                  for _ in range(N_HEADS))
    cur, nxt = sa_ref, sb_ref
    produce(kc_ref, 0, cur)
    for c in range(n_ctx):
        if c + 1 < n_ctx:
            produce(kc_ref, c + 1, nxt)
        elif n_lat:
            produce(kl_ref, 0, nxt)
        carry = consume(vtc_ref, c, cur, carry)
        cur, nxt = nxt, cur
    if n_lat:
        def body(i, carry):
            c = 2 * i
            produce(kl_ref, c + 1, nxt)
            carry = consume(vtl_ref, c, cur, carry)
            produce(kl_ref, c + 2, cur)
            return consume(vtl_ref, c + 1, nxt, carry)

        n_pairs = (n_lat - 1) // 2
        carry = lax.fori_loop(0, n_pairs, body, carry)
        c = 2 * n_pairs
        if n_lat - c == 2:
            produce(kl_ref, c + 1, nxt)
            carry = consume(vtl_ref, c, cur, carry)
            carry = consume(vtl_ref, c + 1, nxt, carry)
        else:
            carry = consume(vtl_ref, c, cur, carry)
    o_t = jnp.concatenate([acc[:MLA_V] / acc[MLA_V:MLA_V + 1] for _, acc in carry], axis=0)
    o_ref[0] = o_t.T


def _attention(q, kc, vtc, kl, vtl, tq, tk):
    b, _, t_q, _ = q.shape
    t_c = kc.shape[2]
    t_l = 0 if kl is None else kl.shape[2]
    assert t_q % tq == 0 and t_c % tk == 0 and t_l % tk == 0
    kv_specs = [pl.BlockSpec((1, N_HEADS, t_c, HEAD_PAD), lambda i, j: (i, 0, 0, 0)),
                pl.BlockSpec((1, 256, t_c), lambda i, j: (i, 0, 0))]
    args = [q, kc, vtc]
    if t_l:
        kv_specs += [pl.BlockSpec((1, N_HEADS, t_l, HEAD_PAD), lambda i, j: (i, 0, 0, 0)),
                     pl.BlockSpec((1, 256, t_l), lambda i, j: (i, 0, 0))]
        args += [kl, vtl]
    return pl.pallas_call(
        functools.partial(_attn_kernel, tk=tk, n_ctx=t_c // tk, n_lat=t_l // tk),
        grid=(b, t_q // tq),
        in_specs=[pl.BlockSpec((1, N_HEADS, tq, HEAD_PAD), lambda i, j: (i, 0, j, 0))] + kv_specs,
        out_specs=pl.BlockSpec((1, tq, 256), lambda i, j: (i, j, 0)),
        out_shape=jax.ShapeDtypeStruct((b, t_q, 256), jnp.float32),
        scratch_shapes=[pltpu.VMEM((N_HEADS, tk, tq), jnp.float32)] * 2,
        compiler_params=_cparams(("parallel", "parallel")),
        name="attention",
    )(*args)


def _pool_kernel(prev_ref, cur_ref, next_ref, w_ref, sc_ref, o_ref, ext_ref, *, tm, n_tok):
    j = pl.program_id(1)
    cur = cur_ref[0]
    ext_ref[8:8 + tm, :] = cur
    ext_ref[0:8, :] = jnp.where(j > 0, prev_ref[0], 0.0)
    ext_ref[8 + tm:16 + tm, :] = jnp.where(j < pl.num_programs(1) - 1, next_ref[0], 0.0)

    def window_sums(c0, widths):
        sh = lambda k: ext_ref[8 + k:8 + k + tm, c0:c0 + LANE]
        acc = sh(-1) + sh(0)
        out = {2: acc}
        for w in (4, 8, 16):
            if w > max(widths):
                break
            for k in list(range(-w // 2, -w // 4)) + list(range(w // 4, w // 2)):
                acc = acc + sh(k)
            out[w] = acc
        return [out[w] for w in widths]

    lane = lax.broadcasted_iota(jnp.int32, (tm, LANE), 1)
    t = j * tm + lax.broadcasted_iota(jnp.int32, (tm, LANE), 0)
    halves = []
    for c0, (wa, wb) in ((0, (2, 4)), (LANE, (8, 16))):
        sa, sb = window_sums(c0, (wa, wb))
        first = lane < POOL_GC
        s = jnp.where(first, sa, sb)
        half = jnp.where(first, wa // 2, wb // 2)
        cnt = jnp.minimum(t + half, n_tok) - jnp.maximum(t - half, 0)
        halves.append(s / cnt.astype(jnp.float32) - cur[:, c0:c0 + LANE])
    p = jnp.concatenate(halves, axis=-1)
    o_ref[0] = _dot(_bf(p), w_ref[...]) * sc_ref[...]


def _pool(u, base, w_bd, scale, tm):
    b, t, _ = u.shape
    cidx = (COL["pool"] - base) // 256
    rb = tm // 8
    last = t // 8 - 1
    return pl.pallas_call(
        functools.partial(_pool_kernel, tm=tm, n_tok=t),
        grid=(b, t // tm),
        in_specs=[pl.BlockSpec((1, 8, 256), lambda i, j: (i, jnp.maximum(j * rb - 1, 0), cidx)),
                  pl.BlockSpec((1, tm, 256), lambda i, j: (i, j, cidx)),
                  pl.BlockSpec((1, 8, 256), lambda i, j: (i, jnp.minimum((j + 1) * rb, last), cidx)),
                  _full(w_bd, 2), _full(scale, 2)],
        out_specs=pl.BlockSpec((1, tm, 256), lambda i, j: (i, j, 0)),
        out_shape=jax.ShapeDtypeStruct((b, t, 256), jnp.float32),
        scratch_shapes=[pltpu.VMEM((tm + 16, 256), jnp.float32)],
        compiler_params=_cparams(("parallel", "parallel")),
        name="pool",
    )(u, u, u, w_bd, scale)


def _head_rms(x, e):
    return lax.rsqrt(_split_dot(x * x, e) * (1.0 / NA_HD) + EPS)


def _na_prep_kernel(q_ref, k_ref, v_ref, gq_ref, gk_ref, e_ref, qo_ref, ko_ref, vo_ref):
    e = e_ref[...]
    q = q_ref[0]
    qo_ref[0] = _bf(q * _head_rms(q, e) * gq_ref[...])
    k = k_ref[0]
    ko_ref[0] = _bf(k * _head_rms(k, e) * gk_ref[...])
    vo_ref[0] = _bf(v_ref[0])


def _na_prep(uq, qcol, u, base, gq, gk, e_mat, tm):
    b, t, _ = u.shape
    tok = pl.BlockSpec((1, tm, 256), lambda i, j: (i, j, 0))
    return pl.pallas_call(
        _na_prep_kernel,
        grid=(b, t // tm),
        in_specs=[pl.BlockSpec((1, tm, 256), lambda i, j: (i, j, qcol)),
                  _ublk(tm, COL["k_na"], base, 256), _ublk(tm, COL["v_na"], base, 256),
                  _full(gq, 2), _full(gk, 2), _full(e_mat, 2)],
        out_specs=[tok, tok, tok],
        out_shape=[jax.ShapeDtypeStruct((b, t, 256), MXU_DTYPE)] * 3,
        compiler_params=_cparams(("parallel", "parallel")),
        name="na_prep",
    )(uq, u, u, gq, gk, e_mat)


def _stack_heads(q):
    qf = q.astype(jnp.float32)
    head = lax.broadcasted_iota(jnp.int32, qf.shape, 1) // NA_HD
    return _bf(jnp.concatenate([jnp.where(head == h, qf, 0.0) for h in range(N_HEADS)], axis=0))


def _unstack_heads(res):
    n = res.shape[0] // N_HEADS
    head = lax.broadcasted_iota(jnp.int32, (n, res.shape[1]), 1) // NA_HD
    out = res[0:n]
    for h in range(1, N_HEADS):
        out = jnp.where(head == h, res[h * n:(h + 1) * n], out)
    return out


def _softmax_pv(scores, values):
    m = functools.reduce(jnp.maximum, [jnp.max(s, axis=-1, keepdims=True) for s in scores])
    ps = [jnp.exp(s - m) for s in scores]
    l = functools.reduce(jnp.add, [jnp.sum(p, axis=-1, keepdims=True) for p in ps])
    acc = functools.reduce(jnp.add, [_dot(_bf(p), v) for p, v in zip(ps, values)])
    return acc / l


def _na_kernel(q_ref, k_ref, v_ref, kc_ref, vc_ref, tbl_ref, o_ref, *, rows, rb):
    jb = pl.program_id(1)
    kc = kc_ref[0]
    vc = vc_ref[0]
    n_loc = NA_WIN_R * GRID_W
    for rr in range(rb):
        r = jb * rb + rr
        rs = jnp.clip(r - NA_WIN_R // 2, 0, rows - NA_WIN_R)
        start = pl.multiple_of(rs * GRID_W, GRID_W)
        qm = _stack_heads(q_ref[0, rr * GRID_W:(rr + 1) * GRID_W, :])
        s_loc = _dot_nt(qm, k_ref[0, pl.ds(start, n_loc), :]) + tbl_ref[r - rs]
        s_ctx = _dot_nt(qm, kc)
        res = _softmax_pv([s_loc, s_ctx], [v_ref[0, pl.ds(start, n_loc), :], vc])
        o_ref[0, rr * GRID_W:(rr + 1) * GRID_W, :] = _unstack_heads(res)


def _na_attention(q, k, v, kc, vc, tbl, rb):
    b, t, _ = q.shape
    tc = kc.shape[1]
    rows = t // GRID_W
    assert rows >= NA_WIN_R and rows % rb == 0
    seq = pl.BlockSpec((1, t, 256), lambda i, j: (i, 0, 0))
    cseq = pl.BlockSpec((1, tc, 256), lambda i, j: (i, 0, 0))
    return pl.pallas_call(
        functools.partial(_na_kernel, rows=rows, rb=rb),
        grid=(b, rows // rb),
        in_specs=[pl.BlockSpec((1, rb * GRID_W, 256), lambda i, j: (i, j, 0)), seq, seq, cseq, cseq,
                  _full(tbl, 2)],
        out_specs=pl.BlockSpec((1, rb * GRID_W, 256), lambda i, j: (i, j, 0)),
        out_shape=jax.ShapeDtypeStruct((b, t, 256), jnp.float32),
        compiler_params=_cparams(("parallel", "parallel")),
        name="na_attention",
    )(q, k, v, kc, vc, tbl)


def _ctx_na_kernel(q_ref, kc_ref, vc_ref, o_ref):
    kc = kc_ref[0]
    vc = vc_ref[0]
    for rr in range(q_ref.shape[1] // GRID_W):
        qm = _stack_heads(q_ref[0, rr * GRID_W:(rr + 1) * GRID_W, :])
        res = _softmax_pv([_dot_nt(qm, kc)], [vc])
        o_ref[0, rr * GRID_W:(rr + 1) * GRID_W, :] = _unstack_heads(res)


def _ctx_na_attention(q, kc, vc):
    b, tc, _ = q.shape
    cseq = pl.BlockSpec((1, tc, 256), lambda i: (i, 0, 0))
    return pl.pallas_call(
        _ctx_na_kernel,
        grid=(b,),
        in_specs=[cseq, cseq, cseq],
        out_specs=cseq,
        out_shape=jax.ShapeDtypeStruct((b, tc, 256), jnp.float32),
        compiler_params=_cparams(("parallel",)),
        name="ctx_na_attention",
    )(q, kc, vc)


def _hg_gates(z, lb):
    e = jnp.exp(-jnp.abs(z))
    r = 1.0 / (1.0 + e)
    pos = z >= 0
    sig = jnp.where(pos, r, e * r)
    sig_neg = jnp.where(pos, e * r, r)
    return jnp.log(lb + (1.0 - lb) * sig), (1.0 - lb) * sig_neg


def _hg_prep_kernel(zf_ref, zb_ref, i_ref, q_ref, lbf_ref, lbb_ref, tl_ref, tu_ref, ta_ref, sel_ref, e_ref,
                    qf_o, kf_o, qb_o, kb_o, i_o, df_o, db_o, oi_o, nat_ref, pm_ref,
                    *, tm, qscale):
    nch = tm // HG_SUB
    q = q_ref[0] * qscale
    xi = i_ref[0]
    i_o[0] = _bf(xi)
    logf_f, kg_f = _hg_gates(zf_ref[0], lbf_ref[...])
    logf_b, kg_b = _hg_gates(zb_ref[0], lbb_ref[...])
    b_f = _split3_dot_left(tl_ref[...], logf_f)
    b_b = _split3_dot_left(tu_ref[...], logf_b)
    tot_f = _split3_dot_left(ta_ref[...], logf_f)
    tot_b = _split3_dot_left(ta_ref[...], logf_b)
    qf_o[0] = _bf(q * jnp.exp(b_f))
    kf_o[0] = _bf(kg_f * jnp.exp(tot_f - b_f))
    qb_o[0] = _bf(q * jnp.exp(b_b))
    kb_o[0] = _bf(kg_b * jnp.exp(tot_b - b_b))
    df_o[0] = jnp.exp(_split3_dot_left(sel_ref[...], logf_f))
    db_o[0] = jnp.exp(_split3_dot_left(sel_ref[...], logf_b))

    log2e = float(np.log2(np.e))
    for a, val in enumerate((q, kg_f, b_f * log2e, kg_b, b_b * log2e, xi)):
        for half in range(2):
            nat_ref[a, half] = val[:, half * LANE:(half + 1) * LANE]
            for p in range(HG_SUB):
                pm_ref[a, half, p * nch:(p + 1) * nch, :] = nat_ref[a, half, pl.ds(p, nch, stride=HG_SUB), :]
    e = e_ref[...]

    def side(q_t, b_t, k_ref, b_ref, i_ref, lo, nblk):
        rows = slice(lo, lo + nblk * nch)
        k3 = k_ref[rows, :].reshape(nblk, nch, LANE)
        b3 = b_ref[rows, :].reshape(nblk, nch, LANE)
        x = q_t[None] * k3 * jnp.exp2(b_t[None] - b3)
        r = _dot(_bf(x.reshape(nblk * nch, LANE)), e) * i_ref[rows, :]
        return jnp.sum(r.reshape(nblk, nch, LANE), axis=0)

    for half in range(2):
        qp, kfp, bfp, kbp, bbp, ip = [pm_ref.at[a, half] for a in range(6)]
        for pt in range(HG_SUB):
            rt = slice(pt * nch, (pt + 1) * nch)
            q_t = qp[rt, :]
            acc = _dot(_bf(q_t * (kfp[rt, :] + kbp[rt, :])), e) * ip[rt, :]
            if pt > 0:
                acc = acc + side(q_t, bfp[rt, :], kfp, bfp, ip, 0, pt)
            if pt < HG_SUB - 1:
                acc = acc + side(q_t, bbp[rt, :], kbp, bbp, ip, (pt + 1) * nch, HG_SUB - 1 - pt)
            nat_ref[0, half, pl.ds(pt, nch, stride=HG_SUB), :] = acc
        oi_o[0, :, half * LANE:(half + 1) * LANE] = nat_ref[0, half]


def _hg_prep(uq, qcol, u, base, lbf, lbb, e_mat):
    b, t, _ = u.shape
    tm = min(HG_PREP_TILE, t)
    nsub = tm // HG_SUB
    tok = pl.BlockSpec((1, tm, 256), lambda i, j: (i, j, 0))
    dec = pl.BlockSpec((1, nsub, 256), lambda i, j: (i, j, 0))
    tl, tu, ta, sel = _hg_consts(tm)
    e_half = e_mat[:LANE, :LANE]
    regroup = pltpu.VMEM((6, 2, tm, LANE), jnp.float32)
    return pl.pallas_call(
        functools.partial(_hg_prep_kernel, tm=tm, qscale=HG_DK ** -0.5),
        grid=(b, t // tm),
        in_specs=[_ublk(tm, COL["zF"], base, 256), _ublk(tm, COL["zB"], base, 256),
                  _ublk(tm, COL["i_hg"], base, 256),
                  pl.BlockSpec((1, tm, 256), lambda i, j: (i, j, qcol)),
                  _full(lbf, 2), _full(lbb, 2), _full(tl, 2), _full(tu, 2), _full(ta, 2), _full(sel, 2),
                  _full(e_half, 2)],
        out_specs=[tok, tok, tok, tok, tok, dec, dec, tok],
        out_shape=[jax.ShapeDtypeStruct((b, t, 256), MXU_DTYPE)] * 5
                  + [jax.ShapeDtypeStruct((b, t // HG_SUB, 256), jnp.float32)] * 2
                  + [jax.ShapeDtypeStruct((b, t, 256), jnp.float32)],
        scratch_shapes=[regroup, regroup],
        compiler_params=_cparams(("parallel", "parallel")),
        name="hg_prep",
    )(u, u, u, uq, lbf, lbb, tl, tu, ta, sel, e_half)


def _hg_scan_kernel(qf_ref, kf_ref, if_ref, df_ref, qb_ref, kb_ref, ib_ref, db_ref, s0f_ref, s0b_ref,
                    of_ref, ob_ref, sf_out, sb_out, sf, sb, *, nsub, nb):
    j = pl.program_id(0)

    @pl.when(j == 0)
    def _():
        sf[...] = s0f_ref[...]
        sb[...] = s0b_ref[...]

    first = lax.broadcasted_iota(jnp.int32, (HG_SUB, LANE), 1) < HG_DK

    def step(state, q_ref, k_ref, i_ref, d_ref, o_ref, bi, c):
        rows = slice(c * HG_SUB, (c + 1) * HG_SUB)
        for p in range(2):
            cols = slice(p * LANE, (p + 1) * LANE)
            s_old = state[bi, p]
            q = q_ref[bi, rows, cols].astype(jnp.float32)
            q2 = _bf(jnp.concatenate([jnp.where(first, q, 0.0), jnp.where(first, 0.0, q)], axis=0))
            r = _dot_nt(q2, _bf(s_old))
            o_ref[bi, rows, cols] = jnp.where(first, r[:HG_SUB], r[HG_SUB:])
            upd = _dot_tn(i_ref[bi, rows, cols], k_ref[bi, rows, cols])
            state[bi, p] = s_old * d_ref[bi, c:c + 1, cols] + upd

    for c in range(nsub):
        for bi in range(nb):
            step(sf, qf_ref, kf_ref, if_ref, df_ref, of_ref, bi, c)
            step(sb, qb_ref, kb_ref, ib_ref, db_ref, ob_ref, bi, nsub - 1 - c)

    @pl.when(j == pl.num_programs(0) - 1)
    def _():
        sf_out[...] = sf[...]
        sb_out[...] = sb[...]


def _hg_scan(qf, kf, qb, kb, xi, df, db, s0f, s0b):
    b, t, _ = qf.shape
    tm = HG_TILE
    n = t // tm
    nsub = tm // HG_SUB
    fwd = pl.BlockSpec((b, tm, 256), lambda j: (0, j, 0))
    bwd = pl.BlockSpec((b, tm, 256), lambda j: (0, n - 1 - j, 0))
    dfwd = pl.BlockSpec((b, nsub, 256), lambda j: (0, j, 0))
    dbwd = pl.BlockSpec((b, nsub, 256), lambda j: (0, n - 1 - j, 0))
    st = pl.BlockSpec((b, 2, LANE, LANE), lambda j: (0, 0, 0, 0))
    return pl.pallas_call(
        functools.partial(_hg_scan_kernel, nsub=nsub, nb=b),
        grid=(n,),
        in_specs=[fwd, fwd, fwd, dfwd, bwd, bwd, bwd, dbwd, st, st],
        out_specs=[fwd, bwd, st, st],
        out_shape=[jax.ShapeDtypeStruct((b, t, 256), jnp.float32)] * 2
                  + [jax.ShapeDtypeStruct((b, 2, LANE, LANE), jnp.float32)] * 2,
        scratch_shapes=[pltpu.VMEM((b, 2, LANE, LANE), jnp.float32)] * 2,
        compiler_params=_cparams(("arbitrary",)),
        name="hg_scan",
    )(qf, kf, xi, df, qb, kb, xi, db, s0f, s0b)


def _merge_kernel(x_ref, a_ref, b_ref, c_ref, of_ref, ob_ref, oi_ref, gc_ref, g_ref, sh_ref, sc_ref, wg_ref,
                  hgn_ref, e_ref, wbr_ref, wo_ref, gate_ref, out_ref):
    d_model = x_ref.shape[2]
    x = x_ref[0]
    h = _bf((x * _rms_scale(x) * g_ref[...]) * (1.0 + sc_ref[0]) + sh_ref[0])
    o = of_ref[0] + ob_ref[0] + oi_ref[0]
    ms = _split_dot(o * o, e_ref[...]) * (1.0 / HG_DK)
    d = o * lax.rsqrt(ms + EPS) * hgn_ref[...] * _sigmoid(gc_ref[0])
    acc = None
    for n, y in enumerate((a_ref[0], b_ref[0], c_ref[0], d)):
        gate = _sigmoid(_dot(h, wg_ref[:, n * d_model:(n + 1) * d_model]))
        term = gate * _dot(_bf(y), wbr_ref[n])
        acc = term if acc is None else acc + term
    out_ref[0] = x + gate_ref[0] * _dot(_bf(acc), wo_ref[...])


def _merge(x, a, bm, cn, o_f, o_b, o_i, u, g1n, w_gate, hgn, e_mat, w_br, w_o, mods, tm):
    b, t, d = x.shape
    tok = pl.BlockSpec((1, tm, 256), lambda i, j: (i, j, 0))
    xs = pl.BlockSpec((1, tm, d), lambda i, j: (i, j, 0))
    return pl.pallas_call(
        _merge_kernel,
        grid=(b, t // tm),
        in_specs=[xs, tok, tok, tok, tok, tok, tok, _ublk(tm, COL["g_cols"], 0, 256),
                  _full(g1n, 2), _mod_spec(mods, 0, d), _mod_spec(mods, 1, d), _full(w_gate, 2),
                  _full(hgn, 2), _full(e_mat, 2), _full(w_br, 2), _full(w_o, 2), _mod_spec(mods, 2, d)],
        out_specs=xs,
        out_shape=jax.ShapeDtypeStruct((b, t, d), jnp.float32),
        compiler_params=_cparams(("parallel", "parallel")),
        name="merge",
    )(x, a, bm, cn, o_f, o_b, o_i, u, g1n, mods, mods, w_gate, hgn, e_mat, w_br, w_o, mods)


def _ffn_kernel(x_ref, g_ref, sh_ref, sc_ref, w1_ref, w2_ref, gate_ref, o_ref, *, tf):
    x = x_ref[0]
    h = _bf((x * _rms_scale(x) * g_ref[...]) * (1.0 + sc_ref[0]) + sh_ref[0])
    acc = None
    for k in range(w1_ref.shape[1] // tf):
        z = jnp.maximum(_dot(h, w1_ref[:, k * tf:(k + 1) * tf]), 0.0)
        part = _dot(_bf(z * z), w2_ref[k * tf:(k + 1) * tf, :])
        acc = part if acc is None else acc + part
    o_ref[0] = x + gate_ref[0] * acc


def _ffn(x, g, mods, w1, w2, tm, tf):
    b, t, d = x.shape
    xs = pl.BlockSpec((1, tm, d), lambda i, j: (i, j, 0))
    resident = lambda a: pl.BlockSpec(a.shape, lambda i, j: (0, 0), pipeline_mode=pl.Buffered(1))
    return pl.pallas_call(
        functools.partial(_ffn_kernel, tf=tf),
        grid=(b, t // tm),
        in_specs=[xs, pl.BlockSpec((1, d), lambda i, j: (0, 0)), _mod_spec(mods, 3, d), _mod_spec(mods, 4, d),
                  resident(w1), resident(w2), _mod_spec(mods, 5, d)],
        out_specs=xs,
        out_shape=jax.ShapeDtypeStruct((b, t, d), jnp.float32),
        compiler_params=_cparams(("parallel", "parallel")),
        name="ffn",
    )(x, g, mods, mods, w1, w2, mods)


def _rope_perm():
    return np.array([(t // 16) * 16 + ((t % 16) + 8) % 16 for t in range(MLA_ROPE)])


def _prep_w_in(w):
    d = w.shape[0]
    o = dict(ckv=0, kr=128, k_na=160, v_na=416, zF=672, zB=928, i_hg=1184, pool=1440, qa=1696, q_na=1952,
             q_hg=2208, g_cols=2464, gates=2720)
    z = lambda n: jnp.zeros((d, n), w.dtype)
    k_rope = w[:, o["kr"]:o["kr"] + MLA_ROPE]
    parts = [w[:, o[name]:o[name] + 256]
             for name in ("pool", "qa", "q_na", "q_hg", "g_cols", "k_na", "v_na", "zF", "zB", "i_hg")]
    parts += [w[:, 0:128], z(64), k_rope, z(32), z(64), k_rope[:, _rope_perm()], z(32), z(128)]
    return _bf(w[:, o["gates"]:]), _bf(jnp.concatenate(parts, axis=1))


def _prep_wq(w):
    r = w.shape[0]
    wh = w.reshape(r, N_HEADS, MLA_QK)
    pre = jnp.pad(wh, ((0, 0), (0, 0), (0, HEAD_PAD - MLA_QK)))
    sw = jnp.pad(wh[:, :, MLA_NOPE:][:, :, _rope_perm()], ((0, 0), (0, 0), (MLA_NOPE, HEAD_PAD - MLA_QK)))
    return _bf(jnp.concatenate([pre.reshape(r, -1), sw.reshape(r, -1)], axis=1))


def _prep_wkv(w):
    r = w.shape[0]
    wh = w.reshape(r, N_HEADS, MLA_NOPE + MLA_V)
    kpart = jnp.pad(wh[:, :, :MLA_NOPE], ((0, 0), (0, 0), (0, HEAD_PAD - MLA_NOPE)))
    return _bf(jnp.concatenate([kpart.reshape(r, -1), wh[:, :, MLA_NOPE:].reshape(r, -1)], axis=1))


def _pad_gain(g):
    plain = jnp.pad(g, (0, HEAD_PAD - MLA_QK))[None]
    sw = jnp.pad(g[MLA_NOPE:][_rope_perm()], (MLA_NOPE, HEAD_PAD - MLA_QK))[None]
    return plain, sw


def _rope_tables(n_tok):
    t = jnp.arange(n_tok)
    pos = jnp.stack([t // GRID_W, t % GRID_W], axis=-1).astype(jnp.float32)
    nf = MLA_ROPE // 4
    inv = ROPE_BASE ** (-jnp.arange(nf, dtype=jnp.float32) / nf)
    ang = pos[:, :, None] * inv
    cos, sin = jnp.cos(ang), jnp.sin(ang)
    cos_t = jnp.concatenate([cos, cos], axis=-1).reshape(n_tok, MLA_ROPE)
    sin_t = jnp.concatenate([-sin, sin], axis=-1).reshape(n_tok, MLA_ROPE)
    ones = jnp.ones((n_tok, MLA_NOPE), jnp.float32)
    zer = jnp.zeros((n_tok, HEAD_PAD - MLA_QK), jnp.float32)
    cos_p = jnp.concatenate([ones, cos_t, zer], axis=-1)
    sin_p = jnp.concatenate([0 * ones, sin_t, zer], axis=-1)
    return cos_p, sin_p


def _plain_tables(n_tok):
    cos_p = jnp.concatenate([jnp.ones((n_tok, MLA_QK), jnp.float32),
                             jnp.zeros((n_tok, HEAD_PAD - MLA_QK), jnp.float32)], axis=-1)
    return cos_p, jnp.zeros((n_tok, HEAD_PAD), jnp.float32)


def _head_indicator(dtype):
    i = np.arange(256) // 64
    return jnp.asarray((i[:, None] == i[None, :]).astype(np.float32), dtype=dtype)


def _na_bias_table(rpb):
    qc = np.arange(GRID_W)[:, None]
    kc = np.arange(GRID_W)[None, :]
    cs = np.clip(qc - NA_WIN_C // 2, 0, GRID_W - NA_WIN_C)
    inwin = (kc >= cs) & (kc < cs + NA_WIN_C)
    dc = np.clip(kc - qc + NA_WIN_C - 1, 0, 2 * NA_WIN_C - 2)
    onehot = (dc[:, :, None] == np.arange(2 * NA_WIN_C - 1)).astype(np.float32)
    t = jnp.einsum('hrd,qkd->hrqk', rpb.astype(jnp.float32), onehot, precision=lax.Precision.HIGHEST)
    t = jnp.where(inwin[None, None], t, NEG)
    tbl = jnp.stack([t[:, NA_WIN_R - 1 - p:2 * NA_WIN_R - 1 - p] for p in range(NA_WIN_R)], axis=0)
    return tbl.transpose(0, 1, 3, 2, 4).reshape(NA_WIN_R, N_HEADS * GRID_W, NA_WIN_R * GRID_W)


def _hg_consts(tm):
    t = np.arange(tm)
    same = (t[:, None] // HG_SUB) == (t[None, :] // HG_SUB)
    tl = same & (t[None, :] <= t[:, None])
    tu = same & (t[None, :] >= t[:, None])
    sel = (np.arange(tm // HG_SUB)[:, None] == (t[None, :] // HG_SUB))
    f = lambda m: jnp.asarray(m.astype(np.float32), dtype=MXU_DTYPE)
    return f(tl), f(tu), f(same), f(sel)


def _block_diag(w):
    g, c, d = w.shape
    out = jnp.zeros((g * c, g * d), w.dtype)
    for i in range(g):
        out = out.at[i * c:(i + 1) * c, i * d:(i + 1) * d].set(w[i])
    return out


def kernel(x, c, ctx, c_ctx, ada_w, ada_b, norm1, norm2, w_in, pool_w, pool_scale, mla_q_norm, mla_wq_b, mla_kv_norm, mla_wkv_b, mla_gq, mla_gk, na_gq, na_gk, na_rpb, hg_lb, hg_norm, w_branch, w_out, w_ff1, w_ff2):
    B, L, D = x.shape
    Tc = ctx.shape[1]
    depth = ada_w.shape[0]
    f32 = jnp.float32
    assert L % HG_TILE == 0 and Tc % HG_TILE == 0 and B < 8

    lb_p = jax.nn.softmax(hg_lb.astype(f32), axis=0)
    lb_all = jnp.clip(jnp.cumsum(lb_p, axis=0)[:depth], LB_EPS, 1.0 - LB_EPS)

    cc = jnp.concatenate([c, c_ctx[None], jnp.zeros((8 - B - 1, D), f32)], axis=0)
    mods = _mods(cc, ada_w, ada_b)

    cos_l, sin_l = _rope_tables(L)
    cos_c, sin_c = _plain_tables(Tc)
    e_mat = _head_indicator(MXU_DTYPE)
    zero_state = jnp.zeros((B, 2, LANE, LANE), f32)
    zero_q = jnp.zeros((B, Tc, 256), f32)
    tl = min(512, L)
    t_big = min(1024, L)

    xl, xc = x, ctx
    for l in range(depth):
        ctx_out = l < depth - 1
        mods_l = mods[l, :B].reshape(B, 1, 6 * D)
        mods_c = mods[l, B].reshape(1, 1, 6 * D)
        g1n = norm1[l][None]
        g2n = norm2[l][None]
        w_gate, w_mix = _prep_w_in(w_in[l])
        wq = _prep_wq(mla_wq_b[l])
        wkv = _prep_wkv(mla_wkv_b[l])
        gq, gqsw = _pad_gain(mla_gq[l])
        gk, gksw = _pad_gain(mla_gk[l])
        qn = mla_q_norm[l][None]
        kvn = mla_kv_norm[l][None]
        na_q_gain = (jnp.tile(na_gq[l], N_HEADS) * NA_HD ** -0.5)[None]
        na_k_gain = jnp.tile(na_gk[l], N_HEADS)[None]
        na_tbl = _na_bias_table(na_rpb[l])
        pool_bd = _bf(_block_diag(pool_w[l]))
        pool_sc = pool_scale[l][None]
        hgn = jnp.tile(hg_norm[l], N_HEADS)[None]
        lbF = lb_all[l, 0][None]
        lbB = lb_all[l, 1][None]
        w_br = _bf(w_branch[l])
        w_o = _bf(w_out[l])
        w1 = _bf(w_ff1[l])
        w2 = _bf(w_ff2[l])

        ul = _inproj(xl, g1n, mods_l, w_mix, tl, D_MIX_PAD, f32)
        if ctx_out:
            uc, cbase = _inproj(xc, g1n, mods_c, w_mix, Tc, MIX_TILE, f32), 0
        else:
            uc, cbase = _inproj(xc, g1n, mods_c, w_mix[:, KV_START:], Tc, KV_TILE, f32), KV_START

        q_mla, k_mla, v_mla = _mla_prep(ul, 0, cos_l, sin_l, qn, wq, kvn, wkv, gq, gqsw, gk, gksw, tl)
        if ctx_out:
            q_mla_c, k_mla_c, v_mla_c = _mla_prep(uc, cbase, cos_c, sin_c, qn, wq, kvn, wkv, gq, gqsw, gk, gksw, Tc)
        else:
            k_mla_c, v_mla_c = _mla_kv_prep(uc, cbase, kvn, wkv, gk, Tc)
        b_l = _attention(q_mla, k_mla_c, v_mla_c, k_mla, v_mla, tl, 256)

        q_na, k_na, v_na = _na_prep(ul, COL["q_na"] // 256, ul, 0, na_q_gain, na_k_gain, e_mat, tl)
        if ctx_out:
            q_na_c, k_na_c, v_na_c = _na_prep(uc, COL["q_na"] // 256, uc, cbase, na_q_gain, na_k_gain, e_mat, Tc)
        else:
            _, k_na_c, v_na_c = _na_prep(zero_q, 0, uc, cbase, na_q_gain, na_k_gain, e_mat, Tc)
        c_l = _na_attention(q_na, k_na, v_na, k_na_c, v_na_c, na_tbl, 4)

        a_l = _pool(ul, 0, pool_bd, pool_sc, tl)

        if ctx_out:
            hc = _hg_prep(uc, COL["q_hg"] // 256, uc, cbase, lbF, lbB, e_mat)
        else:
            hc = _hg_prep(zero_q, 0, uc, cbase, lbF, lbB, e_mat)
        oFc, oBc, sFc, sBc = _hg_scan(hc[0], hc[1], hc[2], hc[3], hc[4], hc[5], hc[6], zero_state, zero_state)
        hl = _hg_prep(ul, COL["q_hg"] // 256, ul, 0, lbF, lbB, e_mat)
        oFl, oBl, _, _ = _hg_scan(hl[0], hl[1], hl[2], hl[3], hl[4], hl[5], hl[6], sFc, sBc)

        xl_new = _merge(xl, a_l, b_l, c_l, oFl, oBl, hl[7], ul, g1n, w_gate, hgn, e_mat, w_br, w_o, mods_l, tl)

        if ctx_out:
            a_c = _pool(uc, cbase, pool_bd, pool_sc, Tc)
            b_c = _attention(q_mla_c, k_mla_c, v_mla_c, None, None, Tc, Tc)
            c_c = _ctx_na_attention(q_na_c, k_na_c, v_na_c)
            xc = _merge(xc, a_c, b_c, c_c, oFc, oBc, hc[7], uc, g1n, w_gate, hgn, e_mat, w_br, w_o, mods_c, Tc)
            xc = _ffn(xc, g2n, mods_c, w1, w2, Tc, 1024)

        xl = _ffn(xl_new, g2n, mods_l, w1, w2, tl, 2048)
    return xl
```

```python
import functools

import jax
import jax.numpy as jnp
import numpy as np
from jax import lax
from jax.experimental import pallas as pl
from jax.experimental.pallas import tpu as pltpu

D_MODEL = 1024
GRID_W = 64
EPS = 1e-6
N_BRANCH = 4
BRANCH_W = 256
POOL_GC = 64
N_HEADS = 4
MLA_NOPE = 64
MLA_ROPE = 32
MLA_V = 64
MLA_QK = 96
ROPE_BASE = 10000.0
NA_HD = 64
NA_WIN_R = 8
NA_WIN_C = 16
HG_DK = 64
LB_EPS = 1e-6

MXU_DTYPE = jnp.bfloat16
LANE = 128
HEAD_PAD = 128
ONES_ROWS = 16
VMEM_LIMIT = 48 * 1024 * 1024
HG_SUB = 16
HG_TILE = 256
HG_PREP_TILE = 256
NEG = -1e30

COL = dict(pool=0, qa=256, q_na=512, q_hg=768, g_cols=1024,
           k_na=1280, v_na=1536, zF=1792, zB=2048, i_hg=2304, ckv=2560, kr=2688, kr_sw=2816)
D_MIX_PAD = 3072
KV_START = 1280
MIX_TILE = 1536
KV_TILE = D_MIX_PAD - KV_START


def _cparams(sem):
    return pltpu.CompilerParams(dimension_semantics=sem, vmem_limit_bytes=VMEM_LIMIT)


def _bf(x):
    return x.astype(MXU_DTYPE)


def _dot(a, b):
    return jnp.dot(a, b, preferred_element_type=jnp.float32)


def _dot_nt(a, b):
    return lax.dot_general(a, b, (((1,), (1,)), ((), ())), preferred_element_type=jnp.float32)


def _dot_tn(a, b):
    return lax.dot_general(a, b, (((0,), (0,)), ((), ())), preferred_element_type=jnp.float32)


def _split2(x):
    hi = _bf(x)
    return hi, _bf(x - hi.astype(jnp.float32))


def _split_dot(x, w):
    hi, lo = _split2(x)
    return _dot(hi, w) + _dot(lo, w)


def _split3_dot_left(w, x):
    hi = _bf(x)
    r1 = x - hi.astype(jnp.float32)
    mid = _bf(r1)
    lo = _bf(r1 - mid.astype(jnp.float32))
    return _dot(w, hi) + _dot(w, mid) + _dot(w, lo)


def _sigmoid(x):
    return 0.5 * jnp.tanh(0.5 * x) + 0.5


def _rms_scale(x):
    return lax.rsqrt(jnp.mean(x * x, axis=-1, keepdims=True) + EPS)


def _full(arr, nargs):
    zeros = (0,) * arr.ndim
    return pl.BlockSpec(arr.shape, lambda *_: zeros)


def _ublk(tm, col, base, width):
    idx = (col - base) // width
    return pl.BlockSpec((1, tm, width), lambda i, j: (i, j, idx))


def _mods_kernel(c_ref, w_ref, b_ref, o_ref):
    c = c_ref[...]
    s = _bf(c * _sigmoid(c))
    o_ref[0] = _dot(s, _bf(w_ref[0])) + b_ref[0]


def _mods(cc, ada_w, ada_b):
    depth, d, n = ada_w.shape
    tn = 1536
    return pl.pallas_call(
        _mods_kernel,
        grid=(depth, n // tn),
        in_specs=[pl.BlockSpec((8, d), lambda l, j: (0, 0)),
                  pl.BlockSpec((1, d, tn), lambda l, j: (l, 0, j)),
                  pl.BlockSpec((1, 1, tn), lambda l, j: (l, 0, j))],
        out_specs=pl.BlockSpec((1, 8, tn), lambda l, j: (l, 0, j)),
        out_shape=jax.ShapeDtypeStruct((depth, 8, n), jnp.float32),
        compiler_params=_cparams(("arbitrary", "arbitrary")),
        name="mods",
    )(cc, ada_w, ada_b.reshape(depth, 1, n))


def _inproj_kernel(x_ref, g_ref, sh_ref, sc_ref, w_ref, o_ref, h_ref):
    @pl.when(pl.program_id(2) == 0)
    def _():
        x = x_ref[0]
        y = x * _rms_scale(x) * g_ref[...]
        h_ref[...] = _bf(y * (1.0 + sc_ref[0]) + sh_ref[0])

    o_ref[0] = _dot(h_ref[...], w_ref[...]).astype(o_ref.dtype)


def _mod_spec(mods, idx, d):
    if mods.shape[0] == 1:
        return pl.BlockSpec((1, 1, d), lambda i, *_: (0, 0, idx))
    return pl.BlockSpec((1, 1, d), lambda i, *_: (i, 0, idx))


def _inproj(x, g, mods, w, tm, tn, out_dtype):
    b, t, d = x.shape
    n = w.shape[1]
    assert n % tn == 0
    return pl.pallas_call(
        _inproj_kernel,
        grid=(b, t // tm, n // tn),
        in_specs=[pl.BlockSpec((1, tm, d), lambda i, j, k: (i, j, 0)),
                  pl.BlockSpec((1, d), lambda i, j, k: (0, 0)),
                  _mod_spec(mods, 0, d), _mod_spec(mods, 1, d),
                  pl.BlockSpec((d, tn), lambda i, j, k: (0, k))],
        out_specs=pl.BlockSpec((1, tm, tn), lambda i, j, k: (i, j, k)),
        out_shape=jax.ShapeDtypeStruct((b, t, n), out_dtype),
        scratch_shapes=[pltpu.VMEM((tm, d), MXU_DTYPE)],
        compiler_params=_cparams(("parallel", "parallel", "arbitrary")),
        name="inproj",
    )(x, g, mods, mods, w)


def _mla_prep_kernel(qa_ref, ckv_ref, kr_ref, krsw_ref, cos_ref, sin_ref, qn_ref, wq_ref, kvn_ref, wkv_ref,
                     gq_ref, gqsw_ref, gk_ref, gksw_ref, q_ref, k_ref, vt_ref, *, scale):
    cos = cos_ref[...]
    sin = sin_ref[...]
    inv_w = 1.0 / MLA_QK
    qa = qa_ref[0]
    qq = _dot(_bf(qa * _rms_scale(qa) * qn_ref[...]), wq_ref[...])
    ckv = ckv_ref[0]
    kv = _dot(_bf(ckv * _rms_scale(ckv) * kvn_ref[...]), wkv_ref[...])
    kr = kr_ref[0]
    kr_rot = krsw_ref[0] * gksw_ref[...] * sin
    gq_cos = gq_ref[...] * cos
    gq_sin = gqsw_ref[...] * sin
    gk_cos = gk_ref[...] * cos
    for h in range(N_HEADS):
        qp = qq[:, h * HEAD_PAD:(h + 1) * HEAD_PAD]
        qs = qq[:, (N_HEADS + h) * HEAD_PAD:(N_HEADS + h + 1) * HEAD_PAD]
        rs = lax.rsqrt(jnp.sum(qp * qp, axis=-1, keepdims=True) * inv_w + EPS)
        q_ref[0, h] = _bf((qp * gq_cos + qs * gq_sin) * (rs * scale))
        kp = kv[:, h * HEAD_PAD:(h + 1) * HEAD_PAD] + kr
        rs = lax.rsqrt(jnp.sum(kp * kp, axis=-1, keepdims=True) * inv_w + EPS)
        k_ref[0, h] = _bf((kp * gk_cos + kr_rot) * rs)
    vt_ref[0] = _bf(kv[:, N_HEADS * HEAD_PAD:].T)


def _mla_prep(u, base, cos, sin, qn, wq, kvn, wkv, gq, gqsw, gk, gksw, tm):
    b, t, _ = u.shape
    tab = pl.BlockSpec((tm, HEAD_PAD), lambda i, j: (j, 0))
    hd = pl.BlockSpec((1, N_HEADS, tm, HEAD_PAD), lambda i, j: (i, 0, j, 0))
    consts = (qn, wq, kvn, wkv, gq, gqsw, gk, gksw)
    return pl.pallas_call(
        functools.partial(_mla_prep_kernel, scale=MLA_QK ** -0.5 * float(np.log2(np.e))),
        grid=(b, t // tm),
        in_specs=[_ublk(tm, COL["qa"], base, 256), _ublk(tm, COL["ckv"], base, 128),
                  _ublk(tm, COL["kr"], base, 128), _ublk(tm, COL["kr_sw"], base, 128), tab, tab]
                 + [_full(a, 2) for a in consts],
        out_specs=[hd, hd, pl.BlockSpec((1, 256, tm), lambda i, j: (i, 0, j))],
        out_shape=[jax.ShapeDtypeStruct((b, N_HEADS, t, HEAD_PAD), MXU_DTYPE),
                   jax.ShapeDtypeStruct((b, N_HEADS, t, HEAD_PAD), MXU_DTYPE),
                   jax.ShapeDtypeStruct((b, 256, t), MXU_DTYPE)],
        compiler_params=_cparams(("parallel", "parallel")),
        name="mla_prep",
    )(u, u, u, u, cos, sin, *consts)


def _mla_kv_prep_kernel(ckv_ref, kr_ref, kvn_ref, wkv_ref, gk_ref, k_ref, vt_ref):
    inv_w = 1.0 / MLA_QK
    ckv = ckv_ref[0]
    kv = _dot(_bf(ckv * _rms_scale(ckv) * kvn_ref[...]), wkv_ref[...])
    kr = kr_ref[0]
    for h in range(N_HEADS):
        kp = kv[:, h * HEAD_PAD:(h + 1) * HEAD_PAD] + kr
        rs = lax.rsqrt(jnp.sum(kp * kp, axis=-1, keepdims=True) * inv_w + EPS)
        k_ref[0, h] = _bf(kp * gk_ref[...] * rs)
    vt_ref[0] = _bf(kv[:, N_HEADS * HEAD_PAD:].T)


def _mla_kv_prep(u, base, kvn, wkv, gk, tm):
    b, t, _ = u.shape
    hd = pl.BlockSpec((1, N_HEADS, tm, HEAD_PAD), lambda i, j: (i, 0, j, 0))
    return pl.pallas_call(
        _mla_kv_prep_kernel,
        grid=(b, t // tm),
        in_specs=[_ublk(tm, COL["ckv"], base, 128), _ublk(tm, COL["kr"], base, 128),
                  _full(kvn, 2), _full(wkv, 2), _full(gk, 2)],
        out_specs=[hd, pl.BlockSpec((1, 256, tm), lambda i, j: (i, 0, j))],
        out_shape=[jax.ShapeDtypeStruct((b, N_HEADS, t, HEAD_PAD), MXU_DTYPE),
                   jax.ShapeDtypeStruct((b, 256, t), MXU_DTYPE)],
        compiler_params=_cparams(("parallel", "parallel")),
        name="mla_kv_prep",
    )(u, u, kvn, wkv, gk)


def _attn_kernel(*refs, tkc, tk, n_ctx, n_lat):
    if n_lat:
        q_ref, kc_ref, vtc_ref, kl_ref, vtl_ref, o_ref, sa_ref, sb_ref = refs
    else:
        q_ref, kc_ref, vtc_ref, o_ref, sa_ref, sb_ref = refs
    tq = q_ref.shape[2]
    qs = [q_ref[0, h] for h in range(N_HEADS)]

    def produce(k_ref, c, s_ref, n):
        start = pl.multiple_of(c * n, n)
        for h in range(N_HEADS):
            s_ref[h, :n] = _dot_nt(k_ref[0, h, pl.ds(start, n), :], qs[h])

    def consume(vt_ref, c, s_ref, carry, n):
        start = pl.multiple_of(c * n, n)
        ones = jnp.ones((ONES_ROWS, n), MXU_DTYPE)
        new = []
        for h in range(N_HEADS):
            m, acc = carry[h]
            s = s_ref[h, :n]
            m_new = jnp.maximum(m, jnp.max(s, axis=0, keepdims=True))
            alpha = jnp.exp2(m - m_new)
            p = _bf(jnp.exp2(s - m_new))
            vt = jnp.concatenate([vt_ref[0, h * MLA_V:(h + 1) * MLA_V, pl.ds(start, n)], ones], axis=0)
            new.append((m_new, alpha * acc + _dot(vt, p)))
        return tuple(new)

    carry = tuple((jnp.full((1, tq), -jnp.inf, jnp.float32), jnp.zeros((MLA_V + ONES_ROWS, tq), jnp.float32))
                  for _ in range(N_HEADS))
    cur, nxt = sa_ref, sb_ref
    produce(kc_ref, 0, cur, tkc)
    for c in range(n_ctx):
        if c + 1 < n_ctx:
            produce(kc_ref, c + 1, nxt, tkc)
        elif n_lat:
            produce(kl_ref, 0, nxt, tk)
        carry = consume(vtc_ref, c, cur, carry, tkc)
        cur, nxt = nxt, cur
    if n_lat:
        def body(i, carry):
            c = 2 * i
            produce(kl_ref, c + 1, nxt, tk)
            carry = consume(vtl_ref, c, cur, carry, tk)
            produce(kl_ref, c + 2, cur, tk)
            return consume(vtl_ref, c + 1, nxt, carry, tk)

        n_pairs = (n_lat - 1) // 2
        carry = lax.fori_loop(0, n_pairs, body, carry)
        c = 2 * n_pairs
        if n_lat - c == 2:
            produce(kl_ref, c + 1, nxt, tk)
            carry = consume(vtl_ref, c, cur, carry, tk)
            carry = consume(vtl_ref, c + 1, nxt, carry, tk)
        else:
            carry = consume(vtl_ref, c, cur, carry, tk)
    o_t = jnp.concatenate([acc[:MLA_V] / acc[MLA_V:MLA_V + 1] for _, acc in carry], axis=0)
    o_ref[0] = o_t.T


def _attention(q, kc, vtc, kl, vtl, tq, tkc, tk):
    b, _, t_q, _ = q.shape
    t_c = kc.shape[2]
    t_l = 0 if kl is None else kl.shape[2]
    assert t_q % tq == 0 and t_c % tkc == 0 and t_l % tk == 0
    kv_specs = [pl.BlockSpec((1, N_HEADS, t_c, HEAD_PAD), lambda i, j: (i, 0, 0, 0)),
                pl.BlockSpec((1, 256, t_c), lambda i, j: (i, 0, 0))]
    args = [q, kc, vtc]
    if t_l:
        kv_specs += [pl.BlockSpec((1, N_HEADS, t_l, HEAD_PAD), lambda i, j: (i, 0, 0, 0)),
                     pl.BlockSpec((1, 256, t_l), lambda i, j: (i, 0, 0))]
        args += [kl, vtl]
    return pl.pallas_call(
        functools.partial(_attn_kernel, tkc=tkc, tk=tk, n_ctx=t_c // tkc, n_lat=t_l // tk),
        grid=(b, t_q // tq),
        in_specs=[pl.BlockSpec((1, N_HEADS, tq, HEAD_PAD), lambda i, j: (i, 0, j, 0))] + kv_specs,
        out_specs=pl.BlockSpec((1, tq, 256), lambda i, j: (i, j, 0)),
        out_shape=jax.ShapeDtypeStruct((b, t_q, 256), jnp.float32),
        scratch_shapes=[pltpu.VMEM((N_HEADS, max(tkc, tk), tq), jnp.float32)] * 2,
        compiler_params=_cparams(("parallel", "parallel")),
        name="attention",
    )(*args)


def _pool_kernel(prev_ref, cur_ref, next_ref, w_ref, sc_ref, o_ref, ext_ref, *, tm, n_tok):
    j = pl.program_id(1)
    cur = cur_ref[0]
    ext_ref[8:8 + tm, :] = cur
    ext_ref[0:8, :] = jnp.where(j > 0, prev_ref[0], 0.0)
    ext_ref[8 + tm:16 + tm, :] = jnp.where(j < pl.num_programs(1) - 1, next_ref[0], 0.0)

    def window_sums(c0, widths):
        sh = lambda k: ext_ref[8 + k:8 + k + tm, c0:c0 + LANE]
        acc = sh(-1) + sh(0)
        out = {2: acc}
        for w in (4, 8, 16):
            if w > max(widths):
                break
            for k in list(range(-w // 2, -w // 4)) + list(range(w // 4, w // 2)):
                acc = acc + sh(k)
            out[w] = acc
        return [out[w] for w in widths]

    lane = lax.broadcasted_iota(jnp.int32, (tm, LANE), 1)
    t = j * tm + lax.broadcasted_iota(jnp.int32, (tm, LANE), 0)
    halves = []
    for c0, (wa, wb) in ((0, (2, 4)), (LANE, (8, 16))):
        sa, sb = window_sums(c0, (wa, wb))
        first = lane < POOL_GC
        s = jnp.where(first, sa, sb)
        half = jnp.where(first, wa // 2, wb // 2)
        cnt = jnp.minimum(t + half, n_tok) - jnp.maximum(t - half, 0)
        halves.append(s / cnt.astype(jnp.float32) - cur[:, c0:c0 + LANE])
    p = jnp.concatenate(halves, axis=-1)
    o_ref[0] = _dot(_bf(p), w_ref[...]) * sc_ref[...]


def _pool(u, base, w_bd, scale, tm):
    b, t, _ = u.shape
    cidx = (COL["pool"] - base) // 256
    rb = tm // 8
    last = t // 8 - 1
    return pl.pallas_call(
        functools.partial(_pool_kernel, tm=tm, n_tok=t),
        grid=(b, t // tm),
        in_specs=[pl.BlockSpec((1, 8, 256), lambda i, j: (i, jnp.maximum(j * rb - 1, 0), cidx)),
                  pl.BlockSpec((1, tm, 256), lambda i, j: (i, j, cidx)),
                  pl.BlockSpec((1, 8, 256), lambda i, j: (i, jnp.minimum((j + 1) * rb, last), cidx)),
                  _full(w_bd, 2), _full(scale, 2)],
        out_specs=pl.BlockSpec((1, tm, 256), lambda i, j: (i, j, 0)),
        out_shape=jax.ShapeDtypeStruct((b, t, 256), jnp.float32),
        scratch_shapes=[pltpu.VMEM((tm + 16, 256), jnp.float32)],
        compiler_params=_cparams(("parallel", "parallel")),
        name="pool",
    )(u, u, u, w_bd, scale)


def _head_rms(x, e):
    return lax.rsqrt(_split_dot(x * x, e) * (1.0 / NA_HD) + EPS)


def _na_prep_kernel(q_ref, k_ref, v_ref, gq_ref, gk_ref, e_ref, qo_ref, ko_ref, vo_ref):
    e = e_ref[...]
    q = q_ref[0]
    qo_ref[0] = _bf(q * _head_rms(q, e) * gq_ref[...])
    k = k_ref[0]
    ko_ref[0] = _bf(k * _head_rms(k, e) * gk_ref[...])
    vo_ref[0] = _bf(v_ref[0])


def _na_prep(uq, qcol, u, base, gq, gk, e_mat, tm):
    b, t, _ = u.shape
    tok = pl.BlockSpec((1, tm, 256), lambda i, j: (i, j, 0))
    return pl.pallas_call(
        _na_prep_kernel,
        grid=(b, t // tm),
        in_specs=[pl.BlockSpec((1, tm, 256), lambda i, j: (i, j, qcol)),
                  _ublk(tm, COL["k_na"], base, 256), _ublk(tm, COL["v_na"], base, 256),
                  _full(gq, 2), _full(gk, 2), _full(e_mat, 2)],
        out_specs=[tok, tok, tok],
        out_shape=[jax.ShapeDtypeStruct((b, t, 256), MXU_DTYPE)] * 3,
        compiler_params=_cparams(("parallel", "parallel")),
        name="na_prep",
    )(uq, u, u, gq, gk, e_mat)


def _stack_heads(q):
    qf = q.astype(jnp.float32)
    head = lax.broadcasted_iota(jnp.int32, qf.shape, 1) // NA_HD
    return _bf(jnp.concatenate([jnp.where(head == h, qf, 0.0) for h in range(N_HEADS)], axis=0))


def _unstack_heads(res):
    n = res.shape[0] // N_HEADS
    head = lax.broadcasted_iota(jnp.int32, (n, res.shape[1]), 1) // NA_HD
    out = res[0:n]
    for h in range(1, N_HEADS):
        out = jnp.where(head == h, res[h * n:(h + 1) * n], out)
    return out


def _softmax_pv(scores, values):
    m = functools.reduce(jnp.maximum, [jnp.max(s, axis=-1, keepdims=True) for s in scores])
    ps = [jnp.exp(s - m) for s in scores]
    l = functools.reduce(jnp.add, [jnp.sum(p, axis=-1, keepdims=True) for p in ps])
    acc = functools.reduce(jnp.add, [_dot(_bf(p), v) for p, v in zip(ps, values)])
    return acc / l


def _na_kernel(q_ref, k_ref, v_ref, kc_ref, vc_ref, tbl_ref, o_ref, *, rows, rb):
    jb = pl.program_id(1)
    kc = kc_ref[0]
    vc = vc_ref[0]
    n_loc = NA_WIN_R * GRID_W
    staged = []
    for rr in range(rb):
        r = jb * rb + rr
        rs = jnp.clip(r - NA_WIN_R // 2, 0, rows - NA_WIN_R)
        start = pl.multiple_of(rs * GRID_W, GRID_W)
        qm = _stack_heads(q_ref[0, rr * GRID_W:(rr + 1) * GRID_W, :])
        s_loc = _dot_nt(qm, k_ref[0, pl.ds(start, n_loc), :]) + tbl_ref[r - rs]
        staged.append((start, s_loc, _dot_nt(qm, kc)))
    for rr, (start, s_loc, s_ctx) in enumerate(staged):
        res = _softmax_pv([s_loc, s_ctx], [v_ref[0, pl.ds(start, n_loc), :], vc])
        o_ref[0, rr * GRID_W:(rr + 1) * GRID_W, :] = _unstack_heads(res)


def _na_attention(q, k, v, kc, vc, tbl, rb):
    b, t, _ = q.shape
    tc = kc.shape[1]
    rows = t // GRID_W
    assert rows >= NA_WIN_R and rows % rb == 0
    seq = pl.BlockSpec((1, t, 256), lambda i, j: (i, 0, 0))
    cseq = pl.BlockSpec((1, tc, 256), lambda i, j: (i, 0, 0))
    return pl.pallas_call(
        functools.partial(_na_kernel, rows=rows, rb=rb),
        grid=(b, rows // rb),
        in_specs=[pl.BlockSpec((1, rb * GRID_W, 256), lambda i, j: (i, j, 0)), seq, seq, cseq, cseq,
                  _full(tbl, 2)],
        out_specs=pl.BlockSpec((1, rb * GRID_W, 256), lambda i, j: (i, j, 0)),
        out_shape=jax.ShapeDtypeStruct((b, t, 256), jnp.float32),
        compiler_params=_cparams(("parallel", "parallel")),
        name="na_attention",
    )(q, k, v, kc, vc, tbl)


def _ctx_na_kernel(q_ref, kc_ref, vc_ref, o_ref):
    kc = kc_ref[0]
    vc = vc_ref[0]
    for rr in range(q_ref.shape[1] // GRID_W):
        qm = _stack_heads(q_ref[0, rr * GRID_W:(rr + 1) * GRID_W, :])
        res = _softmax_pv([_dot_nt(qm, kc)], [vc])
        o_ref[0, rr * GRID_W:(rr + 1) * GRID_W, :] = _unstack_heads(res)


def _ctx_na_attention(q, kc, vc):
    b, tc, _ = q.shape
    cseq = pl.BlockSpec((1, tc, 256), lambda i: (i, 0, 0))
    return pl.pallas_call(
        _ctx_na_kernel,
        grid=(b,),
        in_specs=[cseq, cseq, cseq],
        out_specs=cseq,
        out_shape=jax.ShapeDtypeStruct((b, tc, 256), jnp.float32),
        compiler_params=_cparams(("parallel",)),
        name="ctx_na_attention",
    )(q, kc, vc)


def _hg_gates(z, lb):
    e = jnp.exp(-jnp.abs(z))
    r = 1.0 / (1.0 + e)
    pos = z >= 0
    sig = jnp.where(pos, r, e * r)
    sig_neg = jnp.where(pos, e * r, r)
    return jnp.log(lb + (1.0 - lb) * sig), (1.0 - lb) * sig_neg


def _hg_prep_kernel(zf_ref, zb_ref, i_ref, q_ref, lbf_ref, lbb_ref, tl_ref, tu_ref, ta_ref, sel_ref, e_ref,
                    qf_o, kf_o, qb_o, kb_o, i_o, df_o, db_o, oi_o, nat_ref, pm_ref,
                    *, tm, qscale):
    nch = tm // HG_SUB
    q = q_ref[0] * qscale
    xi = i_ref[0]
    i_o[0] = _bf(xi)
    logf_f, kg_f = _hg_gates(zf_ref[0], lbf_ref[...])
    logf_b, kg_b = _hg_gates(zb_ref[0], lbb_ref[...])
    b_f = _split3_dot_left(tl_ref[...], logf_f)
    b_b = _split3_dot_left(tu_ref[...], logf_b)
    tot_f = _split3_dot_left(ta_ref[...], logf_f)
    tot_b = _split3_dot_left(ta_ref[...], logf_b)
    qf_o[0] = _bf(q * jnp.exp(b_f))
    kf_o[0] = _bf(kg_f * jnp.exp(tot_f - b_f))
    qb_o[0] = _bf(q * jnp.exp(b_b))
    kb_o[0] = _bf(kg_b * jnp.exp(tot_b - b_b))
    df_o[0] = jnp.exp(_split3_dot_left(sel_ref[...], logf_f))
    db_o[0] = jnp.exp(_split3_dot_left(sel_ref[...], logf_b))

    log2e = float(np.log2(np.e))
    for a, val in enumerate((q, kg_f, b_f * log2e, kg_b, b_b * log2e, xi)):
        for half in range(2):
            nat_ref[a, half] = val[:, half * LANE:(half + 1) * LANE]
            for p in range(HG_SUB):
                pm_ref[a, half, p * nch:(p + 1) * nch, :] = nat_ref[a, half, pl.ds(p, nch, stride=HG_SUB), :]
    e = e_ref[...]

    def side(q_t, b_t, k_ref, b_ref, i_ref, lo, nblk):
        rows = slice(lo, lo + nblk * nch)
        k3 = k_ref[rows, :].reshape(nblk, nch, LANE)
        b3 = b_ref[rows, :].reshape(nblk, nch, LANE)
        x = q_t[None] * k3 * jnp.exp2(b_t[None] - b3)
        r = _dot(_bf(x.reshape(nblk * nch, LANE)), e) * i_ref[rows, :]
        return jnp.sum(r.reshape(nblk, nch, LANE), axis=0)

    for half in range(2):
        qp, kfp, bfp, kbp, bbp, ip = [pm_ref.at[a, half] for a in range(6)]
        for pt in range(HG_SUB):
            rt = slice(pt * nch, (pt + 1) * nch)
            q_t = qp[rt, :]
            acc = _dot(_bf(q_t * (kfp[rt, :] + kbp[rt, :])), e) * ip[rt, :]
            if pt > 0:
                acc = acc + side(q_t, bfp[rt, :], kfp, bfp, ip, 0, pt)
            if pt < HG_SUB - 1:
                acc = acc + side(q_t, bbp[rt, :], kbp, bbp, ip, (pt + 1) * nch, HG_SUB - 1 - pt)
            nat_ref[0, half, pl.ds(pt, nch, stride=HG_SUB), :] = acc
        oi_o[0, :, half * LANE:(half + 1) * LANE] = nat_ref[0, half]


def _hg_prep(uq, qcol, u, base, lbf, lbb, e_mat):
    b, t, _ = u.shape
    tm = min(HG_PREP_TILE, t)
    nsub = tm // HG_SUB
    tok = pl.BlockSpec((1, tm, 256), lambda i, j: (i, j, 0))
    dec = pl.BlockSpec((1, nsub, 256), lambda i, j: (i, j, 0))
    tl, tu, ta, sel = _hg_consts(tm)
    e_half = e_mat[:LANE, :LANE]
    regroup = pltpu.VMEM((6, 2, tm, LANE), jnp.float32)
    return pl.pallas_call(
        functools.partial(_hg_prep_kernel, tm=tm, qscale=HG_DK ** -0.5),
        grid=(b, t // tm),
        in_specs=[_ublk(tm, COL["zF"], base, 256), _ublk(tm, COL["zB"], base, 256),
                  _ublk(tm, COL["i_hg"], base, 256),
                  pl.BlockSpec((1, tm, 256), lambda i, j: (i, j, qcol)),
                  _full(lbf, 2), _full(lbb, 2), _full(tl, 2), _full(tu, 2), _full(ta, 2), _full(sel, 2),
                  _full(e_half, 2)],
        out_specs=[tok, tok, tok, tok, tok, dec, dec, tok],
        out_shape=[jax.ShapeDtypeStruct((b, t, 256), MXU_DTYPE)] * 5
                  + [jax.ShapeDtypeStruct((b, t // HG_SUB, 256), jnp.float32)] * 2
                  + [jax.ShapeDtypeStruct((b, t, 256), jnp.float32)],
        scratch_shapes=[regroup, regroup],
        compiler_params=_cparams(("parallel", "parallel")),
        name="hg_prep",
    )(u, u, u, uq, lbf, lbb, tl, tu, ta, sel, e_half)


def _hg_scan_kernel(qf_ref, kf_ref, if_ref, df_ref, qb_ref, kb_ref, ib_ref, db_ref, s0f_ref, s0b_ref,
                    of_ref, ob_ref, sf_out, sb_out, sf, sb, *, nsub, nb):
    j = pl.program_id(0)

    @pl.when(j == 0)
    def _():
        sf[...] = s0f_ref[...]
        sb[...] = s0b_ref[...]

    first = lax.broadcasted_iota(jnp.int32, (HG_SUB, LANE), 1) < HG_DK

    def step(state, q_ref, k_ref, i_ref, d_ref, o_ref, bi, c):
        rows = slice(c * HG_SUB, (c + 1) * HG_SUB)
        for p in range(2):
            cols = slice(p * LANE, (p + 1) * LANE)
            s_old = state[bi, p]
            q = q_ref[bi, rows, cols].astype(jnp.float32)
            q2 = _bf(jnp.concatenate([jnp.where(first, q, 0.0), jnp.where(first, 0.0, q)], axis=0))
            r = _dot_nt(q2, _bf(s_old))
            o_ref[bi, rows, cols] = jnp.where(first, r[:HG_SUB], r[HG_SUB:])
            upd = _dot_tn(i_ref[bi, rows, cols], k_ref[bi, rows, cols])
            state[bi, p] = s_old * d_ref[bi, c:c + 1, cols] + upd

    for c in range(nsub):
        for bi in range(nb):
            step(sf, qf_ref, kf_ref, if_ref, df_ref, of_ref, bi, c)
            step(sb, qb_ref, kb_ref, ib_ref, db_ref, ob_ref, bi, nsub - 1 - c)

    @pl.when(j == pl.num_programs(0) - 1)
    def _():
        sf_out[...] = sf[...]
        sb_out[...] = sb[...]


def _hg_scan(qf, kf, qb, kb, xi, df, db, s0f, s0b):
    b, t, _ = qf.shape
    tm = HG_TILE
    n = t // tm
    nsub = tm // HG_SUB
    fwd = pl.BlockSpec((b, tm, 256), lambda j: (0, j, 0))
    bwd = pl.BlockSpec((b, tm, 256), lambda j: (0, n - 1 - j, 0))
    dfwd = pl.BlockSpec((b, nsub, 256), lambda j: (0, j, 0))
    dbwd = pl.BlockSpec((b, nsub, 256), lambda j: (0, n - 1 - j, 0))
    st = pl.BlockSpec((b, 2, LANE, LANE), lambda j: (0, 0, 0, 0))
    return pl.pallas_call(
        functools.partial(_hg_scan_kernel, nsub=nsub, nb=b),
        grid=(n,),
        in_specs=[fwd, fwd, fwd, dfwd, bwd, bwd, bwd, dbwd, st, st],
        out_specs=[fwd, bwd, st, st],
        out_shape=[jax.ShapeDtypeStruct((b, t, 256), jnp.float32)] * 2
                  + [jax.ShapeDtypeStruct((b, 2, LANE, LANE), jnp.float32)] * 2,
        scratch_shapes=[pltpu.VMEM((b, 2, LANE, LANE), jnp.float32)] * 2,
        compiler_params=_cparams(("arbitrary",)),
        name="hg_scan",
    )(qf, kf, xi, df, qb, kb, xi, db, s0f, s0b)


def _merge_kernel(x_ref, a_ref, b_ref, c_ref, of_ref, ob_ref, oi_ref, gc_ref, g_ref, sh_ref, sc_ref, wg_ref,
                  hgn_ref, e_ref, wbr_ref, wo_ref, gate_ref, out_ref):
    d_model = x_ref.shape[2]
    x = x_ref[0]
    h = _bf((x * _rms_scale(x) * g_ref[...]) * (1.0 + sc_ref[0]) + sh_ref[0])
    o = of_ref[0] + ob_ref[0] + oi_ref[0]
    ms = _split_dot(o * o, e_ref[...]) * (1.0 / HG_DK)
    d = o * lax.rsqrt(ms + EPS) * hgn_ref[...] * _sigmoid(gc_ref[0])
    acc = None
    for n, y in enumerate((a_ref[0], b_ref[0], c_ref[0], d)):
        gate = _sigmoid(_dot(h, wg_ref[:, n * d_model:(n + 1) * d_model]))
        term = gate * _dot(_bf(y), wbr_ref[n])
        acc = term if acc is None else acc + term
    out_ref[0] = x + gate_ref[0] * _dot(_bf(acc), wo_ref[...])


def _merge(x, a, bm, cn, o_f, o_b, o_i, u, g1n, w_gate, hgn, e_mat, w_br, w_o, mods, tm):
    b, t, d = x.shape
    tok = pl.BlockSpec((1, tm, 256), lambda i, j: (i, j, 0))
    xs = pl.BlockSpec((1, tm, d), lambda i, j: (i, j, 0))
    return pl.pallas_call(
        _merge_kernel,
        grid=(b, t // tm),
        in_specs=[xs, tok, tok, tok, tok, tok, tok, _ublk(tm, COL["g_cols"], 0, 256),
                  _full(g1n, 2), _mod_spec(mods, 0, d), _mod_spec(mods, 1, d), _full(w_gate, 2),
                  _full(hgn, 2), _full(e_mat, 2), _full(w_br, 2), _full(w_o, 2), _mod_spec(mods, 2, d)],
        out_specs=xs,
        out_shape=jax.ShapeDtypeStruct((b, t, d), jnp.float32),
        compiler_params=_cparams(("parallel", "parallel")),
        name="merge",
    )(x, a, bm, cn, o_f, o_b, o_i, u, g1n, mods, mods, w_gate, hgn, e_mat, w_br, w_o, mods)


def _ffn_kernel(x_ref, g_ref, sh_ref, sc_ref, w1_ref, w2_ref, gate_ref, o_ref, *, tf):
    x = x_ref[0]
    h = _bf((x * _rms_scale(x) * g_ref[...]) * (1.0 + sc_ref[0]) + sh_ref[0])
    acc = None
    for k in range(w1_ref.shape[1] // tf):
        z = jnp.maximum(_dot(h, w1_ref[:, k * tf:(k + 1) * tf]), 0.0)
        part = _dot(_bf(z * z), w2_ref[k * tf:(k + 1) * tf, :])
        acc = part if acc is None else acc + part
    o_ref[0] = x + gate_ref[0] * acc


def _ffn(x, g, mods, w1, w2, layer, tm, tf):
    b, t, d = x.shape
    xs = pl.BlockSpec((1, tm, d), lambda i, j: (i, j, 0))
    resident = lambda a: pl.BlockSpec((None,) + a.shape[1:], lambda i, j: (layer, 0, 0),
                                      pipeline_mode=pl.Buffered(1))
    return pl.pallas_call(
        functools.partial(_ffn_kernel, tf=tf),
        grid=(b, t // tm),
        in_specs=[xs, pl.BlockSpec((1, d), lambda i, j: (0, 0)), _mod_spec(mods, 3, d), _mod_spec(mods, 4, d),
                  resident(w1), resident(w2), _mod_spec(mods, 5, d)],
        out_specs=xs,
        out_shape=jax.ShapeDtypeStruct((b, t, d), jnp.float32),
        compiler_params=_cparams(("parallel", "parallel")),
        name="ffn",
    )(x, g, mods, mods, w1, w2, mods)


def _rope_perm():
    return np.array([(t // 16) * 16 + ((t % 16) + 8) % 16 for t in range(MLA_ROPE)])


def _prep_w_in(w):
    d = w.shape[0]
    o = dict(ckv=0, kr=128, k_na=160, v_na=416, zF=672, zB=928, i_hg=1184, pool=1440, qa=1696, q_na=1952,
             q_hg=2208, g_cols=2464, gates=2720)
    z = lambda n: jnp.zeros((d, n), w.dtype)
    k_rope = w[:, o["kr"]:o["kr"] + MLA_ROPE]
    parts = [w[:, o[name]:o[name] + 256]
             for name in ("pool", "qa", "q_na", "q_hg", "g_cols", "k_na", "v_na", "zF", "zB", "i_hg")]
    parts += [w[:, 0:128], z(64), k_rope, z(32), z(64), k_rope[:, _rope_perm()], z(32), z(128)]
    return _bf(w[:, o["gates"]:]), _bf(jnp.concatenate(parts, axis=1))


def _prep_wq(w):
    r = w.shape[0]
    wh = w.reshape(r, N_HEADS, MLA_QK)
    pre = jnp.pad(wh, ((0, 0), (0, 0), (0, HEAD_PAD - MLA_QK)))
    sw = jnp.pad(wh[:, :, MLA_NOPE:][:, :, _rope_perm()], ((0, 0), (0, 0), (MLA_NOPE, HEAD_PAD - MLA_QK)))
    return _bf(jnp.concatenate([pre.reshape(r, -1), sw.reshape(r, -1)], axis=1))


def _prep_wkv(w):
    r = w.shape[0]
    wh = w.reshape(r, N_HEADS, MLA_NOPE + MLA_V)
    kpart = jnp.pad(wh[:, :, :MLA_NOPE], ((0, 0), (0, 0), (0, HEAD_PAD - MLA_NOPE)))
    return _bf(jnp.concatenate([kpart.reshape(r, -1), wh[:, :, MLA_NOPE:].reshape(r, -1)], axis=1))


def _pad_gain(g):
    plain = jnp.pad(g, (0, HEAD_PAD - MLA_QK))[None]
    sw = jnp.pad(g[MLA_NOPE:][_rope_perm()], (MLA_NOPE, HEAD_PAD - MLA_QK))[None]
    return plain, sw


def _rope_tables(n_tok):
    t = jnp.arange(n_tok)
    pos = jnp.stack([t // GRID_W, t % GRID_W], axis=-1).astype(jnp.float32)
    nf = MLA_ROPE // 4
    inv = ROPE_BASE ** (-jnp.arange(nf, dtype=jnp.float32) / nf)
    ang = pos[:, :, None] * inv
    cos, sin = jnp.cos(ang), jnp.sin(ang)
    cos_t = jnp.concatenate([cos, cos], axis=-1).reshape(n_tok, MLA_ROPE)
    sin_t = jnp.concatenate([-sin, sin], axis=-1).reshape(n_tok, MLA_ROPE)
    ones = jnp.ones((n_tok, MLA_NOPE), jnp.float32)
    zer = jnp.zeros((n_tok, HEAD_PAD - MLA_QK), jnp.float32)
    cos_p = jnp.concatenate([ones, cos_t, zer], axis=-1)
    sin_p = jnp.concatenate([0 * ones, sin_t, zer], axis=-1)
    return cos_p, sin_p


def _plain_tables(n_tok):
    cos_p = jnp.concatenate([jnp.ones((n_tok, MLA_QK), jnp.float32),
                             jnp.zeros((n_tok, HEAD_PAD - MLA_QK), jnp.float32)], axis=-1)
    return cos_p, jnp.zeros((n_tok, HEAD_PAD), jnp.float32)


def _head_indicator(dtype):
    i = np.arange(256) // 64
    return jnp.asarray((i[:, None] == i[None, :]).astype(np.float32), dtype=dtype)


def _na_bias_table(rpb):
    qc = np.arange(GRID_W)[:, None]
    kc = np.arange(GRID_W)[None, :]
    cs = np.clip(qc - NA_WIN_C // 2, 0, GRID_W - NA_WIN_C)
    inwin = (kc >= cs) & (kc < cs + NA_WIN_C)
    dc = np.clip(kc - qc + NA_WIN_C - 1, 0, 2 * NA_WIN_C - 2)
    onehot = (dc[:, :, None] == np.arange(2 * NA_WIN_C - 1)).astype(np.float32)
    t = jnp.einsum('hrd,qkd->hqrk', rpb.astype(jnp.float32), onehot, precision=lax.Precision.HIGHEST)
    t = jnp.where(inwin[None, :, None, :], t, NEG)
    tbl = jnp.stack([t[:, :, NA_WIN_R - 1 - p:2 * NA_WIN_R - 1 - p] for p in range(NA_WIN_R)], axis=0)
    return tbl.reshape(NA_WIN_R, N_HEADS * GRID_W, NA_WIN_R * GRID_W)


def _hg_consts(tm):
    t = np.arange(tm)
    same = (t[:, None] // HG_SUB) == (t[None, :] // HG_SUB)
    tl = same & (t[None, :] <= t[:, None])
    tu = same & (t[None, :] >= t[:, None])
    sel = (np.arange(tm // HG_SUB)[:, None] == (t[None, :] // HG_SUB))
    f = lambda m: jnp.asarray(m.astype(np.float32), dtype=MXU_DTYPE)
    return f(tl), f(tu), f(same), f(sel)


def _block_diag(w):
    g, c, d = w.shape
    out = jnp.zeros((g * c, g * d), w.dtype)
    for i in range(g):
        out = out.at[i * c:(i + 1) * c, i * d:(i + 1) * d].set(w[i])
    return out


def kernel(x, c, ctx, c_ctx, ada_w, ada_b, norm1, norm2, w_in, pool_w, pool_scale, mla_q_norm, mla_wq_b, mla_kv_norm, mla_wkv_b, mla_gq, mla_gk, na_gq, na_gk, na_rpb, hg_lb, hg_norm, w_branch, w_out, w_ff1, w_ff2):
    B, L, D = x.shape
    Tc = ctx.shape[1]
    depth = ada_w.shape[0]
    f32 = jnp.float32
    assert L % HG_TILE == 0 and Tc % HG_TILE == 0 and B < 8

    lb_p = jax.nn.softmax(hg_lb.astype(f32), axis=0)
    lb_all = jnp.clip(jnp.cumsum(lb_p, axis=0)[:depth], LB_EPS, 1.0 - LB_EPS)

    cc = jnp.concatenate([c, c_ctx[None], jnp.zeros((8 - B - 1, D), f32)], axis=0)
    mods = _mods(cc, ada_w, ada_b)

    cos_l, sin_l = _rope_tables(L)
    cos_c, sin_c = _plain_tables(Tc)
    e_mat = _head_indicator(MXU_DTYPE)
    zero_state = jnp.zeros((B, 2, LANE, LANE), f32)
    zero_q = jnp.zeros((B, Tc, 256), f32)
    tl = min(512, L)
    t_big = min(1024, L)

    w1 = _bf(w_ff1)
    w2 = _bf(w_ff2)
    xl, xc = x, ctx
    for l in range(depth):
        ctx_out = l < depth - 1
        mods_l = mods[l, :B].reshape(B, 1, 6 * D)
        mods_c = mods[l, B].reshape(1, 1, 6 * D)
        g1n = norm1[l][None]
        g2n = norm2[l][None]
        w_gate, w_mix = _prep_w_in(w_in[l])
        wq = _prep_wq(mla_wq_b[l])
        wkv = _prep_wkv(mla_wkv_b[l])
        gq, gqsw = _pad_gain(mla_gq[l])
        gk, gksw = _pad_gain(mla_gk[l])
        qn = mla_q_norm[l][None]
        kvn = mla_kv_norm[l][None]
        na_q_gain = (jnp.tile(na_gq[l], N_HEADS) * NA_HD ** -0.5)[None]
        na_k_gain = jnp.tile(na_gk[l], N_HEADS)[None]
        na_tbl = _na_bias_table(na_rpb[l])
        pool_bd = _bf(_block_diag(pool_w[l]))
        pool_sc = pool_scale[l][None]
        hgn = jnp.tile(hg_norm[l], N_HEADS)[None]
        lbF = lb_all[l, 0][None]
        lbB = lb_all[l, 1][None]
        w_br = _bf(w_branch[l])
        w_o = _bf(w_out[l])

        ul = _inproj(xl, g1n, mods_l, w_mix, tl, D_MIX_PAD, f32)
        if ctx_out:
            uc, cbase = _inproj(xc, g1n, mods_c, w_mix, Tc, MIX_TILE, f32), 0
        else:
            uc, cbase = _inproj(xc, g1n, mods_c, w_mix[:, KV_START:], Tc, KV_TILE, f32), KV_START

        q_mla, k_mla, v_mla = _mla_prep(ul, 0, cos_l, sin_l, qn, wq, kvn, wkv, gq, gqsw, gk, gksw, tl)
        if ctx_out:
            q_mla_c, k_mla_c, v_mla_c = _mla_prep(uc, cbase, cos_c, sin_c, qn, wq, kvn, wkv, gq, gqsw, gk, gksw, Tc)
        else:
            k_mla_c, v_mla_c = _mla_kv_prep(uc, cbase, kvn, wkv, gk, Tc)
        b_l = _attention(q_mla, k_mla_c, v_mla_c, k_mla, v_mla, tl, Tc, tl)

        q_na, k_na, v_na = _na_prep(ul, COL["q_na"] // 256, ul, 0, na_q_gain, na_k_gain, e_mat, tl)
        if ctx_out:
            q_na_c, k_na_c, v_na_c = _na_prep(uc, COL["q_na"] // 256, uc, cbase, na_q_gain, na_k_gain, e_mat, Tc)
        else:
            _, k_na_c, v_na_c = _na_prep(zero_q, 0, uc, cbase, na_q_gain, na_k_gain, e_mat, Tc)
        c_l = _na_attention(q_na, k_na, v_na, k_na_c, v_na_c, na_tbl, 4)

        a_l = _pool(ul, 0, pool_bd, pool_sc, tl)

        if ctx_out:
            hc = _hg_prep(uc, COL["q_hg"] // 256, uc, cbase, lbF, lbB, e_mat)
        else:
            hc = _hg_prep(zero_q, 0, uc, cbase, lbF, lbB, e_mat)
        oFc, oBc, sFc, sBc = _hg_scan(hc[0], hc[1], hc[2], hc[3], hc[4], hc[5], hc[6], zero_state, zero_state)
        hl = _hg_prep(ul, COL["q_hg"] // 256, ul, 0, lbF, lbB, e_mat)
        oFl, oBl, _, _ = _hg_scan(hl[0], hl[1], hl[2], hl[3], hl[4], hl[5], hl[6], sFc, sBc)

        xl_new = _merge(xl, a_l, b_l, c_l, oFl, oBl, hl[7], ul, g1n, w_gate, hgn, e_mat, w_br, w_o, mods_l, tl)

        if ctx_out:
            a_c = _pool(uc, cbase, pool_bd, pool_sc, Tc)
            b_c = _attention(q_mla_c, k_mla_c, v_mla_c, None, None, Tc, Tc, Tc)
            c_c = _ctx_na_attention(q_na_c, k_na_c, v_na_c)
            xc = _merge(xc, a_c, b_c, c_c, oFc, oBc, hc[7], uc, g1n, w_gate, hgn, e_mat, w_br, w_o, mods_c, Tc)
            xc = _ffn(xc, g2n, mods_c, w1, w2, l, Tc, 1024)

        xl = _ffn(xl_new, g2n, mods_l, w1, w2, l, tl, 2048)
    return xl
```

```python
import functools

import jax
import jax.numpy as jnp
import numpy as np
from jax import lax
from jax.experimental import pallas as pl
from jax.experimental.pallas import tpu as pltpu

D_MODEL = 1024
GRID_W = 64
EPS = 1e-6
N_BRANCH = 4
BRANCH_W = 256
POOL_GC = 64
N_HEADS = 4
MLA_NOPE = 64
MLA_ROPE = 32
MLA_V = 64
MLA_QK = 96
ROPE_BASE = 10000.0
NA_HD = 64
NA_WIN_R = 8
NA_WIN_C = 16
HG_DK = 64
LB_EPS = 1e-6

MXU_DTYPE = jnp.bfloat16
LANE = 128
HEAD_PAD = 128
ONES_ROWS = 16
VMEM_LIMIT = 48 * 1024 * 1024
HG_SUB = 16
HG_TILE = 256
HG_PREP_TILE = 256
NEG = -1e30

COL = dict(pool=0, qa=256, q_na=512, q_hg=768, g_cols=1024,
           k_na=1280, v_na=1536, zF=1792, zB=2048, i_hg=2304, ckv=2560, kr=2688, kr_sw=2816)
D_MIX_PAD = 3072
KV_START = 1280
MIX_TILE = 1536
KV_TILE = D_MIX_PAD - KV_START


def _cparams(sem):
    return pltpu.CompilerParams(dimension_semantics=sem, vmem_limit_bytes=VMEM_LIMIT)


def _bf(x):
    return x.astype(MXU_DTYPE)


def _dot(a, b):
    return jnp.dot(a, b, preferred_element_type=jnp.float32)


def _dot_nt(a, b):
    return lax.dot_general(a, b, (((1,), (1,)), ((), ())), preferred_element_type=jnp.float32)


def _dot_tn(a, b):
    return lax.dot_general(a, b, (((0,), (0,)), ((), ())), preferred_element_type=jnp.float32)


def _split2(x):
    hi = _bf(x)
    return hi, _bf(x - hi.astype(jnp.float32))


def _split_dot(x, w):
    hi, lo = _split2(x)
    return _dot(hi, w) + _dot(lo, w)


def _split3_dot_left(w, x):
    hi = _bf(x)
    r1 = x - hi.astype(jnp.float32)
    mid = _bf(r1)
    lo = _bf(r1 - mid.astype(jnp.float32))
    return _dot(w, hi) + _dot(w, mid) + _dot(w, lo)


def _sigmoid(x):
    return 0.5 * jnp.tanh(0.5 * x) + 0.5


def _rms_scale(x):
    return lax.rsqrt(jnp.mean(x * x, axis=-1, keepdims=True) + EPS)


def _full(arr, nargs):
    zeros = (0,) * arr.ndim
    return pl.BlockSpec(arr.shape, lambda *_: zeros)


def _ublk(tm, col, base, width):
    idx = (col - base) // width
    return pl.BlockSpec((1, tm, width), lambda i, j: (i, j, idx))


def _mods_kernel(c_ref, w_ref, b_ref, o_ref):
    c = c_ref[...]
    s = _bf(c * _sigmoid(c))
    o_ref[0] = _dot(s, _bf(w_ref[0])) + b_ref[0]


def _mods(cc, ada_w, ada_b):
    depth, d, n = ada_w.shape
    tn = 1536
    return pl.pallas_call(
        _mods_kernel,
        grid=(depth, n // tn),
        in_specs=[pl.BlockSpec((8, d), lambda l, j: (0, 0)),
                  pl.BlockSpec((1, d, tn), lambda l, j: (l, 0, j)),
                  pl.BlockSpec((1, 1, tn), lambda l, j: (l, 0, j))],
        out_specs=pl.BlockSpec((1, 8, tn), lambda l, j: (l, 0, j)),
        out_shape=jax.ShapeDtypeStruct((depth, 8, n), jnp.float32),
        compiler_params=_cparams(("arbitrary", "arbitrary")),
        name="mods",
    )(cc, ada_w, ada_b.reshape(depth, 1, n))


def _inproj_kernel(x_ref, g_ref, sh_ref, sc_ref, w_ref, o_ref, h_ref):
    @pl.when(pl.program_id(2) == 0)
    def _():
        x = x_ref[0]
        y = x * _rms_scale(x) * g_ref[...]
        h_ref[...] = _bf(y * (1.0 + sc_ref[0]) + sh_ref[0])

    o_ref[0] = _dot(h_ref[...], w_ref[...]).astype(o_ref.dtype)


def _mod_spec(mods, idx, d):
    if mods.shape[0] == 1:
        return pl.BlockSpec((1, 1, d), lambda i, *_: (0, 0, idx))
    return pl.BlockSpec((1, 1, d), lambda i, *_: (i, 0, idx))


def _inproj(x, g, mods, w, tm, tn, out_dtype):
    b, t, d = x.shape
    n = w.shape[1]
    assert n % tn == 0
    return pl.pallas_call(
        _inproj_kernel,
        grid=(b, t // tm, n // tn),
        in_specs=[pl.BlockSpec((1, tm, d), lambda i, j, k: (i, j, 0)),
                  pl.BlockSpec((1, d), lambda i, j, k: (0, 0)),
                  _mod_spec(mods, 0, d), _mod_spec(mods, 1, d),
                  pl.BlockSpec((d, tn), lambda i, j, k: (0, k))],
        out_specs=pl.BlockSpec((1, tm, tn), lambda i, j, k: (i, j, k)),
        out_shape=jax.ShapeDtypeStruct((b, t, n), out_dtype),
        scratch_shapes=[pltpu.VMEM((tm, d), MXU_DTYPE)],
        compiler_params=_cparams(("parallel", "parallel", "arbitrary")),
        name="inproj",
    )(x, g, mods, mods, w)


def _mla_prep_kernel(qa_ref, ckv_ref, kr_ref, krsw_ref, cos_ref, sin_ref, qn_ref, wq_ref, kvn_ref, wkv_ref,
                     gq_ref, gqsw_ref, gk_ref, gksw_ref, q_ref, k_ref, vt_ref, *, scale):
    cos = cos_ref[...]
    sin = sin_ref[...]
    inv_w = 1.0 / MLA_QK
    qa = qa_ref[0]
    qq = _dot(_bf(qa * _rms_scale(qa) * qn_ref[...]), wq_ref[...])
    ckv = ckv_ref[0]
    kv = _dot(_bf(ckv * _rms_scale(ckv) * kvn_ref[...]), wkv_ref[...])
    kr = kr_ref[0]
    kr_rot = krsw_ref[0] * gksw_ref[...] * sin
    gq_cos = gq_ref[...] * cos
    gq_sin = gqsw_ref[...] * sin
    gk_cos = gk_ref[...] * cos
    for h in range(N_HEADS):
        qp = qq[:, h * HEAD_PAD:(h + 1) * HEAD_PAD]
        qs = qq[:, (N_HEADS + h) * HEAD_PAD:(N_HEADS + h + 1) * HEAD_PAD]
        rs = lax.rsqrt(jnp.sum(qp * qp, axis=-1, keepdims=True) * inv_w + EPS)
        q_ref[0, h] = _bf((qp * gq_cos + qs * gq_sin) * (rs * scale))
        kp = kv[:, h * HEAD_PAD:(h + 1) * HEAD_PAD] + kr
        rs = lax.rsqrt(jnp.sum(kp * kp, axis=-1, keepdims=True) * inv_w + EPS)
        k_ref[0, h] = _bf((kp * gk_cos + kr_rot) * rs)
    vt_ref[0] = _bf(kv[:, N_HEADS * HEAD_PAD:].T)


def _mla_prep(u, base, cos, sin, qn, wq, kvn, wkv, gq, gqsw, gk, gksw, tm):
    b, t, _ = u.shape
    tab = pl.BlockSpec((tm, HEAD_PAD), lambda i, j: (j, 0))
    hd = pl.BlockSpec((1, N_HEADS, tm, HEAD_PAD), lambda i, j: (i, 0, j, 0))
    consts = (qn, wq, kvn, wkv, gq, gqsw, gk, gksw)
    return pl.pallas_call(
        functools.partial(_mla_prep_kernel, scale=MLA_QK ** -0.5 * float(np.log2(np.e))),
        grid=(b, t // tm),
        in_specs=[_ublk(tm, COL["qa"], base, 256), _ublk(tm, COL["ckv"], base, 128),
                  _ublk(tm, COL["kr"], base, 128), _ublk(tm, COL["kr_sw"], base, 128), tab, tab]
                 + [_full(a, 2) for a in consts],
        out_specs=[hd, hd, pl.BlockSpec((1, 256, tm), lambda i, j: (i, 0, j))],
        out_shape=[jax.ShapeDtypeStruct((b, N_HEADS, t, HEAD_PAD), MXU_DTYPE),
                   jax.ShapeDtypeStruct((b, N_HEADS, t, HEAD_PAD), MXU_DTYPE),
                   jax.ShapeDtypeStruct((b, 256, t), MXU_DTYPE)],
        compiler_params=_cparams(("parallel", "parallel")),
        name="mla_prep",
    )(u, u, u, u, cos, sin, *consts)


def _mla_kv_prep_kernel(ckv_ref, kr_ref, kvn_ref, wkv_ref, gk_ref, k_ref, vt_ref):
    inv_w = 1.0 / MLA_QK
    ckv = ckv_ref[0]
    kv = _dot(_bf(ckv * _rms_scale(ckv) * kvn_ref[...]), wkv_ref[...])
    kr = kr_ref[0]
    for h in range(N_HEADS):
        kp = kv[:, h * HEAD_PAD:(h + 1) * HEAD_PAD] + kr
        rs = lax.rsqrt(jnp.sum(kp * kp, axis=-1, keepdims=True) * inv_w + EPS)
        k_ref[0, h] = _bf(kp * gk_ref[...] * rs)
    vt_ref[0] = _bf(kv[:, N_HEADS * HEAD_PAD:].T)


def _mla_kv_prep(u, base, kvn, wkv, gk, tm):
    b, t, _ = u.shape
    hd = pl.BlockSpec((1, N_HEADS, tm, HEAD_PAD), lambda i, j: (i, 0, j, 0))
    return pl.pallas_call(
        _mla_kv_prep_kernel,
        grid=(b, t // tm),
        in_specs=[_ublk(tm, COL["ckv"], base, 128), _ublk(tm, COL["kr"], base, 128),
                  _full(kvn, 2), _full(wkv, 2), _full(gk, 2)],
        out_specs=[hd, pl.BlockSpec((1, 256, tm), lambda i, j: (i, 0, j))],
        out_shape=[jax.ShapeDtypeStruct((b, N_HEADS, t, HEAD_PAD), MXU_DTYPE),
                   jax.ShapeDtypeStruct((b, 256, t), MXU_DTYPE)],
        compiler_params=_cparams(("parallel", "parallel")),
        name="mla_kv_prep",
    )(u, u, kvn, wkv, gk)


def _attn_kernel(*refs, tkc, tk, n_ctx, n_lat):
    if n_lat:
        q_ref, kc_ref, vtc_ref, kl_ref, vtl_ref, o_ref, sa_ref, sb_ref = refs
    else:
        q_ref, kc_ref, vtc_ref, o_ref, sa_ref, sb_ref = refs
    tq = q_ref.shape[2]
    qs = [q_ref[0, h] for h in range(N_HEADS)]

    def produce(k_ref, c, s_ref, n):
        start = pl.multiple_of(c * n, n)
        for h in range(N_HEADS):
            s_ref[h, :n] = _dot_nt(k_ref[0, h, pl.ds(start, n), :], qs[h])

    def consume(vt_ref, c, s_ref, carry, n):
        start = pl.multiple_of(c * n, n)
        ones = jnp.ones((ONES_ROWS, n), MXU_DTYPE)
        new = []
        for h in range(N_HEADS):
            m, acc = carry[h]
            s = s_ref[h, :n]
            m_new = jnp.maximum(m, jnp.max(s, axis=0, keepdims=True))
            alpha = jnp.exp2(m - m_new)
            p = _bf(jnp.exp2(s - m_new))
            vt = jnp.concatenate([vt_ref[0, h * MLA_V:(h + 1) * MLA_V, pl.ds(start, n)], ones], axis=0)
            new.append((m_new, alpha * acc + _dot(vt, p)))
        return tuple(new)

    carry = tuple((jnp.full((1, tq), -jnp.inf, jnp.float32), jnp.zeros((MLA_V + ONES_ROWS, tq), jnp.float32))
                  for _ in range(N_HEADS))
    cur, nxt = sa_ref, sb_ref
    produce(kc_ref, 0, cur, tkc)
    for c in range(n_ctx):
        if c + 1 < n_ctx:
            produce(kc_ref, c + 1, nxt, tkc)
        elif n_lat:
            produce(kl_ref, 0, nxt, tk)
        carry = consume(vtc_ref, c, cur, carry, tkc)
        cur, nxt = nxt, cur
    if n_lat:
        def body(i, carry):
            c = 2 * i
            produce(kl_ref, c + 1, nxt, tk)
            carry = consume(vtl_ref, c, cur, carry, tk)
            produce(kl_ref, c + 2, cur, tk)
            return consume(vtl_ref, c + 1, nxt, carry, tk)

        n_pairs = (n_lat - 1) // 2
        carry = lax.fori_loop(0, n_pairs, body, carry)
        c = 2 * n_pairs
        if n_lat - c == 2:
            produce(kl_ref, c + 1, nxt, tk)
            carry = consume(vtl_ref, c, cur, carry, tk)
            carry = consume(vtl_ref, c + 1, nxt, carry, tk)
        else:
            carry = consume(vtl_ref, c, cur, carry, tk)
    o_t = jnp.concatenate([acc[:MLA_V] / acc[MLA_V:MLA_V + 1] for _, acc in carry], axis=0)
    o_ref[0] = o_t.T


def _attention(q, kc, vtc, kl, vtl, tq, tkc, tk):
    b, _, t_q, _ = q.shape
    t_c = kc.shape[2]
    t_l = 0 if kl is None else kl.shape[2]
    assert t_q % tq == 0 and t_c % tkc == 0 and t_l % tk == 0
    kv_specs = [pl.BlockSpec((1, N_HEADS, t_c, HEAD_PAD), lambda i, j: (i, 0, 0, 0)),
                pl.BlockSpec((1, 256, t_c), lambda i, j: (i, 0, 0))]
    args = [q, kc, vtc]
    if t_l:
        kv_specs += [pl.BlockSpec((1, N_HEADS, t_l, HEAD_PAD), lambda i, j: (i, 0, 0, 0)),
                     pl.BlockSpec((1, 256, t_l), lambda i, j: (i, 0, 0))]
        args += [kl, vtl]
    return pl.pallas_call(
        functools.partial(_attn_kernel, tkc=tkc, tk=tk, n_ctx=t_c // tkc, n_lat=t_l // tk),
        grid=(b, t_q // tq),
        in_specs=[pl.BlockSpec((1, N_HEADS, tq, HEAD_PAD), lambda i, j: (i, 0, j, 0))] + kv_specs,
        out_specs=pl.BlockSpec((1, tq, 256), lambda i, j: (i, j, 0)),
        out_shape=jax.ShapeDtypeStruct((b, t_q, 256), jnp.float32),
        scratch_shapes=[pltpu.VMEM((N_HEADS, max(tkc, tk), tq), jnp.float32)] * 2,
        compiler_params=_cparams(("parallel", "parallel")),
        name="attention",
    )(*args)


def _pool_kernel(prev_ref, cur_ref, next_ref, w_ref, sc_ref, o_ref, ext_ref, *, tm, n_tok):
    j = pl.program_id(1)
    cur = cur_ref[0]
    ext_ref[8:8 + tm, :] = cur
    ext_ref[0:8, :] = jnp.where(j > 0, prev_ref[0], 0.0)
    ext_ref[8 + tm:16 + tm, :] = jnp.where(j < pl.num_programs(1) - 1, next_ref[0], 0.0)

    def window_sums(c0, widths):
        sh = lambda k: ext_ref[8 + k:8 + k + tm, c0:c0 + LANE]
        acc = sh(-1) + sh(0)
        out = {2: acc}
        for w in (4, 8, 16):
            if w > max(widths):
                break
            for k in list(range(-w // 2, -w // 4)) + list(range(w // 4, w // 2)):
                acc = acc + sh(k)
            out[w] = acc
        return [out[w] for w in widths]

    lane = lax.broadcasted_iota(jnp.int32, (tm, LANE), 1)
    t = j * tm + lax.broadcasted_iota(jnp.int32, (tm, LANE), 0)
    halves = []
    for c0, (wa, wb) in ((0, (2, 4)), (LANE, (8, 16))):
        sa, sb = window_sums(c0, (wa, wb))
        first = lane < POOL_GC
        s = jnp.where(first, sa, sb)
        half = jnp.where(first, wa // 2, wb // 2)
        cnt = jnp.minimum(t + half, n_tok) - jnp.maximum(t - half, 0)
        halves.append(s / cnt.astype(jnp.float32) - cur[:, c0:c0 + LANE])
    p = jnp.concatenate(halves, axis=-1)
    o_ref[0] = _dot(_bf(p), w_ref[...]) * sc_ref[...]


def _pool(u, base, w_bd, scale, tm):
    b, t, _ = u.shape
    cidx = (COL["pool"] - base) // 256
    rb = tm // 8
    last = t // 8 - 1
    return pl.pallas_call(
        functools.partial(_pool_kernel, tm=tm, n_tok=t),
        grid=(b, t // tm),
        in_specs=[pl.BlockSpec((1, 8, 256), lambda i, j: (i, jnp.maximum(j * rb - 1, 0), cidx)),
                  pl.BlockSpec((1, tm, 256), lambda i, j: (i, j, cidx)),
                  pl.BlockSpec((1, 8, 256), lambda i, j: (i, jnp.minimum((j + 1) * rb, last), cidx)),
                  _full(w_bd, 2), _full(scale, 2)],
        out_specs=pl.BlockSpec((1, tm, 256), lambda i, j: (i, j, 0)),
        out_shape=jax.ShapeDtypeStruct((b, t, 256), jnp.float32),
        scratch_shapes=[pltpu.VMEM((tm + 16, 256), jnp.float32)],
        compiler_params=_cparams(("parallel", "parallel")),
        name="pool",
    )(u, u, u, w_bd, scale)


def _head_rms(x, e):
    return lax.rsqrt(_split_dot(x * x, e) * (1.0 / NA_HD) + EPS)


def _na_prep_kernel(q_ref, k_ref, v_ref, gq_ref, gk_ref, e_ref, qo_ref, ko_ref, vo_ref):
    e = e_ref[...]
    q = q_ref[0]
    qo_ref[0] = _bf(q * _head_rms(q, e) * gq_ref[...])
    k = k_ref[0]
    ko_ref[0] = _bf(k * _head_rms(k, e) * gk_ref[...])
    vo_ref[0] = _bf(v_ref[0])


def _na_prep(uq, qcol, u, base, gq, gk, e_mat, tm):
    b, t, _ = u.shape
    tok = pl.BlockSpec((1, tm, 256), lambda i, j: (i, j, 0))
    return pl.pallas_call(
        _na_prep_kernel,
        grid=(b, t // tm),
        in_specs=[pl.BlockSpec((1, tm, 256), lambda i, j: (i, j, qcol)),
                  _ublk(tm, COL["k_na"], base, 256), _ublk(tm, COL["v_na"], base, 256),
                  _full(gq, 2), _full(gk, 2), _full(e_mat, 2)],
        out_specs=[tok, tok, tok],
        out_shape=[jax.ShapeDtypeStruct((b, t, 256), MXU_DTYPE)] * 3,
        compiler_params=_cparams(("parallel", "parallel")),
        name="na_prep",
    )(uq, u, u, gq, gk, e_mat)


def _stack_heads(q):
    qf = q.astype(jnp.float32)
    head = lax.broadcasted_iota(jnp.int32, qf.shape, 1) // NA_HD
    return _bf(jnp.concatenate([jnp.where(head == h, qf, 0.0) for h in range(N_HEADS)], axis=0))


def _unstack_heads(res):
    n = res.shape[0] // N_HEADS
    head = lax.broadcasted_iota(jnp.int32, (n, res.shape[1]), 1) // NA_HD
    out = res[0:n]
    for h in range(1, N_HEADS):
        out = jnp.where(head == h, res[h * n:(h + 1) * n], out)
    return out


def _softmax_pv(scores, values):
    m = functools.reduce(jnp.maximum, [jnp.max(s, axis=-1, keepdims=True) for s in scores])
    ps = [jnp.exp(s - m) for s in scores]
    l = functools.reduce(jnp.add, [jnp.sum(p, axis=-1, keepdims=True) for p in ps])
    acc = functools.reduce(jnp.add, [_dot(_bf(p), v) for p, v in zip(ps, values)])
    return acc / l


def _na_kernel(q_ref, k_ref, v_ref, kc_ref, vc_ref, tbl_ref, o_ref, *, rows, rb):
    jb = pl.program_id(1)
    kc = kc_ref[0]
    vc = vc_ref[0]
    n_loc = NA_WIN_R * GRID_W
    staged = []
    for rr in range(rb):
        r = jb * rb + rr
        rs = jnp.clip(r - NA_WIN_R // 2, 0, rows - NA_WIN_R)
        start = pl.multiple_of(rs * GRID_W, GRID_W)
        qm = _stack_heads(q_ref[0, rr * GRID_W:(rr + 1) * GRID_W, :])
        s_loc = _dot_nt(qm, k_ref[0, pl.ds(start, n_loc), :]) + tbl_ref[r - rs]
        staged.append((start, s_loc, _dot_nt(qm, kc)))
    for rr, (start, s_loc, s_ctx) in enumerate(staged):
        res = _softmax_pv([s_loc, s_ctx], [v_ref[0, pl.ds(start, n_loc), :], vc])
        o_ref[0, rr * GRID_W:(rr + 1) * GRID_W, :] = _unstack_heads(res)


def _na_attention(q, k, v, kc, vc, tbl, rb):
    b, t, _ = q.shape
    tc = kc.shape[1]
    rows = t // GRID_W
    assert rows >= NA_WIN_R and rows % rb == 0
    seq = pl.BlockSpec((1, t, 256), lambda i, j: (i, 0, 0))
    cseq = pl.BlockSpec((1, tc, 256), lambda i, j: (i, 0, 0))
    return pl.pallas_call(
        functools.partial(_na_kernel, rows=rows, rb=rb),
        grid=(b, rows // rb),
        in_specs=[pl.BlockSpec((1, rb * GRID_W, 256), lambda i, j: (i, j, 0)), seq, seq, cseq, cseq,
                  _full(tbl, 2)],
        out_specs=pl.BlockSpec((1, rb * GRID_W, 256), lambda i, j: (i, j, 0)),
        out_shape=jax.ShapeDtypeStruct((b, t, 256), jnp.float32),
        compiler_params=_cparams(("parallel", "parallel")),
        name="na_attention",
    )(q, k, v, kc, vc, tbl)


def _ctx_na_kernel(q_ref, kc_ref, vc_ref, o_ref):
    kc = kc_ref[0]
    vc = vc_ref[0]
    for rr in range(q_ref.shape[1] // GRID_W):
        qm = _stack_heads(q_ref[0, rr * GRID_W:(rr + 1) * GRID_W, :])
        res = _softmax_pv([_dot_nt(qm, kc)], [vc])
        o_ref[0, rr * GRID_W:(rr + 1) * GRID_W, :] = _unstack_heads(res)


def _ctx_na_attention(q, kc, vc):
    b, tc, _ = q.shape
    cseq = pl.BlockSpec((1, tc, 256), lambda i: (i, 0, 0))
    return pl.pallas_call(
        _ctx_na_kernel,
        grid=(b,),
        in_specs=[cseq, cseq, cseq],
        out_specs=cseq,
        out_shape=jax.ShapeDtypeStruct((b, tc, 256), jnp.float32),
        compiler_params=_cparams(("parallel",)),
        name="ctx_na_attention",
    )(q, kc, vc)


def _hg_gates(z, lb):
    e = jnp.exp(-jnp.abs(z))
    r = 1.0 / (1.0 + e)
    pos = z >= 0
    sig = jnp.where(pos, r, e * r)
    sig_neg = jnp.where(pos, e * r, r)
    return jnp.log(lb + (1.0 - lb) * sig), (1.0 - lb) * sig_neg


def _hg_prep_kernel(zf_ref, zb_ref, i_ref, q_ref, lbf_ref, lbb_ref, tl_ref, tu_ref, ta_ref, sel_ref, e_ref,
                    qf_o, kf_o, qb_o, kb_o, i_o, df_o, db_o, oi_o, nat_ref, pm_ref,
                    *, tm, qscale):
    nch = tm // HG_SUB
    q = q_ref[0] * qscale
    xi = i_ref[0]
    i_o[0] = _bf(xi)
    logf_f, kg_f = _hg_gates(zf_ref[0], lbf_ref[...])
    logf_b, kg_b = _hg_gates(zb_ref[0], lbb_ref[...])
    b_f = _split3_dot_left(tl_ref[...], logf_f)
    b_b = _split3_dot_left(tu_ref[...], logf_b)
    tot_f = _split3_dot_left(ta_ref[...], logf_f)
    tot_b = _split3_dot_left(ta_ref[...], logf_b)
    qf_o[0] = _bf(q * jnp.exp(b_f))
    kf_o[0] = _bf(kg_f * jnp.exp(tot_f - b_f))
    qb_o[0] = _bf(q * jnp.exp(b_b))
    kb_o[0] = _bf(kg_b * jnp.exp(tot_b - b_b))
    df_o[0] = jnp.exp(_split3_dot_left(sel_ref[...], logf_f))
    db_o[0] = jnp.exp(_split3_dot_left(sel_ref[...], logf_b))

    log2e = float(np.log2(np.e))
    for a, val in enumerate((q, kg_f, b_f * log2e, kg_b, b_b * log2e, xi)):
        for half in range(2):
            nat_ref[a, half] = val[:, half * LANE:(half + 1) * LANE]
            for p in range(HG_SUB):
                pm_ref[a, half, p * nch:(p + 1) * nch, :] = nat_ref[a, half, pl.ds(p, nch, stride=HG_SUB), :]
    e = e_ref[...]

    def side(q_t, b_t, k_ref, b_ref, i_ref, lo, nblk):
        rows = slice(lo, lo + nblk * nch)
        k3 = k_ref[rows, :].reshape(nblk, nch, LANE)
        b3 = b_ref[rows, :].reshape(nblk, nch, LANE)
        x = q_t[None] * k3 * jnp.exp2(b_t[None] - b3)
        r = _dot(_bf(x.reshape(nblk * nch, LANE)), e) * i_ref[rows, :]
        return jnp.sum(r.reshape(nblk, nch, LANE), axis=0)

    for half in range(2):
        qp, kfp, bfp, kbp, bbp, ip = [pm_ref.at[a, half] for a in range(6)]
        for pt in range(HG_SUB):
            rt = slice(pt * nch, (pt + 1) * nch)
            q_t = qp[rt, :]
            acc = _dot(_bf(q_t * (kfp[rt, :] + kbp[rt, :])), e) * ip[rt, :]
            if pt > 0:
                acc = acc + side(q_t, bfp[rt, :], kfp, bfp, ip, 0, pt)
            if pt < HG_SUB - 1:
                acc = acc + side(q_t, bbp[rt, :], kbp, bbp, ip, (pt + 1) * nch, HG_SUB - 1 - pt)
            nat_ref[0, half, pl.ds(pt, nch, stride=HG_SUB), :] = acc
        oi_o[0, :, half * LANE:(half + 1) * LANE] = nat_ref[0, half]


def _hg_prep(uq, qcol, u, base, lbf, lbb, e_mat):
    b, t, _ = u.shape
    tm = min(HG_PREP_TILE, t)
    nsub = tm // HG_SUB
    tok = pl.BlockSpec((1, tm, 256), lambda i, j: (i, j, 0))
    dec = pl.BlockSpec((1, nsub, 256), lambda i, j: (i, j, 0))
    tl, tu, ta, sel = _hg_consts(tm)
    e_half = e_mat[:LANE, :LANE]
    regroup = pltpu.VMEM((6, 2, tm, LANE), jnp.float32)
    return pl.pallas_call(
        functools.partial(_hg_prep_kernel, tm=tm, qscale=HG_DK ** -0.5),
        grid=(b, t // tm),
        in_specs=[_ublk(tm, COL["zF"], base, 256), _ublk(tm, COL["zB"], base, 256),
                  _ublk(tm, COL["i_hg"], base, 256),
                  pl.BlockSpec((1, tm, 256), lambda i, j: (i, j, qcol)),
                  _full(lbf, 2), _full(lbb, 2), _full(tl, 2), _full(tu, 2), _full(ta, 2), _full(sel, 2),
                  _full(e_half, 2)],
        out_specs=[tok, tok, tok, tok, tok, dec, dec, tok],
        out_shape=[jax.ShapeDtypeStruct((b, t, 256), MXU_DTYPE)] * 5
                  + [jax.ShapeDtypeStruct((b, t // HG_SUB, 256), jnp.float32)] * 2
                  + [jax.ShapeDtypeStruct((b, t, 256), jnp.float32)],
        scratch_shapes=[regroup, regroup],
        compiler_params=_cparams(("parallel", "parallel")),
        name="hg_prep",
    )(u, u, u, uq, lbf, lbb, tl, tu, ta, sel, e_half)


def _hg_scan_kernel(qf_ref, kf_ref, if_ref, df_ref, qb_ref, kb_ref, ib_ref, db_ref, s0f_ref, s0b_ref,
                    of_ref, ob_ref, sf_out, sb_out, sf, sb, *, nsub, nb):
    j = pl.program_id(0)

    @pl.when(j == 0)
    def _():
        sf[...] = s0f_ref[...]
        sb[...] = s0b_ref[...]

    first = lax.broadcasted_iota(jnp.int32, (HG_SUB, LANE), 1) < HG_DK

    def step(state, q_ref, k_ref, i_ref, d_ref, o_ref, bi, c):
        rows = slice(c * HG_SUB, (c + 1) * HG_SUB)
        for p in range(2):
            cols = slice(p * LANE, (p + 1) * LANE)
            s_old = state[bi, p]
            q = q_ref[bi, rows, cols].astype(jnp.float32)
            q2 = _bf(jnp.concatenate([jnp.where(first, q, 0.0), jnp.where(first, 0.0, q)], axis=0))
            r = _dot_nt(q2, _bf(s_old))
            o_ref[bi, rows, cols] = jnp.where(first, r[:HG_SUB], r[HG_SUB:])
            upd = _dot_tn(i_ref[bi, rows, cols], k_ref[bi, rows, cols])
            state[bi, p] = s_old * d_ref[bi, c:c + 1, cols] + upd

    for c in range(nsub):
        for bi in range(nb):
            step(sf, qf_ref, kf_ref, if_ref, df_ref, of_ref, bi, c)
            step(sb, qb_ref, kb_ref, ib_ref, db_ref, ob_ref, bi, nsub - 1 - c)

    @pl.when(j == pl.num_programs(0) - 1)
    def _():
        sf_out[...] = sf[...]
        sb_out[...] = sb[...]


def _hg_scan(qf, kf, qb, kb, xi, df, db, s0f, s0b):
    b, t, _ = qf.shape
    tm = HG_TILE
    n = t // tm
    nsub = tm // HG_SUB
    fwd = pl.BlockSpec((b, tm, 256), lambda j: (0, j, 0))
    bwd = pl.BlockSpec((b, tm, 256), lambda j: (0, n - 1 - j, 0))
    dfwd = pl.BlockSpec((b, nsub, 256), lambda j: (0, j, 0))
    dbwd = pl.BlockSpec((b, nsub, 256), lambda j: (0, n - 1 - j, 0))
    st = pl.BlockSpec((b, 2, LANE, LANE), lambda j: (0, 0, 0, 0))
    return pl.pallas_call(
        functools.partial(_hg_scan_kernel, nsub=nsub, nb=b),
        grid=(n,),
        in_specs=[fwd, fwd, fwd, dfwd, bwd, bwd, bwd, dbwd, st, st],
        out_specs=[fwd, bwd, st, st],
        out_shape=[jax.ShapeDtypeStruct((b, t, 256), jnp.float32)] * 2
                  + [jax.ShapeDtypeStruct((b, 2, LANE, LANE), jnp.float32)] * 2,
        scratch_shapes=[pltpu.VMEM((b, 2, LANE, LANE), jnp.float32)] * 2,
        compiler_params=_cparams(("arbitrary",)),
        name="hg_scan",
    )(qf, kf, xi, df, qb, kb, xi, db, s0f, s0b)


def _merge_kernel(x_ref, a_ref, b_ref, c_ref, of_ref, ob_ref, oi_ref, gc_ref, g_ref, sh_ref, sc_ref, wg_ref,
                  hgn_ref, e_ref, wbr_ref, wo_ref, gate_ref, out_ref):
    d_model = x_ref.shape[2]
    x = x_ref[0]
    h = _bf((x * _rms_scale(x) * g_ref[...]) * (1.0 + sc_ref[0]) + sh_ref[0])
    o = of_ref[0] + ob_ref[0] + oi_ref[0]
    ms = _split_dot(o * o, e_ref[...]) * (1.0 / HG_DK)
    d = o * lax.rsqrt(ms + EPS) * hgn_ref[...] * _sigmoid(gc_ref[0])
    acc = None
    for n, y in enumerate((a_ref[0], b_ref[0], c_ref[0], d)):
        gate = _sigmoid(_dot(h, wg_ref[:, n * d_model:(n + 1) * d_model]))
        term = gate * _dot(_bf(y), wbr_ref[n])
        acc = term if acc is None else acc + term
    out_ref[0] = x + gate_ref[0] * _dot(_bf(acc), wo_ref[...])


def _merge(x, a, bm, cn, o_f, o_b, o_i, u, g1n, w_gate, hgn, e_mat, w_br, w_o, mods, tm):
    b, t, d = x.shape
    tok = pl.BlockSpec((1, tm, 256), lambda i, j: (i, j, 0))
    xs = pl.BlockSpec((1, tm, d), lambda i, j: (i, j, 0))
    return pl.pallas_call(
        _merge_kernel,
        grid=(b, t // tm),
        in_specs=[xs, tok, tok, tok, tok, tok, tok, _ublk(tm, COL["g_cols"], 0, 256),
                  _full(g1n, 2), _mod_spec(mods, 0, d), _mod_spec(mods, 1, d), _full(w_gate, 2),
                  _full(hgn, 2), _full(e_mat, 2), _full(w_br, 2), _full(w_o, 2), _mod_spec(mods, 2, d)],
        out_specs=xs,
        out_shape=jax.ShapeDtypeStruct((b, t, d), jnp.float32),
        compiler_params=_cparams(("parallel", "parallel")),
        name="merge",
    )(x, a, bm, cn, o_f, o_b, o_i, u, g1n, mods, mods, w_gate, hgn, e_mat, w_br, w_o, mods)


def _ffn_kernel(x_ref, g_ref, sh_ref, sc_ref, w1_ref, w2_ref, gate_ref, o_ref, *, tf):
    x = x_ref[0]
    h = _bf((x * _rms_scale(x) * g_ref[...]) * (1.0 + sc_ref[0]) + sh_ref[0])
    acc = None
    for k in range(w1_ref.shape[1] // tf):
        z = jnp.maximum(_dot(h, w1_ref[:, k * tf:(k + 1) * tf]), 0.0)
        part = _dot(_bf(z * z), w2_ref[k * tf:(k + 1) * tf, :])
        acc = part if acc is None else acc + part
    o_ref[0] = x + gate_ref[0] * acc


def _ffn(x, g, mods, w1, w2, layer, tm, tf):
    b, t, d = x.shape
    xs = pl.BlockSpec((1, tm, d), lambda i, j: (i, j, 0))
    resident = lambda a: pl.BlockSpec((None,) + a.shape[1:], lambda i, j: (layer, 0, 0),
                                      pipeline_mode=pl.Buffered(1))
    return pl.pallas_call(
        functools.partial(_ffn_kernel, tf=tf),
        grid=(b, t // tm),
        in_specs=[xs, pl.BlockSpec((1, d), lambda i, j: (0, 0)), _mod_spec(mods, 3, d), _mod_spec(mods, 4, d),
                  resident(w1), resident(w2), _mod_spec(mods, 5, d)],
        out_specs=xs,
        out_shape=jax.ShapeDtypeStruct((b, t, d), jnp.float32),
        compiler_params=_cparams(("parallel", "parallel")),
        name="ffn",
    )(x, g, mods, mods, w1, w2, mods)


def _rope_perm():
    return np.array([(t // 16) * 16 + ((t % 16) + 8) % 16 for t in range(MLA_ROPE)])


def _prep_w_in(w):
    d = w.shape[0]
    o = dict(ckv=0, kr=128, k_na=160, v_na=416, zF=672, zB=928, i_hg=1184, pool=1440, qa=1696, q_na=1952,
             q_hg=2208, g_cols=2464, gates=2720)
    z = lambda n: jnp.zeros((d, n), w.dtype)
    k_rope = w[:, o["kr"]:o["kr"] + MLA_ROPE]
    parts = [w[:, o[name]:o[name] + 256]
             for name in ("pool", "qa", "q_na", "q_hg", "g_cols", "k_na", "v_na", "zF", "zB", "i_hg")]
    parts += [w[:, 0:128], z(64), k_rope, z(32), z(64), k_rope[:, _rope_perm()], z(32), z(128)]
    return _bf(w[:, o["gates"]:]), _bf(jnp.concatenate(parts, axis=1))


def _prep_wq(w):
    r = w.shape[0]
    wh = w.reshape(r, N_HEADS, MLA_QK)
    pre = jnp.pad(wh, ((0, 0), (0, 0), (0, HEAD_PAD - MLA_QK)))
    sw = jnp.pad(wh[:, :, MLA_NOPE:][:, :, _rope_perm()], ((0, 0), (0, 0), (MLA_NOPE, HEAD_PAD - MLA_QK)))
    return _bf(jnp.concatenate([pre.reshape(r, -1), sw.reshape(r, -1)], axis=1))


def _prep_wkv(w):
    r = w.shape[0]
    wh = w.reshape(r, N_HEADS, MLA_NOPE + MLA_V)
    kpart = jnp.pad(wh[:, :, :MLA_NOPE], ((0, 0), (0, 0), (0, HEAD_PAD - MLA_NOPE)))
    return _bf(jnp.concatenate([kpart.reshape(r, -1), wh[:, :, MLA_NOPE:].reshape(r, -1)], axis=1))


def _pad_gain(g):
    plain = jnp.pad(g, (0, HEAD_PAD - MLA_QK))[None]
    sw = jnp.pad(g[MLA_NOPE:][_rope_perm()], (MLA_NOPE, HEAD_PAD - MLA_QK))[None]
    return plain, sw


def _rope_tables(n_tok):
    t = jnp.arange(n_tok)
    pos = jnp.stack([t // GRID_W, t % GRID_W], axis=-1).astype(jnp.float32)
    nf = MLA_ROPE // 4
    inv = ROPE_BASE ** (-jnp.arange(nf, dtype=jnp.float32) / nf)
    ang = pos[:, :, None] * inv
    cos, sin = jnp.cos(ang), jnp.sin(ang)
    cos_t = jnp.concatenate([cos, cos], axis=-1).reshape(n_tok, MLA_ROPE)
    sin_t = jnp.concatenate([-sin, sin], axis=-1).reshape(n_tok, MLA_ROPE)
    ones = jnp.ones((n_tok, MLA_NOPE), jnp.float32)
    zer = jnp.zeros((n_tok, HEAD_PAD - MLA_QK), jnp.float32)
    cos_p = jnp.concatenate([ones, cos_t, zer], axis=-1)
    sin_p = jnp.concatenate([0 * ones, sin_t, zer], axis=-1)
    return cos_p, sin_p


def _plain_tables(n_tok):
    cos_p = jnp.concatenate([jnp.ones((n_tok, MLA_QK), jnp.float32),
                             jnp.zeros((n_tok, HEAD_PAD - MLA_QK), jnp.float32)], axis=-1)
    return cos_p, jnp.zeros((n_tok, HEAD_PAD), jnp.float32)


def _head_indicator(dtype):
    i = np.arange(256) // 64
    return jnp.asarray((i[:, None] == i[None, :]).astype(np.float32), dtype=dtype)


def _na_bias_table(rpb):
    qc = np.arange(GRID_W)[:, None]
    kc = np.arange(GRID_W)[None, :]
    cs = np.clip(qc - NA_WIN_C // 2, 0, GRID_W - NA_WIN_C)
    inwin = (kc >= cs) & (kc < cs + NA_WIN_C)
    dc = np.clip(kc - qc + NA_WIN_C - 1, 0, 2 * NA_WIN_C - 2)
    onehot = (dc[:, :, None] == np.arange(2 * NA_WIN_C - 1)).astype(np.float32)
    t = jnp.einsum('hrd,qkd->hqrk', rpb.astype(jnp.float32), onehot, precision=lax.Precision.HIGHEST)
    t = jnp.where(inwin[None, :, None, :], t, NEG)
    tbl = jnp.stack([t[:, :, NA_WIN_R - 1 - p:2 * NA_WIN_R - 1 - p] for p in range(NA_WIN_R)], axis=0)
    return tbl.reshape(NA_WIN_R, N_HEADS * GRID_W, NA_WIN_R * GRID_W)


def _hg_consts(tm):
    t = np.arange(tm)
    same = (t[:, None] // HG_SUB) == (t[None, :] // HG_SUB)
    tl = same & (t[None, :] <= t[:, None])
    tu = same & (t[None, :] >= t[:, None])
    sel = (np.arange(tm // HG_SUB)[:, None] == (t[None, :] // HG_SUB))
    f = lambda m: jnp.asarray(m.astype(np.float32), dtype=MXU_DTYPE)
    return f(tl), f(tu), f(same), f(sel)


def _block_diag(w):
    g, c, d = w.shape
    out = jnp.zeros((g * c, g * d), w.dtype)
    for i in range(g):
        out = out.at[i * c:(i + 1) * c, i * d:(i + 1) * d].set(w[i])
    return out


def kernel(x, c, ctx, c_ctx, ada_w, ada_b, norm1, norm2, w_in, pool_w, pool_scale, mla_q_norm, mla_wq_b, mla_kv_norm, mla_wkv_b, mla_gq, mla_gk, na_gq, na_gk, na_rpb, hg_lb, hg_norm, w_branch, w_out, w_ff1, w_ff2):
    B, L, D = x.shape
    Tc = ctx.shape[1]
    depth = ada_w.shape[0]
    f32 = jnp.float32
    assert L % HG_TILE == 0 and Tc % HG_TILE == 0 and B < 8

    lb_p = jax.nn.softmax(hg_lb.astype(f32), axis=0)
    lb_all = jnp.clip(jnp.cumsum(lb_p, axis=0)[:depth], LB_EPS, 1.0 - LB_EPS)

    cc = jnp.concatenate([c, c_ctx[None], jnp.zeros((8 - B - 1, D), f32)], axis=0)
    mods = _mods(cc, ada_w, ada_b)

    cos_l, sin_l = _rope_tables(L)
    cos_c, sin_c = _plain_tables(Tc)
    e_mat = _head_indicator(MXU_DTYPE)
    zero_state = jnp.zeros((B, 2, LANE, LANE), f32)
    zero_q = jnp.zeros((B, Tc, 256), f32)
    tl = min(512, L)
    t_big = min(1024, L)

    w1 = _bf(w_ff1)
    w2 = _bf(w_ff2)
    xl, xc = x, ctx
    for l in range(depth):
        ctx_out = l < depth - 1
        mods_l = mods[l, :B].reshape(B, 1, 6 * D)
        mods_c = mods[l, B].reshape(1, 1, 6 * D)
        g1n = norm1[l][None]
        g2n = norm2[l][None]
        w_gate, w_mix = _prep_w_in(w_in[l])
        wq = _prep_wq(mla_wq_b[l])
        wkv = _prep_wkv(mla_wkv_b[l])
        gq, gqsw = _pad_gain(mla_gq[l])
        gk, gksw = _pad_gain(mla_gk[l])
        qn = mla_q_norm[l][None]
        kvn = mla_kv_norm[l][None]
        na_q_gain = (jnp.tile(na_gq[l], N_HEADS) * NA_HD ** -0.5)[None]
        na_k_gain = jnp.tile(na_gk[l], N_HEADS)[None]
        na_tbl = _na_bias_table(na_rpb[l])
        pool_bd = _bf(_block_diag(pool_w[l]))
        pool_sc = pool_scale[l][None]
        hgn = jnp.tile(hg_norm[l], N_HEADS)[None]
        lbF = lb_all[l, 0][None]
        lbB = lb_all[l, 1][None]
        w_br = _bf(w_branch[l])
        w_o = _bf(w_out[l])

        ul = _inproj(xl, g1n, mods_l, w_mix, tl, D_MIX_PAD, f32)
        if ctx_out:
            uc, cbase = _inproj(xc, g1n, mods_c, w_mix, Tc, MIX_TILE, f32), 0
        else:
            uc, cbase = _inproj(xc, g1n, mods_c, w_mix[:, KV_START:], Tc, KV_TILE, f32), KV_START

        q_mla, k_mla, v_mla = _mla_prep(ul, 0, cos_l, sin_l, qn, wq, kvn, wkv, gq, gqsw, gk, gksw, t_big)
        if ctx_out:
            q_mla_c, k_mla_c, v_mla_c = _mla_prep(uc, cbase, cos_c, sin_c, qn, wq, kvn, wkv, gq, gqsw, gk, gksw, Tc)
        else:
            k_mla_c, v_mla_c = _mla_kv_prep(uc, cbase, kvn, wkv, gk, Tc)
        b_l = _attention(q_mla, k_mla_c, v_mla_c, k_mla, v_mla, tl, Tc, tl)

        q_na, k_na, v_na = _na_prep(ul, COL["q_na"] // 256, ul, 0, na_q_gain, na_k_gain, e_mat, t_big)
        if ctx_out:
            q_na_c, k_na_c, v_na_c = _na_prep(uc, COL["q_na"] // 256, uc, cbase, na_q_gain, na_k_gain, e_mat, Tc)
        else:
            _, k_na_c, v_na_c = _na_prep(zero_q, 0, uc, cbase, na_q_gain, na_k_gain, e_mat, Tc)
        c_l = _na_attention(q_na, k_na, v_na, k_na_c, v_na_c, na_tbl, 8)

        a_l = _pool(ul, 0, pool_bd, pool_sc, t_big)

        if ctx_out:
            hc = _hg_prep(uc, COL["q_hg"] // 256, uc, cbase, lbF, lbB, e_mat)
        else:
            hc = _hg_prep(zero_q, 0, uc, cbase, lbF, lbB, e_mat)
        oFc, oBc, sFc, sBc = _hg_scan(hc[0], hc[1], hc[2], hc[3], hc[4], hc[5], hc[6], zero_state, zero_state)
        hl = _hg_prep(ul, COL["q_hg"] // 256, ul, 0, lbF, lbB, e_mat)
        oFl, oBl, _, _ = _hg_scan(hl[0], hl[1], hl[2], hl[3], hl[4], hl[5], hl[6], sFc, sBc)

        xl_new = _merge(xl, a_l, b_l, c_l, oFl, oBl, hl[7], ul, g1n, w_gate, hgn, e_mat, w_br, w_o, mods_l, tl)

        if ctx_out:
            a_c = _pool(uc, cbase, pool_bd, pool_sc, Tc)
            b_c = _attention(q_mla_c, k_mla_c, v_mla_c, None, None, Tc, Tc, Tc)
            c_c = _ctx_na_attention(q_na_c, k_na_c, v_na_c)
            xc = _merge(xc, a_c, b_c, c_c, oFc, oBc, hc[7], uc, g1n, w_gate, hgn, e_mat, w_br, w_o, mods_c, Tc)
            xc = _ffn(xc, g2n, mods_c, w1, w2, l, Tc, 1024)

        xl = _ffn(xl_new, g2n, mods_l, w1, w2, l, tl, 2048)
    return xl
```

```python
import functools

import jax
import jax.numpy as jnp
import numpy as np
from jax import lax
from jax.experimental import pallas as pl
from jax.experimental.pallas import tpu as pltpu

D_MODEL = 1024
GRID_W = 64
EPS = 1e-6
N_BRANCH = 4
BRANCH_W = 256
POOL_GC = 64
N_HEADS = 4
MLA_NOPE = 64
MLA_ROPE = 32
MLA_V = 64
MLA_QK = 96
ROPE_BASE = 10000.0
NA_HD = 64
NA_WIN_R = 8
NA_WIN_C = 16
HG_DK = 64
LB_EPS = 1e-6

MXU_DTYPE = jnp.bfloat16
LANE = 128
HEAD_PAD = 128
ONES_ROWS = 16
VMEM_LIMIT = 48 * 1024 * 1024
HG_SUB = 16
HG_TILE = 256
HG_PREP_TILE = 256
NEG = -1e30

COL = dict(pool=0, qa=256, q_na=512, q_hg=768, g_cols=1024,
           k_na=1280, v_na=1536, zF=1792, zB=2048, i_hg=2304, ckv=2560, kr=2688, kr_sw=2816)
D_MIX_PAD = 3072
KV_START = 1280
MIX_TILE = 1536
KV_TILE = 256


def _cparams(sem):
    return pltpu.CompilerParams(dimension_semantics=sem, vmem_limit_bytes=VMEM_LIMIT)


def _bf(x):
    return x.astype(MXU_DTYPE)


def _dot(a, b):
    return jnp.dot(a, b, preferred_element_type=jnp.float32)


def _dot_nt(a, b):
    return lax.dot_general(a, b, (((1,), (1,)), ((), ())), preferred_element_type=jnp.float32)


def _dot_tn(a, b):
    return lax.dot_general(a, b, (((0,), (0,)), ((), ())), preferred_element_type=jnp.float32)


def _split2(x):
    hi = _bf(x)
    return hi, _bf(x - hi.astype(jnp.float32))


def _split_dot(x, w):
    hi, lo = _split2(x)
    return _dot(hi, w) + _dot(lo, w)


def _split3_dot_left(w, x):
    hi = _bf(x)
    r1 = x - hi.astype(jnp.float32)
    mid = _bf(r1)
    lo = _bf(r1 - mid.astype(jnp.float32))
    return _dot(w, hi) + _dot(w, mid) + _dot(w, lo)


def _sigmoid(x):
    return 0.5 * jnp.tanh(0.5 * x) + 0.5


def _rms_scale(x):
    return lax.rsqrt(jnp.mean(x * x, axis=-1, keepdims=True) + EPS)


def _full(arr, nargs):
    zeros = (0,) * arr.ndim
    return pl.BlockSpec(arr.shape, lambda *_: zeros)


def _ublk(tm, col, base, width):
    idx = (col - base) // width
    return pl.BlockSpec((1, tm, width), lambda i, j: (i, j, idx))


def _mods_kernel(c_ref, w_ref, b_ref, o_ref):
    c = c_ref[...]
    s = _bf(c * _sigmoid(c))
    o_ref[0] = _dot(s, _bf(w_ref[0])) + b_ref[0]


def _mods(cc, ada_w, ada_b):
    depth, d, n = ada_w.shape
    tn = 1536
    return pl.pallas_call(
        _mods_kernel,
        grid=(depth, n // tn),
        in_specs=[pl.BlockSpec((8, d), lambda l, j: (0, 0)),
                  pl.BlockSpec((1, d, tn), lambda l, j: (l, 0, j)),
                  pl.BlockSpec((1, 1, tn), lambda l, j: (l, 0, j))],
        out_specs=pl.BlockSpec((1, 8, tn), lambda l, j: (l, 0, j)),
        out_shape=jax.ShapeDtypeStruct((depth, 8, n), jnp.float32),
        compiler_params=_cparams(("arbitrary", "arbitrary")),
        name="mods",
    )(cc, ada_w, ada_b.reshape(depth, 1, n))


def _inproj_kernel(x_ref, g_ref, sh_ref, sc_ref, w_ref, o_ref, h_ref):
    @pl.when(pl.program_id(2) == 0)
    def _():
        x = x_ref[0]
        y = x * _rms_scale(x) * g_ref[...]
        h_ref[...] = _bf(y * (1.0 + sc_ref[0]) + sh_ref[0])

    o_ref[0] = _dot(h_ref[...], w_ref[...]).astype(o_ref.dtype)


def _mod_spec(mods, idx, d):
    if mods.shape[0] == 1:
        return pl.BlockSpec((1, 1, d), lambda i, *_: (0, 0, idx))
    return pl.BlockSpec((1, 1, d), lambda i, *_: (i, 0, idx))


def _inproj(x, g, mods, w, layer, col0, n, tm, tn, out_dtype):
    b, t, d = x.shape
    assert n % tn == 0 and col0 % tn == 0
    return pl.pallas_call(
        _inproj_kernel,
        grid=(b, t // tm, n // tn),
        in_specs=[pl.BlockSpec((1, tm, d), lambda i, j, k: (i, j, 0)),
                  pl.BlockSpec((1, d), lambda i, j, k: (0, 0)),
                  _mod_spec(mods, 0, d), _mod_spec(mods, 1, d),
                  pl.BlockSpec((None, d, tn), lambda i, j, k: (layer, 0, k + col0 // tn))],
        out_specs=pl.BlockSpec((1, tm, tn), lambda i, j, k: (i, j, k)),
        out_shape=jax.ShapeDtypeStruct((b, t, n), out_dtype),
        scratch_shapes=[pltpu.VMEM((tm, d), MXU_DTYPE)],
        compiler_params=_cparams(("parallel", "parallel", "arbitrary")),
        name="inproj",
    )(x, g, mods, mods, w)


def _mla_prep_kernel(qa_ref, ckv_ref, kr_ref, krsw_ref, cos_ref, sin_ref, qn_ref, wq_ref, kvn_ref, wkv_ref,
                     gq_ref, gqsw_ref, gk_ref, gksw_ref, q_ref, k_ref, vt_ref, *, scale):
    cos = cos_ref[...]
    sin = sin_ref[...]
    inv_w = 1.0 / MLA_QK
    qa = qa_ref[0]
    qq = _dot(_bf(qa * _rms_scale(qa) * qn_ref[...]), wq_ref[...])
    ckv = ckv_ref[0]
    kv = _dot(_bf(ckv * _rms_scale(ckv) * kvn_ref[...]), wkv_ref[...])
    kr = kr_ref[0]
    kr_rot = krsw_ref[0] * gksw_ref[...] * sin
    gq_cos = gq_ref[...] * cos
    gq_sin = gqsw_ref[...] * sin
    gk_cos = gk_ref[...] * cos
    for h in range(N_HEADS):
        qp = qq[:, h * HEAD_PAD:(h + 1) * HEAD_PAD]
        qs = qq[:, (N_HEADS + h) * HEAD_PAD:(N_HEADS + h + 1) * HEAD_PAD]
        rs = lax.rsqrt(jnp.sum(qp * qp, axis=-1, keepdims=True) * inv_w + EPS)
        q_ref[0, h] = _bf((qp * gq_cos + qs * gq_sin) * (rs * scale))
        kp = kv[:, h * HEAD_PAD:(h + 1) * HEAD_PAD] + kr
        rs = lax.rsqrt(jnp.sum(kp * kp, axis=-1, keepdims=True) * inv_w + EPS)
        k_ref[0, h] = _bf((kp * gk_cos + kr_rot) * rs)
    vt_ref[0] = _bf(kv[:, N_HEADS * HEAD_PAD:].T)


def _mla_prep(u, base, cos, sin, qn, wq, kvn, wkv, gq, gqsw, gk, gksw, tm):
    b, t, _ = u.shape
    tab = pl.BlockSpec((tm, HEAD_PAD), lambda i, j: (j, 0))
    hd = pl.BlockSpec((1, N_HEADS, tm, HEAD_PAD), lambda i, j: (i, 0, j, 0))
    consts = (qn, wq, kvn, wkv, gq, gqsw, gk, gksw)
    return pl.pallas_call(
        functools.partial(_mla_prep_kernel, scale=MLA_QK ** -0.5 * float(np.log2(np.e))),
        grid=(b, t // tm),
        in_specs=[_ublk(tm, COL["qa"], base, 256), _ublk(tm, COL["ckv"], base, 128),
                  _ublk(tm, COL["kr"], base, 128), _ublk(tm, COL["kr_sw"], base, 128), tab, tab]
                 + [_full(a, 2) for a in consts],
        out_specs=[hd, hd, pl.BlockSpec((1, 256, tm), lambda i, j: (i, 0, j))],
        out_shape=[jax.ShapeDtypeStruct((b, N_HEADS, t, HEAD_PAD), MXU_DTYPE),
                   jax.ShapeDtypeStruct((b, N_HEADS, t, HEAD_PAD), MXU_DTYPE),
                   jax.ShapeDtypeStruct((b, 256, t), MXU_DTYPE)],
        compiler_params=_cparams(("parallel", "parallel")),
        name="mla_prep",
    )(u, u, u, u, cos, sin, *consts)


def _mla_kv_prep_kernel(ckv_ref, kr_ref, kvn_ref, wkv_ref, gk_ref, k_ref, vt_ref):
    inv_w = 1.0 / MLA_QK
    ckv = ckv_ref[0]
    kv = _dot(_bf(ckv * _rms_scale(ckv) * kvn_ref[...]), wkv_ref[...])
    kr = kr_ref[0]
    for h in range(N_HEADS):
        kp = kv[:, h * HEAD_PAD:(h + 1) * HEAD_PAD] + kr
        rs = lax.rsqrt(jnp.sum(kp * kp, axis=-1, keepdims=True) * inv_w + EPS)
        k_ref[0, h] = _bf(kp * gk_ref[...] * rs)
    vt_ref[0] = _bf(kv[:, N_HEADS * HEAD_PAD:].T)


def _mla_kv_prep(u, base, kvn, wkv, gk, tm):
    b, t, _ = u.shape
    hd = pl.BlockSpec((1, N_HEADS, tm, HEAD_PAD), lambda i, j: (i, 0, j, 0))
    return pl.pallas_call(
        _mla_kv_prep_kernel,
        grid=(b, t // tm),
        in_specs=[_ublk(tm, COL["ckv"], base, 128), _ublk(tm, COL["kr"], base, 128),
                  _full(kvn, 2), _full(wkv, 2), _full(gk, 2)],
        out_specs=[hd, pl.BlockSpec((1, 256, tm), lambda i, j: (i, 0, j))],
        out_shape=[jax.ShapeDtypeStruct((b, N_HEADS, t, HEAD_PAD), MXU_DTYPE),
                   jax.ShapeDtypeStruct((b, 256, t), MXU_DTYPE)],
        compiler_params=_cparams(("parallel", "parallel")),
        name="mla_kv_prep",
    )(u, u, kvn, wkv, gk)


def _attn_kernel(*refs, tkc, tk, n_ctx, n_lat):
    if n_lat:
        q_ref, kc_ref, vtc_ref, kl_ref, vtl_ref, o_ref, sa_ref, sb_ref = refs
    else:
        q_ref, kc_ref, vtc_ref, o_ref, sa_ref, sb_ref = refs
    tq = q_ref.shape[2]
    qs = [q_ref[0, h] for h in range(N_HEADS)]

    def produce(k_ref, c, s_ref, n):
        start = pl.multiple_of(c * n, n)
        for h in range(N_HEADS):
            s_ref[h, :n] = _dot_nt(k_ref[0, h, pl.ds(start, n), :], qs[h])

    def consume(vt_ref, c, s_ref, carry, n):
        start = pl.multiple_of(c * n, n)
        ones = jnp.ones((ONES_ROWS, n), MXU_DTYPE)
        new = []
        for h in range(N_HEADS):
            m, acc = carry[h]
            s = s_ref[h, :n]
            m_new = jnp.maximum(m, jnp.max(s, axis=0, keepdims=True))
            alpha = jnp.exp2(m - m_new)
            p = _bf(jnp.exp2(s - m_new))
            vt = jnp.concatenate([vt_ref[0, h * MLA_V:(h + 1) * MLA_V, pl.ds(start, n)], ones], axis=0)
            new.append((m_new, alpha * acc + _dot(vt, p)))
        return tuple(new)

    carry = tuple((jnp.full((1, tq), -jnp.inf, jnp.float32), jnp.zeros((MLA_V + ONES_ROWS, tq), jnp.float32))
                  for _ in range(N_HEADS))
    cur, nxt = sa_ref, sb_ref
    produce(kc_ref, 0, cur, tkc)
    for c in range(n_ctx):
        if c + 1 < n_ctx:
            produce(kc_ref, c + 1, nxt, tkc)
        elif n_lat:
            produce(kl_ref, 0, nxt, tk)
        carry = consume(vtc_ref, c, cur, carry, tkc)
        cur, nxt = nxt, cur
    if n_lat:
        def body(i, carry):
            c = 2 * i
            produce(kl_ref, c + 1, nxt, tk)
            carry = consume(vtl_ref, c, cur, carry, tk)
            produce(kl_ref, c + 2, cur, tk)
            return consume(vtl_ref, c + 1, nxt, carry, tk)

        n_pairs = (n_lat - 1) // 2
        carry = lax.fori_loop(0, n_pairs, body, carry)
        c = 2 * n_pairs
        if n_lat - c == 2:
            produce(kl_ref, c + 1, nxt, tk)
            carry = consume(vtl_ref, c, cur, carry, tk)
            carry = consume(vtl_ref, c + 1, nxt, carry, tk)
        else:
            carry = consume(vtl_ref, c, cur, carry, tk)
    o_t = jnp.concatenate([acc[:MLA_V] / acc[MLA_V:MLA_V + 1] for _, acc in carry], axis=0)
    o_ref[0] = o_t.T


def _attention(q, kc, vtc, kl, vtl, tq, tkc, tk):
    b, _, t_q, _ = q.shape
    t_c = kc.shape[2]
    t_l = 0 if kl is None else kl.shape[2]
    assert t_q % tq == 0 and t_c % tkc == 0 and t_l % tk == 0
    kv_specs = [pl.BlockSpec((1, N_HEADS, t_c, HEAD_PAD), lambda i, j: (i, 0, 0, 0)),
                pl.BlockSpec((1, 256, t_c), lambda i, j: (i, 0, 0))]
    args = [q, kc, vtc]
    if t_l:
        kv_specs += [pl.BlockSpec((1, N_HEADS, t_l, HEAD_PAD), lambda i, j: (i, 0, 0, 0)),
                     pl.BlockSpec((1, 256, t_l), lambda i, j: (i, 0, 0))]
        args += [kl, vtl]
    return pl.pallas_call(
        functools.partial(_attn_kernel, tkc=tkc, tk=tk, n_ctx=t_c // tkc, n_lat=t_l // tk),
        grid=(b, t_q // tq),
        in_specs=[pl.BlockSpec((1, N_HEADS, tq, HEAD_PAD), lambda i, j: (i, 0, j, 0))] + kv_specs,
        out_specs=pl.BlockSpec((1, tq, 256), lambda i, j: (i, j, 0)),
        out_shape=jax.ShapeDtypeStruct((b, t_q, 256), jnp.float32),
        scratch_shapes=[pltpu.VMEM((N_HEADS, max(tkc, tk), tq), jnp.float32)] * 2,
        compiler_params=_cparams(("parallel", "parallel")),
        name="attention",
    )(*args)


def _pool_kernel(prev_ref, cur_ref, next_ref, w_ref, sc_ref, o_ref, ext_ref, *, tm, n_tok):
    j = pl.program_id(1)
    cur = cur_ref[0]
    ext_ref[8:8 + tm, :] = cur
    ext_ref[0:8, :] = jnp.where(j > 0, prev_ref[0], 0.0)
    ext_ref[8 + tm:16 + tm, :] = jnp.where(j < pl.num_programs(1) - 1, next_ref[0], 0.0)

    def window_sums(c0, widths):
        sh = lambda k: ext_ref[8 + k:8 + k + tm, c0:c0 + LANE]
        acc = sh(-1) + sh(0)
        out = {2: acc}
        for w in (4, 8, 16):
            if w > max(widths):
                break
            for k in list(range(-w // 2, -w // 4)) + list(range(w // 4, w // 2)):
                acc = acc + sh(k)
            out[w] = acc
        return [out[w] for w in widths]

    lane = lax.broadcasted_iota(jnp.int32, (tm, LANE), 1)
    t = j * tm + lax.broadcasted_iota(jnp.int32, (tm, LANE), 0)
    halves = []
    for c0, (wa, wb) in ((0, (2, 4)), (LANE, (8, 16))):
        sa, sb = window_sums(c0, (wa, wb))
        first = lane < POOL_GC
        s = jnp.where(first, sa, sb)
        half = jnp.where(first, wa // 2, wb // 2)
        cnt = jnp.minimum(t + half, n_tok) - jnp.maximum(t - half, 0)
        halves.append(s / cnt.astype(jnp.float32) - cur[:, c0:c0 + LANE])
    p = jnp.concatenate(halves, axis=-1)
    o_ref[0] = _dot(_bf(p), w_ref[...]) * sc_ref[...]


def _pool(u, base, w_bd, scale, tm):
    b, t, _ = u.shape
    cidx = (COL["pool"] - base) // 256
    rb = tm // 8
    last = t // 8 - 1
    return pl.pallas_call(
        functools.partial(_pool_kernel, tm=tm, n_tok=t),
        grid=(b, t // tm),
        in_specs=[pl.BlockSpec((1, 8, 256), lambda i, j: (i, jnp.maximum(j * rb - 1, 0), cidx)),
                  pl.BlockSpec((1, tm, 256), lambda i, j: (i, j, cidx)),
                  pl.BlockSpec((1, 8, 256), lambda i, j: (i, jnp.minimum((j + 1) * rb, last), cidx)),
                  _full(w_bd, 2), _full(scale, 2)],
        out_specs=pl.BlockSpec((1, tm, 256), lambda i, j: (i, j, 0)),
        out_shape=jax.ShapeDtypeStruct((b, t, 256), jnp.float32),
        scratch_shapes=[pltpu.VMEM((tm + 16, 256), jnp.float32)],
        compiler_params=_cparams(("parallel", "parallel")),
        name="pool",
    )(u, u, u, w_bd, scale)


def _head_rms(x, e):
    return lax.rsqrt(_split_dot(x * x, e) * (1.0 / NA_HD) + EPS)


def _na_prep_kernel(q_ref, k_ref, v_ref, gq_ref, gk_ref, e_ref, qo_ref, ko_ref, vo_ref):
    e = e_ref[...]
    q = q_ref[0]
    qo_ref[0] = _bf(q * _head_rms(q, e) * gq_ref[...])
    k = k_ref[0]
    ko_ref[0] = _bf(k * _head_rms(k, e) * gk_ref[...])
    vo_ref[0] = _bf(v_ref[0])


def _na_prep(uq, qcol, u, base, gq, gk, e_mat, tm):
    b, t, _ = u.shape
    tok = pl.BlockSpec((1, tm, 256), lambda i, j: (i, j, 0))
    return pl.pallas_call(
        _na_prep_kernel,
        grid=(b, t // tm),
        in_specs=[pl.BlockSpec((1, tm, 256), lambda i, j: (i, j, qcol)),
                  _ublk(tm, COL["k_na"], base, 256), _ublk(tm, COL["v_na"], base, 256),
                  _full(gq, 2), _full(gk, 2), _full(e_mat, 2)],
        out_specs=[tok, tok, tok],
        out_shape=[jax.ShapeDtypeStruct((b, t, 256), MXU_DTYPE)] * 3,
        compiler_params=_cparams(("parallel", "parallel")),
        name="na_prep",
    )(uq, u, u, gq, gk, e_mat)


def _stack_heads(q):
    qf = q.astype(jnp.float32)
    head = lax.broadcasted_iota(jnp.int32, qf.shape, 1) // NA_HD
    return _bf(jnp.concatenate([jnp.where(head == h, qf, 0.0) for h in range(N_HEADS)], axis=0))


def _unstack_heads(res):
    n = res.shape[0] // N_HEADS
    head = lax.broadcasted_iota(jnp.int32, (n, res.shape[1]), 1) // NA_HD
    out = res[0:n]
    for h in range(1, N_HEADS):
        out = jnp.where(head == h, res[h * n:(h + 1) * n], out)
    return out


def _softmax_pv(scores, values):
    m = functools.reduce(jnp.maximum, [jnp.max(s, axis=-1, keepdims=True) for s in scores])
    ps = [jnp.exp(s - m) for s in scores]
    l = functools.reduce(jnp.add, [jnp.sum(p, axis=-1, keepdims=True) for p in ps])
    acc = functools.reduce(jnp.add, [_dot(_bf(p), v) for p, v in zip(ps, values)])
    return acc / l


def _na_kernel(q_ref, k_ref, v_ref, kc_ref, vc_ref, tbl_ref, o_ref, *, rows, rb):
    jb = pl.program_id(1)
    kc = kc_ref[0]
    vc = vc_ref[0]
    n_loc = NA_WIN_R * GRID_W
    staged = []
    for rr in range(rb):
        r = jb * rb + rr
        rs = jnp.clip(r - NA_WIN_R // 2, 0, rows - NA_WIN_R)
        start = pl.multiple_of(rs * GRID_W, GRID_W)
        qm = _stack_heads(q_ref[0, rr * GRID_W:(rr + 1) * GRID_W, :])
        s_loc = _dot_nt(qm, k_ref[0, pl.ds(start, n_loc), :]) + tbl_ref[r - rs]
        staged.append((start, s_loc, _dot_nt(qm, kc)))
    for rr, (start, s_loc, s_ctx) in enumerate(staged):
        res = _softmax_pv([s_loc, s_ctx], [v_ref[0, pl.ds(start, n_loc), :], vc])
        o_ref[0, rr * GRID_W:(rr + 1) * GRID_W, :] = _unstack_heads(res)


def _na_attention(q, k, v, kc, vc, tbls, layer, rb):
    b, t, _ = q.shape
    tc = kc.shape[1]
    rows = t // GRID_W
    assert rows >= NA_WIN_R and rows % rb == 0
    seq = pl.BlockSpec((1, t, 256), lambda i, j: (i, 0, 0))
    cseq = pl.BlockSpec((1, tc, 256), lambda i, j: (i, 0, 0))
    return pl.pallas_call(
        functools.partial(_na_kernel, rows=rows, rb=rb),
        grid=(b, rows // rb),
        in_specs=[pl.BlockSpec((1, rb * GRID_W, 256), lambda i, j: (i, j, 0)), seq, seq, cseq, cseq,
                  pl.BlockSpec((None,) + tbls.shape[1:], lambda i, j: (layer, 0, 0, 0))],
        out_specs=pl.BlockSpec((1, rb * GRID_W, 256), lambda i, j: (i, j, 0)),
        out_shape=jax.ShapeDtypeStruct((b, t, 256), jnp.float32),
        compiler_params=_cparams(("parallel", "parallel")),
        name="na_attention",
    )(q, k, v, kc, vc, tbls)


def _ctx_na_kernel(q_ref, kc_ref, vc_ref, o_ref):
    kc = kc_ref[0]
    vc = vc_ref[0]
    for rr in range(q_ref.shape[1] // GRID_W):
        qm = _stack_heads(q_ref[0, rr * GRID_W:(rr + 1) * GRID_W, :])
        res = _softmax_pv([_dot_nt(qm, kc)], [vc])
        o_ref[0, rr * GRID_W:(rr + 1) * GRID_W, :] = _unstack_heads(res)


def _ctx_na_attention(q, kc, vc):
    b, tc, _ = q.shape
    cseq = pl.BlockSpec((1, tc, 256), lambda i: (i, 0, 0))
    return pl.pallas_call(
        _ctx_na_kernel,
        grid=(b,),
        in_specs=[cseq, cseq, cseq],
        out_specs=cseq,
        out_shape=jax.ShapeDtypeStruct((b, tc, 256), jnp.float32),
        compiler_params=_cparams(("parallel",)),
        name="ctx_na_attention",
    )(q, kc, vc)


def _hg_gates(z, lb):
    e = jnp.exp(-jnp.abs(z))
    r = 1.0 / (1.0 + e)
    pos = z >= 0
    sig = jnp.where(pos, r, e * r)
    sig_neg = jnp.where(pos, e * r, r)
    return jnp.log(lb + (1.0 - lb) * sig), (1.0 - lb) * sig_neg


def _hg_prep_kernel(zf_ref, zb_ref, i_ref, q_ref, lbf_ref, lbb_ref, tl_ref, tu_ref, ta_ref, sel_ref, e_ref,
                    qf_o, kf_o, qb_o, kb_o, i_o, df_o, db_o, oi_o, nat_ref, pm_ref,
                    *, tm, qscale):
    nch = tm // HG_SUB
    q = q_ref[0] * qscale
    xi = i_ref[0]
    i_o[0] = _bf(xi)
    logf_f, kg_f = _hg_gates(zf_ref[0], lbf_ref[...])
    logf_b, kg_b = _hg_gates(zb_ref[0], lbb_ref[...])
    b_f = _split3_dot_left(tl_ref[...], logf_f)
    b_b = _split3_dot_left(tu_ref[...], logf_b)
    tot_f = _split3_dot_left(ta_ref[...], logf_f)
    tot_b = _split3_dot_left(ta_ref[...], logf_b)
    qf_o[0] = _bf(q * jnp.exp(b_f))
    kf_o[0] = _bf(kg_f * jnp.exp(tot_f - b_f))
    qb_o[0] = _bf(q * jnp.exp(b_b))
    kb_o[0] = _bf(kg_b * jnp.exp(tot_b - b_b))
    df_o[0] = jnp.exp(_split3_dot_left(sel_ref[...], logf_f))
    db_o[0] = jnp.exp(_split3_dot_left(sel_ref[...], logf_b))

    log2e = float(np.log2(np.e))
    for a, val in enumerate((q, kg_f, b_f * log2e, kg_b, b_b * log2e, xi)):
        for half in range(2):
            nat_ref[a, half] = val[:, half * LANE:(half + 1) * LANE]
            for p in range(HG_SUB):
                pm_ref[a, half, p * nch:(p + 1) * nch, :] = nat_ref[a, half, pl.ds(p, nch, stride=HG_SUB), :]
    e = e_ref[...]

    def side(q_t, b_t, k_ref, b_ref, i_ref, lo, nblk):
        rows = slice(lo, lo + nblk * nch)
        k3 = k_ref[rows, :].reshape(nblk, nch, LANE)
        b3 = b_ref[rows, :].reshape(nblk, nch, LANE)
        x = q_t[None] * k3 * jnp.exp2(b_t[None] - b3)
        r = _dot(_bf(x.reshape(nblk * nch, LANE)), e) * i_ref[rows, :]
        return jnp.sum(r.reshape(nblk, nch, LANE), axis=0)

    for half in range(2):
        qp, kfp, bfp, kbp, bbp, ip = [pm_ref.at[a, half] for a in range(6)]
        for pt in range(HG_SUB):
            rt = slice(pt * nch, (pt + 1) * nch)
            q_t = qp[rt, :]
            acc = _dot(_bf(q_t * (kfp[rt, :] + kbp[rt, :])), e) * ip[rt, :]
            if pt > 0:
                acc = acc + side(q_t, bfp[rt, :], kfp, bfp, ip, 0, pt)
            if pt < HG_SUB - 1:
                acc = acc + side(q_t, bbp[rt, :], kbp, bbp, ip, (pt + 1) * nch, HG_SUB - 1 - pt)
            nat_ref[0, half, pl.ds(pt, nch, stride=HG_SUB), :] = acc
        oi_o[0, :, half * LANE:(half + 1) * LANE] = nat_ref[0, half]


def _hg_prep(uq, qcol, u, base, lbf, lbb, e_mat):
    b, t, _ = u.shape
    tm = min(HG_PREP_TILE, t)
    nsub = tm // HG_SUB
    tok = pl.BlockSpec((1, tm, 256), lambda i, j: (i, j, 0))
    dec = pl.BlockSpec((1, nsub, 256), lambda i, j: (i, j, 0))
    tl, tu, ta, sel = _hg_consts(tm)
    e_half = e_mat[:LANE, :LANE]
    regroup = pltpu.VMEM((6, 2, tm, LANE), jnp.float32)
    return pl.pallas_call(
        functools.partial(_hg_prep_kernel, tm=tm, qscale=HG_DK ** -0.5),
        grid=(b, t // tm),
        in_specs=[_ublk(tm, COL["zF"], base, 256), _ublk(tm, COL["zB"], base, 256),
                  _ublk(tm, COL["i_hg"], base, 256),
                  pl.BlockSpec((1, tm, 256), lambda i, j: (i, j, qcol)),
                  _full(lbf, 2), _full(lbb, 2), _full(tl, 2), _full(tu, 2), _full(ta, 2), _full(sel, 2),
                  _full(e_half, 2)],
        out_specs=[tok, tok, tok, tok, tok, dec, dec, tok],
        out_shape=[jax.ShapeDtypeStruct((b, t, 256), MXU_DTYPE)] * 5
                  + [jax.ShapeDtypeStruct((b, t // HG_SUB, 256), jnp.float32)] * 2
                  + [jax.ShapeDtypeStruct((b, t, 256), jnp.float32)],
        scratch_shapes=[regroup, regroup],
        compiler_params=_cparams(("parallel", "parallel")),
        name="hg_prep",
    )(u, u, u, uq, lbf, lbb, tl, tu, ta, sel, e_half)


def _hg_scan_kernel(qf_ref, kf_ref, if_ref, df_ref, qb_ref, kb_ref, ib_ref, db_ref, s0f_ref, s0b_ref,
                    of_ref, ob_ref, sf_out, sb_out, sf, sb, *, nsub, nb):
    j = pl.program_id(0)

    @pl.when(j == 0)
    def _():
        sf[...] = s0f_ref[...]
        sb[...] = s0b_ref[...]

    first = lax.broadcasted_iota(jnp.int32, (HG_SUB, LANE), 1) < HG_DK

    def step(state, q_ref, k_ref, i_ref, d_ref, o_ref, bi, c):
        rows = slice(c * HG_SUB, (c + 1) * HG_SUB)
        for p in range(2):
            cols = slice(p * LANE, (p + 1) * LANE)
            s_old = state[bi, p]
            q = q_ref[bi, rows, cols].astype(jnp.float32)
            q2 = _bf(jnp.concatenate([jnp.where(first, q, 0.0), jnp.where(first, 0.0, q)], axis=0))
            r = _dot_nt(q2, _bf(s_old))
            o_ref[bi, rows, cols] = jnp.where(first, r[:HG_SUB], r[HG_SUB:])
            upd = _dot_tn(i_ref[bi, rows, cols], k_ref[bi, rows, cols])
            state[bi, p] = s_old * d_ref[bi, c:c + 1, cols] + upd

    for c in range(nsub):
        for bi in range(nb):
            step(sf, qf_ref, kf_ref, if_ref, df_ref, of_ref, bi, c)
            step(sb, qb_ref, kb_ref, ib_ref, db_ref, ob_ref, bi, nsub - 1 - c)

    @pl.when(j == pl.num_programs(0) - 1)
    def _():
        sf_out[...] = sf[...]
        sb_out[...] = sb[...]


def _hg_scan(qf, kf, qb, kb, xi, df, db, s0f, s0b):
    b, t, _ = qf.shape
    tm = HG_TILE
    n = t // tm
    nsub = tm // HG_SUB
    fwd = pl.BlockSpec((b, tm, 256), lambda j: (0, j, 0))
    bwd = pl.BlockSpec((b, tm, 256), lambda j: (0, n - 1 - j, 0))
    dfwd = pl.BlockSpec((b, nsub, 256), lambda j: (0, j, 0))
    dbwd = pl.BlockSpec((b, nsub, 256), lambda j: (0, n - 1 - j, 0))
    st = pl.BlockSpec((b, 2, LANE, LANE), lambda j: (0, 0, 0, 0))
    return pl.pallas_call(
        functools.partial(_hg_scan_kernel, nsub=nsub, nb=b),
        grid=(n,),
        in_specs=[fwd, fwd, fwd, dfwd, bwd, bwd, bwd, dbwd, st, st],
        out_specs=[fwd, bwd, st, st],
        out_shape=[jax.ShapeDtypeStruct((b, t, 256), jnp.float32)] * 2
                  + [jax.ShapeDtypeStruct((b, 2, LANE, LANE), jnp.float32)] * 2,
        scratch_shapes=[pltpu.VMEM((b, 2, LANE, LANE), jnp.float32)] * 2,
        compiler_params=_cparams(("arbitrary",)),
        name="hg_scan",
    )(qf, kf, xi, df, qb, kb, xi, db, s0f, s0b)


def _merge_kernel(x_ref, a_ref, b_ref, c_ref, of_ref, ob_ref, oi_ref, gc_ref, g_ref, sh_ref, sc_ref, wg_ref,
                  hgn_ref, e_ref, wbr_ref, wo_ref, gate_ref, out_ref):
    d_model = x_ref.shape[2]
    x = x_ref[0]
    h = _bf((x * _rms_scale(x) * g_ref[...]) * (1.0 + sc_ref[0]) + sh_ref[0])
    o = of_ref[0] + ob_ref[0] + oi_ref[0]
    ms = _split_dot(o * o, e_ref[...]) * (1.0 / HG_DK)
    d = o * lax.rsqrt(ms + EPS) * hgn_ref[...] * _sigmoid(gc_ref[0])
    acc = None
    for n, y in enumerate((a_ref[0], b_ref[0], c_ref[0], d)):
        gate = _sigmoid(_dot(h, wg_ref[:, n * d_model:(n + 1) * d_model]))
        term = gate * _dot(_bf(y), wbr_ref[n])
        acc = term if acc is None else acc + term
    out_ref[0] = x + gate_ref[0] * _dot(_bf(acc), wo_ref[...])


def _merge(x, a, bm, cn, o_f, o_b, o_i, u, g1n, w_gate, hgn, e_mat, w_br, w_o, layer, mods, tm):
    b, t, d = x.shape
    tok = pl.BlockSpec((1, tm, 256), lambda i, j: (i, j, 0))
    xs = pl.BlockSpec((1, tm, d), lambda i, j: (i, j, 0))
    of_layer = lambda a: pl.BlockSpec((None,) + a.shape[1:], lambda i, j: (layer,) + (0,) * (a.ndim - 1))
    return pl.pallas_call(
        _merge_kernel,
        grid=(b, t // tm),
        in_specs=[xs, tok, tok, tok, tok, tok, tok, _ublk(tm, COL["g_cols"], 0, 256),
                  _full(g1n, 2), _mod_spec(mods, 0, d), _mod_spec(mods, 1, d), of_layer(w_gate),
                  _full(hgn, 2), _full(e_mat, 2), of_layer(w_br), of_layer(w_o), _mod_spec(mods, 2, d)],
        out_specs=xs,
        out_shape=jax.ShapeDtypeStruct((b, t, d), jnp.float32),
        compiler_params=_cparams(("parallel", "parallel")),
        name="merge",
    )(x, a, bm, cn, o_f, o_b, o_i, u, g1n, mods, mods, w_gate, hgn, e_mat, w_br, w_o, mods)


def _ffn_kernel(x_ref, g_ref, sh_ref, sc_ref, w1_ref, w2_ref, gate_ref, o_ref, *, tf):
    x = x_ref[0]
    h = _bf((x * _rms_scale(x) * g_ref[...]) * (1.0 + sc_ref[0]) + sh_ref[0])
    acc = None
    for k in range(w1_ref.shape[1] // tf):
        z = jnp.maximum(_dot(h, w1_ref[:, k * tf:(k + 1) * tf]), 0.0)
        part = _dot(_bf(z * z), w2_ref[k * tf:(k + 1) * tf, :])
        acc = part if acc is None else acc + part
    o_ref[0] = x + gate_ref[0] * acc


def _ffn(x, g, mods, w1, w2, layer, tm, tf):
    b, t, d = x.shape
    xs = pl.BlockSpec((1, tm, d), lambda i, j: (i, j, 0))
    resident = lambda a: pl.BlockSpec((None,) + a.shape[1:], lambda i, j: (layer, 0, 0),
                                      pipeline_mode=pl.Buffered(1))
    return pl.pallas_call(
        functools.partial(_ffn_kernel, tf=tf),
        grid=(b, t // tm),
        in_specs=[xs, pl.BlockSpec((1, d), lambda i, j: (0, 0)), _mod_spec(mods, 3, d), _mod_spec(mods, 4, d),
                  resident(w1), resident(w2), _mod_spec(mods, 5, d)],
        out_specs=xs,
        out_shape=jax.ShapeDtypeStruct((b, t, d), jnp.float32),
        compiler_params=_cparams(("parallel", "parallel")),
        name="ffn",
    )(x, g, mods, mods, w1, w2, mods)


def _rope_perm():
    return np.array([(t // 16) * 16 + ((t % 16) + 8) % 16 for t in range(MLA_ROPE)])


def _prep_w_in(w):
    o = dict(ckv=0, kr=128, k_na=160, v_na=416, zF=672, zB=928, i_hg=1184, pool=1440, qa=1696, q_na=1952,
             q_hg=2208, g_cols=2464, gates=2720)
    z = lambda n: jnp.zeros(w.shape[:2] + (n,), w.dtype)
    k_rope = w[..., o["kr"]:o["kr"] + MLA_ROPE]
    parts = [w[..., o[name]:o[name] + 256]
             for name in ("pool", "qa", "q_na", "q_hg", "g_cols", "k_na", "v_na", "zF", "zB", "i_hg")]
    parts += [w[..., 0:128], z(64), k_rope, z(32), z(64), k_rope[..., _rope_perm()], z(32), z(128)]
    return _bf(w[..., o["gates"]:]), _bf(jnp.concatenate(parts, axis=-1))


def _prep_wq(w):
    r = w.shape[0]
    wh = w.reshape(r, N_HEADS, MLA_QK)
    pre = jnp.pad(wh, ((0, 0), (0, 0), (0, HEAD_PAD - MLA_QK)))
    sw = jnp.pad(wh[:, :, MLA_NOPE:][:, :, _rope_perm()], ((0, 0), (0, 0), (MLA_NOPE, HEAD_PAD - MLA_QK)))
    return _bf(jnp.concatenate([pre.reshape(r, -1), sw.reshape(r, -1)], axis=1))


def _prep_wkv(w):
    r = w.shape[0]
    wh = w.reshape(r, N_HEADS, MLA_NOPE + MLA_V)
    kpart = jnp.pad(wh[:, :, :MLA_NOPE], ((0, 0), (0, 0), (0, HEAD_PAD - MLA_NOPE)))
    return _bf(jnp.concatenate([kpart.reshape(r, -1), wh[:, :, MLA_NOPE:].reshape(r, -1)], axis=1))


def _pad_gain(g):
    plain = jnp.pad(g, (0, HEAD_PAD - MLA_QK))[None]
    sw = jnp.pad(g[MLA_NOPE:][_rope_perm()], (MLA_NOPE, HEAD_PAD - MLA_QK))[None]
    return plain, sw


def _rope_tables(n_tok):
    t = jnp.arange(n_tok)
    pos = jnp.stack([t // GRID_W, t % GRID_W], axis=-1).astype(jnp.float32)
    nf = MLA_ROPE // 4
    inv = ROPE_BASE ** (-jnp.arange(nf, dtype=jnp.float32) / nf)
    ang = pos[:, :, None] * inv
    cos, sin = jnp.cos(ang), jnp.sin(ang)
    cos_t = jnp.concatenate([cos, cos], axis=-1).reshape(n_tok, MLA_ROPE)
    sin_t = jnp.concatenate([-sin, sin], axis=-1).reshape(n_tok, MLA_ROPE)
    ones = jnp.ones((n_tok, MLA_NOPE), jnp.float32)
    zer = jnp.zeros((n_tok, HEAD_PAD - MLA_QK), jnp.float32)
    cos_p = jnp.concatenate([ones, cos_t, zer], axis=-1)
    sin_p = jnp.concatenate([0 * ones, sin_t, zer], axis=-1)
    return cos_p, sin_p


def _plain_tables(n_tok):
    cos_p = jnp.concatenate([jnp.ones((n_tok, MLA_QK), jnp.float32),
                             jnp.zeros((n_tok, HEAD_PAD - MLA_QK), jnp.float32)], axis=-1)
    return cos_p, jnp.zeros((n_tok, HEAD_PAD), jnp.float32)


def _head_indicator(dtype):
    i = np.arange(256) // 64
    return jnp.asarray((i[:, None] == i[None, :]).astype(np.float32), dtype=dtype)


def _na_bias_tables(rpb):
    qc = np.arange(GRID_W)[:, None]
    kc = np.arange(GRID_W)[None, :]
    cs = np.clip(qc - NA_WIN_C // 2, 0, GRID_W - NA_WIN_C)
    inwin = (kc >= cs) & (kc < cs + NA_WIN_C)
    dc = np.clip(kc - qc + NA_WIN_C - 1, 0, 2 * NA_WIN_C - 2)
    col_hot = (dc[:, :, None] == np.arange(2 * NA_WIN_C - 1)).astype(np.float32)
    p = np.arange(NA_WIN_R)[:, None, None]
    j = np.arange(NA_WIN_R)[None, :, None]
    row_hot = (np.arange(2 * NA_WIN_R - 1)[None, None, :] == j + NA_WIN_R - 1 - p).astype(np.float32)
    t = jnp.einsum('lhrd,pjr,qkd->lphqjk', rpb.astype(jnp.float32), row_hot, col_hot,
                   precision=lax.Precision.HIGHEST)
    t = jnp.where(inwin[None, None, None, :, None, :], t, NEG)
    return t.reshape(rpb.shape[0], NA_WIN_R, N_HEADS * GRID_W, NA_WIN_R * GRID_W)


def _hg_consts(tm):
    t = np.arange(tm)
    same = (t[:, None] // HG_SUB) == (t[None, :] // HG_SUB)
    tl = same & (t[None, :] <= t[:, None])
    tu = same & (t[None, :] >= t[:, None])
    sel = (np.arange(tm // HG_SUB)[:, None] == (t[None, :] // HG_SUB))
    f = lambda m: jnp.asarray(m.astype(np.float32), dtype=MXU_DTYPE)
    return f(tl), f(tu), f(same), f(sel)


def _block_diag(w):
    g, c, d = w.shape
    out = jnp.zeros((g * c, g * d), w.dtype)
    for i in range(g):
        out = out.at[i * c:(i + 1) * c, i * d:(i + 1) * d].set(w[i])
    return out


def kernel(x, c, ctx, c_ctx, ada_w, ada_b, norm1, norm2, w_in, pool_w, pool_scale, mla_q_norm, mla_wq_b, mla_kv_norm, mla_wkv_b, mla_gq, mla_gk, na_gq, na_gk, na_rpb, hg_lb, hg_norm, w_branch, w_out, w_ff1, w_ff2):
    B, L, D = x.shape
    Tc = ctx.shape[1]
    depth = ada_w.shape[0]
    f32 = jnp.float32
    assert L % HG_TILE == 0 and Tc % HG_TILE == 0 and B < 8

    lb_p = jax.nn.softmax(hg_lb.astype(f32), axis=0)
    lb_all = jnp.clip(jnp.cumsum(lb_p, axis=0)[:depth], LB_EPS, 1.0 - LB_EPS)

    cc = jnp.concatenate([c, c_ctx[None], jnp.zeros((8 - B - 1, D), f32)], axis=0)
    mods = _mods(cc, ada_w, ada_b)

    cos_l, sin_l = _rope_tables(L)
    cos_c, sin_c = _plain_tables(Tc)
    e_mat = _head_indicator(MXU_DTYPE)
    zero_state = jnp.zeros((B, 2, LANE, LANE), f32)
    zero_q = jnp.zeros((B, Tc, 256), f32)
    tl = min(512, L)
    t_big = min(1024, L)

    na_tbls = _na_bias_tables(na_rpb)
    w_gate, w_mix = _prep_w_in(w_in)
    w_br = _bf(w_branch)
    w_o = _bf(w_out)
    w1 = _bf(w_ff1)
    w2 = _bf(w_ff2)
    xl, xc = x, ctx
    for l in range(depth):
        ctx_out = l < depth - 1
        mods_l = mods[l, :B].reshape(B, 1, 6 * D)
        mods_c = mods[l, B].reshape(1, 1, 6 * D)
        g1n = norm1[l][None]
        g2n = norm2[l][None]
        wq = _prep_wq(mla_wq_b[l])
        wkv = _prep_wkv(mla_wkv_b[l])
        gq, gqsw = _pad_gain(mla_gq[l])
        gk, gksw = _pad_gain(mla_gk[l])
        qn = mla_q_norm[l][None]
        kvn = mla_kv_norm[l][None]
        na_q_gain = (jnp.tile(na_gq[l], N_HEADS) * NA_HD ** -0.5)[None]
        na_k_gain = jnp.tile(na_gk[l], N_HEADS)[None]
        pool_bd = _bf(_block_diag(pool_w[l]))
        pool_sc = pool_scale[l][None]
        hgn = jnp.tile(hg_norm[l], N_HEADS)[None]
        lbF = lb_all[l, 0][None]
        lbB = lb_all[l, 1][None]

        ul = _inproj(xl, g1n, mods_l, w_mix, l, 0, D_MIX_PAD, tl, D_MIX_PAD, f32)
        if ctx_out:
            uc, cbase = _inproj(xc, g1n, mods_c, w_mix, l, 0, D_MIX_PAD, Tc, MIX_TILE, f32), 0
        else:
            uc = _inproj(xc, g1n, mods_c, w_mix, l, KV_START, D_MIX_PAD - KV_START, Tc, KV_TILE, f32)
            cbase = KV_START

        q_mla, k_mla, v_mla = _mla_prep(ul, 0, cos_l, sin_l, qn, wq, kvn, wkv, gq, gqsw, gk, gksw, t_big)
        if ctx_out:
            q_mla_c, k_mla_c, v_mla_c = _mla_prep(uc, cbase, cos_c, sin_c, qn, wq, kvn, wkv, gq, gqsw, gk, gksw, Tc)
        else:
            k_mla_c, v_mla_c = _mla_kv_prep(uc, cbase, kvn, wkv, gk, Tc)
        b_l = _attention(q_mla, k_mla_c, v_mla_c, k_mla, v_mla, tl, Tc, tl)

        q_na, k_na, v_na = _na_prep(ul, COL["q_na"] // 256, ul, 0, na_q_gain, na_k_gain, e_mat, t_big)
        if ctx_out:
            q_na_c, k_na_c, v_na_c = _na_prep(uc, COL["q_na"] // 256, uc, cbase, na_q_gain, na_k_gain, e_mat, Tc)
        else:
            _, k_na_c, v_na_c = _na_prep(zero_q, 0, uc, cbase, na_q_gain, na_k_gain, e_mat, Tc)
        c_l = _na_attention(q_na, k_na, v_na, k_na_c, v_na_c, na_tbls, l, 8)

        a_l = _pool(ul, 0, pool_bd, pool_sc, t_big)

        if ctx_out:
            hc = _hg_prep(uc, COL["q_hg"] // 256, uc, cbase, lbF, lbB, e_mat)
        else:
            hc = _hg_prep(zero_q, 0, uc, cbase, lbF, lbB, e_mat)
        oFc, oBc, sFc, sBc = _hg_scan(hc[0], hc[1], hc[2], hc[3], hc[4], hc[5], hc[6], zero_state, zero_state)
        hl = _hg_prep(ul, COL["q_hg"] // 256, ul, 0, lbF, lbB, e_mat)
        oFl, oBl, _, _ = _hg_scan(hl[0], hl[1], hl[2], hl[3], hl[4], hl[5], hl[6], sFc, sBc)

        xl_new = _merge(xl, a_l, b_l, c_l, oFl, oBl, hl[7], ul, g1n, w_gate, hgn, e_mat, w_br, w_o, l, mods_l, tl)

        if ctx_out:
            a_c = _pool(uc, cbase, pool_bd, pool_sc, Tc)
            b_c = _attention(q_mla_c, k_mla_c, v_mla_c, None, None, Tc, Tc, Tc)
            c_c = _ctx_na_attention(q_na_c, k_na_c, v_na_c)
            xc = _merge(xc, a_c, b_c, c_c, oFc, oBc, hc[7], uc, g1n, w_gate, hgn, e_mat, w_br, w_o, l, mods_c, Tc)
            xc = _ffn(xc, g2n, mods_c, w1, w2, l, Tc, 1024)

        xl = _ffn(xl_new, g2n, mods_l, w1, w2, l, tl, 2048)
    return xl
```

```python
import functools

import jax
import jax.numpy as jnp
import numpy as np
from jax import lax
from jax.experimental import pallas as pl
from jax.experimental.pallas import tpu as pltpu

D_MODEL = 1024
GRID_W = 64
EPS = 1e-6
N_BRANCH = 4
BRANCH_W = 256
POOL_GC = 64
N_HEADS = 4
MLA_KV_RANK = 128
MLA_NOPE = 64
MLA_ROPE = 32
MLA_V = 64
MLA_QK = 96
ROPE_BASE = 10000.0
NA_HD = 64
NA_WIN_R = 8
NA_WIN_C = 16
HG_DK = 64
LB_EPS = 1e-6

MXU_DTYPE = jnp.bfloat16
LANE = 128
HEAD_PAD = 128
ONES_ROWS = 16
VMEM_LIMIT = 48 * 1024 * 1024
HG_SUB = 16
HG_TILE = 256
HG_PREP_TILE = 256
NEG = -1e30

COL = dict(pool=0, qa=256, q_na=512, q_hg=768, g_cols=1024,
           k_na=1280, v_na=1536, zF=1792, zB=2048, i_hg=2304, ckv=2560, kr=2688, kr_sw=2816)
D_MIX_PAD = 3072
KV_START = 1280
MIX_TILE = 1536
KV_TILE = 256


def _cparams(sem):
    return pltpu.CompilerParams(dimension_semantics=sem, vmem_limit_bytes=VMEM_LIMIT)


def _bf(x):
    return x.astype(MXU_DTYPE)


def _dot(a, b):
    return jnp.dot(a, b, preferred_element_type=jnp.float32)


def _dot_nt(a, b):
    return lax.dot_general(a, b, (((1,), (1,)), ((), ())), preferred_element_type=jnp.float32)


def _dot_tn(a, b):
    return lax.dot_general(a, b, (((0,), (0,)), ((), ())), preferred_element_type=jnp.float32)


def _split2(x):
    hi = _bf(x)
    return hi, _bf(x - hi.astype(jnp.float32))


def _split_dot(x, w):
    hi, lo = _split2(x)
    return _dot(hi, w) + _dot(lo, w)


def _split3_dot_left(w, x):
    hi = _bf(x)
    r1 = x - hi.astype(jnp.float32)
    mid = _bf(r1)
    lo = _bf(r1 - mid.astype(jnp.float32))
    return _dot(w, hi) + _dot(w, mid) + _dot(w, lo)


def _sigmoid(x):
    return 0.5 * jnp.tanh(0.5 * x) + 0.5


def _rms_scale(x):
    return lax.rsqrt(jnp.mean(x * x, axis=-1, keepdims=True) + EPS)


def _full(arr, nargs):
    zeros = (0,) * arr.ndim
    return pl.BlockSpec(arr.shape, lambda *_: zeros)


def _ublk(tm, col, base, width):
    idx = (col - base) // width
    return pl.BlockSpec((1, tm, width), lambda i, j: (i, j, idx))


def _mods_kernel(c_ref, w_ref, b_ref, o_ref):
    c = c_ref[...]
    s = _bf(c * _sigmoid(c))
    o_ref[0] = _dot(s, _bf(w_ref[0])) + b_ref[0]


def _mods(cc, ada_w, ada_b):
    depth, d, n = ada_w.shape
    tn = 1536
    return pl.pallas_call(
        _mods_kernel,
        grid=(depth, n // tn),
        in_specs=[pl.BlockSpec((8, d), lambda l, j: (0, 0)),
                  pl.BlockSpec((1, d, tn), lambda l, j: (l, 0, j)),
                  pl.BlockSpec((1, 1, tn), lambda l, j: (l, 0, j))],
        out_specs=pl.BlockSpec((1, 8, tn), lambda l, j: (l, 0, j)),
        out_shape=jax.ShapeDtypeStruct((depth, 8, n), jnp.float32),
        compiler_params=_cparams(("arbitrary", "arbitrary")),
        name="mods",
    )(cc, ada_w, ada_b.reshape(depth, 1, n))


def _inproj_kernel(x_ref, g_ref, sh_ref, sc_ref, w_ref, o_ref, h_ref):
    @pl.when(pl.program_id(2) == 0)
    def _():
        x = x_ref[0]
        y = x * _rms_scale(x) * g_ref[...]
        h_ref[...] = _bf(y * (1.0 + sc_ref[0]) + sh_ref[0])

    o_ref[0] = _dot(h_ref[...], w_ref[...]).astype(o_ref.dtype)


def _mod_spec(mods, idx, d):
    if mods.shape[0] == 1:
        return pl.BlockSpec((1, 1, d), lambda i, *_: (0, 0, idx))
    return pl.BlockSpec((1, 1, d), lambda i, *_: (i, 0, idx))


def _inproj(x, g, mods, w, layer, col0, n, tm, tn, out_dtype):
    b, t, d = x.shape
    assert n % tn == 0 and col0 % tn == 0
    return pl.pallas_call(
        _inproj_kernel,
        grid=(b, t // tm, n // tn),
        in_specs=[pl.BlockSpec((1, tm, d), lambda i, j, k: (i, j, 0)),
                  pl.BlockSpec((1, d), lambda i, j, k: (0, 0)),
                  _mod_spec(mods, 0, d), _mod_spec(mods, 1, d),
                  pl.BlockSpec((None, d, tn), lambda i, j, k: (layer, 0, k + col0 // tn))],
        out_specs=pl.BlockSpec((1, tm, tn), lambda i, j, k: (i, j, k)),
        out_shape=jax.ShapeDtypeStruct((b, t, n), out_dtype),
        scratch_shapes=[pltpu.VMEM((tm, d), MXU_DTYPE)],
        compiler_params=_cparams(("parallel", "parallel", "arbitrary")),
        name="inproj",
    )(x, g, mods, mods, w)


def _mla_prep_kernel(qa_ref, ckv_ref, kr_ref, krsw_ref, cos_ref, sin_ref, qn_ref, wq_ref, kvn_ref, wkv_ref,
                     gq_ref, gqsw_ref, gk_ref, gksw_ref, q_ref, k_ref, vt_ref, *, scale):
    cos = cos_ref[...]
    sin = sin_ref[...]
    inv_w = 1.0 / MLA_QK
    qa = qa_ref[0]
    qq = _dot(_bf(qa * _rms_scale(qa) * qn_ref[...]), wq_ref[...])
    ckv = ckv_ref[0]
    kv = _dot(_bf(ckv * _rms_scale(ckv) * kvn_ref[...]), wkv_ref[...])
    kr = kr_ref[0]
    kr_rot = krsw_ref[0] * gksw_ref[...] * sin
    gq_cos = gq_ref[...] * cos
    gq_sin = gqsw_ref[...] * sin
    gk_cos = gk_ref[...] * cos
    for h in range(N_HEADS):
        qp = qq[:, h * HEAD_PAD:(h + 1) * HEAD_PAD]
        qs = qq[:, (N_HEADS + h) * HEAD_PAD:(N_HEADS + h + 1) * HEAD_PAD]
        rs = lax.rsqrt(jnp.sum(qp * qp, axis=-1, keepdims=True) * inv_w + EPS)
        q_ref[0, h] = _bf((qp * gq_cos + qs * gq_sin) * (rs * scale))
        kp = kv[:, h * HEAD_PAD:(h + 1) * HEAD_PAD] + kr
        rs = lax.rsqrt(jnp.sum(kp * kp, axis=-1, keepdims=True) * inv_w + EPS)
        k_ref[0, h] = _bf((kp * gk_cos + kr_rot) * rs)
    vt_ref[0] = _bf(kv[:, N_HEADS * HEAD_PAD:].T)


def _mla_prep(u, base, cos, sin, qn, wq, kvn, wkv, gq, gqsw, gk, gksw, tm):
    b, t, _ = u.shape
    tab = pl.BlockSpec((tm, HEAD_PAD), lambda i, j: (j, 0))
    hd = pl.BlockSpec((1, N_HEADS, tm, HEAD_PAD), lambda i, j: (i, 0, j, 0))
    consts = (qn, wq, kvn, wkv, gq, gqsw, gk, gksw)
    return pl.pallas_call(
        functools.partial(_mla_prep_kernel, scale=MLA_QK ** -0.5 * float(np.log2(np.e))),
        grid=(b, t // tm),
        in_specs=[_ublk(tm, COL["qa"], base, 256), _ublk(tm, COL["ckv"], base, 128),
                  _ublk(tm, COL["kr"], base, 128), _ublk(tm, COL["kr_sw"], base, 128), tab, tab]
                 + [_full(a, 2) for a in consts],
        out_specs=[hd, hd, pl.BlockSpec((1, 256, tm), lambda i, j: (i, 0, j))],
        out_shape=[jax.ShapeDtypeStruct((b, N_HEADS, t, HEAD_PAD), MXU_DTYPE),
                   jax.ShapeDtypeStruct((b, N_HEADS, t, HEAD_PAD), MXU_DTYPE),
                   jax.ShapeDtypeStruct((b, 256, t), MXU_DTYPE)],
        compiler_params=_cparams(("parallel", "parallel")),
        name="mla_prep",
    )(u, u, u, u, cos, sin, *consts)


def _mla_kv_prep_kernel(ckv_ref, kr_ref, kvn_ref, wkv_ref, gk_ref, k_ref, vt_ref):
    inv_w = 1.0 / MLA_QK
    ckv = ckv_ref[0]
    kv = _dot(_bf(ckv * _rms_scale(ckv) * kvn_ref[...]), wkv_ref[...])
    kr = kr_ref[0]
    for h in range(N_HEADS):
        kp = kv[:, h * HEAD_PAD:(h + 1) * HEAD_PAD] + kr
        rs = lax.rsqrt(jnp.sum(kp * kp, axis=-1, keepdims=True) * inv_w + EPS)
        k_ref[0, h] = _bf(kp * gk_ref[...] * rs)
    vt_ref[0] = _bf(kv[:, N_HEADS * HEAD_PAD:].T)


def _mla_kv_prep(u, base, kvn, wkv, gk, tm):
    b, t, _ = u.shape
    hd = pl.BlockSpec((1, N_HEADS, tm, HEAD_PAD), lambda i, j: (i, 0, j, 0))
    return pl.pallas_call(
        _mla_kv_prep_kernel,
        grid=(b, t // tm),
        in_specs=[_ublk(tm, COL["ckv"], base, 128), _ublk(tm, COL["kr"], base, 128),
                  _full(kvn, 2), _full(wkv, 2), _full(gk, 2)],
        out_specs=[hd, pl.BlockSpec((1, 256, tm), lambda i, j: (i, 0, j))],
        out_shape=[jax.ShapeDtypeStruct((b, N_HEADS, t, HEAD_PAD), MXU_DTYPE),
                   jax.ShapeDtypeStruct((b, 256, t), MXU_DTYPE)],
        compiler_params=_cparams(("parallel", "parallel")),
        name="mla_kv_prep",
    )(u, u, kvn, wkv, gk)


def _attn_kernel(*refs, tkc, tk, n_ctx, n_lat):
    if n_lat:
        q_ref, kc_ref, vtc_ref, kl_ref, vtl_ref, o_ref, sa_ref, sb_ref = refs
    else:
        q_ref, kc_ref, vtc_ref, o_ref, sa_ref, sb_ref = refs
    tq = q_ref.shape[2]
    qs = [q_ref[0, h] for h in range(N_HEADS)]

    def produce(k_ref, c, s_ref, n):
        start = pl.multiple_of(c * n, n)
        for h in range(N_HEADS):
            s_ref[h, :n] = _dot_nt(k_ref[0, h, pl.ds(start, n), :], qs[h])

    def consume(vt_ref, c, s_ref, carry, n):
        start = pl.multiple_of(c * n, n)
        ones = jnp.ones((ONES_ROWS, n), MXU_DTYPE)
        new = []
        for h in range(N_HEADS):
            m, acc = carry[h]
            s = s_ref[h, :n]
            m_new = jnp.maximum(m, jnp.max(s, axis=0, keepdims=True))
            alpha = jnp.exp2(m - m_new)
            p = _bf(jnp.exp2(s - m_new))
            vt = jnp.concatenate([vt_ref[0, h * MLA_V:(h + 1) * MLA_V, pl.ds(start, n)], ones], axis=0)
            new.append((m_new, alpha * acc + _dot(vt, p)))
        return tuple(new)

    carry = tuple((jnp.full((1, tq), -jnp.inf, jnp.float32), jnp.zeros((MLA_V + ONES_ROWS, tq), jnp.float32))
                  for _ in range(N_HEADS))
    cur, nxt = sa_ref, sb_ref
    produce(kc_ref, 0, cur, tkc)
    for c in range(n_ctx):
        if c + 1 < n_ctx:
            produce(kc_ref, c + 1, nxt, tkc)
        elif n_lat:
            produce(kl_ref, 0, nxt, tk)
        carry = consume(vtc_ref, c, cur, carry, tkc)
        cur, nxt = nxt, cur
    if n_lat:
        def body(i, carry):
            c = 2 * i
            produce(kl_ref, c + 1, nxt, tk)
            carry = consume(vtl_ref, c, cur, carry, tk)
            produce(kl_ref, c + 2, cur, tk)
            return consume(vtl_ref, c + 1, nxt, carry, tk)

        n_pairs = (n_lat - 1) // 2
        carry = lax.fori_loop(0, n_pairs, body, carry)
        c = 2 * n_pairs
        if n_lat - c == 2:
            produce(kl_ref, c + 1, nxt, tk)
            carry = consume(vtl_ref, c, cur, carry, tk)
            carry = consume(vtl_ref, c + 1, nxt, carry, tk)
        else:
            carry = consume(vtl_ref, c, cur, carry, tk)
    o_t = jnp.concatenate([acc[:MLA_V] / acc[MLA_V:MLA_V + 1] for _, acc in carry], axis=0)
    o_ref[0] = o_t.T


def _attention(q, kc, vtc, kl, vtl, tq, tkc, tk):
    b, _, t_q, _ = q.shape
    t_c = kc.shape[2]
    t_l = 0 if kl is None else kl.shape[2]
    assert t_q % tq == 0 and t_c % tkc == 0 and t_l % tk == 0
    kv_specs = [pl.BlockSpec((1, N_HEADS, t_c, HEAD_PAD), lambda i, j: (i, 0, 0, 0)),
                pl.BlockSpec((1, 256, t_c), lambda i, j: (i, 0, 0))]
    args = [q, kc, vtc]
    if t_l:
        kv_specs += [pl.BlockSpec((1, N_HEADS, t_l, HEAD_PAD), lambda i, j: (i, 0, 0, 0)),
                     pl.BlockSpec((1, 256, t_l), lambda i, j: (i, 0, 0))]
        args += [kl, vtl]
    return pl.pallas_call(
        functools.partial(_attn_kernel, tkc=tkc, tk=tk, n_ctx=t_c // tkc, n_lat=t_l // tk),
        grid=(b, t_q // tq),
        in_specs=[pl.BlockSpec((1, N_HEADS, tq, HEAD_PAD), lambda i, j: (i, 0, j, 0))] + kv_specs,
        out_specs=pl.BlockSpec((1, tq, 256), lambda i, j: (i, j, 0)),
        out_shape=jax.ShapeDtypeStruct((b, t_q, 256), jnp.float32),
        scratch_shapes=[pltpu.VMEM((N_HEADS, max(tkc, tk), tq), jnp.float32)] * 2,
        compiler_params=_cparams(("parallel", "parallel")),
        name="attention",
    )(*args)


def _pool_kernel(prev_ref, cur_ref, next_ref, w_ref, sc_ref, o_ref, ext_ref, *, tm, n_tok):
    j = pl.program_id(1)
    cur = cur_ref[0]
    ext_ref[8:8 + tm, :] = cur
    ext_ref[0:8, :] = jnp.where(j > 0, prev_ref[0], 0.0)
    ext_ref[8 + tm:16 + tm, :] = jnp.where(j < pl.num_programs(1) - 1, next_ref[0], 0.0)

    def window_sums(c0, widths):
        sh = lambda k: ext_ref[8 + k:8 + k + tm, c0:c0 + LANE]
        acc = sh(-1) + sh(0)
        out = {2: acc}
        for w in (4, 8, 16):
            if w > max(widths):
                break
            for k in list(range(-w // 2, -w // 4)) + list(range(w // 4, w // 2)):
                acc = acc + sh(k)
            out[w] = acc
        return [out[w] for w in widths]

    lane = lax.broadcasted_iota(jnp.int32, (tm, LANE), 1)
    t = j * tm + lax.broadcasted_iota(jnp.int32, (tm, LANE), 0)
    halves = []
    for c0, (wa, wb) in ((0, (2, 4)), (LANE, (8, 16))):
        sa, sb = window_sums(c0, (wa, wb))
        first = lane < POOL_GC
        s = jnp.where(first, sa, sb)
        half = jnp.where(first, wa // 2, wb // 2)
        cnt = jnp.minimum(t + half, n_tok) - jnp.maximum(t - half, 0)
        halves.append(s / cnt.astype(jnp.float32) - cur[:, c0:c0 + LANE])
    p = jnp.concatenate(halves, axis=-1)
    o_ref[0] = _dot(_bf(p), w_ref[...]) * sc_ref[...]


def _pool(u, base, w_bd, scale, tm):
    b, t, _ = u.shape
    cidx = (COL["pool"] - base) // 256
    rb = tm // 8
    last = t // 8 - 1
    return pl.pallas_call(
        functools.partial(_pool_kernel, tm=tm, n_tok=t),
        grid=(b, t // tm),
        in_specs=[pl.BlockSpec((1, 8, 256), lambda i, j: (i, jnp.maximum(j * rb - 1, 0), cidx)),
                  pl.BlockSpec((1, tm, 256), lambda i, j: (i, j, cidx)),
                  pl.BlockSpec((1, 8, 256), lambda i, j: (i, jnp.minimum((j + 1) * rb, last), cidx)),
                  _full(w_bd, 2), _full(scale, 2)],
        out_specs=pl.BlockSpec((1, tm, 256), lambda i, j: (i, j, 0)),
        out_shape=jax.ShapeDtypeStruct((b, t, 256), jnp.float32),
        scratch_shapes=[pltpu.VMEM((tm + 16, 256), jnp.float32)],
        compiler_params=_cparams(("parallel", "parallel")),
        name="pool",
    )(u, u, u, w_bd, scale)


def _head_rms(x, e):
    return lax.rsqrt(_split_dot(x * x, e) * (1.0 / NA_HD) + EPS)


def _na_prep_kernel(q_ref, k_ref, v_ref, gq_ref, gk_ref, e_ref, qo_ref, ko_ref, vo_ref):
    e = e_ref[...]
    q = q_ref[0]
    qo_ref[0] = _bf(q * _head_rms(q, e) * gq_ref[...])
    k = k_ref[0]
    ko_ref[0] = _bf(k * _head_rms(k, e) * gk_ref[...])
    vo_ref[0] = _bf(v_ref[0])


def _na_prep(uq, qcol, u, base, gq, gk, e_mat, tm):
    b, t, _ = u.shape
    tok = pl.BlockSpec((1, tm, 256), lambda i, j: (i, j, 0))
    return pl.pallas_call(
        _na_prep_kernel,
        grid=(b, t // tm),
        in_specs=[pl.BlockSpec((1, tm, 256), lambda i, j: (i, j, qcol)),
                  _ublk(tm, COL["k_na"], base, 256), _ublk(tm, COL["v_na"], base, 256),
                  _full(gq, 2), _full(gk, 2), _full(e_mat, 2)],
        out_specs=[tok, tok, tok],
        out_shape=[jax.ShapeDtypeStruct((b, t, 256), MXU_DTYPE)] * 3,
        compiler_params=_cparams(("parallel", "parallel")),
        name="na_prep",
    )(uq, u, u, gq, gk, e_mat)


def _stack_heads(q):
    qf = q.astype(jnp.float32)
    head = lax.broadcasted_iota(jnp.int32, qf.shape, 1) // NA_HD
    return _bf(jnp.concatenate([jnp.where(head == h, qf, 0.0) for h in range(N_HEADS)], axis=0))


def _unstack_heads(res):
    n = res.shape[0] // N_HEADS
    head = lax.broadcasted_iota(jnp.int32, (n, res.shape[1]), 1) // NA_HD
    out = res[0:n]
    for h in range(1, N_HEADS):
        out = jnp.where(head == h, res[h * n:(h + 1) * n], out)
    return out


def _softmax_pv(scores, values):
    m = functools.reduce(jnp.maximum, [jnp.max(s, axis=-1, keepdims=True) for s in scores])
    ps = [jnp.exp(s - m) for s in scores]
    l = functools.reduce(jnp.add, [jnp.sum(p, axis=-1, keepdims=True) for p in ps])
    acc = functools.reduce(jnp.add, [_dot(_bf(p), v) for p, v in zip(ps, values)])
    return acc / l


def _na_kernel(q_ref, k_ref, v_ref, kc_ref, vc_ref, tbl_ref, o_ref, *, rows, rb):
    jb = pl.program_id(1)
    kc = kc_ref[0]
    vc = vc_ref[0]
    n_loc = NA_WIN_R * GRID_W
    staged = []
    for rr in range(rb):
        r = jb * rb + rr
        rs = jnp.clip(r - NA_WIN_R // 2, 0, rows - NA_WIN_R)
        start = pl.multiple_of(rs * GRID_W, GRID_W)
        qm = _stack_heads(q_ref[0, rr * GRID_W:(rr + 1) * GRID_W, :])
        s_loc = _dot_nt(qm, k_ref[0, pl.ds(start, n_loc), :]) + tbl_ref[r - rs]
        staged.append((start, s_loc, _dot_nt(qm, kc)))
    for rr, (start, s_loc, s_ctx) in enumerate(staged):
        res = _softmax_pv([s_loc, s_ctx], [v_ref[0, pl.ds(start, n_loc), :], vc])
        o_ref[0, rr * GRID_W:(rr + 1) * GRID_W, :] = _unstack_heads(res)


def _na_attention(q, k, v, kc, vc, tbls, layer, rb):
    b, t, _ = q.shape
    tc = kc.shape[1]
    rows = t // GRID_W
    assert rows >= NA_WIN_R and rows % rb == 0
    seq = pl.BlockSpec((1, t, 256), lambda i, j: (i, 0, 0))
    cseq = pl.BlockSpec((1, tc, 256), lambda i, j: (i, 0, 0))
    return pl.pallas_call(
        functools.partial(_na_kernel, rows=rows, rb=rb),
        grid=(b, rows // rb),
        in_specs=[pl.BlockSpec((1, rb * GRID_W, 256), lambda i, j: (i, j, 0)), seq, seq, cseq, cseq,
                  pl.BlockSpec((None,) + tbls.shape[1:], lambda i, j: (layer, 0, 0, 0))],
        out_specs=pl.BlockSpec((1, rb * GRID_W, 256), lambda i, j: (i, j, 0)),
        out_shape=jax.ShapeDtypeStruct((b, t, 256), jnp.float32),
        compiler_params=_cparams(("parallel", "parallel")),
        name="na_attention",
    )(q, k, v, kc, vc, tbls)


def _ctx_na_kernel(q_ref, kc_ref, vc_ref, o_ref):
    kc = kc_ref[0]
    vc = vc_ref[0]
    for rr in range(q_ref.shape[1] // GRID_W):
        qm = _stack_heads(q_ref[0, rr * GRID_W:(rr + 1) * GRID_W, :])
        res = _softmax_pv([_dot_nt(qm, kc)], [vc])
        o_ref[0, rr * GRID_W:(rr + 1) * GRID_W, :] = _unstack_heads(res)


def _ctx_na_attention(q, kc, vc):
    b, tc, _ = q.shape
    cseq = pl.BlockSpec((1, tc, 256), lambda i: (i, 0, 0))
    return pl.pallas_call(
        _ctx_na_kernel,
        grid=(b,),
        in_specs=[cseq, cseq, cseq],
        out_specs=cseq,
        out_shape=jax.ShapeDtypeStruct((b, tc, 256), jnp.float32),
        compiler_params=_cparams(("parallel",)),
        name="ctx_na_attention",
    )(q, kc, vc)


def _hg_gates(z, lb):
    e = jnp.exp(-jnp.abs(z))
    r = 1.0 / (1.0 + e)
    pos = z >= 0
    sig = jnp.where(pos, r, e * r)
    sig_neg = jnp.where(pos, e * r, r)
    return jnp.log(lb + (1.0 - lb) * sig), (1.0 - lb) * sig_neg


def _hg_prep_kernel(zf_ref, zb_ref, i_ref, q_ref, lbf_ref, lbb_ref, tl_ref, tu_ref, ta_ref, sel_ref, e_ref,
                    qf_o, kf_o, qb_o, kb_o, i_o, df_o, db_o, oi_o, nat_ref, pm_ref,
                    *, tm, qscale):
    nch = tm // HG_SUB
    q = q_ref[0] * qscale
    xi = i_ref[0]
    i_o[0] = _bf(xi)
    logf_f, kg_f = _hg_gates(zf_ref[0], lbf_ref[...])
    logf_b, kg_b = _hg_gates(zb_ref[0], lbb_ref[...])
    b_f = _split3_dot_left(tl_ref[...], logf_f)
    b_b = _split3_dot_left(tu_ref[...], logf_b)
    tot_f = _split3_dot_left(ta_ref[...], logf_f)
    tot_b = _split3_dot_left(ta_ref[...], logf_b)
    qf_o[0] = _bf(q * jnp.exp(b_f))
    kf_o[0] = _bf(kg_f * jnp.exp(tot_f - b_f))
    qb_o[0] = _bf(q * jnp.exp(b_b))
    kb_o[0] = _bf(kg_b * jnp.exp(tot_b - b_b))
    df_o[0] = jnp.exp(_split3_dot_left(sel_ref[...], logf_f))
    db_o[0] = jnp.exp(_split3_dot_left(sel_ref[...], logf_b))

    log2e = float(np.log2(np.e))
    for a, val in enumerate((q, kg_f, b_f * log2e, kg_b, b_b * log2e, xi)):
        for half in range(2):
            nat_ref[a, half] = val[:, half * LANE:(half + 1) * LANE]
            for p in range(HG_SUB):
                pm_ref[a, half, p * nch:(p + 1) * nch, :] = nat_ref[a, half, pl.ds(p, nch, stride=HG_SUB), :]
    e = e_ref[...]

    def side(q_t, b_t, k_ref, b_ref, i_ref, lo, nblk):
        rows = slice(lo, lo + nblk * nch)
        k3 = k_ref[rows, :].reshape(nblk, nch, LANE)
        b3 = b_ref[rows, :].reshape(nblk, nch, LANE)
        x = q_t[None] * k3 * jnp.exp2(b_t[None] - b3)
        r = _dot(_bf(x.reshape(nblk * nch, LANE)), e) * i_ref[rows, :]
        return jnp.sum(r.reshape(nblk, nch, LANE), axis=0)

    for half in range(2):
        qp, kfp, bfp, kbp, bbp, ip = [pm_ref.at[a, half] for a in range(6)]
        for pt in range(HG_SUB):
            rt = slice(pt * nch, (pt + 1) * nch)
            q_t = qp[rt, :]
            acc = _dot(_bf(q_t * (kfp[rt, :] + kbp[rt, :])), e) * ip[rt, :]
            if pt > 0:
                acc = acc + side(q_t, bfp[rt, :], kfp, bfp, ip, 0, pt)
            if pt < HG_SUB - 1:
                acc = acc + side(q_t, bbp[rt, :], kbp, bbp, ip, (pt + 1) * nch, HG_SUB - 1 - pt)
            nat_ref[0, half, pl.ds(pt, nch, stride=HG_SUB), :] = acc
        oi_o[0, :, half * LANE:(half + 1) * LANE] = nat_ref[0, half]


def _hg_prep(uq, qcol, u, base, lbf, lbb, e_mat):
    b, t, _ = u.shape
    tm = min(HG_PREP_TILE, t)
    nsub = tm // HG_SUB
    tok = pl.BlockSpec((1, tm, 256), lambda i, j: (i, j, 0))
    dec = pl.BlockSpec((1, nsub, 256), lambda i, j: (i, j, 0))
    tl, tu, ta, sel = _hg_consts(tm)
    e_half = e_mat[:LANE, :LANE]
    regroup = pltpu.VMEM((6, 2, tm, LANE), jnp.float32)
    return pl.pallas_call(
        functools.partial(_hg_prep_kernel, tm=tm, qscale=HG_DK ** -0.5),
        grid=(b, t // tm),
        in_specs=[_ublk(tm, COL["zF"], base, 256), _ublk(tm, COL["zB"], base, 256),
                  _ublk(tm, COL["i_hg"], base, 256),
                  pl.BlockSpec((1, tm, 256), lambda i, j: (i, j, qcol)),
                  _full(lbf, 2), _full(lbb, 2), _full(tl, 2), _full(tu, 2), _full(ta, 2), _full(sel, 2),
                  _full(e_half, 2)],
        out_specs=[tok, tok, tok, tok, tok, dec, dec, tok],
        out_shape=[jax.ShapeDtypeStruct((b, t, 256), MXU_DTYPE)] * 5
                  + [jax.ShapeDtypeStruct((b, t // HG_SUB, 256), jnp.float32)] * 2
                  + [jax.ShapeDtypeStruct((b, t, 256), jnp.float32)],
        scratch_shapes=[regroup, regroup],
        compiler_params=_cparams(("parallel", "parallel")),
        name="hg_prep",
    )(u, u, u, uq, lbf, lbb, tl, tu, ta, sel, e_half)


def _hg_scan_kernel(qf_ref, kf_ref, if_ref, df_ref, qb_ref, kb_ref, ib_ref, db_ref, s0f_ref, s0b_ref,
                    of_ref, ob_ref, sf_out, sb_out, sf, sb, *, nsub, nb):
    j = pl.program_id(0)

    @pl.when(j == 0)
    def _():
        sf[...] = s0f_ref[...]
        sb[...] = s0b_ref[...]

    first = lax.broadcasted_iota(jnp.int32, (HG_SUB, LANE), 1) < HG_DK

    def step(state, q_ref, k_ref, i_ref, d_ref, o_ref, bi, c):
        rows = slice(c * HG_SUB, (c + 1) * HG_SUB)
        for p in range(2):
            cols = slice(p * LANE, (p + 1) * LANE)
            s_old = state[bi, p]
            q = q_ref[bi, rows, cols].astype(jnp.float32)
            q2 = _bf(jnp.concatenate([jnp.where(first, q, 0.0), jnp.where(first, 0.0, q)], axis=0))
            r = _dot_nt(q2, _bf(s_old))
            o_ref[bi, rows, cols] = jnp.where(first, r[:HG_SUB], r[HG_SUB:])
            upd = _dot_tn(i_ref[bi, rows, cols], k_ref[bi, rows, cols])
            state[bi, p] = s_old * d_ref[bi, c:c + 1, cols] + upd

    for c in range(nsub):
        for bi in range(nb):
            step(sf, qf_ref, kf_ref, if_ref, df_ref, of_ref, bi, c)
            step(sb, qb_ref, kb_ref, ib_ref, db_ref, ob_ref, bi, nsub - 1 - c)

    @pl.when(j == pl.num_programs(0) - 1)
    def _():
        sf_out[...] = sf[...]
        sb_out[...] = sb[...]


def _hg_scan(qf, kf, qb, kb, xi, df, db, s0f, s0b):
    b, t, _ = qf.shape
    tm = HG_TILE
    n = t // tm
    nsub = tm // HG_SUB
    fwd = pl.BlockSpec((b, tm, 256), lambda j: (0, j, 0))
    bwd = pl.BlockSpec((b, tm, 256), lambda j: (0, n - 1 - j, 0))
    dfwd = pl.BlockSpec((b, nsub, 256), lambda j: (0, j, 0))
    dbwd = pl.BlockSpec((b, nsub, 256), lambda j: (0, n - 1 - j, 0))
    st = pl.BlockSpec((b, 2, LANE, LANE), lambda j: (0, 0, 0, 0))
    return pl.pallas_call(
        functools.partial(_hg_scan_kernel, nsub=nsub, nb=b),
        grid=(n,),
        in_specs=[fwd, fwd, fwd, dfwd, bwd, bwd, bwd, dbwd, st, st],
        out_specs=[fwd, bwd, st, st],
        out_shape=[jax.ShapeDtypeStruct((b, t, 256), jnp.float32)] * 2
                  + [jax.ShapeDtypeStruct((b, 2, LANE, LANE), jnp.float32)] * 2,
        scratch_shapes=[pltpu.VMEM((b, 2, LANE, LANE), jnp.float32)] * 2,
        compiler_params=_cparams(("arbitrary",)),
        name="hg_scan",
    )(qf, kf, xi, df, qb, kb, xi, db, s0f, s0b)


def _merge_kernel(x_ref, a_ref, b_ref, c_ref, of_ref, ob_ref, oi_ref, gc_ref, g_ref, sh_ref, sc_ref, wg_ref,
                  hgn_ref, e_ref, wbr_ref, wo_ref, gate_ref, out_ref):
    d_model = x_ref.shape[2]
    x = x_ref[0]
    h = _bf((x * _rms_scale(x) * g_ref[...]) * (1.0 + sc_ref[0]) + sh_ref[0])
    o = of_ref[0] + ob_ref[0] + oi_ref[0]
    ms = _split_dot(o * o, e_ref[...]) * (1.0 / HG_DK)
    d = o * lax.rsqrt(ms + EPS) * hgn_ref[...] * _sigmoid(gc_ref[0])
    acc = None
    for n, y in enumerate((a_ref[0], b_ref[0], c_ref[0], d)):
        gate = _sigmoid(_dot(h, wg_ref[:, n * d_model:(n + 1) * d_model]))
        term = gate * _dot(_bf(y), wbr_ref[n])
        acc = term if acc is None else acc + term
    out_ref[0] = x + gate_ref[0] * _dot(_bf(acc), wo_ref[...])


def _merge(x, a, bm, cn, o_f, o_b, o_i, u, g1n, w_gate, hgn, e_mat, w_br, w_o, layer, mods, tm):
    b, t, d = x.shape
    tok = pl.BlockSpec((1, tm, 256), lambda i, j: (i, j, 0))
    xs = pl.BlockSpec((1, tm, d), lambda i, j: (i, j, 0))
    of_layer = lambda a: pl.BlockSpec((None,) + a.shape[1:], lambda i, j: (layer,) + (0,) * (a.ndim - 1))
    return pl.pallas_call(
        _merge_kernel,
        grid=(b, t // tm),
        in_specs=[xs, tok, tok, tok, tok, tok, tok, _ublk(tm, COL["g_cols"], 0, 256),
                  _full(g1n, 2), _mod_spec(mods, 0, d), _mod_spec(mods, 1, d), of_layer(w_gate),
                  _full(hgn, 2), _full(e_mat, 2), of_layer(w_br), of_layer(w_o), _mod_spec(mods, 2, d)],
        out_specs=xs,
        out_shape=jax.ShapeDtypeStruct((b, t, d), jnp.float32),
        compiler_params=_cparams(("parallel", "parallel")),
        name="merge",
    )(x, a, bm, cn, o_f, o_b, o_i, u, g1n, mods, mods, w_gate, hgn, e_mat, w_br, w_o, mods)


def _ffn_kernel(x_ref, g_ref, sh_ref, sc_ref, w1_ref, w2_ref, gate_ref, o_ref, *, tf):
    x = x_ref[0]
    h = _bf((x * _rms_scale(x) * g_ref[...]) * (1.0 + sc_ref[0]) + sh_ref[0])
    acc = None
    for k in range(w1_ref.shape[1] // tf):
        z = jnp.maximum(_dot(h, w1_ref[:, k * tf:(k + 1) * tf]), 0.0)
        part = _dot(_bf(z * z), w2_ref[k * tf:(k + 1) * tf, :])
        acc = part if acc is None else acc + part
    o_ref[0] = x + gate_ref[0] * acc


def _ffn(x, g, mods, w1, w2, layer, tm, tf):
    b, t, d = x.shape
    xs = pl.BlockSpec((1, tm, d), lambda i, j: (i, j, 0))
    resident = lambda a: pl.BlockSpec((None,) + a.shape[1:], lambda i, j: (layer, 0, 0),
                                      pipeline_mode=pl.Buffered(1))
    return pl.pallas_call(
        functools.partial(_ffn_kernel, tf=tf),
        grid=(b, t // tm),
        in_specs=[xs, pl.BlockSpec((1, d), lambda i, j: (0, 0)), _mod_spec(mods, 3, d), _mod_spec(mods, 4, d),
                  resident(w1), resident(w2), _mod_spec(mods, 5, d)],
        out_specs=xs,
        out_shape=jax.ShapeDtypeStruct((b, t, d), jnp.float32),
        compiler_params=_cparams(("parallel", "parallel")),
        name="ffn",
    )(x, g, mods, mods, w1, w2, mods)


def _rope_perm():
    return np.array([(t // 16) * 16 + ((t % 16) + 8) % 16 for t in range(MLA_ROPE)])


W_IN_COLS = dict(ckv=0, kr=128, k_na=160, v_na=416, zF=672, zB=928, i_hg=1184, pool=1440, qa=1696, q_na=1952,
                 q_hg=2208, g_cols=2464, gates=2720)


def _w_in_relayout_kernel(w_ref, gate_ref, mix_ref):
    w = w_ref[0]
    tr = w.shape[0]
    o = W_IN_COLS
    gate_ref[0] = _bf(w[:, o["gates"]:])
    z = lambda n: jnp.zeros((tr, n), jnp.float32)
    kr = w[:, o["kr"]:o["kr"] + MLA_ROPE]
    quarter = MLA_ROPE // 4
    kr_sw = jnp.concatenate([kr[:, quarter:2 * quarter], kr[:, :quarter],
                             kr[:, 3 * quarter:], kr[:, 2 * quarter:3 * quarter]], axis=1)
    parts = [w[:, o[name]:o[name] + 256]
             for name in ("pool", "qa", "q_na", "q_hg", "g_cols", "k_na", "v_na", "zF", "zB", "i_hg")]
    parts += [w[:, :MLA_KV_RANK], z(MLA_NOPE), kr, z(HEAD_PAD - MLA_QK), z(MLA_NOPE), kr_sw, z(HEAD_PAD - MLA_QK),
              z(LANE)]
    mix_ref[0] = _bf(jnp.concatenate(parts, axis=1))


def _prep_w_in(w):
    depth, d, n = w.shape
    tr = 128
    n_gate = n - W_IN_COLS["gates"]
    return pl.pallas_call(
        _w_in_relayout_kernel,
        grid=(depth, d // tr),
        in_specs=[pl.BlockSpec((1, tr, n), lambda l, i: (l, i, 0))],
        out_specs=[pl.BlockSpec((1, tr, n_gate), lambda l, i: (l, i, 0)),
                   pl.BlockSpec((1, tr, D_MIX_PAD), lambda l, i: (l, i, 0))],
        out_shape=[jax.ShapeDtypeStruct((depth, d, n_gate), MXU_DTYPE),
                   jax.ShapeDtypeStruct((depth, d, D_MIX_PAD), MXU_DTYPE)],
        compiler_params=_cparams(("parallel", "parallel")),
        name="w_in_relayout",
    )(w)


def _prep_wq(w):
    r = w.shape[0]
    wh = w.reshape(r, N_HEADS, MLA_QK)
    pre = jnp.pad(wh, ((0, 0), (0, 0), (0, HEAD_PAD - MLA_QK)))
    sw = jnp.pad(wh[:, :, MLA_NOPE:][:, :, _rope_perm()], ((0, 0), (0, 0), (MLA_NOPE, HEAD_PAD - MLA_QK)))
    return _bf(jnp.concatenate([pre.reshape(r, -1), sw.reshape(r, -1)], axis=1))


def _prep_wkv(w):
    r = w.shape[0]
    wh = w.reshape(r, N_HEADS, MLA_NOPE + MLA_V)
    kpart = jnp.pad(wh[:, :, :MLA_NOPE], ((0, 0), (0, 0), (0, HEAD_PAD - MLA_NOPE)))
    return _bf(jnp.concatenate([kpart.reshape(r, -1), wh[:, :, MLA_NOPE:].reshape(r, -1)], axis=1))


def _pad_gain(g):
    plain = jnp.pad(g, (0, HEAD_PAD - MLA_QK))[None]
    sw = jnp.pad(g[MLA_NOPE:][_rope_perm()], (MLA_NOPE, HEAD_PAD - MLA_QK))[None]
    return plain, sw


def _rope_tables(n_tok):
    t = jnp.arange(n_tok)
    pos = jnp.stack([t // GRID_W, t % GRID_W], axis=-1).astype(jnp.float32)
    nf = MLA_ROPE // 4
    inv = ROPE_BASE ** (-jnp.arange(nf, dtype=jnp.float32) / nf)
    ang = pos[:, :, None] * inv
    cos, sin = jnp.cos(ang), jnp.sin(ang)
    cos_t = jnp.concatenate([cos, cos], axis=-1).reshape(n_tok, MLA_ROPE)
    sin_t = jnp.concatenate([-sin, sin], axis=-1).reshape(n_tok, MLA_ROPE)
    ones = jnp.ones((n_tok, MLA_NOPE), jnp.float32)
    zer = jnp.zeros((n_tok, HEAD_PAD - MLA_QK), jnp.float32)
    cos_p = jnp.concatenate([ones, cos_t, zer], axis=-1)
    sin_p = jnp.concatenate([0 * ones, sin_t, zer], axis=-1)
    return cos_p, sin_p


def _plain_tables(n_tok):
    cos_p = jnp.concatenate([jnp.ones((n_tok, MLA_QK), jnp.float32),
                             jnp.zeros((n_tok, HEAD_PAD - MLA_QK), jnp.float32)], axis=-1)
    return cos_p, jnp.zeros((n_tok, HEAD_PAD), jnp.float32)


def _head_indicator(dtype):
    i = np.arange(256) // 64
    return jnp.asarray((i[:, None] == i[None, :]).astype(np.float32), dtype=dtype)


def _na_bias_tables(rpb):
    qc = np.arange(GRID_W)[:, None]
    kc = np.arange(GRID_W)[None, :]
    cs = np.clip(qc - NA_WIN_C // 2, 0, GRID_W - NA_WIN_C)
    inwin = (kc >= cs) & (kc < cs + NA_WIN_C)
    dc = np.clip(kc - qc + NA_WIN_C - 1, 0, 2 * NA_WIN_C - 2)
    col_hot = (dc[:, :, None] == np.arange(2 * NA_WIN_C - 1)).astype(np.float32)
    p = np.arange(NA_WIN_R)[:, None, None]
    j = np.arange(NA_WIN_R)[None, :, None]
    row_hot = (np.arange(2 * NA_WIN_R - 1)[None, None, :] == j + NA_WIN_R - 1 - p).astype(np.float32)
    t = jnp.einsum('lhrd,pjr,qkd->lphqjk', rpb.astype(jnp.float32), row_hot, col_hot,
                   precision=lax.Precision.HIGHEST)
    t = jnp.where(inwin[None, None, None, :, None, :], t, NEG)
    return t.reshape(rpb.shape[0], NA_WIN_R, N_HEADS * GRID_W, NA_WIN_R * GRID_W)


def _hg_consts(tm):
    t = np.arange(tm)
    same = (t[:, None] // HG_SUB) == (t[None, :] // HG_SUB)
    tl = same & (t[None, :] <= t[:, None])
    tu = same & (t[None, :] >= t[:, None])
    sel = (np.arange(tm // HG_SUB)[:, None] == (t[None, :] // HG_SUB))
    f = lambda m: jnp.asarray(m.astype(np.float32), dtype=MXU_DTYPE)
    return f(tl), f(tu), f(same), f(sel)


def _block_diag(w):
    g, c, d = w.shape
    out = jnp.zeros((g * c, g * d), w.dtype)
    for i in range(g):
        out = out.at[i * c:(i + 1) * c, i * d:(i + 1) * d].set(w[i])
    return out


def kernel(x, c, ctx, c_ctx, ada_w, ada_b, norm1, norm2, w_in, pool_w, pool_scale, mla_q_norm, mla_wq_b, mla_kv_norm, mla_wkv_b, mla_gq, mla_gk, na_gq, na_gk, na_rpb, hg_lb, hg_norm, w_branch, w_out, w_ff1, w_ff2):
    B, L, D = x.shape
    Tc = ctx.shape[1]
    depth = ada_w.shape[0]
    f32 = jnp.float32
    assert L % HG_TILE == 0 and Tc % HG_TILE == 0 and B < 8

    lb_p = jax.nn.softmax(hg_lb.astype(f32), axis=0)
    lb_all = jnp.clip(jnp.cumsum(lb_p, axis=0)[:depth], LB_EPS, 1.0 - LB_EPS)

    cc = jnp.concatenate([c, c_ctx[None], jnp.zeros((8 - B - 1, D), f32)], axis=0)
    mods = _mods(cc, ada_w, ada_b)

    cos_l, sin_l = _rope_tables(L)
    cos_c, sin_c = _plain_tables(Tc)
    e_mat = _head_indicator(MXU_DTYPE)
    zero_state = jnp.zeros((B, 2, LANE, LANE), f32)
    zero_q = jnp.zeros((B, Tc, 256), f32)
    tl = min(512, L)
    t_big = min(1024, L)

    na_tbls = _na_bias_tables(na_rpb)
    w_gate, w_mix = _prep_w_in(w_in)
    w_br = _bf(w_branch)
    w_o = _bf(w_out)
    w1 = _bf(w_ff1)
    w2 = _bf(w_ff2)
    xl, xc = x, ctx
    for l in range(depth):
        ctx_out = l < depth - 1
        mods_l = mods[l, :B].reshape(B, 1, 6 * D)
        mods_c = mods[l, B].reshape(1, 1, 6 * D)
        g1n = norm1[l][None]
        g2n = norm2[l][None]
        wq = _prep_wq(mla_wq_b[l])
        wkv = _prep_wkv(mla_wkv_b[l])
        gq, gqsw = _pad_gain(mla_gq[l])
        gk, gksw = _pad_gain(mla_gk[l])
        qn = mla_q_norm[l][None]
        kvn = mla_kv_norm[l][None]
        na_q_gain = (jnp.tile(na_gq[l], N_HEADS) * NA_HD ** -0.5)[None]
        na_k_gain = jnp.tile(na_gk[l], N_HEADS)[None]
        pool_bd = _bf(_block_diag(pool_w[l]))
        pool_sc = pool_scale[l][None]
        hgn = jnp.tile(hg_norm[l], N_HEADS)[None]
        lbF = lb_all[l, 0][None]
        lbB = lb_all[l, 1][None]

        ul = _inproj(xl, g1n, mods_l, w_mix, l, 0, D_MIX_PAD, tl, D_MIX_PAD, f32)
        if ctx_out:
            uc, cbase = _inproj(xc, g1n, mods_c, w_mix, l, 0, D_MIX_PAD, Tc, MIX_TILE, f32), 0
        else:
            uc = _inproj(xc, g1n, mods_c, w_mix, l, KV_START, D_MIX_PAD - KV_START, Tc, KV_TILE, f32)
            cbase = KV_START

        q_mla, k_mla, v_mla = _mla_prep(ul, 0, cos_l, sin_l, qn, wq, kvn, wkv, gq, gqsw, gk, gksw, t_big)
        if ctx_out:
            q_mla_c, k_mla_c, v_mla_c = _mla_prep(uc, cbase, cos_c, sin_c, qn, wq, kvn, wkv, gq, gqsw, gk, gksw, Tc)
        else:
            k_mla_c, v_mla_c = _mla_kv_prep(uc, cbase, kvn, wkv, gk, Tc)
        b_l = _attention(q_mla, k_mla_c, v_mla_c, k_mla, v_mla, tl, Tc, tl)

        q_na, k_na, v_na = _na_prep(ul, COL["q_na"] // 256, ul, 0, na_q_gain, na_k_gain, e_mat, t_big)
        if ctx_out:
            q_na_c, k_na_c, v_na_c = _na_prep(uc, COL["q_na"] // 256, uc, cbase, na_q_gain, na_k_gain, e_mat, Tc)
        else:
            _, k_na_c, v_na_c = _na_prep(zero_q, 0, uc, cbase, na_q_gain, na_k_gain, e_mat, Tc)
        c_l = _na_attention(q_na, k_na, v_na, k_na_c, v_na_c, na_tbls, l, 8)

        a_l = _pool(ul, 0, pool_bd, pool_sc, t_big)

        if ctx_out:
            hc = _hg_prep(uc, COL["q_hg"] // 256, uc, cbase, lbF, lbB, e_mat)
        else:
            hc = _hg_prep(zero_q, 0, uc, cbase, lbF, lbB, e_mat)
        oFc, oBc, sFc, sBc = _hg_scan(hc[0], hc[1], hc[2], hc[3], hc[4], hc[5], hc[6], zero_state, zero_state)
        hl = _hg_prep(ul, COL["q_hg"] // 256, ul, 0, lbF, lbB, e_mat)
        oFl, oBl, _, _ = _hg_scan(hl[0], hl[1], hl[2], hl[3], hl[4], hl[5], hl[6], sFc, sBc)

        xl_new = _merge(xl, a_l, b_l, c_l, oFl, oBl, hl[7], ul, g1n, w_gate, hgn, e_mat, w_br, w_o, l, mods_l, tl)

        if ctx_out:
            a_c = _pool(uc, cbase, pool_bd, pool_sc, Tc)
            b_c = _attention(q_mla_c, k_mla_c, v_mla_c, None, None, Tc, Tc, Tc)
            c_c = _ctx_na_attention(q_na_c, k_na_c, v_na_c)
            xc = _merge(xc, a_c, b_c, c_c, oFc, oBc, hc[7], uc, g1n, w_gate, hgn, e_mat, w_br, w_o, l, mods_c, Tc)
            xc = _ffn(xc, g2n, mods_c, w1, w2, l, Tc, 1024)

        xl = _ffn(xl_new, g2n, mods_l, w1, w2, l, tl, 2048)
    return xl
```

```python
import functools

import jax
import jax.numpy as jnp
import numpy as np
from jax import lax
from jax.experimental import pallas as pl
from jax.experimental.pallas import tpu as pltpu

GRID_W = 64
EPS = 1e-6
POOL_GC = 64
N_HEADS = 4
MLA_KV_RANK = 128
MLA_NOPE = 64
MLA_ROPE = 32
MLA_V = 64
MLA_QK = 96
ROPE_BASE = 10000.0
NA_HD = 64
NA_WIN_R = 8
NA_WIN_C = 16
HG_DK = 64
LB_EPS = 1e-6

MXU_DTYPE = jnp.bfloat16
LANE = 128
HEAD_PAD = 128
ONES_ROWS = 16
VMEM_LIMIT = 48 * 1024 * 1024
HG_SUB = 16
HG_TILE = 256
HG_PREP_TILE = 256
NEG = -1e30

COL = dict(pool=0, qa=256, q_na=512, q_hg=768, g_cols=1024,
           k_na=1280, v_na=1536, zF=1792, zB=2048, i_hg=2304, ckv=2560, kr=2688)
D_MIX_PAD = 2816
KV_START = 1280
MIX_TILE = 1408
KV_TILE = 256


def _cparams(sem):
    return pltpu.CompilerParams(dimension_semantics=sem, vmem_limit_bytes=VMEM_LIMIT)


def _bf(x):
    return x.astype(MXU_DTYPE)


def _dot(a, b):
    return jnp.dot(a, b, preferred_element_type=jnp.float32)


def _dot_nt(a, b):
    return lax.dot_general(a, b, (((1,), (1,)), ((), ())), preferred_element_type=jnp.float32)


def _dot_tn(a, b):
    return lax.dot_general(a, b, (((0,), (0,)), ((), ())), preferred_element_type=jnp.float32)


def _split2(x):
    hi = _bf(x)
    return hi, _bf(x - hi.astype(jnp.float32))


def _split_dot(x, w):
    hi, lo = _split2(x)
    return _dot(hi, w) + _dot(lo, w)


def _split3_dot_left(w, x):
    hi = _bf(x)
    r1 = x - hi.astype(jnp.float32)
    mid = _bf(r1)
    lo = _bf(r1 - mid.astype(jnp.float32))
    return _dot(w, hi) + _dot(w, mid) + _dot(w, lo)


def _sigmoid(x):
    return 0.5 * jnp.tanh(0.5 * x) + 0.5


def _rms_scale(x):
    return lax.rsqrt(jnp.mean(x * x, axis=-1, keepdims=True) + EPS)


def _full(arr, nargs):
    zeros = (0,) * arr.ndim
    return pl.BlockSpec(arr.shape, lambda *_: zeros)


def _ublk(tm, col, base, width):
    idx = (col - base) // width
    return pl.BlockSpec((1, tm, width), lambda i, j: (i, j, idx))


def _mods_kernel(c_ref, w_ref, b_ref, o_ref):
    c = c_ref[...]
    s = _bf(c * _sigmoid(c))
    o_ref[0] = _dot(s, _bf(w_ref[0])) + b_ref[0]


def _mods(cc, ada_w, ada_b):
    depth, d, n = ada_w.shape
    tn = 1536
    return pl.pallas_call(
        _mods_kernel,
        grid=(depth, n // tn),
        in_specs=[pl.BlockSpec((8, d), lambda l, j: (0, 0)),
                  pl.BlockSpec((1, d, tn), lambda l, j: (l, 0, j)),
                  pl.BlockSpec((1, 1, tn), lambda l, j: (l, 0, j))],
        out_specs=pl.BlockSpec((1, 8, tn), lambda l, j: (l, 0, j)),
        out_shape=jax.ShapeDtypeStruct((depth, 8, n), jnp.float32),
        compiler_params=_cparams(("arbitrary", "arbitrary")),
        name="mods",
    )(cc, ada_w, ada_b.reshape(depth, 1, n))


def _inproj_kernel(x_ref, g_ref, sh_ref, sc_ref, w_ref, o_ref, h_ref):
    @pl.when(pl.program_id(2) == 0)
    def _():
        x = x_ref[0]
        y = x * _rms_scale(x) * g_ref[...]
        h_ref[...] = _bf(y * (1.0 + sc_ref[0]) + sh_ref[0])

    o_ref[0] = _dot(h_ref[...], w_ref[...]).astype(o_ref.dtype)


def _mod_spec(mods, idx, d):
    if mods.shape[0] == 1:
        return pl.BlockSpec((1, 1, d), lambda i, *_: (0, 0, idx))
    return pl.BlockSpec((1, 1, d), lambda i, *_: (i, 0, idx))


def _inproj(x, g, mods, w, layer, col0, n, tm, tn, out_dtype):
    b, t, d = x.shape
    assert n % tn == 0 and col0 % tn == 0
    return pl.pallas_call(
        _inproj_kernel,
        grid=(b, t // tm, n // tn),
        in_specs=[pl.BlockSpec((1, tm, d), lambda i, j, k: (i, j, 0)),
                  pl.BlockSpec((1, d), lambda i, j, k: (0, 0)),
                  _mod_spec(mods, 0, d), _mod_spec(mods, 1, d),
                  pl.BlockSpec((None, d, tn), lambda i, j, k: (layer, 0, k + col0 // tn))],
        out_specs=pl.BlockSpec((1, tm, tn), lambda i, j, k: (i, j, k)),
        out_shape=jax.ShapeDtypeStruct((b, t, n), out_dtype),
        scratch_shapes=[pltpu.VMEM((tm, d), MXU_DTYPE)],
        compiler_params=_cparams(("parallel", "parallel", "arbitrary")),
        name="inproj",
    )(x, g, mods, mods, w)


def _k_rope_only(krp):
    lane = lax.broadcasted_iota(jnp.int32, krp.shape, 1)
    return jnp.where(lane < MLA_QK, krp, 0.0)


def _mla_prep_kernel(qa_ref, ckv_ref, kr_ref, cos_ref, sin_ref, qn_ref, wq_ref, kvn_ref, wkv_ref,
                     gq_ref, gqsw_ref, gk_ref, gksw_ref, q_ref, k_ref, vt_ref, *, scale):
    cos = cos_ref[...]
    sin = sin_ref[...]
    inv_w = 1.0 / MLA_QK
    qa = qa_ref[0]
    qq = _dot(_bf(qa * _rms_scale(qa) * qn_ref[...]), wq_ref[...])
    ckv = ckv_ref[0]
    kv = _dot(_bf(ckv * _rms_scale(ckv) * kvn_ref[...]), wkv_ref[...])
    krp = kr_ref[0]
    kr = _k_rope_only(krp)
    kr_rot = pltpu.roll(krp, HEAD_PAD - MLA_ROPE, axis=1) * gksw_ref[...] * sin
    gq_cos = gq_ref[...] * cos
    gq_sin = gqsw_ref[...] * sin
    gk_cos = gk_ref[...] * cos
    for h in range(N_HEADS):
        qp = qq[:, h * HEAD_PAD:(h + 1) * HEAD_PAD]
        qs = qq[:, (N_HEADS + h) * HEAD_PAD:(N_HEADS + h + 1) * HEAD_PAD]
        rs = lax.rsqrt(jnp.sum(qp * qp, axis=-1, keepdims=True) * inv_w + EPS)
        q_ref[0, h] = _bf((qp * gq_cos + qs * gq_sin) * (rs * scale))
        kp = kv[:, h * HEAD_PAD:(h + 1) * HEAD_PAD] + kr
        rs = lax.rsqrt(jnp.sum(kp * kp, axis=-1, keepdims=True) * inv_w + EPS)
        k_ref[0, h] = _bf((kp * gk_cos + kr_rot) * rs)
    vt_ref[0] = _bf(kv[:, N_HEADS * HEAD_PAD:].T)


def _mla_prep(u, base, cos, sin, qn, wq, kvn, wkv, gq, gqsw, gk, gksw, tm):
    b, t, _ = u.shape
    tab = pl.BlockSpec((tm, HEAD_PAD), lambda i, j: (j, 0))
    hd = pl.BlockSpec((1, N_HEADS, tm, HEAD_PAD), lambda i, j: (i, 0, j, 0))
    consts = (qn, wq, kvn, wkv, gq, gqsw, gk, gksw)
    return pl.pallas_call(
        functools.partial(_mla_prep_kernel, scale=MLA_QK ** -0.5 * float(np.log2(np.e))),
        grid=(b, t // tm),
        in_specs=[_ublk(tm, COL["qa"], base, 256), _ublk(tm, COL["ckv"], base, 128),
                  _ublk(tm, COL["kr"], base, 128), tab, tab]
                 + [_full(a, 2) for a in consts],
        out_specs=[hd, hd, pl.BlockSpec((1, 256, tm), lambda i, j: (i, 0, j))],
        out_shape=[jax.ShapeDtypeStruct((b, N_HEADS, t, HEAD_PAD), MXU_DTYPE),
                   jax.ShapeDtypeStruct((b, N_HEADS, t, HEAD_PAD), MXU_DTYPE),
                   jax.ShapeDtypeStruct((b, 256, t), MXU_DTYPE)],
        compiler_params=_cparams(("parallel", "parallel")),
        name="mla_prep",
    )(u, u, u, cos, sin, *consts)


def _mla_kv_prep_kernel(ckv_ref, kr_ref, kvn_ref, wkv_ref, gk_ref, k_ref, vt_ref):
    inv_w = 1.0 / MLA_QK
    ckv = ckv_ref[0]
    kv = _dot(_bf(ckv * _rms_scale(ckv) * kvn_ref[...]), wkv_ref[...])
    kr = _k_rope_only(kr_ref[0])
    for h in range(N_HEADS):
        kp = kv[:, h * HEAD_PAD:(h + 1) * HEAD_PAD] + kr
        rs = lax.rsqrt(jnp.sum(kp * kp, axis=-1, keepdims=True) * inv_w + EPS)
        k_ref[0, h] = _bf(kp * gk_ref[...] * rs)
    vt_ref[0] = _bf(kv[:, N_HEADS * HEAD_PAD:].T)


def _mla_kv_prep(u, base, kvn, wkv, gk, tm):
    b, t, _ = u.shape
    hd = pl.BlockSpec((1, N_HEADS, tm, HEAD_PAD), lambda i, j: (i, 0, j, 0))
    return pl.pallas_call(
        _mla_kv_prep_kernel,
        grid=(b, t // tm),
        in_specs=[_ublk(tm, COL["ckv"], base, 128), _ublk(tm, COL["kr"], base, 128),
                  _full(kvn, 2), _full(wkv, 2), _full(gk, 2)],
        out_specs=[hd, pl.BlockSpec((1, 256, tm), lambda i, j: (i, 0, j))],
        out_shape=[jax.ShapeDtypeStruct((b, N_HEADS, t, HEAD_PAD), MXU_DTYPE),
                   jax.ShapeDtypeStruct((b, 256, t), MXU_DTYPE)],
        compiler_params=_cparams(("parallel", "parallel")),
        name="mla_kv_prep",
    )(u, u, kvn, wkv, gk)


def _attn_kernel(*refs, tkc, tk, n_ctx, n_lat):
    if n_lat:
        q_ref, kc_ref, vtc_ref, kl_ref, vtl_ref, o_ref, sa_ref, sb_ref = refs
    else:
        q_ref, kc_ref, vtc_ref, o_ref, sa_ref, sb_ref = refs
    tq = q_ref.shape[2]
    qs = [q_ref[0, h] for h in range(N_HEADS)]

    def produce(k_ref, c, s_ref, n):
        start = pl.multiple_of(c * n, n)
        for h in range(N_HEADS):
            s_ref[h, :n] = _dot_nt(k_ref[0, h, pl.ds(start, n), :], qs[h])

    def consume(vt_ref, c, s_ref, carry, n):
        start = pl.multiple_of(c * n, n)
        ones = jnp.ones((ONES_ROWS, n), MXU_DTYPE)
        new = []
        for h in range(N_HEADS):
            m, acc = carry[h]
            s = s_ref[h, :n]
            m_new = jnp.maximum(m, jnp.max(s, axis=0, keepdims=True))
            alpha = jnp.exp2(m - m_new)
            p = _bf(jnp.exp2(s - m_new))
            vt = jnp.concatenate([vt_ref[0, h * MLA_V:(h + 1) * MLA_V, pl.ds(start, n)], ones], axis=0)
            new.append((m_new, alpha * acc + _dot(vt, p)))
        return tuple(new)

    carry = tuple((jnp.full((1, tq), -jnp.inf, jnp.float32), jnp.zeros((MLA_V + ONES_ROWS, tq), jnp.float32))
                  for _ in range(N_HEADS))
    cur, nxt = sa_ref, sb_ref
    produce(kc_ref, 0, cur, tkc)
    for c in range(n_ctx):
        if c + 1 < n_ctx:
            produce(kc_ref, c + 1, nxt, tkc)
        elif n_lat:
            produce(kl_ref, 0, nxt, tk)
        carry = consume(vtc_ref, c, cur, carry, tkc)
        cur, nxt = nxt, cur
    if n_lat:
        def body(i, carry):
            c = 2 * i
            produce(kl_ref, c + 1, nxt, tk)
            carry = consume(vtl_ref, c, cur, carry, tk)
            produce(kl_ref, c + 2, cur, tk)
            return consume(vtl_ref, c + 1, nxt, carry, tk)

        n_pairs = (n_lat - 1) // 2
        carry = lax.fori_loop(0, n_pairs, body, carry)
        c = 2 * n_pairs
        if n_lat - c == 2:
            produce(kl_ref, c + 1, nxt, tk)
            carry = consume(vtl_ref, c, cur, carry, tk)
            carry = consume(vtl_ref, c + 1, nxt, carry, tk)
        else:
            carry = consume(vtl_ref, c, cur, carry, tk)
    o_t = jnp.concatenate([acc[:MLA_V] / acc[MLA_V:MLA_V + 1] for _, acc in carry], axis=0)
    o_ref[0] = o_t.T


def _attention(q, kc, vtc, kl, vtl, tq, tkc, tk):
    b, _, t_q, _ = q.shape
    t_c = kc.shape[2]
    t_l = 0 if kl is None else kl.shape[2]
    assert t_q % tq == 0 and t_c % tkc == 0 and t_l % tk == 0
    kv_specs = [pl.BlockSpec((1, N_HEADS, t_c, HEAD_PAD), lambda i, j: (i, 0, 0, 0)),
                pl.BlockSpec((1, 256, t_c), lambda i, j: (i, 0, 0))]
    args = [q, kc, vtc]
    if t_l:
        kv_specs += [pl.BlockSpec((1, N_HEADS, t_l, HEAD_PAD), lambda i, j: (i, 0, 0, 0)),
                     pl.BlockSpec((1, 256, t_l), lambda i, j: (i, 0, 0))]
        args += [kl, vtl]
    return pl.pallas_call(
        functools.partial(_attn_kernel, tkc=tkc, tk=tk, n_ctx=t_c // tkc, n_lat=t_l // tk),
        grid=(b, t_q // tq),
        in_specs=[pl.BlockSpec((1, N_HEADS, tq, HEAD_PAD), lambda i, j: (i, 0, j, 0))] + kv_specs,
        out_specs=pl.BlockSpec((1, tq, 256), lambda i, j: (i, j, 0)),
        out_shape=jax.ShapeDtypeStruct((b, t_q, 256), jnp.float32),
        scratch_shapes=[pltpu.VMEM((N_HEADS, max(tkc, tk), tq), jnp.float32)] * 2,
        compiler_params=_cparams(("parallel", "parallel")),
        name="attention",
    )(*args)


def _pool_kernel(prev_ref, cur_ref, next_ref, w_ref, sc_ref, o_ref, ext_ref, *, tm, n_tok):
    j = pl.program_id(1)
    cur = cur_ref[0]
    ext_ref[8:8 + tm, :] = cur
    ext_ref[0:8, :] = jnp.where(j > 0, prev_ref[0], 0.0)
    ext_ref[8 + tm:16 + tm, :] = jnp.where(j < pl.num_programs(1) - 1, next_ref[0], 0.0)

    def window_sums(c0, widths):
        sh = lambda k: ext_ref[8 + k:8 + k + tm, c0:c0 + LANE]
        acc = sh(-1) + sh(0)
        out = {2: acc}
        for w in (4, 8, 16):
            if w > max(widths):
                break
            for k in list(range(-w // 2, -w // 4)) + list(range(w // 4, w // 2)):
                acc = acc + sh(k)
            out[w] = acc
        return [out[w] for w in widths]

    lane = lax.broadcasted_iota(jnp.int32, (tm, LANE), 1)
    t = j * tm + lax.broadcasted_iota(jnp.int32, (tm, LANE), 0)
    halves = []
    for c0, (wa, wb) in ((0, (2, 4)), (LANE, (8, 16))):
        sa, sb = window_sums(c0, (wa, wb))
        first = lane < POOL_GC
        s = jnp.where(first, sa, sb)
        half = jnp.where(first, wa // 2, wb // 2)
        cnt = jnp.minimum(t + half, n_tok) - jnp.maximum(t - half, 0)
        halves.append(s / cnt.astype(jnp.float32) - cur[:, c0:c0 + LANE])
    p = jnp.concatenate(halves, axis=-1)
    o_ref[0] = _dot(_bf(p), w_ref[...]) * sc_ref[...]


def _pool(u, base, w_bd, scale, tm):
    b, t, _ = u.shape
    cidx = (COL["pool"] - base) // 256
    rb = tm // 8
    last = t // 8 - 1
    return pl.pallas_call(
        functools.partial(_pool_kernel, tm=tm, n_tok=t),
        grid=(b, t // tm),
        in_specs=[pl.BlockSpec((1, 8, 256), lambda i, j: (i, jnp.maximum(j * rb - 1, 0), cidx)),
                  pl.BlockSpec((1, tm, 256), lambda i, j: (i, j, cidx)),
                  pl.BlockSpec((1, 8, 256), lambda i, j: (i, jnp.minimum((j + 1) * rb, last), cidx)),
                  _full(w_bd, 2), _full(scale, 2)],
        out_specs=pl.BlockSpec((1, tm, 256), lambda i, j: (i, j, 0)),
        out_shape=jax.ShapeDtypeStruct((b, t, 256), jnp.float32),
        scratch_shapes=[pltpu.VMEM((tm + 16, 256), jnp.float32)],
        compiler_params=_cparams(("parallel", "parallel")),
        name="pool",
    )(u, u, u, w_bd, scale)


def _head_rms(x, e):
    return lax.rsqrt(_split_dot(x * x, e) * (1.0 / NA_HD) + EPS)


def _na_prep_kernel(q_ref, k_ref, v_ref, gq_ref, gk_ref, e_ref, qo_ref, ko_ref, vo_ref):
    e = e_ref[...]
    q = q_ref[0]
    qo_ref[0] = _bf(q * _head_rms(q, e) * gq_ref[...])
    k = k_ref[0]
    ko_ref[0] = _bf(k * _head_rms(k, e) * gk_ref[...])
    vo_ref[0] = _bf(v_ref[0])


def _na_prep(uq, qcol, u, base, gq, gk, e_mat, tm):
    b, t, _ = u.shape
    tok = pl.BlockSpec((1, tm, 256), lambda i, j: (i, j, 0))
    return pl.pallas_call(
        _na_prep_kernel,
        grid=(b, t // tm),
        in_specs=[pl.BlockSpec((1, tm, 256), lambda i, j: (i, j, qcol)),
                  _ublk(tm, COL["k_na"], base, 256), _ublk(tm, COL["v_na"], base, 256),
                  _full(gq, 2), _full(gk, 2), _full(e_mat, 2)],
        out_specs=[tok, tok, tok],
        out_shape=[jax.ShapeDtypeStruct((b, t, 256), MXU_DTYPE)] * 3,
        compiler_params=_cparams(("parallel", "parallel")),
        name="na_prep",
    )(uq, u, u, gq, gk, e_mat)


def _stack_heads(q):
    qf = q.astype(jnp.float32)
    head = lax.broadcasted_iota(jnp.int32, qf.shape, 1) // NA_HD
    return _bf(jnp.concatenate([jnp.where(head == h, qf, 0.0) for h in range(N_HEADS)], axis=0))


def _unstack_heads(res):
    n = res.shape[0] // N_HEADS
    head = lax.broadcasted_iota(jnp.int32, (n, res.shape[1]), 1) // NA_HD
    out = res[0:n]
    for h in range(1, N_HEADS):
        out = jnp.where(head == h, res[h * n:(h + 1) * n], out)
    return out


def _softmax_pv(scores, values):
    m = functools.reduce(jnp.maximum, [jnp.max(s, axis=-1, keepdims=True) for s in scores])
    ps = [jnp.exp(s - m) for s in scores]
    l = functools.reduce(jnp.add, [jnp.sum(p, axis=-1, keepdims=True) for p in ps])
    acc = functools.reduce(jnp.add, [_dot(_bf(p), v) for p, v in zip(ps, values)])
    return acc / l


def _na_kernel(q_ref, k_ref, v_ref, kc_ref, vc_ref, tbl_ref, o_ref, *, rows, rb):
    jb = pl.program_id(1)
    kc = kc_ref[0]
    vc = vc_ref[0]
    n_loc = NA_WIN_R * GRID_W
    staged = []
    for rr in range(rb):
        r = jb * rb + rr
        rs = jnp.clip(r - NA_WIN_R // 2, 0, rows - NA_WIN_R)
        start = pl.multiple_of(rs * GRID_W, GRID_W)
        qm = _stack_heads(q_ref[0, rr * GRID_W:(rr + 1) * GRID_W, :])
        s_loc = _dot_nt(qm, k_ref[0, pl.ds(start, n_loc), :]) + tbl_ref[r - rs]
        staged.append((start, s_loc, _dot_nt(qm, kc)))
    for rr, (start, s_loc, s_ctx) in enumerate(staged):
        res = _softmax_pv([s_loc, s_ctx], [v_ref[0, pl.ds(start, n_loc), :], vc])
        o_ref[0, rr * GRID_W:(rr + 1) * GRID_W, :] = _unstack_heads(res)


def _na_attention(q, k, v, kc, vc, tbls, layer, rb):
    b, t, _ = q.shape
    tc = kc.shape[1]
    rows = t // GRID_W
    assert rows >= NA_WIN_R and rows % rb == 0
    seq = pl.BlockSpec((1, t, 256), lambda i, j: (i, 0, 0))
    cseq = pl.BlockSpec((1, tc, 256), lambda i, j: (i, 0, 0))
    return pl.pallas_call(
        functools.partial(_na_kernel, rows=rows, rb=rb),
        grid=(b, rows // rb),
        in_specs=[pl.BlockSpec((1, rb * GRID_W, 256), lambda i, j: (i, j, 0)), seq, seq, cseq, cseq,
                  pl.BlockSpec((None,) + tbls.shape[1:], lambda i, j: (layer, 0, 0, 0))],
        out_specs=pl.BlockSpec((1, rb * GRID_W, 256), lambda i, j: (i, j, 0)),
        out_shape=jax.ShapeDtypeStruct((b, t, 256), jnp.float32),
        compiler_params=_cparams(("parallel", "parallel")),
        name="na_attention",
    )(q, k, v, kc, vc, tbls)


def _ctx_na_kernel(q_ref, kc_ref, vc_ref, o_ref):
    kc = kc_ref[0]
    vc = vc_ref[0]
    for rr in range(q_ref.shape[1] // GRID_W):
        qm = _stack_heads(q_ref[0, rr * GRID_W:(rr + 1) * GRID_W, :])
        res = _softmax_pv([_dot_nt(qm, kc)], [vc])
        o_ref[0, rr * GRID_W:(rr + 1) * GRID_W, :] = _unstack_heads(res)


def _ctx_na_attention(q, kc, vc):
    b, tc, _ = q.shape
    cseq = pl.BlockSpec((1, tc, 256), lambda i: (i, 0, 0))
    return pl.pallas_call(
        _ctx_na_kernel,
        grid=(b,),
        in_specs=[cseq, cseq, cseq],
        out_specs=cseq,
        out_shape=jax.ShapeDtypeStruct((b, tc, 256), jnp.float32),
        compiler_params=_cparams(("parallel",)),
        name="ctx_na_attention",
    )(q, kc, vc)


def _hg_gates(z, lb):
    e = jnp.exp(-jnp.abs(z))
    r = 1.0 / (1.0 + e)
    pos = z >= 0
    sig = jnp.where(pos, r, e * r)
    sig_neg = jnp.where(pos, e * r, r)
    return jnp.log(lb + (1.0 - lb) * sig), (1.0 - lb) * sig_neg


def _hg_prep_kernel(zf_ref, zb_ref, i_ref, q_ref, lbf_ref, lbb_ref, tl_ref, tu_ref, ta_ref, sel_ref, e_ref,
                    qf_o, kf_o, qb_o, kb_o, i_o, df_o, db_o, oi_o, nat_ref, pm_ref,
                    *, tm, qscale):
    nch = tm // HG_SUB
    q = q_ref[0] * qscale
    xi = i_ref[0]
    i_o[0] = _bf(xi)
    logf_f, kg_f = _hg_gates(zf_ref[0], lbf_ref[...])
    logf_b, kg_b = _hg_gates(zb_ref[0], lbb_ref[...])
    b_f = _split3_dot_left(tl_ref[...], logf_f)
    b_b = _split3_dot_left(tu_ref[...], logf_b)
    tot_f = _split3_dot_left(ta_ref[...], logf_f)
    tot_b = _split3_dot_left(ta_ref[...], logf_b)
    qf_o[0] = _bf(q * jnp.exp(b_f))
    kf_o[0] = _bf(kg_f * jnp.exp(tot_f - b_f))
    qb_o[0] = _bf(q * jnp.exp(b_b))
    kb_o[0] = _bf(kg_b * jnp.exp(tot_b - b_b))
    df_o[0] = jnp.exp(_split3_dot_left(sel_ref[...], logf_f))
    db_o[0] = jnp.exp(_split3_dot_left(sel_ref[...], logf_b))

    log2e = float(np.log2(np.e))
    for a, val in enumerate((q, kg_f, b_f * log2e, kg_b, b_b * log2e, xi)):
        for half in range(2):
            nat_ref[a, half] = val[:, half * LANE:(half + 1) * LANE]
            for p in range(HG_SUB):
                pm_ref[a, half, p * nch:(p + 1) * nch, :] = nat_ref[a, half, pl.ds(p, nch, stride=HG_SUB), :]
    e = e_ref[...]

    def side(q_t, b_t, k_ref, b_ref, i_ref, lo, nblk):
        rows = slice(lo, lo + nblk * nch)
        k3 = k_ref[rows, :].reshape(nblk, nch, LANE)
        b3 = b_ref[rows, :].reshape(nblk, nch, LANE)
        x = q_t[None] * k3 * jnp.exp2(b_t[None] - b3)
        r = _dot(_bf(x.reshape(nblk * nch, LANE)), e) * i_ref[rows, :]
        return jnp.sum(r.reshape(nblk, nch, LANE), axis=0)

    for half in range(2):
        qp, kfp, bfp, kbp, bbp, ip = [pm_ref.at[a, half] for a in range(6)]
        for pt in range(HG_SUB):
            rt = slice(pt * nch, (pt + 1) * nch)
            q_t = qp[rt, :]
            acc = _dot(_bf(q_t * (kfp[rt, :] + kbp[rt, :])), e) * ip[rt, :]
            if pt > 0:
                acc = acc + side(q_t, bfp[rt, :], kfp, bfp, ip, 0, pt)
            if pt < HG_SUB - 1:
                acc = acc + side(q_t, bbp[rt, :], kbp, bbp, ip, (pt + 1) * nch, HG_SUB - 1 - pt)
            nat_ref[0, half, pl.ds(pt, nch, stride=HG_SUB), :] = acc
        oi_o[0, :, half * LANE:(half + 1) * LANE] = nat_ref[0, half]


def _hg_prep(uq, qcol, u, base, lbf, lbb, e_mat):
    b, t, _ = u.shape
    tm = min(HG_PREP_TILE, t)
    nsub = tm // HG_SUB
    tok = pl.BlockSpec((1, tm, 256), lambda i, j: (i, j, 0))
    dec = pl.BlockSpec((1, nsub, 256), lambda i, j: (i, j, 0))
    tl, tu, ta, sel = _hg_consts(tm)
    e_half = e_mat[:LANE, :LANE]
    regroup = pltpu.VMEM((6, 2, tm, LANE), jnp.float32)
    return pl.pallas_call(
        functools.partial(_hg_prep_kernel, tm=tm, qscale=HG_DK ** -0.5),
        grid=(b, t // tm),
        in_specs=[_ublk(tm, COL["zF"], base, 256), _ublk(tm, COL["zB"], base, 256),
                  _ublk(tm, COL["i_hg"], base, 256),
                  pl.BlockSpec((1, tm, 256), lambda i, j: (i, j, qcol)),
                  _full(lbf, 2), _full(lbb, 2), _full(tl, 2), _full(tu, 2), _full(ta, 2), _full(sel, 2),
                  _full(e_half, 2)],
        out_specs=[tok, tok, tok, tok, tok, dec, dec, tok],
        out_shape=[jax.ShapeDtypeStruct((b, t, 256), MXU_DTYPE)] * 5
                  + [jax.ShapeDtypeStruct((b, t // HG_SUB, 256), jnp.float32)] * 2
                  + [jax.ShapeDtypeStruct((b, t, 256), jnp.float32)],
        scratch_shapes=[regroup, regroup],
        compiler_params=_cparams(("parallel", "parallel")),
        name="hg_prep",
    )(u, u, u, uq, lbf, lbb, tl, tu, ta, sel, e_half)


def _hg_scan_kernel(qf_ref, kf_ref, if_ref, df_ref, qb_ref, kb_ref, ib_ref, db_ref, s0f_ref, s0b_ref,
                    of_ref, ob_ref, sf_out, sb_out, sf, sb, *, nsub, nb):
    j = pl.program_id(0)

    @pl.when(j == 0)
    def _():
        sf[...] = s0f_ref[...]
        sb[...] = s0b_ref[...]

    first = lax.broadcasted_iota(jnp.int32, (HG_SUB, LANE), 1) < HG_DK

    def step(state, q_ref, k_ref, i_ref, d_ref, o_ref, bi, c):
        rows = slice(c * HG_SUB, (c + 1) * HG_SUB)
        for p in range(2):
            cols = slice(p * LANE, (p + 1) * LANE)
            s_old = state[bi, p]
            q = q_ref[bi, rows, cols].astype(jnp.float32)
            q2 = _bf(jnp.concatenate([jnp.where(first, q, 0.0), jnp.where(first, 0.0, q)], axis=0))
            r = _dot_nt(q2, _bf(s_old))
            o_ref[bi, rows, cols] = jnp.where(first, r[:HG_SUB], r[HG_SUB:])
            upd = _dot_tn(i_ref[bi, rows, cols], k_ref[bi, rows, cols])
            state[bi, p] = s_old * d_ref[bi, c:c + 1, cols] + upd

    for c in range(nsub):
        for bi in range(nb):
            step(sf, qf_ref, kf_ref, if_ref, df_ref, of_ref, bi, c)
            step(sb, qb_ref, kb_ref, ib_ref, db_ref, ob_ref, bi, nsub - 1 - c)

    @pl.when(j == pl.num_programs(0) - 1)
    def _():
        sf_out[...] = sf[...]
        sb_out[...] = sb[...]


def _hg_scan(qf, kf, qb, kb, xi, df, db, s0f, s0b):
    b, t, _ = qf.shape
    tm = HG_TILE
    n = t // tm
    nsub = tm // HG_SUB
    fwd = pl.BlockSpec((b, tm, 256), lambda j: (0, j, 0))
    bwd = pl.BlockSpec((b, tm, 256), lambda j: (0, n - 1 - j, 0))
    dfwd = pl.BlockSpec((b, nsub, 256), lambda j: (0, j, 0))
    dbwd = pl.BlockSpec((b, nsub, 256), lambda j: (0, n - 1 - j, 0))
    st = pl.BlockSpec((b, 2, LANE, LANE), lambda j: (0, 0, 0, 0))
    return pl.pallas_call(
        functools.partial(_hg_scan_kernel, nsub=nsub, nb=b),
        grid=(n,),
        in_specs=[fwd, fwd, fwd, dfwd, bwd, bwd, bwd, dbwd, st, st],
        out_specs=[fwd, bwd, st, st],
        out_shape=[jax.ShapeDtypeStruct((b, t, 256), jnp.float32)] * 2
                  + [jax.ShapeDtypeStruct((b, 2, LANE, LANE), jnp.float32)] * 2,
        scratch_shapes=[pltpu.VMEM((b, 2, LANE, LANE), jnp.float32)] * 2,
        compiler_params=_cparams(("arbitrary",)),
        name="hg_scan",
    )(qf, kf, xi, df, qb, kb, xi, db, s0f, s0b)


def _merge_kernel(x_ref, a_ref, b_ref, c_ref, of_ref, ob_ref, oi_ref, gc_ref, g_ref, sh_ref, sc_ref, wg_ref,
                  hgn_ref, e_ref, wbr_ref, wo_ref, gate_ref, out_ref):
    d_model = x_ref.shape[2]
    x = x_ref[0]
    h = _bf((x * _rms_scale(x) * g_ref[...]) * (1.0 + sc_ref[0]) + sh_ref[0])
    o = of_ref[0] + ob_ref[0] + oi_ref[0]
    ms = _split_dot(o * o, e_ref[...]) * (1.0 / HG_DK)
    d = o * lax.rsqrt(ms + EPS) * hgn_ref[...] * _sigmoid(gc_ref[0])
    acc = None
    for n, y in enumerate((a_ref[0], b_ref[0], c_ref[0], d)):
        gate = _sigmoid(_dot(h, wg_ref[:, n * d_model:(n + 1) * d_model]))
        term = gate * _dot(_bf(y), wbr_ref[n])
        acc = term if acc is None else acc + term
    out_ref[0] = x + gate_ref[0] * _dot(_bf(acc), wo_ref[...])


def _merge(x, a, bm, cn, o_f, o_b, o_i, u, g1n, w_gate, hgn, e_mat, w_br, w_o, layer, mods, tm):
    b, t, d = x.shape
    tok = pl.BlockSpec((1, tm, 256), lambda i, j: (i, j, 0))
    xs = pl.BlockSpec((1, tm, d), lambda i, j: (i, j, 0))
    of_layer = lambda a: pl.BlockSpec((None,) + a.shape[1:], lambda i, j: (layer,) + (0,) * (a.ndim - 1))
    return pl.pallas_call(
        _merge_kernel,
        grid=(b, t // tm),
        in_specs=[xs, tok, tok, tok, tok, tok, tok, _ublk(tm, COL["g_cols"], 0, 256),
                  _full(g1n, 2), _mod_spec(mods, 0, d), _mod_spec(mods, 1, d), of_layer(w_gate),
                  _full(hgn, 2), _full(e_mat, 2), of_layer(w_br), of_layer(w_o), _mod_spec(mods, 2, d)],
        out_specs=xs,
        out_shape=jax.ShapeDtypeStruct((b, t, d), jnp.float32),
        compiler_params=_cparams(("parallel", "parallel")),
        name="merge",
    )(x, a, bm, cn, o_f, o_b, o_i, u, g1n, mods, mods, w_gate, hgn, e_mat, w_br, w_o, mods)


def _ffn_kernel(x_ref, g_ref, sh_ref, sc_ref, w1_ref, w2_ref, gate_ref, o_ref, *, tf):
    x = x_ref[0]
    h = _bf((x * _rms_scale(x) * g_ref[...]) * (1.0 + sc_ref[0]) + sh_ref[0])
    acc = None
    for k in range(w1_ref.shape[1] // tf):
        z = jnp.maximum(_dot(h, w1_ref[:, k * tf:(k + 1) * tf]), 0.0)
        part = _dot(_bf(z * z), w2_ref[k * tf:(k + 1) * tf, :])
        acc = part if acc is None else acc + part
    o_ref[0] = x + gate_ref[0] * acc


def _ffn(x, g, mods, w1, w2, layer, tm, tf):
    b, t, d = x.shape
    xs = pl.BlockSpec((1, tm, d), lambda i, j: (i, j, 0))
    resident = lambda a: pl.BlockSpec((None,) + a.shape[1:], lambda i, j: (layer, 0, 0),
                                      pipeline_mode=pl.Buffered(1))
    return pl.pallas_call(
        functools.partial(_ffn_kernel, tf=tf),
        grid=(b, t // tm),
        in_specs=[xs, pl.BlockSpec((1, d), lambda i, j: (0, 0)), _mod_spec(mods, 3, d), _mod_spec(mods, 4, d),
                  resident(w1), resident(w2), _mod_spec(mods, 5, d)],
        out_specs=xs,
        out_shape=jax.ShapeDtypeStruct((b, t, d), jnp.float32),
        compiler_params=_cparams(("parallel", "parallel")),
        name="ffn",
    )(x, g, mods, mods, w1, w2, mods)


def _rope_perm():
    return np.array([(t // 16) * 16 + ((t % 16) + 8) % 16 for t in range(MLA_ROPE)])


W_IN_COLS = dict(ckv=0, kr=128, k_na=160, v_na=416, zF=672, zB=928, i_hg=1184, pool=1440, qa=1696, q_na=1952,
                 q_hg=2208, g_cols=2464, gates=2720)


def _w_in_relayout_kernel(w_ref, gate_ref, mix_ref):
    w = w_ref[0]
    tr = w.shape[0]
    o = W_IN_COLS
    gate_ref[0] = _bf(w[:, o["gates"]:])
    z = lambda n: jnp.zeros((tr, n), jnp.float32)
    kr = w[:, o["kr"]:o["kr"] + MLA_ROPE]
    quarter = MLA_ROPE // 4
    kr_sw = jnp.concatenate([kr[:, quarter:2 * quarter], kr[:, :quarter],
                             kr[:, 3 * quarter:], kr[:, 2 * quarter:3 * quarter]], axis=1)
    parts = [w[:, o[name]:o[name] + 256]
             for name in ("pool", "qa", "q_na", "q_hg", "g_cols", "k_na", "v_na", "zF", "zB", "i_hg")]
    parts += [w[:, :MLA_KV_RANK], z(MLA_NOPE), kr, kr_sw]
    mix_ref[0] = _bf(jnp.concatenate(parts, axis=1))


def _prep_w_in(w):
    depth, d, n = w.shape
    tr = 128
    n_gate = n - W_IN_COLS["gates"]
    return pl.pallas_call(
        _w_in_relayout_kernel,
        grid=(depth, d // tr),
        in_specs=[pl.BlockSpec((1, tr, n), lambda l, i: (l, i, 0))],
        out_specs=[pl.BlockSpec((1, tr, n_gate), lambda l, i: (l, i, 0)),
                   pl.BlockSpec((1, tr, D_MIX_PAD), lambda l, i: (l, i, 0))],
        out_shape=[jax.ShapeDtypeStruct((depth, d, n_gate), MXU_DTYPE),
                   jax.ShapeDtypeStruct((depth, d, D_MIX_PAD), MXU_DTYPE)],
        compiler_params=_cparams(("parallel", "parallel")),
        name="w_in_relayout",
    )(w)


def _prep_wq(w):
    r = w.shape[0]
    wh = w.reshape(r, N_HEADS, MLA_QK)
    pre = jnp.pad(wh, ((0, 0), (0, 0), (0, HEAD_PAD - MLA_QK)))
    sw = jnp.pad(wh[:, :, MLA_NOPE:][:, :, _rope_perm()], ((0, 0), (0, 0), (MLA_NOPE, HEAD_PAD - MLA_QK)))
    return _bf(jnp.concatenate([pre.reshape(r, -1), sw.reshape(r, -1)], axis=1))


def _prep_wkv(w):
    r = w.shape[0]
    wh = w.reshape(r, N_HEADS, MLA_NOPE + MLA_V)
    kpart = jnp.pad(wh[:, :, :MLA_NOPE], ((0, 0), (0, 0), (0, HEAD_PAD - MLA_NOPE)))
    return _bf(jnp.concatenate([kpart.reshape(r, -1), wh[:, :, MLA_NOPE:].reshape(r, -1)], axis=1))


def _pad_gain(g):
    plain = jnp.pad(g, (0, HEAD_PAD - MLA_QK))[None]
    sw = jnp.pad(g[MLA_NOPE:][_rope_perm()], (MLA_NOPE, HEAD_PAD - MLA_QK))[None]
    return plain, sw


def _rope_tables(n_tok):
    t = jnp.arange(n_tok)
    pos = jnp.stack([t // GRID_W, t % GRID_W], axis=-1).astype(jnp.float32)
    nf = MLA_ROPE // 4
    inv = ROPE_BASE ** (-jnp.arange(nf, dtype=jnp.float32) / nf)
    ang = pos[:, :, None] * inv
    cos, sin = jnp.cos(ang), jnp.sin(ang)
    cos_t = jnp.concatenate([cos, cos], axis=-1).reshape(n_tok, MLA_ROPE)
    sin_t = jnp.concatenate([-sin, sin], axis=-1).reshape(n_tok, MLA_ROPE)
    ones = jnp.ones((n_tok, MLA_NOPE), jnp.float32)
    zer = jnp.zeros((n_tok, HEAD_PAD - MLA_QK), jnp.float32)
    cos_p = jnp.concatenate([ones, cos_t, zer], axis=-1)
    sin_p = jnp.concatenate([0 * ones, sin_t, zer], axis=-1)
    return cos_p, sin_p


def _plain_tables(n_tok):
    cos_p = jnp.concatenate([jnp.ones((n_tok, MLA_QK), jnp.float32),
                             jnp.zeros((n_tok, HEAD_PAD - MLA_QK), jnp.float32)], axis=-1)
    return cos_p, jnp.zeros((n_tok, HEAD_PAD), jnp.float32)


def _head_indicator(dtype):
    i = np.arange(256) // 64
    return jnp.asarray((i[:, None] == i[None, :]).astype(np.float32), dtype=dtype)


def _na_bias_tables(rpb):
    qc = np.arange(GRID_W)[:, None]
    kc = np.arange(GRID_W)[None, :]
    cs = np.clip(qc - NA_WIN_C // 2, 0, GRID_W - NA_WIN_C)
    inwin = (kc >= cs) & (kc < cs + NA_WIN_C)
    dc = np.clip(kc - qc + NA_WIN_C - 1, 0, 2 * NA_WIN_C - 2)
    col_hot = (dc[:, :, None] == np.arange(2 * NA_WIN_C - 1)).astype(np.float32)
    p = np.arange(NA_WIN_R)[:, None, None]
    j = np.arange(NA_WIN_R)[None, :, None]
    row_hot = (np.arange(2 * NA_WIN_R - 1)[None, None, :] == j + NA_WIN_R - 1 - p).astype(np.float32)
    t = jnp.einsum('lhrd,pjr,qkd->lphqjk', rpb.astype(jnp.float32), row_hot, col_hot,
                   precision=lax.Precision.HIGHEST)
    t = jnp.where(inwin[None, None, None, :, None, :], t, NEG)
    return t.reshape(rpb.shape[0], NA_WIN_R, N_HEADS * GRID_W, NA_WIN_R * GRID_W)


def _hg_consts(tm):
    t = np.arange(tm)
    same = (t[:, None] // HG_SUB) == (t[None, :] // HG_SUB)
    tl = same & (t[None, :] <= t[:, None])
    tu = same & (t[None, :] >= t[:, None])
    sel = (np.arange(tm // HG_SUB)[:, None] == (t[None, :] // HG_SUB))
    f = lambda m: jnp.asarray(m.astype(np.float32), dtype=MXU_DTYPE)
    return f(tl), f(tu), f(same), f(sel)


def _block_diag(w):
    g, c, d = w.shape
    out = jnp.zeros((g * c, g * d), w.dtype)
    for i in range(g):
        out = out.at[i * c:(i + 1) * c, i * d:(i + 1) * d].set(w[i])
    return out


def kernel(x, c, ctx, c_ctx, ada_w, ada_b, norm1, norm2, w_in, pool_w, pool_scale, mla_q_norm, mla_wq_b, mla_kv_norm, mla_wkv_b, mla_gq, mla_gk, na_gq, na_gk, na_rpb, hg_lb, hg_norm, w_branch, w_out, w_ff1, w_ff2):
    B, L, D = x.shape
    Tc = ctx.shape[1]
    depth = ada_w.shape[0]
    f32 = jnp.float32
    assert L % HG_TILE == 0 and Tc % HG_TILE == 0 and B < 8

    lb_p = jax.nn.softmax(hg_lb.astype(f32), axis=0)
    lb_all = jnp.clip(jnp.cumsum(lb_p, axis=0)[:depth], LB_EPS, 1.0 - LB_EPS)

    cc = jnp.concatenate([c, c_ctx[None], jnp.zeros((8 - B - 1, D), f32)], axis=0)
    mods = _mods(cc, ada_w, ada_b)

    cos_l, sin_l = _rope_tables(L)
    cos_c, sin_c = _plain_tables(Tc)
    e_mat = _head_indicator(MXU_DTYPE)
    zero_state = jnp.zeros((B, 2, LANE, LANE), f32)
    zero_q = jnp.zeros((B, Tc, 256), f32)
    tl = min(512, L)
    t_big = min(1024, L)

    na_tbls = _na_bias_tables(na_rpb)
    w_gate, w_mix = _prep_w_in(w_in)
    w_br = _bf(w_branch)
    w_o = _bf(w_out)
    w1 = _bf(w_ff1)
    w2 = _bf(w_ff2)
    xl, xc = x, ctx
    for l in range(depth):
        ctx_out = l < depth - 1
        mods_l = mods[l, :B].reshape(B, 1, 6 * D)
        mods_c = mods[l, B].reshape(1, 1, 6 * D)
        g1n = norm1[l][None]
        g2n = norm2[l][None]
        wq = _prep_wq(mla_wq_b[l])
        wkv = _prep_wkv(mla_wkv_b[l])
        gq, gqsw = _pad_gain(mla_gq[l])
        gk, gksw = _pad_gain(mla_gk[l])
        qn = mla_q_norm[l][None]
        kvn = mla_kv_norm[l][None]
        na_q_gain = (jnp.tile(na_gq[l], N_HEADS) * NA_HD ** -0.5)[None]
        na_k_gain = jnp.tile(na_gk[l], N_HEADS)[None]
        pool_bd = _bf(_block_diag(pool_w[l]))
        pool_sc = pool_scale[l][None]
        hgn = jnp.tile(hg_norm[l], N_HEADS)[None]
        lbF = lb_all[l, 0][None]
        lbB = lb_all[l, 1][None]

        ul = _inproj(xl, g1n, mods_l, w_mix, l, 0, D_MIX_PAD, tl, D_MIX_PAD, f32)
        if ctx_out:
            uc, cbase = _inproj(xc, g1n, mods_c, w_mix, l, 0, D_MIX_PAD, Tc, MIX_TILE, f32), 0
        else:
            uc = _inproj(xc, g1n, mods_c, w_mix, l, KV_START, D_MIX_PAD - KV_START, Tc, KV_TILE, f32)
            cbase = KV_START

        q_mla, k_mla, v_mla = _mla_prep(ul, 0, cos_l, sin_l, qn, wq, kvn, wkv, gq, gqsw, gk, gksw, t_big)
        if ctx_out:
            q_mla_c, k_mla_c, v_mla_c = _mla_prep(uc, cbase, cos_c, sin_c, qn, wq, kvn, wkv, gq, gqsw, gk, gksw, Tc)
        else:
            k_mla_c, v_mla_c = _mla_kv_prep(uc, cbase, kvn, wkv, gk, Tc)
        b_l = _attention(q_mla, k_mla_c, v_mla_c, k_mla, v_mla, tl, Tc, tl)

        q_na, k_na, v_na = _na_prep(ul, COL["q_na"] // 256, ul, 0, na_q_gain, na_k_gain, e_mat, t_big)
        if ctx_out:
            q_na_c, k_na_c, v_na_c = _na_prep(uc, COL["q_na"] // 256, uc, cbase, na_q_gain, na_k_gain, e_mat, Tc)
        else:
            _, k_na_c, v_na_c = _na_prep(zero_q, 0, uc, cbase, na_q_gain, na_k_gain, e_mat, Tc)
        c_l = _na_attention(q_na, k_na, v_na, k_na_c, v_na_c, na_tbls, l, 8)

        a_l = _pool(ul, 0, pool_bd, pool_sc, t_big)

        if ctx_out:
            hc = _hg_prep(uc, COL["q_hg"] // 256, uc, cbase, lbF, lbB, e_mat)
        else:
            hc = _hg_prep(zero_q, 0, uc, cbase, lbF, lbB, e_mat)
        oFc, oBc, sFc, sBc = _hg_scan(hc[0], hc[1], hc[2], hc[3], hc[4], hc[5], hc[6], zero_state, zero_state)
        hl = _hg_prep(ul, COL["q_hg"] // 256, ul, 0, lbF, lbB, e_mat)
        oFl, oBl, _, _ = _hg_scan(hl[0], hl[1], hl[2], hl[3], hl[4], hl[5], hl[6], sFc, sBc)

        xl_new = _merge(xl, a_l, b_l, c_l, oFl, oBl, hl[7], ul, g1n, w_gate, hgn, e_mat, w_br, w_o, l, mods_l, tl)

        if ctx_out:
            a_c = _pool(uc, cbase, pool_bd, pool_sc, Tc)
            b_c = _attention(q_mla_c, k_mla_c, v_mla_c, None, None, Tc, Tc, Tc)
            c_c = _ctx_na_attention(q_na_c, k_na_c, v_na_c)
            xc = _merge(xc, a_c, b_c, c_c, oFc, oBc, hc[7], uc, g1n, w_gate, hgn, e_mat, w_br, w_o, l, mods_c, Tc)
            xc = _ffn(xc, g2n, mods_c, w1, w2, l, Tc, 1024)

        xl = _ffn(xl_new, g2n, mods_l, w1, w2, l, tl, 2048)
    return xl
```

```python
import functools

import jax
import jax.numpy as jnp
import numpy as np
from jax import lax
from jax.experimental import pallas as pl
from jax.experimental.pallas import tpu as pltpu

GRID_W = 64
EPS = 1e-6
POOL_GC = 64
N_HEADS = 4
MLA_KV_RANK = 128
MLA_NOPE = 64
MLA_ROPE = 32
MLA_V = 64
MLA_QK = 96
ROPE_BASE = 10000.0
NA_HD = 64
NA_WIN_R = 8
NA_WIN_C = 16
HG_DK = 64
LB_EPS = 1e-6

MXU_DTYPE = jnp.bfloat16
LANE = 128
HEAD_PAD = 128
ONES_ROWS = 16
VMEM_LIMIT = 48 * 1024 * 1024
HG_SUB = 16
HG_TILE = 256
HG_PREP_TILE = 256
NEG = -1e30

COL = dict(k_na=0, v_na=256, zF=512, zB=768, i_hg=1024, ckv=1280, kr=1408,
           pool=1536, qa=1792, q_na=2048, q_hg=2304, g_cols=2560)
D_MIX_PAD = 2816
D_KV = 1536
MIX_TILE = 1408


def _cparams(sem):
    return pltpu.CompilerParams(dimension_semantics=sem, vmem_limit_bytes=VMEM_LIMIT)


def _bf(x):
    return x.astype(MXU_DTYPE)


def _dot(a, b):
    return jnp.dot(a, b, preferred_element_type=jnp.float32)


def _dot_nt(a, b):
    return lax.dot_general(a, b, (((1,), (1,)), ((), ())), preferred_element_type=jnp.float32)


def _dot_tn(a, b):
    return lax.dot_general(a, b, (((0,), (0,)), ((), ())), preferred_element_type=jnp.float32)


def _split2(x):
    hi = _bf(x)
    return hi, _bf(x - hi.astype(jnp.float32))


def _split_dot(x, w):
    hi, lo = _split2(x)
    return _dot(hi, w) + _dot(lo, w)


def _split3_dot_left(w, x):
    hi = _bf(x)
    r1 = x - hi.astype(jnp.float32)
    mid = _bf(r1)
    lo = _bf(r1 - mid.astype(jnp.float32))
    return _dot(w, hi) + _dot(w, mid) + _dot(w, lo)


def _sigmoid(x):
    return 0.5 * jnp.tanh(0.5 * x) + 0.5


def _rms_scale(x):
    return lax.rsqrt(jnp.mean(x * x, axis=-1, keepdims=True) + EPS)


def _full(arr, nargs):
    zeros = (0,) * arr.ndim
    return pl.BlockSpec(arr.shape, lambda *_: zeros)


def _ublk(tm, col, base, width):
    idx = (col - base) // width
    return pl.BlockSpec((1, tm, width), lambda i, j: (i, j, idx))


def _mods_kernel(c_ref, w_ref, b_ref, o_ref):
    c = c_ref[...]
    s = _bf(c * _sigmoid(c))
    o_ref[0] = _dot(s, _bf(w_ref[0])) + b_ref[0]


def _mods(cc, ada_w, ada_b):
    depth, d, n = ada_w.shape
    tn = 1536
    return pl.pallas_call(
        _mods_kernel,
        grid=(depth, n // tn),
        in_specs=[pl.BlockSpec((8, d), lambda l, j: (0, 0)),
                  pl.BlockSpec((1, d, tn), lambda l, j: (l, 0, j)),
                  pl.BlockSpec((1, 1, tn), lambda l, j: (l, 0, j))],
        out_specs=pl.BlockSpec((1, 8, tn), lambda l, j: (l, 0, j)),
        out_shape=jax.ShapeDtypeStruct((depth, 8, n), jnp.float32),
        compiler_params=_cparams(("arbitrary", "arbitrary")),
        name="mods",
    )(cc, ada_w, ada_b.reshape(depth, 1, n))


def _inproj_kernel(x_ref, g_ref, sh_ref, sc_ref, w_ref, o_ref, h_ref):
    @pl.when(pl.program_id(2) == 0)
    def _():
        x = x_ref[0]
        y = x * _rms_scale(x) * g_ref[...]
        h_ref[...] = _bf(y * (1.0 + sc_ref[0]) + sh_ref[0])

    o_ref[0] = _dot(h_ref[...], w_ref[...]).astype(o_ref.dtype)


def _mod_spec(mods, idx, d):
    if mods.shape[0] == 1:
        return pl.BlockSpec((1, 1, d), lambda i, *_: (0, 0, idx))
    return pl.BlockSpec((1, 1, d), lambda i, *_: (i, 0, idx))


def _inproj(x, g, mods, w, layer, col0, n, tm, tn, out_dtype):
    b, t, d = x.shape
    assert n % tn == 0 and col0 % tn == 0
    return pl.pallas_call(
        _inproj_kernel,
        grid=(b, t // tm, n // tn),
        in_specs=[pl.BlockSpec((1, tm, d), lambda i, j, k: (i, j, 0)),
                  pl.BlockSpec((1, d), lambda i, j, k: (0, 0)),
                  _mod_spec(mods, 0, d), _mod_spec(mods, 1, d),
                  pl.BlockSpec((None, d, tn), lambda i, j, k: (layer, 0, k + col0 // tn))],
        out_specs=pl.BlockSpec((1, tm, tn), lambda i, j, k: (i, j, k)),
        out_shape=jax.ShapeDtypeStruct((b, t, n), out_dtype),
        scratch_shapes=[pltpu.VMEM((tm, d), MXU_DTYPE)],
        compiler_params=_cparams(("parallel", "parallel", "arbitrary")),
        name="inproj",
    )(x, g, mods, mods, w)


def _k_rope_only(krp):
    lane = lax.broadcasted_iota(jnp.int32, krp.shape, 1)
    return jnp.where(lane < MLA_QK, krp, 0.0)


def _mla_prep_kernel(qa_ref, ckv_ref, kr_ref, cos_ref, sin_ref, qn_ref, wq_ref, kvn_ref, wkv_ref,
                     gq_ref, gqsw_ref, gk_ref, gksw_ref, q_ref, k_ref, vt_ref, *, scale):
    cos = cos_ref[...]
    sin = sin_ref[...]
    inv_w = 1.0 / MLA_QK
    qa = qa_ref[0]
    qq = _dot(_bf(qa * _rms_scale(qa) * qn_ref[...]), wq_ref[...])
    ckv = ckv_ref[0]
    kv = _dot(_bf(ckv * _rms_scale(ckv) * kvn_ref[...]), wkv_ref[...])
    krp = kr_ref[0]
    kr = _k_rope_only(krp)
    kr_rot = pltpu.roll(krp, HEAD_PAD - MLA_ROPE, axis=1) * gksw_ref[...] * sin
    gq_cos = gq_ref[...] * cos
    gq_sin = gqsw_ref[...] * sin
    gk_cos = gk_ref[...] * cos
    for h in range(N_HEADS):
        qp = qq[:, h * HEAD_PAD:(h + 1) * HEAD_PAD]
        qs = qq[:, (N_HEADS + h) * HEAD_PAD:(N_HEADS + h + 1) * HEAD_PAD]
        rs = lax.rsqrt(jnp.sum(qp * qp, axis=-1, keepdims=True) * inv_w + EPS)
        q_ref[0, h] = _bf((qp * gq_cos + qs * gq_sin) * (rs * scale))
        kp = kv[:, h * HEAD_PAD:(h + 1) * HEAD_PAD] + kr
        rs = lax.rsqrt(jnp.sum(kp * kp, axis=-1, keepdims=True) * inv_w + EPS)
        k_ref[0, h] = _bf((kp * gk_cos + kr_rot) * rs)
    vt_ref[0] = _bf(kv[:, N_HEADS * HEAD_PAD:].T)


def _mla_prep(u, base, cos, sin, qn, wq, kvn, wkv, gq, gqsw, gk, gksw, tm):
    b, t, _ = u.shape
    tab = pl.BlockSpec((tm, HEAD_PAD), lambda i, j: (j, 0))
    hd = pl.BlockSpec((1, N_HEADS, tm, HEAD_PAD), lambda i, j: (i, 0, j, 0))
    consts = (qn, wq, kvn, wkv, gq, gqsw, gk, gksw)
    return pl.pallas_call(
        functools.partial(_mla_prep_kernel, scale=MLA_QK ** -0.5 * float(np.log2(np.e))),
        grid=(b, t // tm),
        in_specs=[_ublk(tm, COL["qa"], base, 256), _ublk(tm, COL["ckv"], base, 128),
                  _ublk(tm, COL["kr"], base, 128), tab, tab]
                 + [_full(a, 2) for a in consts],
        out_specs=[hd, hd, pl.BlockSpec((1, 256, tm), lambda i, j: (i, 0, j))],
        out_shape=[jax.ShapeDtypeStruct((b, N_HEADS, t, HEAD_PAD), MXU_DTYPE),
                   jax.ShapeDtypeStruct((b, N_HEADS, t, HEAD_PAD), MXU_DTYPE),
                   jax.ShapeDtypeStruct((b, 256, t), MXU_DTYPE)],
        compiler_params=_cparams(("parallel", "parallel")),
        name="mla_prep",
    )(u, u, u, cos, sin, *consts)


def _mla_kv_prep_kernel(ckv_ref, kr_ref, kvn_ref, wkv_ref, gk_ref, k_ref, vt_ref):
    inv_w = 1.0 / MLA_QK
    ckv = ckv_ref[0]
    kv = _dot(_bf(ckv * _rms_scale(ckv) * kvn_ref[...]), wkv_ref[...])
    kr = _k_rope_only(kr_ref[0])
    for h in range(N_HEADS):
        kp = kv[:, h * HEAD_PAD:(h + 1) * HEAD_PAD] + kr
        rs = lax.rsqrt(jnp.sum(kp * kp, axis=-1, keepdims=True) * inv_w + EPS)
        k_ref[0, h] = _bf(kp * gk_ref[...] * rs)
    vt_ref[0] = _bf(kv[:, N_HEADS * HEAD_PAD:].T)


def _mla_kv_prep(u, base, kvn, wkv, gk, tm):
    b, t, _ = u.shape
    hd = pl.BlockSpec((1, N_HEADS, tm, HEAD_PAD), lambda i, j: (i, 0, j, 0))
    return pl.pallas_call(
        _mla_kv_prep_kernel,
        grid=(b, t // tm),
        in_specs=[_ublk(tm, COL["ckv"], base, 128), _ublk(tm, COL["kr"], base, 128),
                  _full(kvn, 2), _full(wkv, 2), _full(gk, 2)],
        out_specs=[hd, pl.BlockSpec((1, 256, tm), lambda i, j: (i, 0, j))],
        out_shape=[jax.ShapeDtypeStruct((b, N_HEADS, t, HEAD_PAD), MXU_DTYPE),
                   jax.ShapeDtypeStruct((b, 256, t), MXU_DTYPE)],
        compiler_params=_cparams(("parallel", "parallel")),
        name="mla_kv_prep",
    )(u, u, kvn, wkv, gk)


def _attn_kernel(*refs, tkc, tk, n_ctx, n_lat):
    if n_lat:
        q_ref, kc_ref, vtc_ref, kl_ref, vtl_ref, o_ref, sa_ref, sb_ref = refs
    else:
        q_ref, kc_ref, vtc_ref, o_ref, sa_ref, sb_ref = refs
    tq = q_ref.shape[2]
    qs = [q_ref[0, h] for h in range(N_HEADS)]

    def produce(k_ref, c, s_ref, n):
        start = pl.multiple_of(c * n, n)
        for h in range(N_HEADS):
            s_ref[h, :n] = _dot_nt(k_ref[0, h, pl.ds(start, n), :], qs[h])

    def consume(vt_ref, c, s_ref, carry, n):
        start = pl.multiple_of(c * n, n)
        ones = jnp.ones((ONES_ROWS, n), MXU_DTYPE)
        new = []
        for h in range(N_HEADS):
            m, acc = carry[h]
            s = s_ref[h, :n]
            m_new = jnp.maximum(m, jnp.max(s, axis=0, keepdims=True))
            alpha = jnp.exp2(m - m_new)
            p = _bf(jnp.exp2(s - m_new))
            vt = jnp.concatenate([vt_ref[0, h * MLA_V:(h + 1) * MLA_V, pl.ds(start, n)], ones], axis=0)
            new.append((m_new, alpha * acc + _dot(vt, p)))
        return tuple(new)

    carry = tuple((jnp.full((1, tq), -jnp.inf, jnp.float32), jnp.zeros((MLA_V + ONES_ROWS, tq), jnp.float32))
                  for _ in range(N_HEADS))
    cur, nxt = sa_ref, sb_ref
    produce(kc_ref, 0, cur, tkc)
    for c in range(n_ctx):
        if c + 1 < n_ctx:
            produce(kc_ref, c + 1, nxt, tkc)
        elif n_lat:
            produce(kl_ref, 0, nxt, tk)
        carry = consume(vtc_ref, c, cur, carry, tkc)
        cur, nxt = nxt, cur
    if n_lat:
        def body(i, carry):
            c = 2 * i
            produce(kl_ref, c + 1, nxt, tk)
            carry = consume(vtl_ref, c, cur, carry, tk)
            produce(kl_ref, c + 2, cur, tk)
            return consume(vtl_ref, c + 1, nxt, carry, tk)

        n_pairs = (n_lat - 1) // 2
        carry = lax.fori_loop(0, n_pairs, body, carry)
        c = 2 * n_pairs
        if n_lat - c == 2:
            produce(kl_ref, c + 1, nxt, tk)
            carry = consume(vtl_ref, c, cur, carry, tk)
            carry = consume(vtl_ref, c + 1, nxt, carry, tk)
        else:
            carry = consume(vtl_ref, c, cur, carry, tk)
    o_t = jnp.concatenate([acc[:MLA_V] / acc[MLA_V:MLA_V + 1] for _, acc in carry], axis=0)
    o_ref[0] = o_t.T


def _attention(q, kc, vtc, kl, vtl, tq, tkc, tk):
    b, _, t_q, _ = q.shape
    t_c = kc.shape[2]
    t_l = 0 if kl is None else kl.shape[2]
    assert t_q % tq == 0 and t_c % tkc == 0 and t_l % tk == 0
    kv_specs = [pl.BlockSpec((1, N_HEADS, t_c, HEAD_PAD), lambda i, j: (i, 0, 0, 0)),
                pl.BlockSpec((1, 256, t_c), lambda i, j: (i, 0, 0))]
    args = [q, kc, vtc]
    if t_l:
        kv_specs += [pl.BlockSpec((1, N_HEADS, t_l, HEAD_PAD), lambda i, j: (i, 0, 0, 0)),
                     pl.BlockSpec((1, 256, t_l), lambda i, j: (i, 0, 0))]
        args += [kl, vtl]
    return pl.pallas_call(
        functools.partial(_attn_kernel, tkc=tkc, tk=tk, n_ctx=t_c // tkc, n_lat=t_l // tk),
        grid=(b, t_q // tq),
        in_specs=[pl.BlockSpec((1, N_HEADS, tq, HEAD_PAD), lambda i, j: (i, 0, j, 0))] + kv_specs,
        out_specs=pl.BlockSpec((1, tq, 256), lambda i, j: (i, j, 0)),
        out_shape=jax.ShapeDtypeStruct((b, t_q, 256), jnp.float32),
        scratch_shapes=[pltpu.VMEM((N_HEADS, max(tkc, tk), tq), jnp.float32)] * 2,
        compiler_params=_cparams(("parallel", "parallel")),
        name="attention",
    )(*args)


def _pool_kernel(prev_ref, cur_ref, next_ref, w_ref, sc_ref, o_ref, ext_ref, *, tm, n_tok):
    j = pl.program_id(1)
    cur = cur_ref[0]
    ext_ref[8:8 + tm, :] = cur
    ext_ref[0:8, :] = jnp.where(j > 0, prev_ref[0], 0.0)
    ext_ref[8 + tm:16 + tm, :] = jnp.where(j < pl.num_programs(1) - 1, next_ref[0], 0.0)

    def window_sums(c0, widths):
        sh = lambda k: ext_ref[8 + k:8 + k + tm, c0:c0 + LANE]
        acc = sh(-1) + sh(0)
        out = {2: acc}
        for w in (4, 8, 16):
            if w > max(widths):
                break
            for k in list(range(-w // 2, -w // 4)) + list(range(w // 4, w // 2)):
                acc = acc + sh(k)
            out[w] = acc
        return [out[w] for w in widths]

    lane = lax.broadcasted_iota(jnp.int32, (tm, LANE), 1)
    t = j * tm + lax.broadcasted_iota(jnp.int32, (tm, LANE), 0)
    halves = []
    for c0, (wa, wb) in ((0, (2, 4)), (LANE, (8, 16))):
        sa, sb = window_sums(c0, (wa, wb))
        first = lane < POOL_GC
        s = jnp.where(first, sa, sb)
        half = jnp.where(first, wa // 2, wb // 2)
        cnt = jnp.minimum(t + half, n_tok) - jnp.maximum(t - half, 0)
        halves.append(s / cnt.astype(jnp.float32) - cur[:, c0:c0 + LANE])
    p = jnp.concatenate(halves, axis=-1)
    o_ref[0] = _dot(_bf(p), w_ref[...]) * sc_ref[...]


def _pool(u, base, w_bd, scale, tm):
    b, t, _ = u.shape
    cidx = (COL["pool"] - base) // 256
    rb = tm // 8
    last = t // 8 - 1
    return pl.pallas_call(
        functools.partial(_pool_kernel, tm=tm, n_tok=t),
        grid=(b, t // tm),
        in_specs=[pl.BlockSpec((1, 8, 256), lambda i, j: (i, jnp.maximum(j * rb - 1, 0), cidx)),
                  pl.BlockSpec((1, tm, 256), lambda i, j: (i, j, cidx)),
                  pl.BlockSpec((1, 8, 256), lambda i, j: (i, jnp.minimum((j + 1) * rb, last), cidx)),
                  _full(w_bd, 2), _full(scale, 2)],
        out_specs=pl.BlockSpec((1, tm, 256), lambda i, j: (i, j, 0)),
        out_shape=jax.ShapeDtypeStruct((b, t, 256), jnp.float32),
        scratch_shapes=[pltpu.VMEM((tm + 16, 256), jnp.float32)],
        compiler_params=_cparams(("parallel", "parallel")),
        name="pool",
    )(u, u, u, w_bd, scale)


def _head_rms(x, e):
    return lax.rsqrt(_split_dot(x * x, e) * (1.0 / NA_HD) + EPS)


def _na_prep_kernel(q_ref, k_ref, v_ref, gq_ref, gk_ref, e_ref, qo_ref, ko_ref, vo_ref):
    e = e_ref[...]
    q = q_ref[0]
    qo_ref[0] = _bf(q * _head_rms(q, e) * gq_ref[...])
    k = k_ref[0]
    ko_ref[0] = _bf(k * _head_rms(k, e) * gk_ref[...])
    vo_ref[0] = _bf(v_ref[0])


def _na_prep(uq, qcol, u, base, gq, gk, e_mat, tm):
    b, t, _ = u.shape
    tok = pl.BlockSpec((1, tm, 256), lambda i, j: (i, j, 0))
    return pl.pallas_call(
        _na_prep_kernel,
        grid=(b, t // tm),
        in_specs=[pl.BlockSpec((1, tm, 256), lambda i, j: (i, j, qcol)),
                  _ublk(tm, COL["k_na"], base, 256), _ublk(tm, COL["v_na"], base, 256),
                  _full(gq, 2), _full(gk, 2), _full(e_mat, 2)],
        out_specs=[tok, tok, tok],
        out_shape=[jax.ShapeDtypeStruct((b, t, 256), MXU_DTYPE)] * 3,
        compiler_params=_cparams(("parallel", "parallel")),
        name="na_prep",
    )(uq, u, u, gq, gk, e_mat)


def _stack_heads(q):
    qf = q.astype(jnp.float32)
    head = lax.broadcasted_iota(jnp.int32, qf.shape, 1) // NA_HD
    return _bf(jnp.concatenate([jnp.where(head == h, qf, 0.0) for h in range(N_HEADS)], axis=0))


def _unstack_heads(res):
    n = res.shape[0] // N_HEADS
    head = lax.broadcasted_iota(jnp.int32, (n, res.shape[1]), 1) // NA_HD
    out = res[0:n]
    for h in range(1, N_HEADS):
        out = jnp.where(head == h, res[h * n:(h + 1) * n], out)
    return out


def _softmax_pv(scores, values):
    m = functools.reduce(jnp.maximum, [jnp.max(s, axis=-1, keepdims=True) for s in scores])
    ps = [jnp.exp(s - m) for s in scores]
    l = functools.reduce(jnp.add, [jnp.sum(p, axis=-1, keepdims=True) for p in ps])
    acc = functools.reduce(jnp.add, [_dot(_bf(p), v) for p, v in zip(ps, values)])
    return acc / l


def _na_kernel(q_ref, k_ref, v_ref, kc_ref, vc_ref, tbl_ref, o_ref, *, rows, rb):
    jb = pl.program_id(1)
    kc = kc_ref[0]
    vc = vc_ref[0]
    n_loc = NA_WIN_R * GRID_W
    staged = []
    for rr in range(rb):
        r = jb * rb + rr
        rs = jnp.clip(r - NA_WIN_R // 2, 0, rows - NA_WIN_R)
        start = pl.multiple_of(rs * GRID_W, GRID_W)
        qm = _stack_heads(q_ref[0, rr * GRID_W:(rr + 1) * GRID_W, :])
        s_loc = _dot_nt(qm, k_ref[0, pl.ds(start, n_loc), :]) + tbl_ref[r - rs]
        staged.append((start, s_loc, _dot_nt(qm, kc)))
    for rr, (start, s_loc, s_ctx) in enumerate(staged):
        res = _softmax_pv([s_loc, s_ctx], [v_ref[0, pl.ds(start, n_loc), :], vc])
        o_ref[0, rr * GRID_W:(rr + 1) * GRID_W, :] = _unstack_heads(res)


def _na_attention(q, k, v, kc, vc, tbls, layer, rb):
    b, t, _ = q.shape
    tc = kc.shape[1]
    rows = t // GRID_W
    assert rows >= NA_WIN_R and rows % rb == 0
    seq = pl.BlockSpec((1, t, 256), lambda i, j: (i, 0, 0))
    cseq = pl.BlockSpec((1, tc, 256), lambda i, j: (i, 0, 0))
    return pl.pallas_call(
        functools.partial(_na_kernel, rows=rows, rb=rb),
        grid=(b, rows // rb),
        in_specs=[pl.BlockSpec((1, rb * GRID_W, 256), lambda i, j: (i, j, 0)), seq, seq, cseq, cseq,
                  pl.BlockSpec((None,) + tbls.shape[1:], lambda i, j: (layer, 0, 0, 0))],
        out_specs=pl.BlockSpec((1, rb * GRID_W, 256), lambda i, j: (i, j, 0)),
        out_shape=jax.ShapeDtypeStruct((b, t, 256), jnp.float32),
        compiler_params=_cparams(("parallel", "parallel")),
        name="na_attention",
    )(q, k, v, kc, vc, tbls)


def _ctx_na_kernel(q_ref, kc_ref, vc_ref, o_ref):
    kc = kc_ref[0]
    vc = vc_ref[0]
    for rr in range(q_ref.shape[1] // GRID_W):
        qm = _stack_heads(q_ref[0, rr * GRID_W:(rr + 1) * GRID_W, :])
        res = _softmax_pv([_dot_nt(qm, kc)], [vc])
        o_ref[0, rr * GRID_W:(rr + 1) * GRID_W, :] = _unstack_heads(res)


def _ctx_na_attention(q, kc, vc):
    b, tc, _ = q.shape
    cseq = pl.BlockSpec((1, tc, 256), lambda i: (i, 0, 0))
    return pl.pallas_call(
        _ctx_na_kernel,
        grid=(b,),
        in_specs=[cseq, cseq, cseq],
        out_specs=cseq,
        out_shape=jax.ShapeDtypeStruct((b, tc, 256), jnp.float32),
        compiler_params=_cparams(("parallel",)),
        name="ctx_na_attention",
    )(q, kc, vc)


def _hg_gates(z, lb):
    e = jnp.exp(-jnp.abs(z))
    r = 1.0 / (1.0 + e)
    pos = z >= 0
    sig = jnp.where(pos, r, e * r)
    sig_neg = jnp.where(pos, e * r, r)
    return jnp.log(lb + (1.0 - lb) * sig), (1.0 - lb) * sig_neg


def _hg_prep_kernel(zf_ref, zb_ref, i_ref, q_ref, lbf_ref, lbb_ref, tl_ref, tu_ref, ta_ref, sel_ref, e_ref,
                    qf_o, kf_o, qb_o, kb_o, i_o, df_o, db_o, oi_o, nat_ref, pm_ref,
                    *, tm, qscale):
    nch = tm // HG_SUB
    q = q_ref[0] * qscale
    xi = i_ref[0]
    i_o[0] = _bf(xi)
    logf_f, kg_f = _hg_gates(zf_ref[0], lbf_ref[...])
    logf_b, kg_b = _hg_gates(zb_ref[0], lbb_ref[...])
    b_f = _split3_dot_left(tl_ref[...], logf_f)
    b_b = _split3_dot_left(tu_ref[...], logf_b)
    tot_f = _split3_dot_left(ta_ref[...], logf_f)
    tot_b = _split3_dot_left(ta_ref[...], logf_b)
    qf_o[0] = _bf(q * jnp.exp(b_f))
    kf_o[0] = _bf(kg_f * jnp.exp(tot_f - b_f))
    qb_o[0] = _bf(q * jnp.exp(b_b))
    kb_o[0] = _bf(kg_b * jnp.exp(tot_b - b_b))
    df_o[0] = jnp.exp(_split3_dot_left(sel_ref[...], logf_f))
    db_o[0] = jnp.exp(_split3_dot_left(sel_ref[...], logf_b))

    log2e = float(np.log2(np.e))
    for a, val in enumerate((q, kg_f, b_f * log2e, kg_b, b_b * log2e, xi)):
        for half in range(2):
            nat_ref[a, half] = val[:, half * LANE:(half + 1) * LANE]
            for p in range(HG_SUB):
                pm_ref[a, half, p * nch:(p + 1) * nch, :] = nat_ref[a, half, pl.ds(p, nch, stride=HG_SUB), :]
    e = e_ref[...]

    def side(q_t, b_t, k_ref, b_ref, i_ref, lo, nblk):
        rows = slice(lo, lo + nblk * nch)
        k3 = k_ref[rows, :].reshape(nblk, nch, LANE)
        b3 = b_ref[rows, :].reshape(nblk, nch, LANE)
        x = q_t[None] * k3 * jnp.exp2(b_t[None] - b3)
        r = _dot(_bf(x.reshape(nblk * nch, LANE)), e) * i_ref[rows, :]
        return jnp.sum(r.reshape(nblk, nch, LANE), axis=0)

    for half in range(2):
        qp, kfp, bfp, kbp, bbp, ip = [pm_ref.at[a, half] for a in range(6)]
        for pt in range(HG_SUB):
            rt = slice(pt * nch, (pt + 1) * nch)
            q_t = qp[rt, :]
            acc = _dot(_bf(q_t * (kfp[rt, :] + kbp[rt, :])), e) * ip[rt, :]
            if pt > 0:
                acc = acc + side(q_t, bfp[rt, :], kfp, bfp, ip, 0, pt)
            if pt < HG_SUB - 1:
                acc = acc + side(q_t, bbp[rt, :], kbp, bbp, ip, (pt + 1) * nch, HG_SUB - 1 - pt)
            nat_ref[0, half, pl.ds(pt, nch, stride=HG_SUB), :] = acc
        oi_o[0, :, half * LANE:(half + 1) * LANE] = nat_ref[0, half]


def _hg_prep(uq, qcol, u, base, lbf, lbb, e_mat):
    b, t, _ = u.shape
    tm = min(HG_PREP_TILE, t)
    nsub = tm // HG_SUB
    tok = pl.BlockSpec((1, tm, 256), lambda i, j: (i, j, 0))
    dec = pl.BlockSpec((1, nsub, 256), lambda i, j: (i, j, 0))
    tl, tu, ta, sel = _hg_consts(tm)
    e_half = e_mat[:LANE, :LANE]
    regroup = pltpu.VMEM((6, 2, tm, LANE), jnp.float32)
    return pl.pallas_call(
        functools.partial(_hg_prep_kernel, tm=tm, qscale=HG_DK ** -0.5),
        grid=(b, t // tm),
        in_specs=[_ublk(tm, COL["zF"], base, 256), _ublk(tm, COL["zB"], base, 256),
                  _ublk(tm, COL["i_hg"], base, 256),
                  pl.BlockSpec((1, tm, 256), lambda i, j: (i, j, qcol)),
                  _full(lbf, 2), _full(lbb, 2), _full(tl, 2), _full(tu, 2), _full(ta, 2), _full(sel, 2),
                  _full(e_half, 2)],
        out_specs=[tok, tok, tok, tok, tok, dec, dec, tok],
        out_shape=[jax.ShapeDtypeStruct((b, t, 256), MXU_DTYPE)] * 5
                  + [jax.ShapeDtypeStruct((b, t // HG_SUB, 256), jnp.float32)] * 2
                  + [jax.ShapeDtypeStruct((b, t, 256), jnp.float32)],
        scratch_shapes=[regroup, regroup],
        compiler_params=_cparams(("parallel", "parallel")),
        name="hg_prep",
    )(u, u, u, uq, lbf, lbb, tl, tu, ta, sel, e_half)


def _hg_scan_kernel(qf_ref, kf_ref, if_ref, df_ref, qb_ref, kb_ref, ib_ref, db_ref, s0f_ref, s0b_ref,
                    of_ref, ob_ref, sf_out, sb_out, sf, sb, *, nsub, nb):
    j = pl.program_id(0)

    @pl.when(j == 0)
    def _():
        sf[...] = s0f_ref[...]
        sb[...] = s0b_ref[...]

    first = lax.broadcasted_iota(jnp.int32, (HG_SUB, LANE), 1) < HG_DK

    def step(state, q_ref, k_ref, i_ref, d_ref, o_ref, bi, c):
        rows = slice(c * HG_SUB, (c + 1) * HG_SUB)
        for p in range(2):
            cols = slice(p * LANE, (p + 1) * LANE)
            s_old = state[bi, p]
            q = q_ref[bi, rows, cols].astype(jnp.float32)
            q2 = _bf(jnp.concatenate([jnp.where(first, q, 0.0), jnp.where(first, 0.0, q)], axis=0))
            r = _dot_nt(q2, _bf(s_old))
            o_ref[bi, rows, cols] = jnp.where(first, r[:HG_SUB], r[HG_SUB:])
            upd = _dot_tn(i_ref[bi, rows, cols], k_ref[bi, rows, cols])
            state[bi, p] = s_old * d_ref[bi, c:c + 1, cols] + upd

    for c in range(nsub):
        for bi in range(nb):
            step(sf, qf_ref, kf_ref, if_ref, df_ref, of_ref, bi, c)
            step(sb, qb_ref, kb_ref, ib_ref, db_ref, ob_ref, bi, nsub - 1 - c)

    @pl.when(j == pl.num_programs(0) - 1)
    def _():
        sf_out[...] = sf[...]
        sb_out[...] = sb[...]


def _hg_scan(qf, kf, qb, kb, xi, df, db, s0f, s0b):
    b, t, _ = qf.shape
    tm = HG_TILE
    n = t // tm
    nsub = tm // HG_SUB
    fwd = pl.BlockSpec((b, tm, 256), lambda j: (0, j, 0))
    bwd = pl.BlockSpec((b, tm, 256), lambda j: (0, n - 1 - j, 0))
    dfwd = pl.BlockSpec((b, nsub, 256), lambda j: (0, j, 0))
    dbwd = pl.BlockSpec((b, nsub, 256), lambda j: (0, n - 1 - j, 0))
    st = pl.BlockSpec((b, 2, LANE, LANE), lambda j: (0, 0, 0, 0))
    return pl.pallas_call(
        functools.partial(_hg_scan_kernel, nsub=nsub, nb=b),
        grid=(n,),
        in_specs=[fwd, fwd, fwd, dfwd, bwd, bwd, bwd, dbwd, st, st],
        out_specs=[fwd, bwd, st, st],
        out_shape=[jax.ShapeDtypeStruct((b, t, 256), jnp.float32)] * 2
                  + [jax.ShapeDtypeStruct((b, 2, LANE, LANE), jnp.float32)] * 2,
        scratch_shapes=[pltpu.VMEM((b, 2, LANE, LANE), jnp.float32)] * 2,
        compiler_params=_cparams(("arbitrary",)),
        name="hg_scan",
    )(qf, kf, xi, df, qb, kb, xi, db, s0f, s0b)


def _merge_kernel(x_ref, a_ref, b_ref, c_ref, of_ref, ob_ref, oi_ref, gc_ref, g_ref, sh_ref, sc_ref, wg_ref,
                  hgn_ref, e_ref, wbr_ref, wo_ref, gate_ref, out_ref):
    d_model = x_ref.shape[2]
    x = x_ref[0]
    h = _bf((x * _rms_scale(x) * g_ref[...]) * (1.0 + sc_ref[0]) + sh_ref[0])
    o = of_ref[0] + ob_ref[0] + oi_ref[0]
    ms = _split_dot(o * o, e_ref[...]) * (1.0 / HG_DK)
    d = o * lax.rsqrt(ms + EPS) * hgn_ref[...] * _sigmoid(gc_ref[0])
    acc = None
    for n, y in enumerate((a_ref[0], b_ref[0], c_ref[0], d)):
        gate = _sigmoid(_dot(h, wg_ref[:, n * d_model:(n + 1) * d_model]))
        term = gate * _dot(_bf(y), wbr_ref[n])
        acc = term if acc is None else acc + term
    out_ref[0] = x + gate_ref[0] * _dot(_bf(acc), wo_ref[...])


def _merge(x, a, bm, cn, o_f, o_b, o_i, u, g1n, w_gate, hgn, e_mat, w_br, w_o, layer, mods, tm):
    b, t, d = x.shape
    tok = pl.BlockSpec((1, tm, 256), lambda i, j: (i, j, 0))
    xs = pl.BlockSpec((1, tm, d), lambda i, j: (i, j, 0))
    of_layer = lambda a: pl.BlockSpec((None,) + a.shape[1:], lambda i, j: (layer,) + (0,) * (a.ndim - 1))
    return pl.pallas_call(
        _merge_kernel,
        grid=(b, t // tm),
        in_specs=[xs, tok, tok, tok, tok, tok, tok, _ublk(tm, COL["g_cols"], 0, 256),
                  _full(g1n, 2), _mod_spec(mods, 0, d), _mod_spec(mods, 1, d), of_layer(w_gate),
                  _full(hgn, 2), _full(e_mat, 2), of_layer(w_br), of_layer(w_o), _mod_spec(mods, 2, d)],
        out_specs=xs,
        out_shape=jax.ShapeDtypeStruct((b, t, d), jnp.float32),
        compiler_params=_cparams(("parallel", "parallel")),
        name="merge",
    )(x, a, bm, cn, o_f, o_b, o_i, u, g1n, mods, mods, w_gate, hgn, e_mat, w_br, w_o, mods)


def _ffn_kernel(x_ref, g_ref, sh_ref, sc_ref, w1_ref, w2_ref, gate_ref, o_ref, *, tf):
    x = x_ref[0]
    h = _bf((x * _rms_scale(x) * g_ref[...]) * (1.0 + sc_ref[0]) + sh_ref[0])
    acc = None
    for k in range(w1_ref.shape[1] // tf):
        z = jnp.maximum(_dot(h, w1_ref[:, k * tf:(k + 1) * tf]), 0.0)
        part = _dot(_bf(z * z), w2_ref[k * tf:(k + 1) * tf, :])
        acc = part if acc is None else acc + part
    o_ref[0] = x + gate_ref[0] * acc


def _ffn(x, g, mods, w1, w2, layer, tm, tf):
    b, t, d = x.shape
    xs = pl.BlockSpec((1, tm, d), lambda i, j: (i, j, 0))
    resident = lambda a: pl.BlockSpec((None,) + a.shape[1:], lambda i, j: (layer, 0, 0),
                                      pipeline_mode=pl.Buffered(1))
    return pl.pallas_call(
        functools.partial(_ffn_kernel, tf=tf),
        grid=(b, t // tm),
        in_specs=[xs, pl.BlockSpec((1, d), lambda i, j: (0, 0)), _mod_spec(mods, 3, d), _mod_spec(mods, 4, d),
                  resident(w1), resident(w2), _mod_spec(mods, 5, d)],
        out_specs=xs,
        out_shape=jax.ShapeDtypeStruct((b, t, d), jnp.float32),
        compiler_params=_cparams(("parallel", "parallel")),
        name="ffn",
    )(x, g, mods, mods, w1, w2, mods)


def _rope_perm():
    return np.array([(t // 16) * 16 + ((t % 16) + 8) % 16 for t in range(MLA_ROPE)])


W_IN_COLS = dict(ckv=0, kr=128, k_na=160, v_na=416, zF=672, zB=928, i_hg=1184, pool=1440, qa=1696, q_na=1952,
                 q_hg=2208, g_cols=2464, gates=2720)


def _w_in_relayout_kernel(w_ref, gate_ref, mix_ref):
    w = w_ref[0]
    tr = w.shape[0]
    o = W_IN_COLS
    gate_ref[0] = _bf(w[:, o["gates"]:])
    z = lambda n: jnp.zeros((tr, n), jnp.float32)
    kr = w[:, o["kr"]:o["kr"] + MLA_ROPE]
    quarter = MLA_ROPE // 4
    kr_sw = jnp.concatenate([kr[:, quarter:2 * quarter], kr[:, :quarter],
                             kr[:, 3 * quarter:], kr[:, 2 * quarter:3 * quarter]], axis=1)
    parts = [w[:, o[name]:o[name] + 256] for name in ("k_na", "v_na", "zF", "zB", "i_hg")]
    parts += [w[:, :MLA_KV_RANK], z(MLA_NOPE), kr, kr_sw]
    parts += [w[:, o[name]:o[name] + 256] for name in ("pool", "qa", "q_na", "q_hg", "g_cols")]
    mix_ref[0] = _bf(jnp.concatenate(parts, axis=1))


def _prep_w_in(w):
    depth, d, n = w.shape
    tr = 128
    n_gate = n - W_IN_COLS["gates"]
    return pl.pallas_call(
        _w_in_relayout_kernel,
        grid=(depth, d // tr),
        in_specs=[pl.BlockSpec((1, tr, n), lambda l, i: (l, i, 0))],
        out_specs=[pl.BlockSpec((1, tr, n_gate), lambda l, i: (l, i, 0)),
                   pl.BlockSpec((1, tr, D_MIX_PAD), lambda l, i: (l, i, 0))],
        out_shape=[jax.ShapeDtypeStruct((depth, d, n_gate), MXU_DTYPE),
                   jax.ShapeDtypeStruct((depth, d, D_MIX_PAD), MXU_DTYPE)],
        compiler_params=_cparams(("parallel", "parallel")),
        name="w_in_relayout",
    )(w)


def _prep_wq(w):
    r = w.shape[0]
    wh = w.reshape(r, N_HEADS, MLA_QK)
    pre = jnp.pad(wh, ((0, 0), (0, 0), (0, HEAD_PAD - MLA_QK)))
    sw = jnp.pad(wh[:, :, MLA_NOPE:][:, :, _rope_perm()], ((0, 0), (0, 0), (MLA_NOPE, HEAD_PAD - MLA_QK)))
    return _bf(jnp.concatenate([pre.reshape(r, -1), sw.reshape(r, -1)], axis=1))


def _prep_wkv(w):
    r = w.shape[0]
    wh = w.reshape(r, N_HEADS, MLA_NOPE + MLA_V)
    kpart = jnp.pad(wh[:, :, :MLA_NOPE], ((0, 0), (0, 0), (0, HEAD_PAD - MLA_NOPE)))
    return _bf(jnp.concatenate([kpart.reshape(r, -1), wh[:, :, MLA_NOPE:].reshape(r, -1)], axis=1))


def _pad_gain(g):
    plain = jnp.pad(g, (0, HEAD_PAD - MLA_QK))[None]
    sw = jnp.pad(g[MLA_NOPE:][_rope_perm()], (MLA_NOPE, HEAD_PAD - MLA_QK))[None]
    return plain, sw


def _rope_tables(n_tok):
    t = jnp.arange(n_tok)
    pos = jnp.stack([t // GRID_W, t % GRID_W], axis=-1).astype(jnp.float32)
    nf = MLA_ROPE // 4
    inv = ROPE_BASE ** (-jnp.arange(nf, dtype=jnp.float32) / nf)
    ang = pos[:, :, None] * inv
    cos, sin = jnp.cos(ang), jnp.sin(ang)
    cos_t = jnp.concatenate([cos, cos], axis=-1).reshape(n_tok, MLA_ROPE)
    sin_t = jnp.concatenate([-sin, sin], axis=-1).reshape(n_tok, MLA_ROPE)
    ones = jnp.ones((n_tok, MLA_NOPE), jnp.float32)
    zer = jnp.zeros((n_tok, HEAD_PAD - MLA_QK), jnp.float32)
    cos_p = jnp.concatenate([ones, cos_t, zer], axis=-1)
    sin_p = jnp.concatenate([0 * ones, sin_t, zer], axis=-1)
    return cos_p, sin_p


def _plain_tables(n_tok):
    cos_p = jnp.concatenate([jnp.ones((n_tok, MLA_QK), jnp.float32),
                             jnp.zeros((n_tok, HEAD_PAD - MLA_QK), jnp.float32)], axis=-1)
    return cos_p, jnp.zeros((n_tok, HEAD_PAD), jnp.float32)


def _head_indicator(dtype):
    i = np.arange(256) // 64
    return jnp.asarray((i[:, None] == i[None, :]).astype(np.float32), dtype=dtype)


def _na_bias_tables(rpb):
    qc = np.arange(GRID_W)[:, None]
    kc = np.arange(GRID_W)[None, :]
    cs = np.clip(qc - NA_WIN_C // 2, 0, GRID_W - NA_WIN_C)
    inwin = (kc >= cs) & (kc < cs + NA_WIN_C)
    dc = np.clip(kc - qc + NA_WIN_C - 1, 0, 2 * NA_WIN_C - 2)
    col_hot = (dc[:, :, None] == np.arange(2 * NA_WIN_C - 1)).astype(np.float32)
    p = np.arange(NA_WIN_R)[:, None, None]
    j = np.arange(NA_WIN_R)[None, :, None]
    row_hot = (np.arange(2 * NA_WIN_R - 1)[None, None, :] == j + NA_WIN_R - 1 - p).astype(np.float32)
    t = jnp.einsum('lhrd,pjr,qkd->lphqjk', rpb.astype(jnp.float32), row_hot, col_hot,
                   precision=lax.Precision.HIGHEST)
    t = jnp.where(inwin[None, None, None, :, None, :], t, NEG)
    return t.reshape(rpb.shape[0], NA_WIN_R, N_HEADS * GRID_W, NA_WIN_R * GRID_W)


def _hg_consts(tm):
    t = np.arange(tm)
    same = (t[:, None] // HG_SUB) == (t[None, :] // HG_SUB)
    tl = same & (t[None, :] <= t[:, None])
    tu = same & (t[None, :] >= t[:, None])
    sel = (np.arange(tm // HG_SUB)[:, None] == (t[None, :] // HG_SUB))
    f = lambda m: jnp.asarray(m.astype(np.float32), dtype=MXU_DTYPE)
    return f(tl), f(tu), f(same), f(sel)


def _block_diag(w):
    g, c, d = w.shape
    out = jnp.zeros((g * c, g * d), w.dtype)
    for i in range(g):
        out = out.at[i * c:(i + 1) * c, i * d:(i + 1) * d].set(w[i])
    return out


def kernel(x, c, ctx, c_ctx, ada_w, ada_b, norm1, norm2, w_in, pool_w, pool_scale, mla_q_norm, mla_wq_b, mla_kv_norm, mla_wkv_b, mla_gq, mla_gk, na_gq, na_gk, na_rpb, hg_lb, hg_norm, w_branch, w_out, w_ff1, w_ff2):
    B, L, D = x.shape
    Tc = ctx.shape[1]
    depth = ada_w.shape[0]
    f32 = jnp.float32
    assert L % HG_TILE == 0 and Tc % HG_TILE == 0 and B < 8

    lb_p = jax.nn.softmax(hg_lb.astype(f32), axis=0)
    lb_all = jnp.clip(jnp.cumsum(lb_p, axis=0)[:depth], LB_EPS, 1.0 - LB_EPS)

    cc = jnp.concatenate([c, c_ctx[None], jnp.zeros((8 - B - 1, D), f32)], axis=0)
    mods = _mods(cc, ada_w, ada_b)

    cos_l, sin_l = _rope_tables(L)
    cos_c, sin_c = _plain_tables(Tc)
    e_mat = _head_indicator(MXU_DTYPE)
    zero_state = jnp.zeros((B, 2, LANE, LANE), f32)
    zero_q = jnp.zeros((B, Tc, 256), f32)
    tl = min(512, L)
    t_big = min(1024, L)

    na_tbls = _na_bias_tables(na_rpb)
    w_gate, w_mix = _prep_w_in(w_in)
    w_br = _bf(w_branch)
    w_o = _bf(w_out)
    w1 = _bf(w_ff1)
    w2 = _bf(w_ff2)
    xl, xc = x, ctx
    for l in range(depth):
        ctx_out = l < depth - 1
        mods_l = mods[l, :B].reshape(B, 1, 6 * D)
        mods_c = mods[l, B].reshape(1, 1, 6 * D)
        g1n = norm1[l][None]
        g2n = norm2[l][None]
        wq = _prep_wq(mla_wq_b[l])
        wkv = _prep_wkv(mla_wkv_b[l])
        gq, gqsw = _pad_gain(mla_gq[l])
        gk, gksw = _pad_gain(mla_gk[l])
        qn = mla_q_norm[l][None]
        kvn = mla_kv_norm[l][None]
        na_q_gain = (jnp.tile(na_gq[l], N_HEADS) * NA_HD ** -0.5)[None]
        na_k_gain = jnp.tile(na_gk[l], N_HEADS)[None]
        pool_bd = _bf(_block_diag(pool_w[l]))
        pool_sc = pool_scale[l][None]
        hgn = jnp.tile(hg_norm[l], N_HEADS)[None]
        lbF = lb_all[l, 0][None]
        lbB = lb_all[l, 1][None]

        ul = _inproj(xl, g1n, mods_l, w_mix, l, 0, D_MIX_PAD, tl, D_MIX_PAD, f32)
        if ctx_out:
            uc = _inproj(xc, g1n, mods_c, w_mix, l, 0, D_MIX_PAD, Tc, MIX_TILE, f32)
        else:
            uc = _inproj(xc, g1n, mods_c, w_mix, l, 0, D_KV, Tc, D_KV, f32)
        cbase = 0

        q_mla, k_mla, v_mla = _mla_prep(ul, 0, cos_l, sin_l, qn, wq, kvn, wkv, gq, gqsw, gk, gksw, t_big)
        if ctx_out:
            q_mla_c, k_mla_c, v_mla_c = _mla_prep(uc, cbase, cos_c, sin_c, qn, wq, kvn, wkv, gq, gqsw, gk, gksw, Tc)
        else:
            k_mla_c, v_mla_c = _mla_kv_prep(uc, cbase, kvn, wkv, gk, Tc)
        b_l = _attention(q_mla, k_mla_c, v_mla_c, k_mla, v_mla, tl, Tc, tl)

        q_na, k_na, v_na = _na_prep(ul, COL["q_na"] // 256, ul, 0, na_q_gain, na_k_gain, e_mat, t_big)
        if ctx_out:
            q_na_c, k_na_c, v_na_c = _na_prep(uc, COL["q_na"] // 256, uc, cbase, na_q_gain, na_k_gain, e_mat, Tc)
        else:
            _, k_na_c, v_na_c = _na_prep(zero_q, 0, uc, cbase, na_q_gain, na_k_gain, e_mat, Tc)
        c_l = _na_attention(q_na, k_na, v_na, k_na_c, v_na_c, na_tbls, l, 8)

        a_l = _pool(ul, 0, pool_bd, pool_sc, t_big)

        if ctx_out:
            hc = _hg_prep(uc, COL["q_hg"] // 256, uc, cbase, lbF, lbB, e_mat)
        else:
            hc = _hg_prep(zero_q, 0, uc, cbase, lbF, lbB, e_mat)
        oFc, oBc, sFc, sBc = _hg_scan(hc[0], hc[1], hc[2], hc[3], hc[4], hc[5], hc[6], zero_state, zero_state)
        hl = _hg_prep(ul, COL["q_hg"] // 256, ul, 0, lbF, lbB, e_mat)
        oFl, oBl, _, _ = _hg_scan(hl[0], hl[1], hl[2], hl[3], hl[4], hl[5], hl[6], sFc, sBc)

        xl_new = _merge(xl, a_l, b_l, c_l, oFl, oBl, hl[7], ul, g1n, w_gate, hgn, e_mat, w_br, w_o, l, mods_l, tl)

        if ctx_out:
            a_c = _pool(uc, cbase, pool_bd, pool_sc, Tc)
            b_c = _attention(q_mla_c, k_mla_c, v_mla_c, None, None, Tc, Tc, Tc)
            c_c = _ctx_na_attention(q_na_c, k_na_c, v_na_c)
            xc = _merge(xc, a_c, b_c, c_c, oFc, oBc, hc[7], uc, g1n, w_gate, hgn, e_mat, w_br, w_o, l, mods_c, Tc)
            xc = _ffn(xc, g2n, mods_c, w1, w2, l, Tc, 1024)

        xl = _ffn(xl_new, g2n, mods_l, w1, w2, l, tl, 2048)
    return xl
```

```python
import functools

import jax
import jax.numpy as jnp
import numpy as np
from jax import lax
from jax.experimental import pallas as pl
from jax.experimental.pallas import tpu as pltpu

GRID_W = 64
EPS = 1e-6
POOL_GC = 64
N_HEADS = 4
MLA_KV_RANK = 128
MLA_NOPE = 64
MLA_ROPE = 32
MLA_V = 64
MLA_QK = 96
ROPE_BASE = 10000.0
NA_HD = 64
NA_WIN_R = 8
NA_WIN_C = 16
HG_DK = 64
LB_EPS = 1e-6

MXU_DTYPE = jnp.bfloat16
LANE = 128
HEAD_PAD = 128
ONES_ROWS = 16
VMEM_LIMIT = 48 * 1024 * 1024
HG_SUB = 16
HG_TILE = 256
HG_PREP_TILE = 256
NEG = -1e30

COL = dict(k_na=0, v_na=256, zF=512, zB=768, i_hg=1024, ckv=1280, kr=1408,
           pool=1536, qa=1792, q_na=2048, q_hg=2304, g_cols=2560)
D_MIX_PAD = 2816
D_KV = 1536
MIX_TILE = 1408


def _cparams(sem):
    return pltpu.CompilerParams(dimension_semantics=sem, vmem_limit_bytes=VMEM_LIMIT)


def _bf(x):
    return x.astype(MXU_DTYPE)


def _dot(a, b):
    return jnp.dot(a, b, preferred_element_type=jnp.float32)


def _dot_nt(a, b):
    return lax.dot_general(a, b, (((1,), (1,)), ((), ())), preferred_element_type=jnp.float32)


def _dot_tn(a, b):
    return lax.dot_general(a, b, (((0,), (0,)), ((), ())), preferred_element_type=jnp.float32)


def _split2(x):
    hi = _bf(x)
    return hi, _bf(x - hi.astype(jnp.float32))


def _split_dot(x, w):
    hi, lo = _split2(x)
    return _dot(hi, w) + _dot(lo, w)


def _split3_dot_left(w, x):
    hi = _bf(x)
    r1 = x - hi.astype(jnp.float32)
    mid = _bf(r1)
    lo = _bf(r1 - mid.astype(jnp.float32))
    return _dot(w, hi) + _dot(w, mid) + _dot(w, lo)


def _sigmoid(x):
    return 0.5 * jnp.tanh(0.5 * x) + 0.5


def _rms_scale(x):
    return lax.rsqrt(jnp.mean(x * x, axis=-1, keepdims=True) + EPS)


def _full(arr, nargs):
    zeros = (0,) * arr.ndim
    return pl.BlockSpec(arr.shape, lambda *_: zeros)


def _ublk(tm, col, base, width):
    idx = (col - base) // width
    return pl.BlockSpec((1, tm, width), lambda i, j: (i, j, idx))


def _mods_kernel(c_ref, w_ref, b_ref, o_ref):
    c = c_ref[...]
    s = _bf(c * _sigmoid(c))
    o_ref[0] = _dot(s, _bf(w_ref[0])) + b_ref[0]


def _mods(cc, ada_w, ada_b):
    depth, d, n = ada_w.shape
    tn = 1536
    return pl.pallas_call(
        _mods_kernel,
        grid=(depth, n // tn),
        in_specs=[pl.BlockSpec((8, d), lambda l, j: (0, 0)),
                  pl.BlockSpec((1, d, tn), lambda l, j: (l, 0, j)),
                  pl.BlockSpec((1, 1, tn), lambda l, j: (l, 0, j))],
        out_specs=pl.BlockSpec((1, 8, tn), lambda l, j: (l, 0, j)),
        out_shape=jax.ShapeDtypeStruct((depth, 8, n), jnp.float32),
        compiler_params=_cparams(("arbitrary", "arbitrary")),
        name="mods",
    )(cc, ada_w, ada_b.reshape(depth, 1, n))


def _inproj_kernel(x_ref, g_ref, sh_ref, sc_ref, w_ref, o_ref, h_ref):
    @pl.when(pl.program_id(2) == 0)
    def _():
        x = x_ref[0]
        y = x * _rms_scale(x) * g_ref[...]
        h_ref[...] = _bf(y * (1.0 + sc_ref[0]) + sh_ref[0])

    o_ref[0] = _dot(h_ref[...], w_ref[...]).astype(o_ref.dtype)


def _mod_spec(mods, idx, d):
    if mods.shape[0] == 1:
        return pl.BlockSpec((1, 1, d), lambda i, *_: (0, 0, idx))
    return pl.BlockSpec((1, 1, d), lambda i, *_: (i, 0, idx))


def _inproj(x, g, mods, w, layer, col0, n, tm, tn, out_dtype):
    b, t, d = x.shape
    assert n % tn == 0 and col0 % tn == 0
    return pl.pallas_call(
        _inproj_kernel,
        grid=(b, t // tm, n // tn),
        in_specs=[pl.BlockSpec((1, tm, d), lambda i, j, k: (i, j, 0)),
                  pl.BlockSpec((1, d), lambda i, j, k: (0, 0)),
                  _mod_spec(mods, 0, d), _mod_spec(mods, 1, d),
                  pl.BlockSpec((None, d, tn), lambda i, j, k: (layer, 0, k + col0 // tn))],
        out_specs=pl.BlockSpec((1, tm, tn), lambda i, j, k: (i, j, k)),
        out_shape=jax.ShapeDtypeStruct((b, t, n), out_dtype),
        scratch_shapes=[pltpu.VMEM((tm, d), MXU_DTYPE)],
        compiler_params=_cparams(("parallel", "parallel", "arbitrary")),
        name="inproj",
    )(x, g, mods, mods, w)


def _k_rope_only(krp):
    lane = lax.broadcasted_iota(jnp.int32, krp.shape, 1)
    return jnp.where(lane < MLA_QK, krp, 0.0)


def _mla_prep_kernel(qa_ref, ckv_ref, kr_ref, cos_ref, sin_ref, qn_ref, wq_ref, kvn_ref, wkv_ref,
                     gq_ref, gqsw_ref, gk_ref, gksw_ref, q_ref, k_ref, vt_ref, *, scale):
    cos = cos_ref[...]
    sin = sin_ref[...]
    inv_w = 1.0 / MLA_QK
    qa = qa_ref[0]
    qq = _dot(_bf(qa * _rms_scale(qa) * qn_ref[...]), wq_ref[...])
    ckv = ckv_ref[0]
    kv = _dot(_bf(ckv * _rms_scale(ckv) * kvn_ref[...]), wkv_ref[...])
    krp = kr_ref[0]
    kr = _k_rope_only(krp)
    kr_rot = pltpu.roll(krp, HEAD_PAD - MLA_ROPE, axis=1) * gksw_ref[...] * sin
    gq_cos = gq_ref[...] * cos
    gq_sin = gqsw_ref[...] * sin
    gk_cos = gk_ref[...] * cos
    for h in range(N_HEADS):
        qp = qq[:, h * HEAD_PAD:(h + 1) * HEAD_PAD]
        qs = qq[:, (N_HEADS + h) * HEAD_PAD:(N_HEADS + h + 1) * HEAD_PAD]
        rs = lax.rsqrt(jnp.sum(qp * qp, axis=-1, keepdims=True) * inv_w + EPS)
        q_ref[0, h] = _bf((qp * gq_cos + qs * gq_sin) * (rs * scale))
        kp = kv[:, h * HEAD_PAD:(h + 1) * HEAD_PAD] + kr
        rs = lax.rsqrt(jnp.sum(kp * kp, axis=-1, keepdims=True) * inv_w + EPS)
        k_ref[0, h] = _bf((kp * gk_cos + kr_rot) * rs)
    vt_ref[0] = _bf(kv[:, N_HEADS * HEAD_PAD:].T)


def _mla_prep(u, base, cos, sin, qn, wq, kvn, wkv, gq, gqsw, gk, gksw, tm):
    b, t, _ = u.shape
    tab = pl.BlockSpec((tm, HEAD_PAD), lambda i, j: (j, 0))
    hd = pl.BlockSpec((1, N_HEADS, tm, HEAD_PAD), lambda i, j: (i, 0, j, 0))
    consts = (qn, wq, kvn, wkv, gq, gqsw, gk, gksw)
    return pl.pallas_call(
        functools.partial(_mla_prep_kernel, scale=MLA_QK ** -0.5 * float(np.log2(np.e))),
        grid=(b, t // tm),
        in_specs=[_ublk(tm, COL["qa"], base, 256), _ublk(tm, COL["ckv"], base, 128),
                  _ublk(tm, COL["kr"], base, 128), tab, tab]
                 + [_full(a, 2) for a in consts],
        out_specs=[hd, hd, pl.BlockSpec((1, 256, tm), lambda i, j: (i, 0, j))],
        out_shape=[jax.ShapeDtypeStruct((b, N_HEADS, t, HEAD_PAD), MXU_DTYPE),
                   jax.ShapeDtypeStruct((b, N_HEADS, t, HEAD_PAD), MXU_DTYPE),
                   jax.ShapeDtypeStruct((b, 256, t), MXU_DTYPE)],
        compiler_params=_cparams(("parallel", "parallel")),
        name="mla_prep",
    )(u, u, u, cos, sin, *consts)


def _mla_kv_prep_kernel(ckv_ref, kr_ref, kvn_ref, wkv_ref, gk_ref, k_ref, vt_ref):
    inv_w = 1.0 / MLA_QK
    ckv = ckv_ref[0]
    kv = _dot(_bf(ckv * _rms_scale(ckv) * kvn_ref[...]), wkv_ref[...])
    kr = _k_rope_only(kr_ref[0])
    for h in range(N_HEADS):
        kp = kv[:, h * HEAD_PAD:(h + 1) * HEAD_PAD] + kr
        rs = lax.rsqrt(jnp.sum(kp * kp, axis=-1, keepdims=True) * inv_w + EPS)
        k_ref[0, h] = _bf(kp * gk_ref[...] * rs)
    vt_ref[0] = _bf(kv[:, N_HEADS * HEAD_PAD:].T)


def _mla_kv_prep(u, base, kvn, wkv, gk, tm):
    b, t, _ = u.shape
    hd = pl.BlockSpec((1, N_HEADS, tm, HEAD_PAD), lambda i, j: (i, 0, j, 0))
    return pl.pallas_call(
        _mla_kv_prep_kernel,
        grid=(b, t // tm),
        in_specs=[_ublk(tm, COL["ckv"], base, 128), _ublk(tm, COL["kr"], base, 128),
                  _full(kvn, 2), _full(wkv, 2), _full(gk, 2)],
        out_specs=[hd, pl.BlockSpec((1, 256, tm), lambda i, j: (i, 0, j))],
        out_shape=[jax.ShapeDtypeStruct((b, N_HEADS, t, HEAD_PAD), MXU_DTYPE),
                   jax.ShapeDtypeStruct((b, 256, t), MXU_DTYPE)],
        compiler_params=_cparams(("parallel", "parallel")),
        name="mla_kv_prep",
    )(u, u, kvn, wkv, gk)


def _attn_kernel(*refs, tkc, tk, n_ctx, n_lat):
    if n_lat:
        q_ref, kc_ref, vtc_ref, kl_ref, vtl_ref, o_ref, sa_ref, sb_ref = refs
    else:
        q_ref, kc_ref, vtc_ref, o_ref, sa_ref, sb_ref = refs
    tq = q_ref.shape[2]
    qs = [q_ref[0, h] for h in range(N_HEADS)]

    def produce(k_ref, c, s_ref, n):
        start = pl.multiple_of(c * n, n)
        for h in range(N_HEADS):
            s_ref[h, :n] = _dot_nt(k_ref[0, h, pl.ds(start, n), :], qs[h])

    def consume(vt_ref, c, s_ref, carry, n):
        start = pl.multiple_of(c * n, n)
        ones = jnp.ones((ONES_ROWS, n), MXU_DTYPE)
        new = []
        for h in range(N_HEADS):
            m, acc = carry[h]
            s = s_ref[h, :n]
            m_new = jnp.maximum(m, jnp.max(s, axis=0, keepdims=True))
            alpha = jnp.exp2(m - m_new)
            p = _bf(jnp.exp2(s - m_new))
            vt = jnp.concatenate([vt_ref[0, h * MLA_V:(h + 1) * MLA_V, pl.ds(start, n)], ones], axis=0)
            new.append((m_new, alpha * acc + _dot(vt, p)))
        return tuple(new)

    carry = tuple((jnp.full((1, tq), -jnp.inf, jnp.float32), jnp.zeros((MLA_V + ONES_ROWS, tq), jnp.float32))
                  for _ in range(N_HEADS))
    cur, nxt = sa_ref, sb_ref
    produce(kc_ref, 0, cur, tkc)
    for c in range(n_ctx):
        if c + 1 < n_ctx:
            produce(kc_ref, c + 1, nxt, tkc)
        elif n_lat:
            produce(kl_ref, 0, nxt, tk)
        carry = consume(vtc_ref, c, cur, carry, tkc)
        cur, nxt = nxt, cur
    if n_lat:
        def body(i, carry):
            c = 2 * i
            produce(kl_ref, c + 1, nxt, tk)
            carry = consume(vtl_ref, c, cur, carry, tk)
            produce(kl_ref, c + 2, cur, tk)
            return consume(vtl_ref, c + 1, nxt, carry, tk)

        n_pairs = (n_lat - 1) // 2
        carry = lax.fori_loop(0, n_pairs, body, carry)
        c = 2 * n_pairs
        if n_lat - c == 2:
            produce(kl_ref, c + 1, nxt, tk)
            carry = consume(vtl_ref, c, cur, carry, tk)
            carry = consume(vtl_ref, c + 1, nxt, carry, tk)
        else:
            carry = consume(vtl_ref, c, cur, carry, tk)
    o_t = jnp.concatenate([acc[:MLA_V] / acc[MLA_V:MLA_V + 1] for _, acc in carry], axis=0)
    o_ref[0] = o_t.T


def _attention(q, kc, vtc, kl, vtl, tq, tkc, tk):
    b, _, t_q, _ = q.shape
    t_c = kc.shape[2]
    t_l = 0 if kl is None else kl.shape[2]
    assert t_q % tq == 0 and t_c % tkc == 0 and t_l % tk == 0
    kv_specs = [pl.BlockSpec((1, N_HEADS, t_c, HEAD_PAD), lambda i, j: (i, 0, 0, 0)),
                pl.BlockSpec((1, 256, t_c), lambda i, j: (i, 0, 0))]
    args = [q, kc, vtc]
    if t_l:
        kv_specs += [pl.BlockSpec((1, N_HEADS, t_l, HEAD_PAD), lambda i, j: (i, 0, 0, 0)),
                     pl.BlockSpec((1, 256, t_l), lambda i, j: (i, 0, 0))]
        args += [kl, vtl]
    return pl.pallas_call(
        functools.partial(_attn_kernel, tkc=tkc, tk=tk, n_ctx=t_c // tkc, n_lat=t_l // tk),
        grid=(b, t_q // tq),
        in_specs=[pl.BlockSpec((1, N_HEADS, tq, HEAD_PAD), lambda i, j: (i, 0, j, 0))] + kv_specs,
        out_specs=pl.BlockSpec((1, tq, 256), lambda i, j: (i, j, 0)),
        out_shape=jax.ShapeDtypeStruct((b, t_q, 256), jnp.float32),
        scratch_shapes=[pltpu.VMEM((N_HEADS, max(tkc, tk), tq), jnp.float32)] * 2,
        compiler_params=_cparams(("parallel", "parallel")),
        name="attention",
    )(*args)


def _pool_kernel(prev_ref, cur_ref, next_ref, w_ref, sc_ref, o_ref, ext_ref, *, tm, n_tok):
    j = pl.program_id(1)
    cur = cur_ref[0]
    ext_ref[8:8 + tm, :] = cur
    ext_ref[0:8, :] = jnp.where(j > 0, prev_ref[0], 0.0)
    ext_ref[8 + tm:16 + tm, :] = jnp.where(j < pl.num_programs(1) - 1, next_ref[0], 0.0)

    def window_sums(c0, widths):
        sh = lambda k: ext_ref[8 + k:8 + k + tm, c0:c0 + LANE]
        acc = sh(-1) + sh(0)
        out = {2: acc}
        for w in (4, 8, 16):
            if w > max(widths):
                break
            for k in list(range(-w // 2, -w // 4)) + list(range(w // 4, w // 2)):
                acc = acc + sh(k)
            out[w] = acc
        return [out[w] for w in widths]

    lane = lax.broadcasted_iota(jnp.int32, (tm, LANE), 1)
    t = j * tm + lax.broadcasted_iota(jnp.int32, (tm, LANE), 0)
    halves = []
    for c0, (wa, wb) in ((0, (2, 4)), (LANE, (8, 16))):
        sa, sb = window_sums(c0, (wa, wb))
        first = lane < POOL_GC
        s = jnp.where(first, sa, sb)
        half = jnp.where(first, wa // 2, wb // 2)
        cnt = jnp.minimum(t + half, n_tok) - jnp.maximum(t - half, 0)
        halves.append(s / cnt.astype(jnp.float32) - cur[:, c0:c0 + LANE])
    p = jnp.concatenate(halves, axis=-1)
    o_ref[0] = _dot(_bf(p), w_ref[...]) * sc_ref[...]


def _pool(u, base, w_bd, scale, tm):
    b, t, _ = u.shape
    cidx = (COL["pool"] - base) // 256
    rb = tm // 8
    last = t // 8 - 1
    return pl.pallas_call(
        functools.partial(_pool_kernel, tm=tm, n_tok=t),
        grid=(b, t // tm),
        in_specs=[pl.BlockSpec((1, 8, 256), lambda i, j: (i, jnp.maximum(j * rb - 1, 0), cidx)),
                  pl.BlockSpec((1, tm, 256), lambda i, j: (i, j, cidx)),
                  pl.BlockSpec((1, 8, 256), lambda i, j: (i, jnp.minimum((j + 1) * rb, last), cidx)),
                  _full(w_bd, 2), _full(scale, 2)],
        out_specs=pl.BlockSpec((1, tm, 256), lambda i, j: (i, j, 0)),
        out_shape=jax.ShapeDtypeStruct((b, t, 256), jnp.float32),
        scratch_shapes=[pltpu.VMEM((tm + 16, 256), jnp.float32)],
        compiler_params=_cparams(("parallel", "parallel")),
        name="pool",
    )(u, u, u, w_bd, scale)


def _head_rms(x, e):
    return lax.rsqrt(_split_dot(x * x, e) * (1.0 / NA_HD) + EPS)


def _na_prep_kernel(q_ref, k_ref, v_ref, gq_ref, gk_ref, e_ref, qo_ref, ko_ref, vo_ref):
    e = e_ref[...]
    q = q_ref[0]
    qo_ref[0] = _bf(q * _head_rms(q, e) * gq_ref[...])
    k = k_ref[0]
    ko_ref[0] = _bf(k * _head_rms(k, e) * gk_ref[...])
    vo_ref[0] = _bf(v_ref[0])


def _na_prep(uq, qcol, u, base, gq, gk, e_mat, tm):
    b, t, _ = u.shape
    tok = pl.BlockSpec((1, tm, 256), lambda i, j: (i, j, 0))
    return pl.pallas_call(
        _na_prep_kernel,
        grid=(b, t // tm),
        in_specs=[pl.BlockSpec((1, tm, 256), lambda i, j: (i, j, qcol)),
                  _ublk(tm, COL["k_na"], base, 256), _ublk(tm, COL["v_na"], base, 256),
                  _full(gq, 2), _full(gk, 2), _full(e_mat, 2)],
        out_specs=[tok, tok, tok],
        out_shape=[jax.ShapeDtypeStruct((b, t, 256), MXU_DTYPE)] * 3,
        compiler_params=_cparams(("parallel", "parallel")),
        name="na_prep",
    )(uq, u, u, gq, gk, e_mat)


def _stack_heads(q):
    qf = q.astype(jnp.float32)
    head = lax.broadcasted_iota(jnp.int32, qf.shape, 1) // NA_HD
    return _bf(jnp.concatenate([jnp.where(head == h, qf, 0.0) for h in range(N_HEADS)], axis=0))


def _unstack_heads(res):
    n = res.shape[0] // N_HEADS
    head = lax.broadcasted_iota(jnp.int32, (n, res.shape[1]), 1) // NA_HD
    out = res[0:n]
    for h in range(1, N_HEADS):
        out = jnp.where(head == h, res[h * n:(h + 1) * n], out)
    return out


def _softmax_pv(scores, values):
    m = functools.reduce(jnp.maximum, [jnp.max(s, axis=-1, keepdims=True) for s in scores])
    ps = [jnp.exp(s - m) for s in scores]
    l = functools.reduce(jnp.add, [jnp.sum(p, axis=-1, keepdims=True) for p in ps])
    acc = functools.reduce(jnp.add, [_dot(_bf(p), v) for p, v in zip(ps, values)])
    return acc / l


def _na_kernel(q_ref, k_ref, v_ref, kc_ref, vc_ref, tbl_ref, o_ref, *, rows, rb):
    jb = pl.program_id(1)
    kc = kc_ref[0]
    vc = vc_ref[0]
    n_loc = NA_WIN_R * GRID_W
    staged = []
    for rr in range(rb):
        r = jb * rb + rr
        rs = jnp.clip(r - NA_WIN_R // 2, 0, rows - NA_WIN_R)
        start = pl.multiple_of(rs * GRID_W, GRID_W)
        qm = _stack_heads(q_ref[0, rr * GRID_W:(rr + 1) * GRID_W, :])
        s_loc = _dot_nt(qm, k_ref[0, pl.ds(start, n_loc), :]) + tbl_ref[r - rs]
        staged.append((start, s_loc, _dot_nt(qm, kc)))
    for rr, (start, s_loc, s_ctx) in enumerate(staged):
        res = _softmax_pv([s_loc, s_ctx], [v_ref[0, pl.ds(start, n_loc), :], vc])
        o_ref[0, rr * GRID_W:(rr + 1) * GRID_W, :] = _unstack_heads(res)


def _na_attention(q, k, v, kc, vc, tbls, layer, rb):
    b, t, _ = q.shape
    tc = kc.shape[1]
    rows = t // GRID_W
    assert rows >= NA_WIN_R and rows % rb == 0
    seq = pl.BlockSpec((1, t, 256), lambda i, j: (i, 0, 0))
    cseq = pl.BlockSpec((1, tc, 256), lambda i, j: (i, 0, 0))
    return pl.pallas_call(
        functools.partial(_na_kernel, rows=rows, rb=rb),
        grid=(b, rows // rb),
        in_specs=[pl.BlockSpec((1, rb * GRID_W, 256), lambda i, j: (i, j, 0)), seq, seq, cseq, cseq,
                  pl.BlockSpec((None,) + tbls.shape[1:], lambda i, j: (layer, 0, 0, 0))],
        out_specs=pl.BlockSpec((1, rb * GRID_W, 256), lambda i, j: (i, j, 0)),
        out_shape=jax.ShapeDtypeStruct((b, t, 256), jnp.float32),
        compiler_params=_cparams(("parallel", "parallel")),
        name="na_attention",
    )(q, k, v, kc, vc, tbls)


def _ctx_na_kernel(q_ref, kc_ref, vc_ref, o_ref):
    kc = kc_ref[0]
    vc = vc_ref[0]
    for rr in range(q_ref.shape[1] // GRID_W):
        qm = _stack_heads(q_ref[0, rr * GRID_W:(rr + 1) * GRID_W, :])
        res = _softmax_pv([_dot_nt(qm, kc)], [vc])
        o_ref[0, rr * GRID_W:(rr + 1) * GRID_W, :] = _unstack_heads(res)


def _ctx_na_attention(q, kc, vc):
    b, tc, _ = q.shape
    cseq = pl.BlockSpec((1, tc, 256), lambda i: (i, 0, 0))
    return pl.pallas_call(
        _ctx_na_kernel,
        grid=(b,),
        in_specs=[cseq, cseq, cseq],
        out_specs=cseq,
        out_shape=jax.ShapeDtypeStruct((b, tc, 256), jnp.float32),
        compiler_params=_cparams(("parallel",)),
        name="ctx_na_attention",
    )(q, kc, vc)


def _hg_gates(z, lb):
    e = jnp.exp(-jnp.abs(z))
    r = 1.0 / (1.0 + e)
    pos = z >= 0
    sig = jnp.where(pos, r, e * r)
    sig_neg = jnp.where(pos, e * r, r)
    return jnp.log(lb + (1.0 - lb) * sig), (1.0 - lb) * sig_neg


def _hg_prep_kernel(zf_ref, zb_ref, i_ref, q_ref, lbf_ref, lbb_ref, tl_ref, tu_ref, ta_ref, sel_ref, e_ref,
                    qf_o, kf_o, qb_o, kb_o, i_o, df_o, db_o, oi_o, nat_ref, pm_ref,
                    *, tm, qscale):
    nch = tm // HG_SUB
    q = q_ref[0] * qscale
    xi = i_ref[0]
    i_o[0] = _bf(xi)
    logf_f, kg_f = _hg_gates(zf_ref[0], lbf_ref[...])
    logf_b, kg_b = _hg_gates(zb_ref[0], lbb_ref[...])
    b_f = _split3_dot_left(tl_ref[...], logf_f)
    b_b = _split3_dot_left(tu_ref[...], logf_b)
    tot_f = _split3_dot_left(ta_ref[...], logf_f)
    tot_b = _split3_dot_left(ta_ref[...], logf_b)
    qf_o[0] = _bf(q * jnp.exp(b_f))
    kf_o[0] = _bf(kg_f * jnp.exp(tot_f - b_f))
    qb_o[0] = _bf(q * jnp.exp(b_b))
    kb_o[0] = _bf(kg_b * jnp.exp(tot_b - b_b))
    df_o[0] = jnp.exp(_split3_dot_left(sel_ref[...], logf_f))
    db_o[0] = jnp.exp(_split3_dot_left(sel_ref[...], logf_b))

    log2e = float(np.log2(np.e))
    for a, val in enumerate((q, kg_f, b_f * log2e, kg_b, b_b * log2e, xi)):
        for half in range(2):
            nat_ref[a, half] = val[:, half * LANE:(half + 1) * LANE]
            for p in range(HG_SUB):
                pm_ref[a, half, p * nch:(p + 1) * nch, :] = nat_ref[a, half, pl.ds(p, nch, stride=HG_SUB), :]
    e = e_ref[...]

    def side(q_t, b_t, k_ref, b_ref, i_ref, lo, nblk):
        rows = slice(lo, lo + nblk * nch)
        k3 = k_ref[rows, :].reshape(nblk, nch, LANE)
        b3 = b_ref[rows, :].reshape(nblk, nch, LANE)
        x = q_t[None] * k3 * jnp.exp2(b_t[None] - b3)
        r = _dot(_bf(x.reshape(nblk * nch, LANE)), e) * i_ref[rows, :]
        return jnp.sum(r.reshape(nblk, nch, LANE), axis=0)

    for half in range(2):
        qp, kfp, bfp, kbp, bbp, ip = [pm_ref.at[a, half] for a in range(6)]
        for pt in range(HG_SUB):
            rt = slice(pt * nch, (pt + 1) * nch)
            q_t = qp[rt, :]
            acc = _dot(_bf(q_t * (kfp[rt, :] + kbp[rt, :])), e) * ip[rt, :]
            if pt > 0:
                acc = acc + side(q_t, bfp[rt, :], kfp, bfp, ip, 0, pt)
            if pt < HG_SUB - 1:
                acc = acc + side(q_t, bbp[rt, :], kbp, bbp, ip, (pt + 1) * nch, HG_SUB - 1 - pt)
            nat_ref[0, half, pl.ds(pt, nch, stride=HG_SUB), :] = acc
        oi_o[0, :, half * LANE:(half + 1) * LANE] = nat_ref[0, half]


def _hg_prep(uq, qcol, u, base, lbf, lbb, e_mat):
    b, t, _ = u.shape
    tm = min(HG_PREP_TILE, t)
    nsub = tm // HG_SUB
    tok = pl.BlockSpec((1, tm, 256), lambda i, j: (i, j, 0))
    dec = pl.BlockSpec((1, nsub, 256), lambda i, j: (i, j, 0))
    tl, tu, ta, sel = _hg_consts(tm)
    e_half = e_mat[:LANE, :LANE]
    regroup = pltpu.VMEM((6, 2, tm, LANE), jnp.float32)
    return pl.pallas_call(
        functools.partial(_hg_prep_kernel, tm=tm, qscale=HG_DK ** -0.5),
        grid=(b, t // tm),
        in_specs=[_ublk(tm, COL["zF"], base, 256), _ublk(tm, COL["zB"], base, 256),
                  _ublk(tm, COL["i_hg"], base, 256),
                  pl.BlockSpec((1, tm, 256), lambda i, j: (i, j, qcol)),
                  _full(lbf, 2), _full(lbb, 2), _full(tl, 2), _full(tu, 2), _full(ta, 2), _full(sel, 2),
                  _full(e_half, 2)],
        out_specs=[tok, tok, tok, tok, tok, dec, dec, tok],
        out_shape=[jax.ShapeDtypeStruct((b, t, 256), MXU_DTYPE)] * 5
                  + [jax.ShapeDtypeStruct((b, t // HG_SUB, 256), jnp.float32)] * 2
                  + [jax.ShapeDtypeStruct((b, t, 256), jnp.float32)],
        scratch_shapes=[regroup, regroup],
        compiler_params=_cparams(("parallel", "parallel")),
        name="hg_prep",
    )(u, u, u, uq, lbf, lbb, tl, tu, ta, sel, e_half)


def _hg_scan_kernel(qf_ref, kf_ref, if_ref, df_ref, qb_ref, kb_ref, ib_ref, db_ref, s0f_ref, s0b_ref,
                    of_ref, ob_ref, sf_out, sb_out, sf, sb, *, nsub, nb):
    j = pl.program_id(0)

    @pl.when(j == 0)
    def _():
        sf[...] = s0f_ref[...]
        sb[...] = s0b_ref[...]

    first = lax.broadcasted_iota(jnp.int32, (HG_SUB, LANE), 1) < HG_DK

    def step(state, q_ref, k_ref, i_ref, d_ref, o_ref, bi, c):
        rows = slice(c * HG_SUB, (c + 1) * HG_SUB)
        for p in range(2):
            cols = slice(p * LANE, (p + 1) * LANE)
            s_old = state[bi, p]
            q = q_ref[bi, rows, cols].astype(jnp.float32)
            q2 = _bf(jnp.concatenate([jnp.where(first, q, 0.0), jnp.where(first, 0.0, q)], axis=0))
            r = _dot_nt(q2, _bf(s_old))
            o_ref[bi, rows, cols] = jnp.where(first, r[:HG_SUB], r[HG_SUB:])
            upd = _dot_tn(i_ref[bi, rows, cols], k_ref[bi, rows, cols])
            state[bi, p] = s_old * d_ref[bi, c:c + 1, cols] + upd

    for c in range(nsub):
        for bi in range(nb):
            step(sf, qf_ref, kf_ref, if_ref, df_ref, of_ref, bi, c)
            step(sb, qb_ref, kb_ref, ib_ref, db_ref, ob_ref, bi, nsub - 1 - c)

    @pl.when(j == pl.num_programs(0) - 1)
    def _():
        sf_out[...] = sf[...]
        sb_out[...] = sb[...]


def _hg_scan(qf, kf, qb, kb, xi, df, db, s0f, s0b):
    b, t, _ = qf.shape
    tm = HG_TILE
    n = t // tm
    nsub = tm // HG_SUB
    fwd = pl.BlockSpec((b, tm, 256), lambda j: (0, j, 0))
    bwd = pl.BlockSpec((b, tm, 256), lambda j: (0, n - 1 - j, 0))
    dfwd = pl.BlockSpec((b, nsub, 256), lambda j: (0, j, 0))
    dbwd = pl.BlockSpec((b, nsub, 256), lambda j: (0, n - 1 - j, 0))
    st = pl.BlockSpec((b, 2, LANE, LANE), lambda j: (0, 0, 0, 0))
    return pl.pallas_call(
        functools.partial(_hg_scan_kernel, nsub=nsub, nb=b),
        grid=(n,),
        in_specs=[fwd, fwd, fwd, dfwd, bwd, bwd, bwd, dbwd, st, st],
        out_specs=[fwd, bwd, st, st],
        out_shape=[jax.ShapeDtypeStruct((b, t, 256), jnp.float32)] * 2
                  + [jax.ShapeDtypeStruct((b, 2, LANE, LANE), jnp.float32)] * 2,
        scratch_shapes=[pltpu.VMEM((b, 2, LANE, LANE), jnp.float32)] * 2,
        compiler_params=_cparams(("arbitrary",)),
        name="hg_scan",
    )(qf, kf, xi, df, qb, kb, xi, db, s0f, s0b)


def _merge_kernel(x_ref, a_ref, b_ref, c_ref, of_ref, ob_ref, oi_ref, gc_ref, g_ref, sh_ref, sc_ref, wg_ref,
                  hgn_ref, e_ref, wbr_ref, wo_ref, gate_ref, out_ref):
    d_model = x_ref.shape[2]
    x = x_ref[0]
    h = _bf((x * _rms_scale(x) * g_ref[...]) * (1.0 + sc_ref[0]) + sh_ref[0])
    o = of_ref[0] + ob_ref[0] + oi_ref[0]
    ms = _split_dot(o * o, e_ref[...]) * (1.0 / HG_DK)
    d = o * lax.rsqrt(ms + EPS) * hgn_ref[...] * _sigmoid(gc_ref[0])
    acc = None
    for n, y in enumerate((a_ref[0], b_ref[0], c_ref[0], d)):
        gate = _sigmoid(_dot(h, wg_ref[:, n * d_model:(n + 1) * d_model]))
        term = gate * _dot(_bf(y), wbr_ref[n])
        acc = term if acc is None else acc + term
    out_ref[0] = x + gate_ref[0] * _dot(_bf(acc), wo_ref[...])


def _merge(x, a, bm, cn, o_f, o_b, o_i, u, g1n, w_gate, hgn, e_mat, w_br, w_o, layer, mods, tm):
    b, t, d = x.shape
    tok = pl.BlockSpec((1, tm, 256), lambda i, j: (i, j, 0))
    xs = pl.BlockSpec((1, tm, d), lambda i, j: (i, j, 0))
    of_layer = lambda a: pl.BlockSpec((None,) + a.shape[1:], lambda i, j: (layer,) + (0,) * (a.ndim - 1))
    return pl.pallas_call(
        _merge_kernel,
        grid=(b, t // tm),
        in_specs=[xs, tok, tok, tok, tok, tok, tok, _ublk(tm, COL["g_cols"], 0, 256),
                  _full(g1n, 2), _mod_spec(mods, 0, d), _mod_spec(mods, 1, d), of_layer(w_gate),
                  _full(hgn, 2), _full(e_mat, 2), of_layer(w_br), of_layer(w_o), _mod_spec(mods, 2, d)],
        out_specs=xs,
        out_shape=jax.ShapeDtypeStruct((b, t, d), jnp.float32),
        compiler_params=_cparams(("parallel", "parallel")),
        name="merge",
    )(x, a, bm, cn, o_f, o_b, o_i, u, g1n, mods, mods, w_gate, hgn, e_mat, w_br, w_o, mods)


def _ffn_kernel(x_ref, g_ref, sh_ref, sc_ref, w1_ref, w2_ref, gate_ref, o_ref, *, tf):
    x = x_ref[0]
    h = _bf((x * _rms_scale(x) * g_ref[...]) * (1.0 + sc_ref[0]) + sh_ref[0])
    acc = None
    for k in range(w1_ref.shape[1] // tf):
        z = jnp.maximum(_dot(h, w1_ref[:, k * tf:(k + 1) * tf]), 0.0)
        part = _dot(_bf(z * z), w2_ref[k * tf:(k + 1) * tf, :])
        acc = part if acc is None else acc + part
    o_ref[0] = x + gate_ref[0] * acc


def _ffn(x, g, mods, w1, w2, layer, tm, tf):
    b, t, d = x.shape
    xs = pl.BlockSpec((1, tm, d), lambda i, j: (i, j, 0))
    resident = lambda a: pl.BlockSpec((None,) + a.shape[1:], lambda i, j: (layer, 0, 0),
                                      pipeline_mode=pl.Buffered(1))
    return pl.pallas_call(
        functools.partial(_ffn_kernel, tf=tf),
        grid=(b, t // tm),
        in_specs=[xs, pl.BlockSpec((1, d), lambda i, j: (0, 0)), _mod_spec(mods, 3, d), _mod_spec(mods, 4, d),
                  resident(w1), resident(w2), _mod_spec(mods, 5, d)],
        out_specs=xs,
        out_shape=jax.ShapeDtypeStruct((b, t, d), jnp.float32),
        compiler_params=_cparams(("parallel", "parallel")),
        name="ffn",
    )(x, g, mods, mods, w1, w2, mods)


def _rope_perm():
    return np.array([(t // 16) * 16 + ((t % 16) + 8) % 16 for t in range(MLA_ROPE)])


W_IN_COLS = dict(ckv=0, kr=128, k_na=160, v_na=416, zF=672, zB=928, i_hg=1184, pool=1440, qa=1696, q_na=1952,
                 q_hg=2208, g_cols=2464, gates=2720)


def _w_in_relayout_kernel(w_ref, gate_ref, mix_ref):
    w = w_ref[0]
    tr = w.shape[0]
    o = W_IN_COLS
    gate_ref[0] = _bf(w[:, o["gates"]:])
    z = lambda n: jnp.zeros((tr, n), jnp.float32)
    kr = w[:, o["kr"]:o["kr"] + MLA_ROPE]
    quarter = MLA_ROPE // 4
    kr_sw = jnp.concatenate([kr[:, quarter:2 * quarter], kr[:, :quarter],
                             kr[:, 3 * quarter:], kr[:, 2 * quarter:3 * quarter]], axis=1)
    parts = [w[:, o[name]:o[name] + 256] for name in ("k_na", "v_na", "zF", "zB", "i_hg")]
    parts += [w[:, :MLA_KV_RANK], z(MLA_NOPE), kr, kr_sw]
    parts += [w[:, o[name]:o[name] + 256] for name in ("pool", "qa", "q_na", "q_hg", "g_cols")]
    mix_ref[0] = _bf(jnp.concatenate(parts, axis=1))


def _prep_w_in(w):
    depth, d, n = w.shape
    tr = 128
    n_gate = n - W_IN_COLS["gates"]
    return pl.pallas_call(
        _w_in_relayout_kernel,
        grid=(depth, d // tr),
        in_specs=[pl.BlockSpec((1, tr, n), lambda l, i: (l, i, 0))],
        out_specs=[pl.BlockSpec((1, tr, n_gate), lambda l, i: (l, i, 0)),
                   pl.BlockSpec((1, tr, D_MIX_PAD), lambda l, i: (l, i, 0))],
        out_shape=[jax.ShapeDtypeStruct((depth, d, n_gate), MXU_DTYPE),
                   jax.ShapeDtypeStruct((depth, d, D_MIX_PAD), MXU_DTYPE)],
        compiler_params=_cparams(("parallel", "parallel")),
        name="w_in_relayout",
    )(w)


def _prep_wq(w):
    r = w.shape[0]
    wh = w.reshape(r, N_HEADS, MLA_QK)
    pre = jnp.pad(wh, ((0, 0), (0, 0), (0, HEAD_PAD - MLA_QK)))
    sw = jnp.pad(wh[:, :, MLA_NOPE:][:, :, _rope_perm()], ((0, 0), (0, 0), (MLA_NOPE, HEAD_PAD - MLA_QK)))
    return _bf(jnp.concatenate([pre.reshape(r, -1), sw.reshape(r, -1)], axis=1))


def _prep_wkv(w):
    r = w.shape[0]
    wh = w.reshape(r, N_HEADS, MLA_NOPE + MLA_V)
    kpart = jnp.pad(wh[:, :, :MLA_NOPE], ((0, 0), (0, 0), (0, HEAD_PAD - MLA_NOPE)))
    return _bf(jnp.concatenate([kpart.reshape(r, -1), wh[:, :, MLA_NOPE:].reshape(r, -1)], axis=1))


def _pad_gain(g):
    plain = jnp.pad(g, (0, HEAD_PAD - MLA_QK))[None]
    sw = jnp.pad(g[MLA_NOPE:][_rope_perm()], (MLA_NOPE, HEAD_PAD - MLA_QK))[None]
    return plain, sw


def _rope_tables(n_tok):
    t = jnp.arange(n_tok)
    pos = jnp.stack([t // GRID_W, t % GRID_W], axis=-1).astype(jnp.float32)
    nf = MLA_ROPE // 4
    inv = ROPE_BASE ** (-jnp.arange(nf, dtype=jnp.float32) / nf)
    ang = pos[:, :, None] * inv
    cos, sin = jnp.cos(ang), jnp.sin(ang)
    cos_t = jnp.concatenate([cos, cos], axis=-1).reshape(n_tok, MLA_ROPE)
    sin_t = jnp.concatenate([-sin, sin], axis=-1).reshape(n_tok, MLA_ROPE)
    ones = jnp.ones((n_tok, MLA_NOPE), jnp.float32)
    zer = jnp.zeros((n_tok, HEAD_PAD - MLA_QK), jnp.float32)
    cos_p = jnp.concatenate([ones, cos_t, zer], axis=-1)
    sin_p = jnp.concatenate([0 * ones, sin_t, zer], axis=-1)
    return cos_p, sin_p


def _plain_tables(n_tok):
    cos_p = jnp.concatenate([jnp.ones((n_tok, MLA_QK), jnp.float32),
                             jnp.zeros((n_tok, HEAD_PAD - MLA_QK), jnp.float32)], axis=-1)
    return cos_p, jnp.zeros((n_tok, HEAD_PAD), jnp.float32)


def _head_indicator(dtype):
    i = np.arange(256) // 64
    return jnp.asarray((i[:, None] == i[None, :]).astype(np.float32), dtype=dtype)


def _na_bias_tables(rpb):
    qc = np.arange(GRID_W)[:, None]
    kc = np.arange(GRID_W)[None, :]
    cs = np.clip(qc - NA_WIN_C // 2, 0, GRID_W - NA_WIN_C)
    inwin = (kc >= cs) & (kc < cs + NA_WIN_C)
    dc = np.clip(kc - qc + NA_WIN_C - 1, 0, 2 * NA_WIN_C - 2)
    col_hot = (dc[:, :, None] == np.arange(2 * NA_WIN_C - 1)).astype(np.float32)
    p = np.arange(NA_WIN_R)[:, None, None]
    j = np.arange(NA_WIN_R)[None, :, None]
    row_hot = (np.arange(2 * NA_WIN_R - 1)[None, None, :] == j + NA_WIN_R - 1 - p).astype(np.float32)
    t = jnp.einsum('lhrd,pjr,qkd->lphqjk', rpb.astype(jnp.float32), row_hot, col_hot,
                   precision=lax.Precision.HIGHEST)
    t = jnp.where(inwin[None, None, None, :, None, :], t, NEG)
    return t.reshape(rpb.shape[0], NA_WIN_R, N_HEADS * GRID_W, NA_WIN_R * GRID_W)


def _hg_consts(tm):
    t = np.arange(tm)
    same = (t[:, None] // HG_SUB) == (t[None, :] // HG_SUB)
    tl = same & (t[None, :] <= t[:, None])
    tu = same & (t[None, :] >= t[:, None])
    sel = (np.arange(tm // HG_SUB)[:, None] == (t[None, :] // HG_SUB))
    f = lambda m: jnp.asarray(m.astype(np.float32), dtype=MXU_DTYPE)
    return f(tl), f(tu), f(same), f(sel)


def _block_diag(w):
    g, c, d = w.shape
    out = jnp.zeros((g * c, g * d), w.dtype)
    for i in range(g):
        out = out.at[i * c:(i + 1) * c, i * d:(i + 1) * d].set(w[i])
    return out


def kernel(x, c, ctx, c_ctx, ada_w, ada_b, norm1, norm2, w_in, pool_w, pool_scale, mla_q_norm, mla_wq_b, mla_kv_norm, mla_wkv_b, mla_gq, mla_gk, na_gq, na_gk, na_rpb, hg_lb, hg_norm, w_branch, w_out, w_ff1, w_ff2):
    B, L, D = x.shape
    Tc = ctx.shape[1]
    depth = ada_w.shape[0]
    f32 = jnp.float32
    assert L % HG_TILE == 0 and Tc % HG_TILE == 0 and B < 8

    lb_p = jax.nn.softmax(hg_lb.astype(f32), axis=0)
    lb_all = jnp.clip(jnp.cumsum(lb_p, axis=0)[:depth], LB_EPS, 1.0 - LB_EPS)

    cc = jnp.concatenate([c, c_ctx[None], jnp.zeros((8 - B - 1, D), f32)], axis=0)
    mods = _mods(cc, ada_w, ada_b)

    cos_l, sin_l = _rope_tables(L)
    cos_c, sin_c = _plain_tables(Tc)
    e_mat = _head_indicator(MXU_DTYPE)
    zero_state = jnp.zeros((B, 2, LANE, LANE), f32)
    zero_q = jnp.zeros((B, Tc, 256), f32)
    tl = min(512, L)
    t_big = min(1024, L)

    na_tbls = _na_bias_tables(na_rpb)
    w_gate, w_mix = _prep_w_in(w_in)
    w_br = _bf(w_branch)
    w_o = _bf(w_out)
    w1 = _bf(w_ff1)
    w2 = _bf(w_ff2)
    xl, xc = x, ctx
    for l in range(depth):
        ctx_out = l < depth - 1
        mods_l = mods[l, :B].reshape(B, 1, 6 * D)
        mods_c = mods[l, B].reshape(1, 1, 6 * D)
        g1n = norm1[l][None]
        g2n = norm2[l][None]
        wq = _prep_wq(mla_wq_b[l])
        wkv = _prep_wkv(mla_wkv_b[l])
        gq, gqsw = _pad_gain(mla_gq[l])
        gk, gksw = _pad_gain(mla_gk[l])
        qn = mla_q_norm[l][None]
        kvn = mla_kv_norm[l][None]
        na_q_gain = (jnp.tile(na_gq[l], N_HEADS) * NA_HD ** -0.5)[None]
        na_k_gain = jnp.tile(na_gk[l], N_HEADS)[None]
        pool_bd = _bf(_block_diag(pool_w[l]))
        pool_sc = pool_scale[l][None]
        hgn = jnp.tile(hg_norm[l], N_HEADS)[None]
        lbF = lb_all[l, 0][None]
        lbB = lb_all[l, 1][None]

        ul = _inproj(xl, g1n, mods_l, w_mix, l, 0, D_MIX_PAD, tl, D_MIX_PAD, f32)
        if ctx_out:
            uc = _inproj(xc, g1n, mods_c, w_mix, l, 0, D_MIX_PAD, Tc, MIX_TILE, f32)
        else:
            uc = _inproj(xc, g1n, mods_c, w_mix, l, 0, D_KV, Tc, D_KV, f32)
        cbase = 0

        q_mla, k_mla, v_mla = _mla_prep(ul, 0, cos_l, sin_l, qn, wq, kvn, wkv, gq, gqsw, gk, gksw, t_big)
        if ctx_out:
            q_mla_c, k_mla_c, v_mla_c = _mla_prep(uc, cbase, cos_c, sin_c, qn, wq, kvn, wkv, gq, gqsw, gk, gksw, Tc)
        else:
            k_mla_c, v_mla_c = _mla_kv_prep(uc, cbase, kvn, wkv, gk, Tc)
        b_l = _attention(q_mla, k_mla_c, v_mla_c, k_mla, v_mla, tl, Tc, tl)

        q_na, k_na, v_na = _na_prep(ul, COL["q_na"] // 256, ul, 0, na_q_gain, na_k_gain, e_mat, t_big)
        if ctx_out:
            q_na_c, k_na_c, v_na_c = _na_prep(uc, COL["q_na"] // 256, uc, cbase, na_q_gain, na_k_gain, e_mat, Tc)
        else:
            _, k_na_c, v_na_c = _na_prep(zero_q, 0, uc, cbase, na_q_gain, na_k_gain, e_mat, Tc)
        c_l = _na_attention(q_na, k_na, v_na, k_na_c, v_na_c, na_tbls, l, min(16, L // GRID_W))

        a_l = _pool(ul, 0, pool_bd, pool_sc, t_big)

        if ctx_out:
            hc = _hg_prep(uc, COL["q_hg"] // 256, uc, cbase, lbF, lbB, e_mat)
        else:
            hc = _hg_prep(zero_q, 0, uc, cbase, lbF, lbB, e_mat)
        oFc, oBc, sFc, sBc = _hg_scan(hc[0], hc[1], hc[2], hc[3], hc[4], hc[5], hc[6], zero_state, zero_state)
        hl = _hg_prep(ul, COL["q_hg"] // 256, ul, 0, lbF, lbB, e_mat)
        oFl, oBl, _, _ = _hg_scan(hl[0], hl[1], hl[2], hl[3], hl[4], hl[5], hl[6], sFc, sBc)

        xl_new = _merge(xl, a_l, b_l, c_l, oFl, oBl, hl[7], ul, g1n, w_gate, hgn, e_mat, w_br, w_o, l, mods_l, tl)

        if ctx_out:
            a_c = _pool(uc, cbase, pool_bd, pool_sc, Tc)
            b_c = _attention(q_mla_c, k_mla_c, v_mla_c, None, None, Tc, Tc, Tc)
            c_c = _ctx_na_attention(q_na_c, k_na_c, v_na_c)
            xc = _merge(xc, a_c, b_c, c_c, oFc, oBc, hc[7], uc, g1n, w_gate, hgn, e_mat, w_br, w_o, l, mods_c, Tc)
            xc = _ffn(xc, g2n, mods_c, w1, w2, l, Tc, 1024)

        xl = _ffn(xl_new, g2n, mods_l, w1, w2, l, tl, 2048)
    return xl
```

```python
import functools

import jax
import jax.numpy as jnp
import numpy as np
from jax import lax
from jax.experimental import pallas as pl
from jax.experimental.pallas import tpu as pltpu

GRID_W = 64
EPS = 1e-6
POOL_GC = 64
N_HEADS = 4
MLA_KV_RANK = 128
MLA_NOPE = 64
MLA_ROPE = 32
MLA_V = 64
MLA_QK = 96
ROPE_BASE = 10000.0
NA_HD = 64
NA_WIN_R = 8
NA_WIN_C = 16
HG_DK = 64
LB_EPS = 1e-6

MXU_DTYPE = jnp.bfloat16
LANE = 128
HEAD_PAD = 128
ONES_ROWS = 16
VMEM_LIMIT = 48 * 1024 * 1024
HG_SUB = 16
HG_TILE = 256
HG_PREP_TILE = 256
NEG = -1e30

COL = dict(k_na=0, v_na=256, zF=512, zB=768, i_hg=1024, ckv=1280, kr=1408,
           pool=1536, qa=1792, q_na=2048, q_hg=2304, g_cols=2560)
D_MIX_PAD = 2816
D_KV = 1536
MIX_TILE = 1408


def _cparams(sem):
    return pltpu.CompilerParams(dimension_semantics=sem, vmem_limit_bytes=VMEM_LIMIT)


def _bf(x):
    return x.astype(MXU_DTYPE)


def _dot(a, b):
    return jnp.dot(a, b, preferred_element_type=jnp.float32)


def _dot_nt(a, b):
    return lax.dot_general(a, b, (((1,), (1,)), ((), ())), preferred_element_type=jnp.float32)


def _dot_tn(a, b):
    return lax.dot_general(a, b, (((0,), (0,)), ((), ())), preferred_element_type=jnp.float32)


def _split2(x):
    hi = _bf(x)
    return hi, _bf(x - hi.astype(jnp.float32))


def _split_dot(x, w):
    hi, lo = _split2(x)
    return _dot(hi, w) + _dot(lo, w)


def _split3_dot_left(w, x):
    hi = _bf(x)
    r1 = x - hi.astype(jnp.float32)
    mid = _bf(r1)
    lo = _bf(r1 - mid.astype(jnp.float32))
    return _dot(w, hi) + _dot(w, mid) + _dot(w, lo)


def _sigmoid(x):
    return 0.5 * jnp.tanh(0.5 * x) + 0.5


def _rms_scale(x):
    return lax.rsqrt(jnp.mean(x * x, axis=-1, keepdims=True) + EPS)


def _full(arr, nargs):
    zeros = (0,) * arr.ndim
    return pl.BlockSpec(arr.shape, lambda *_: zeros)


def _ublk(tm, col, base, width):
    idx = (col - base) // width
    return pl.BlockSpec((1, tm, width), lambda i, j: (i, j, idx))


def _mods_kernel(c_ref, w_ref, b_ref, o_ref):
    c = c_ref[...]
    s = _bf(c * _sigmoid(c))
    o_ref[0] = _dot(s, _bf(w_ref[0])) + b_ref[0]


def _mods(cc, ada_w, ada_b):
    depth, d, n = ada_w.shape
    tn = 1536
    return pl.pallas_call(
        _mods_kernel,
        grid=(depth, n // tn),
        in_specs=[pl.BlockSpec((8, d), lambda l, j: (0, 0)),
                  pl.BlockSpec((1, d, tn), lambda l, j: (l, 0, j)),
                  pl.BlockSpec((1, 1, tn), lambda l, j: (l, 0, j))],
        out_specs=pl.BlockSpec((1, 8, tn), lambda l, j: (l, 0, j)),
        out_shape=jax.ShapeDtypeStruct((depth, 8, n), jnp.float32),
        compiler_params=_cparams(("arbitrary", "arbitrary")),
        name="mods",
    )(cc, ada_w, ada_b.reshape(depth, 1, n))


def _inproj_kernel(x_ref, g_ref, sh_ref, sc_ref, w_ref, o_ref, h_ref):
    @pl.when(pl.program_id(2) == 0)
    def _():
        x = x_ref[0]
        y = x * _rms_scale(x) * g_ref[...]
        h_ref[...] = _bf(y * (1.0 + sc_ref[0]) + sh_ref[0])

    o_ref[0] = _dot(h_ref[...], w_ref[...]).astype(o_ref.dtype)


def _mod_spec(mods, idx, d):
    if mods.shape[0] == 1:
        return pl.BlockSpec((1, 1, d), lambda i, *_: (0, 0, idx))
    return pl.BlockSpec((1, 1, d), lambda i, *_: (i, 0, idx))


def _inproj(x, g, mods, w, layer, col0, n, tm, tn, out_dtype):
    b, t, d = x.shape
    assert n % tn == 0 and col0 % tn == 0
    return pl.pallas_call(
        _inproj_kernel,
        grid=(b, t // tm, n // tn),
        in_specs=[pl.BlockSpec((1, tm, d), lambda i, j, k: (i, j, 0)),
                  pl.BlockSpec((1, d), lambda i, j, k: (0, 0)),
                  _mod_spec(mods, 0, d), _mod_spec(mods, 1, d),
                  pl.BlockSpec((None, d, tn), lambda i, j, k: (layer, 0, k + col0 // tn))],
        out_specs=pl.BlockSpec((1, tm, tn), lambda i, j, k: (i, j, k)),
        out_shape=jax.ShapeDtypeStruct((b, t, n), out_dtype),
        scratch_shapes=[pltpu.VMEM((tm, d), MXU_DTYPE)],
        compiler_params=_cparams(("parallel", "parallel", "arbitrary")),
        name="inproj",
    )(x, g, mods, mods, w)


def _k_rope_only(krp):
    lane = lax.broadcasted_iota(jnp.int32, krp.shape, 1)
    return jnp.where(lane < MLA_QK, krp, 0.0)


def _mla_prep_kernel(qa_ref, ckv_ref, kr_ref, cos_ref, sin_ref, qn_ref, wq_ref, kvn_ref, wkv_ref,
                     gq_ref, gqsw_ref, gk_ref, gksw_ref, q_ref, k_ref, vt_ref, *, scale):
    cos = cos_ref[...]
    sin = sin_ref[...]
    inv_w = 1.0 / MLA_QK
    qa = qa_ref[0]
    qq = _dot(_bf(qa * _rms_scale(qa) * qn_ref[...]), wq_ref[...])
    ckv = ckv_ref[0]
    kv = _dot(_bf(ckv * _rms_scale(ckv) * kvn_ref[...]), wkv_ref[...])
    krp = kr_ref[0]
    kr = _k_rope_only(krp)
    kr_rot = pltpu.roll(krp, HEAD_PAD - MLA_ROPE, axis=1) * gksw_ref[...] * sin
    gq_cos = gq_ref[...] * cos
    gq_sin = gqsw_ref[...] * sin
    gk_cos = gk_ref[...] * cos
    for h in range(N_HEADS):
        qp = qq[:, h * HEAD_PAD:(h + 1) * HEAD_PAD]
        qs = qq[:, (N_HEADS + h) * HEAD_PAD:(N_HEADS + h + 1) * HEAD_PAD]
        rs = lax.rsqrt(jnp.sum(qp * qp, axis=-1, keepdims=True) * inv_w + EPS)
        q_ref[0, h] = _bf((qp * gq_cos + qs * gq_sin) * (rs * scale))
        kp = kv[:, h * HEAD_PAD:(h + 1) * HEAD_PAD] + kr
        rs = lax.rsqrt(jnp.sum(kp * kp, axis=-1, keepdims=True) * inv_w + EPS)
        k_ref[0, h] = _bf((kp * gk_cos + kr_rot) * rs)
    vt_ref[0] = _bf(kv[:, N_HEADS * HEAD_PAD:].T)


def _mla_prep(u, base, cos, sin, qn, wq, kvn, wkv, gq, gqsw, gk, gksw, tm):
    b, t, _ = u.shape
    tab = pl.BlockSpec((tm, HEAD_PAD), lambda i, j: (j, 0))
    hd = pl.BlockSpec((1, N_HEADS, tm, HEAD_PAD), lambda i, j: (i, 0, j, 0))
    consts = (qn, wq, kvn, wkv, gq, gqsw, gk, gksw)
    return pl.pallas_call(
        functools.partial(_mla_prep_kernel, scale=MLA_QK ** -0.5 * float(np.log2(np.e))),
        grid=(b, t // tm),
        in_specs=[_ublk(tm, COL["qa"], base, 256), _ublk(tm, COL["ckv"], base, 128),
                  _ublk(tm, COL["kr"], base, 128), tab, tab]
                 + [_full(a, 2) for a in consts],
        out_specs=[hd, hd, pl.BlockSpec((1, 256, tm), lambda i, j: (i, 0, j))],
        out_shape=[jax.ShapeDtypeStruct((b, N_HEADS, t, HEAD_PAD), MXU_DTYPE),
                   jax.ShapeDtypeStruct((b, N_HEADS, t, HEAD_PAD), MXU_DTYPE),
                   jax.ShapeDtypeStruct((b, 256, t), MXU_DTYPE)],
        compiler_params=_cparams(("parallel", "parallel")),
        name="mla_prep",
    )(u, u, u, cos, sin, *consts)


def _mla_kv_prep_kernel(ckv_ref, kr_ref, kvn_ref, wkv_ref, gk_ref, k_ref, vt_ref):
    inv_w = 1.0 / MLA_QK
    ckv = ckv_ref[0]
    kv = _dot(_bf(ckv * _rms_scale(ckv) * kvn_ref[...]), wkv_ref[...])
    kr = _k_rope_only(kr_ref[0])
    for h in range(N_HEADS):
        kp = kv[:, h * HEAD_PAD:(h + 1) * HEAD_PAD] + kr
        rs = lax.rsqrt(jnp.sum(kp * kp, axis=-1, keepdims=True) * inv_w + EPS)
        k_ref[0, h] = _bf(kp * gk_ref[...] * rs)
    vt_ref[0] = _bf(kv[:, N_HEADS * HEAD_PAD:].T)


def _mla_kv_prep(u, base, kvn, wkv, gk, tm):
    b, t, _ = u.shape
    hd = pl.BlockSpec((1, N_HEADS, tm, HEAD_PAD), lambda i, j: (i, 0, j, 0))
    return pl.pallas_call(
        _mla_kv_prep_kernel,
        grid=(b, t // tm),
        in_specs=[_ublk(tm, COL["ckv"], base, 128), _ublk(tm, COL["kr"], base, 128),
                  _full(kvn, 2), _full(wkv, 2), _full(gk, 2)],
        out_specs=[hd, pl.BlockSpec((1, 256, tm), lambda i, j: (i, 0, j))],
        out_shape=[jax.ShapeDtypeStruct((b, N_HEADS, t, HEAD_PAD), MXU_DTYPE),
                   jax.ShapeDtypeStruct((b, 256, t), MXU_DTYPE)],
        compiler_params=_cparams(("parallel", "parallel")),
        name="mla_kv_prep",
    )(u, u, kvn, wkv, gk)


def _attn_kernel(*refs, tkc, tk, n_ctx, n_lat):
    if n_lat:
        q_ref, kc_ref, vtc_ref, kl_ref, vtl_ref, o_ref, sa_ref, sb_ref = refs
    else:
        q_ref, kc_ref, vtc_ref, o_ref, sa_ref, sb_ref = refs
    tq = q_ref.shape[2]
    qs = [q_ref[0, h] for h in range(N_HEADS)]

    def produce(k_ref, c, s_ref, n):
        start = pl.multiple_of(c * n, n)
        for h in range(N_HEADS):
            s_ref[h, :n] = _dot_nt(k_ref[0, h, pl.ds(start, n), :], qs[h])

    def consume(vt_ref, c, s_ref, carry, n):
        start = pl.multiple_of(c * n, n)
        ones = jnp.ones((ONES_ROWS, n), MXU_DTYPE)
        new = []
        for h in range(N_HEADS):
            m, acc = carry[h]
            s = s_ref[h, :n]
            m_new = jnp.maximum(m, jnp.max(s, axis=0, keepdims=True))
            alpha = jnp.exp2(m - m_new)
            p = _bf(jnp.exp2(s - m_new))
            vt = jnp.concatenate([vt_ref[0, h * MLA_V:(h + 1) * MLA_V, pl.ds(start, n)], ones], axis=0)
            new.append((m_new, alpha * acc + _dot(vt, p)))
        return tuple(new)

    carry = tuple((jnp.full((1, tq), -jnp.inf, jnp.float32), jnp.zeros((MLA_V + ONES_ROWS, tq), jnp.float32))
                  for _ in range(N_HEADS))
    cur, nxt = sa_ref, sb_ref
    produce(kc_ref, 0, cur, tkc)
    for c in range(n_ctx):
        if c + 1 < n_ctx:
            produce(kc_ref, c + 1, nxt, tkc)
        elif n_lat:
            produce(kl_ref, 0, nxt, tk)
        carry = consume(vtc_ref, c, cur, carry, tkc)
        cur, nxt = nxt, cur
    if n_lat:
        def body(i, carry):
            c = 2 * i
            produce(kl_ref, c + 1, nxt, tk)
            carry = consume(vtl_ref, c, cur, carry, tk)
            produce(kl_ref, c + 2, cur, tk)
            return consume(vtl_ref, c + 1, nxt, carry, tk)

        n_pairs = (n_lat - 1) // 2
        carry = lax.fori_loop(0, n_pairs, body, carry)
        c = 2 * n_pairs
        if n_lat - c == 2:
            produce(kl_ref, c + 1, nxt, tk)
            carry = consume(vtl_ref, c, cur, carry, tk)
            carry = consume(vtl_ref, c + 1, nxt, carry, tk)
        else:
            carry = consume(vtl_ref, c, cur, carry, tk)
    o_t = jnp.concatenate([acc[:MLA_V] / acc[MLA_V:MLA_V + 1] for _, acc in carry], axis=0)
    o_ref[0] = _bf(o_t.T)


def _attention(q, kc, vtc, kl, vtl, tq, tkc, tk):
    b, _, t_q, _ = q.shape
    t_c = kc.shape[2]
    t_l = 0 if kl is None else kl.shape[2]
    assert t_q % tq == 0 and t_c % tkc == 0 and t_l % tk == 0
    kv_specs = [pl.BlockSpec((1, N_HEADS, t_c, HEAD_PAD), lambda i, j: (i, 0, 0, 0)),
                pl.BlockSpec((1, 256, t_c), lambda i, j: (i, 0, 0))]
    args = [q, kc, vtc]
    if t_l:
        kv_specs += [pl.BlockSpec((1, N_HEADS, t_l, HEAD_PAD), lambda i, j: (i, 0, 0, 0)),
                     pl.BlockSpec((1, 256, t_l), lambda i, j: (i, 0, 0))]
        args += [kl, vtl]
    return pl.pallas_call(
        functools.partial(_attn_kernel, tkc=tkc, tk=tk, n_ctx=t_c // tkc, n_lat=t_l // tk),
        grid=(b, t_q // tq),
        in_specs=[pl.BlockSpec((1, N_HEADS, tq, HEAD_PAD), lambda i, j: (i, 0, j, 0))] + kv_specs,
        out_specs=pl.BlockSpec((1, tq, 256), lambda i, j: (i, j, 0)),
        out_shape=jax.ShapeDtypeStruct((b, t_q, 256), MXU_DTYPE),
        scratch_shapes=[pltpu.VMEM((N_HEADS, max(tkc, tk), tq), jnp.float32)] * 2,
        compiler_params=_cparams(("parallel", "parallel")),
        name="attention",
    )(*args)


def _pool_kernel(prev_ref, cur_ref, next_ref, w_ref, sc_ref, o_ref, ext_ref, *, tm, n_tok):
    j = pl.program_id(1)
    cur = cur_ref[0]
    ext_ref[8:8 + tm, :] = cur
    ext_ref[0:8, :] = jnp.where(j > 0, prev_ref[0], 0.0)
    ext_ref[8 + tm:16 + tm, :] = jnp.where(j < pl.num_programs(1) - 1, next_ref[0], 0.0)

    def window_sums(c0, widths):
        sh = lambda k: ext_ref[8 + k:8 + k + tm, c0:c0 + LANE]
        acc = sh(-1) + sh(0)
        out = {2: acc}
        for w in (4, 8, 16):
            if w > max(widths):
                break
            for k in list(range(-w // 2, -w // 4)) + list(range(w // 4, w // 2)):
                acc = acc + sh(k)
            out[w] = acc
        return [out[w] for w in widths]

    lane = lax.broadcasted_iota(jnp.int32, (tm, LANE), 1)
    t = j * tm + lax.broadcasted_iota(jnp.int32, (tm, LANE), 0)
    halves = []
    for c0, (wa, wb) in ((0, (2, 4)), (LANE, (8, 16))):
        sa, sb = window_sums(c0, (wa, wb))
        first = lane < POOL_GC
        s = jnp.where(first, sa, sb)
        half = jnp.where(first, wa // 2, wb // 2)
        cnt = jnp.minimum(t + half, n_tok) - jnp.maximum(t - half, 0)
        halves.append(s / cnt.astype(jnp.float32) - cur[:, c0:c0 + LANE])
    p = jnp.concatenate(halves, axis=-1)
    o_ref[0] = _bf(_dot(_bf(p), w_ref[...]) * sc_ref[...])


def _pool(u, base, w_bd, scale, tm):
    b, t, _ = u.shape
    cidx = (COL["pool"] - base) // 256
    rb = tm // 8
    last = t // 8 - 1
    return pl.pallas_call(
        functools.partial(_pool_kernel, tm=tm, n_tok=t),
        grid=(b, t // tm),
        in_specs=[pl.BlockSpec((1, 8, 256), lambda i, j: (i, jnp.maximum(j * rb - 1, 0), cidx)),
                  pl.BlockSpec((1, tm, 256), lambda i, j: (i, j, cidx)),
                  pl.BlockSpec((1, 8, 256), lambda i, j: (i, jnp.minimum((j + 1) * rb, last), cidx)),
                  _full(w_bd, 2), _full(scale, 2)],
        out_specs=pl.BlockSpec((1, tm, 256), lambda i, j: (i, j, 0)),
        out_shape=jax.ShapeDtypeStruct((b, t, 256), MXU_DTYPE),
        scratch_shapes=[pltpu.VMEM((tm + 16, 256), jnp.float32)],
        compiler_params=_cparams(("parallel", "parallel")),
        name="pool",
    )(u, u, u, w_bd, scale)


def _head_rms(x, e):
    return lax.rsqrt(_split_dot(x * x, e) * (1.0 / NA_HD) + EPS)


def _na_prep_kernel(q_ref, k_ref, v_ref, gq_ref, gk_ref, e_ref, qo_ref, ko_ref, vo_ref):
    e = e_ref[...]
    q = q_ref[0]
    qo_ref[0] = _bf(q * _head_rms(q, e) * gq_ref[...])
    k = k_ref[0]
    ko_ref[0] = _bf(k * _head_rms(k, e) * gk_ref[...])
    vo_ref[0] = _bf(v_ref[0])


def _na_prep(uq, qcol, u, base, gq, gk, e_mat, tm):
    b, t, _ = u.shape
    tok = pl.BlockSpec((1, tm, 256), lambda i, j: (i, j, 0))
    return pl.pallas_call(
        _na_prep_kernel,
        grid=(b, t // tm),
        in_specs=[pl.BlockSpec((1, tm, 256), lambda i, j: (i, j, qcol)),
                  _ublk(tm, COL["k_na"], base, 256), _ublk(tm, COL["v_na"], base, 256),
                  _full(gq, 2), _full(gk, 2), _full(e_mat, 2)],
        out_specs=[tok, tok, tok],
        out_shape=[jax.ShapeDtypeStruct((b, t, 256), MXU_DTYPE)] * 3,
        compiler_params=_cparams(("parallel", "parallel")),
        name="na_prep",
    )(uq, u, u, gq, gk, e_mat)


def _stack_heads(q):
    qf = q.astype(jnp.float32)
    head = lax.broadcasted_iota(jnp.int32, qf.shape, 1) // NA_HD
    return _bf(jnp.concatenate([jnp.where(head == h, qf, 0.0) for h in range(N_HEADS)], axis=0))


def _unstack_heads(res):
    n = res.shape[0] // N_HEADS
    head = lax.broadcasted_iota(jnp.int32, (n, res.shape[1]), 1) // NA_HD
    out = res[0:n]
    for h in range(1, N_HEADS):
        out = jnp.where(head == h, res[h * n:(h + 1) * n], out)
    return out


def _softmax_pv(scores, values):
    m = functools.reduce(jnp.maximum, [jnp.max(s, axis=-1, keepdims=True) for s in scores])
    ps = [jnp.exp(s - m) for s in scores]
    l = functools.reduce(jnp.add, [jnp.sum(p, axis=-1, keepdims=True) for p in ps])
    acc = functools.reduce(jnp.add, [_dot(_bf(p), v) for p, v in zip(ps, values)])
    return acc / l


def _na_kernel(q_ref, k_ref, v_ref, kc_ref, vc_ref, tbl_ref, o_ref, *, rows, rb):
    jb = pl.program_id(1)
    kc = kc_ref[0]
    vc = vc_ref[0]
    n_loc = NA_WIN_R * GRID_W
    staged = []
    for rr in range(rb):
        r = jb * rb + rr
        rs = jnp.clip(r - NA_WIN_R // 2, 0, rows - NA_WIN_R)
        start = pl.multiple_of(rs * GRID_W, GRID_W)
        qm = _stack_heads(q_ref[0, rr * GRID_W:(rr + 1) * GRID_W, :])
        s_loc = _dot_nt(qm, k_ref[0, pl.ds(start, n_loc), :]) + tbl_ref[r - rs]
        staged.append((start, s_loc, _dot_nt(qm, kc)))
    for rr, (start, s_loc, s_ctx) in enumerate(staged):
        res = _softmax_pv([s_loc, s_ctx], [v_ref[0, pl.ds(start, n_loc), :], vc])
        o_ref[0, rr * GRID_W:(rr + 1) * GRID_W, :] = _bf(_unstack_heads(res))


def _na_attention(q, k, v, kc, vc, tbls, layer, rb):
    b, t, _ = q.shape
    tc = kc.shape[1]
    rows = t // GRID_W
    assert rows >= NA_WIN_R and rows % rb == 0
    seq = pl.BlockSpec((1, t, 256), lambda i, j: (i, 0, 0))
    cseq = pl.BlockSpec((1, tc, 256), lambda i, j: (i, 0, 0))
    return pl.pallas_call(
        functools.partial(_na_kernel, rows=rows, rb=rb),
        grid=(b, rows // rb),
        in_specs=[pl.BlockSpec((1, rb * GRID_W, 256), lambda i, j: (i, j, 0)), seq, seq, cseq, cseq,
                  pl.BlockSpec((None,) + tbls.shape[1:], lambda i, j: (layer, 0, 0, 0))],
        out_specs=pl.BlockSpec((1, rb * GRID_W, 256), lambda i, j: (i, j, 0)),
        out_shape=jax.ShapeDtypeStruct((b, t, 256), MXU_DTYPE),
        compiler_params=_cparams(("parallel", "parallel")),
        name="na_attention",
    )(q, k, v, kc, vc, tbls)


def _ctx_na_kernel(q_ref, kc_ref, vc_ref, o_ref):
    kc = kc_ref[0]
    vc = vc_ref[0]
    for rr in range(q_ref.shape[1] // GRID_W):
        qm = _stack_heads(q_ref[0, rr * GRID_W:(rr + 1) * GRID_W, :])
        res = _softmax_pv([_dot_nt(qm, kc)], [vc])
        o_ref[0, rr * GRID_W:(rr + 1) * GRID_W, :] = _bf(_unstack_heads(res))


def _ctx_na_attention(q, kc, vc):
    b, tc, _ = q.shape
    cseq = pl.BlockSpec((1, tc, 256), lambda i: (i, 0, 0))
    return pl.pallas_call(
        _ctx_na_kernel,
        grid=(b,),
        in_specs=[cseq, cseq, cseq],
        out_specs=cseq,
        out_shape=jax.ShapeDtypeStruct((b, tc, 256), MXU_DTYPE),
        compiler_params=_cparams(("parallel",)),
        name="ctx_na_attention",
    )(q, kc, vc)


def _hg_gates(z, lb):
    e = jnp.exp(-jnp.abs(z))
    r = 1.0 / (1.0 + e)
    pos = z >= 0
    sig = jnp.where(pos, r, e * r)
    sig_neg = jnp.where(pos, e * r, r)
    return jnp.log(lb + (1.0 - lb) * sig), (1.0 - lb) * sig_neg


def _hg_prep_kernel(zf_ref, zb_ref, i_ref, q_ref, lbf_ref, lbb_ref, tl_ref, tu_ref, ta_ref, sel_ref, e_ref,
                    qf_o, kf_o, qb_o, kb_o, i_o, df_o, db_o, oi_o, nat_ref, pm_ref,
                    *, tm, qscale):
    nch = tm // HG_SUB
    q = q_ref[0] * qscale
    xi = i_ref[0]
    i_o[0] = _bf(xi)
    logf_f, kg_f = _hg_gates(zf_ref[0], lbf_ref[...])
    logf_b, kg_b = _hg_gates(zb_ref[0], lbb_ref[...])
    b_f = _split3_dot_left(tl_ref[...], logf_f)
    b_b = _split3_dot_left(tu_ref[...], logf_b)
    tot_f = _split3_dot_left(ta_ref[...], logf_f)
    tot_b = _split3_dot_left(ta_ref[...], logf_b)
    qf_o[0] = _bf(q * jnp.exp(b_f))
    kf_o[0] = _bf(kg_f * jnp.exp(tot_f - b_f))
    qb_o[0] = _bf(q * jnp.exp(b_b))
    kb_o[0] = _bf(kg_b * jnp.exp(tot_b - b_b))
    df_o[0] = jnp.exp(_split3_dot_left(sel_ref[...], logf_f))
    db_o[0] = jnp.exp(_split3_dot_left(sel_ref[...], logf_b))

    log2e = float(np.log2(np.e))
    for a, val in enumerate((q, kg_f, b_f * log2e, kg_b, b_b * log2e, xi)):
        for half in range(2):
            nat_ref[a, half] = val[:, half * LANE:(half + 1) * LANE]
            for p in range(HG_SUB):
                pm_ref[a, half, p * nch:(p + 1) * nch, :] = nat_ref[a, half, pl.ds(p, nch, stride=HG_SUB), :]
    e = e_ref[...]

    def side(q_t, b_t, k_ref, b_ref, i_ref, lo, nblk):
        rows = slice(lo, lo + nblk * nch)
        k3 = k_ref[rows, :].reshape(nblk, nch, LANE)
        b3 = b_ref[rows, :].reshape(nblk, nch, LANE)
        x = q_t[None] * k3 * jnp.exp2(b_t[None] - b3)
        r = _dot(_bf(x.reshape(nblk * nch, LANE)), e) * i_ref[rows, :]
        return jnp.sum(r.reshape(nblk, nch, LANE), axis=0)

    for half in range(2):
        qp, kfp, bfp, kbp, bbp, ip = [pm_ref.at[a, half] for a in range(6)]
        for pt in range(HG_SUB):
            rt = slice(pt * nch, (pt + 1) * nch)
            q_t = qp[rt, :]
            acc = _dot(_bf(q_t * (kfp[rt, :] + kbp[rt, :])), e) * ip[rt, :]
            if pt > 0:
                acc = acc + side(q_t, bfp[rt, :], kfp, bfp, ip, 0, pt)
            if pt < HG_SUB - 1:
                acc = acc + side(q_t, bbp[rt, :], kbp, bbp, ip, (pt + 1) * nch, HG_SUB - 1 - pt)
            nat_ref[0, half, pl.ds(pt, nch, stride=HG_SUB), :] = acc
        oi_o[0, :, half * LANE:(half + 1) * LANE] = nat_ref[0, half]


def _hg_prep(uq, qcol, u, base, lbf, lbb, e_mat):
    b, t, _ = u.shape
    tm = min(HG_PREP_TILE, t)
    nsub = tm // HG_SUB
    tok = pl.BlockSpec((1, tm, 256), lambda i, j: (i, j, 0))
    dec = pl.BlockSpec((1, nsub, 256), lambda i, j: (i, j, 0))
    tl, tu, ta, sel = _hg_consts(tm)
    e_half = e_mat[:LANE, :LANE]
    regroup = pltpu.VMEM((6, 2, tm, LANE), jnp.float32)
    return pl.pallas_call(
        functools.partial(_hg_prep_kernel, tm=tm, qscale=HG_DK ** -0.5),
        grid=(b, t // tm),
        in_specs=[_ublk(tm, COL["zF"], base, 256), _ublk(tm, COL["zB"], base, 256),
                  _ublk(tm, COL["i_hg"], base, 256),
                  pl.BlockSpec((1, tm, 256), lambda i, j: (i, j, qcol)),
                  _full(lbf, 2), _full(lbb, 2), _full(tl, 2), _full(tu, 2), _full(ta, 2), _full(sel, 2),
                  _full(e_half, 2)],
        out_specs=[tok, tok, tok, tok, tok, dec, dec, tok],
        out_shape=[jax.ShapeDtypeStruct((b, t, 256), MXU_DTYPE)] * 5
                  + [jax.ShapeDtypeStruct((b, t // HG_SUB, 256), jnp.float32)] * 2
                  + [jax.ShapeDtypeStruct((b, t, 256), jnp.float32)],
        scratch_shapes=[regroup, regroup],
        compiler_params=_cparams(("parallel", "parallel")),
        name="hg_prep",
    )(u, u, u, uq, lbf, lbb, tl, tu, ta, sel, e_half)


def _hg_scan_kernel(qf_ref, kf_ref, if_ref, df_ref, qb_ref, kb_ref, ib_ref, db_ref, s0f_ref, s0b_ref,
                    of_ref, ob_ref, sf_out, sb_out, sf, sb, *, nsub, nb):
    j = pl.program_id(0)

    @pl.when(j == 0)
    def _():
        sf[...] = s0f_ref[...]
        sb[...] = s0b_ref[...]

    first = lax.broadcasted_iota(jnp.int32, (HG_SUB, LANE), 1) < HG_DK

    def step(state, q_ref, k_ref, i_ref, d_ref, o_ref, bi, c):
        rows = slice(c * HG_SUB, (c + 1) * HG_SUB)
        for p in range(2):
            cols = slice(p * LANE, (p + 1) * LANE)
            s_old = state[bi, p]
            q = q_ref[bi, rows, cols].astype(jnp.float32)
            q2 = _bf(jnp.concatenate([jnp.where(first, q, 0.0), jnp.where(first, 0.0, q)], axis=0))
            r = _dot_nt(q2, _bf(s_old))
            o_ref[bi, rows, cols] = jnp.where(first, r[:HG_SUB], r[HG_SUB:])
            upd = _dot_tn(i_ref[bi, rows, cols], k_ref[bi, rows, cols])
            state[bi, p] = s_old * d_ref[bi, c:c + 1, cols] + upd

    for c in range(nsub):
        for bi in range(nb):
            step(sf, qf_ref, kf_ref, if_ref, df_ref, of_ref, bi, c)
            step(sb, qb_ref, kb_ref, ib_ref, db_ref, ob_ref, bi, nsub - 1 - c)

    @pl.when(j == pl.num_programs(0) - 1)
    def _():
        sf_out[...] = sf[...]
        sb_out[...] = sb[...]


def _hg_scan(qf, kf, qb, kb, xi, df, db, s0f, s0b):
    b, t, _ = qf.shape
    tm = HG_TILE
    n = t // tm
    nsub = tm // HG_SUB
    fwd = pl.BlockSpec((b, tm, 256), lambda j: (0, j, 0))
    bwd = pl.BlockSpec((b, tm, 256), lambda j: (0, n - 1 - j, 0))
    dfwd = pl.BlockSpec((b, nsub, 256), lambda j: (0, j, 0))
    dbwd = pl.BlockSpec((b, nsub, 256), lambda j: (0, n - 1 - j, 0))
    st = pl.BlockSpec((b, 2, LANE, LANE), lambda j: (0, 0, 0, 0))
    return pl.pallas_call(
        functools.partial(_hg_scan_kernel, nsub=nsub, nb=b),
        grid=(n,),
        in_specs=[fwd, fwd, fwd, dfwd, bwd, bwd, bwd, dbwd, st, st],
        out_specs=[fwd, bwd, st, st],
        out_shape=[jax.ShapeDtypeStruct((b, t, 256), jnp.float32)] * 2
                  + [jax.ShapeDtypeStruct((b, 2, LANE, LANE), jnp.float32)] * 2,
        scratch_shapes=[pltpu.VMEM((b, 2, LANE, LANE), jnp.float32)] * 2,
        compiler_params=_cparams(("arbitrary",)),
        name="hg_scan",
    )(qf, kf, xi, df, qb, kb, xi, db, s0f, s0b)


def _merge_kernel(x_ref, a_ref, b_ref, c_ref, of_ref, ob_ref, oi_ref, gc_ref, g_ref, sh_ref, sc_ref, wg_ref,
                  hgn_ref, e_ref, wbr_ref, wo_ref, gate_ref, out_ref):
    d_model = x_ref.shape[2]
    x = x_ref[0]
    h = _bf((x * _rms_scale(x) * g_ref[...]) * (1.0 + sc_ref[0]) + sh_ref[0])
    o = of_ref[0] + ob_ref[0] + oi_ref[0]
    ms = _split_dot(o * o, e_ref[...]) * (1.0 / HG_DK)
    d = o * lax.rsqrt(ms + EPS) * hgn_ref[...] * _sigmoid(gc_ref[0])
    acc = None
    for n, y in enumerate((a_ref[0], b_ref[0], c_ref[0], d)):
        gate = _sigmoid(_dot(h, wg_ref[:, n * d_model:(n + 1) * d_model]))
        term = gate * _dot(_bf(y), wbr_ref[n])
        acc = term if acc is None else acc + term
    out_ref[0] = x + gate_ref[0] * _dot(_bf(acc), wo_ref[...])


def _merge(x, a, bm, cn, o_f, o_b, o_i, u, g1n, w_gate, hgn, e_mat, w_br, w_o, layer, mods, tm):
    b, t, d = x.shape
    tok = pl.BlockSpec((1, tm, 256), lambda i, j: (i, j, 0))
    xs = pl.BlockSpec((1, tm, d), lambda i, j: (i, j, 0))
    of_layer = lambda a: pl.BlockSpec((None,) + a.shape[1:], lambda i, j: (layer,) + (0,) * (a.ndim - 1))
    return pl.pallas_call(
        _merge_kernel,
        grid=(b, t // tm),
        in_specs=[xs, tok, tok, tok, tok, tok, tok, _ublk(tm, COL["g_cols"], 0, 256),
                  _full(g1n, 2), _mod_spec(mods, 0, d), _mod_spec(mods, 1, d), of_layer(w_gate),
                  _full(hgn, 2), _full(e_mat, 2), of_layer(w_br), of_layer(w_o), _mod_spec(mods, 2, d)],
        out_specs=xs,
        out_shape=jax.ShapeDtypeStruct((b, t, d), jnp.float32),
        compiler_params=_cparams(("parallel", "parallel")),
        name="merge",
    )(x, a, bm, cn, o_f, o_b, o_i, u, g1n, mods, mods, w_gate, hgn, e_mat, w_br, w_o, mods)


def _ffn_kernel(x_ref, g_ref, sh_ref, sc_ref, w1_ref, w2_ref, gate_ref, o_ref, *, tf):
    x = x_ref[0]
    h = _bf((x * _rms_scale(x) * g_ref[...]) * (1.0 + sc_ref[0]) + sh_ref[0])
    acc = None
    for k in range(w1_ref.shape[1] // tf):
        z = jnp.maximum(_dot(h, w1_ref[:, k * tf:(k + 1) * tf]), 0.0)
        part = _dot(_bf(z * z), w2_ref[k * tf:(k + 1) * tf, :])
        acc = part if acc is None else acc + part
    o_ref[0] = x + gate_ref[0] * acc


def _ffn(x, g, mods, w1, w2, layer, tm, tf):
    b, t, d = x.shape
    xs = pl.BlockSpec((1, tm, d), lambda i, j: (i, j, 0))
    resident = lambda a: pl.BlockSpec((None,) + a.shape[1:], lambda i, j: (layer, 0, 0),
                                      pipeline_mode=pl.Buffered(1))
    return pl.pallas_call(
        functools.partial(_ffn_kernel, tf=tf),
        grid=(b, t // tm),
        in_specs=[xs, pl.BlockSpec((1, d), lambda i, j: (0, 0)), _mod_spec(mods, 3, d), _mod_spec(mods, 4, d),
                  resident(w1), resident(w2), _mod_spec(mods, 5, d)],
        out_specs=xs,
        out_shape=jax.ShapeDtypeStruct((b, t, d), jnp.float32),
        compiler_params=_cparams(("parallel", "parallel")),
        name="ffn",
    )(x, g, mods, mods, w1, w2, mods)


def _rope_perm():
    return np.array([(t // 16) * 16 + ((t % 16) + 8) % 16 for t in range(MLA_ROPE)])


W_IN_COLS = dict(ckv=0, kr=128, k_na=160, v_na=416, zF=672, zB=928, i_hg=1184, pool=1440, qa=1696, q_na=1952,
                 q_hg=2208, g_cols=2464, gates=2720)


def _w_in_relayout_kernel(w_ref, gate_ref, mix_ref):
    w = w_ref[0]
    tr = w.shape[0]
    o = W_IN_COLS
    gate_ref[0] = _bf(w[:, o["gates"]:])
    z = lambda n: jnp.zeros((tr, n), jnp.float32)
    kr = w[:, o["kr"]:o["kr"] + MLA_ROPE]
    quarter = MLA_ROPE // 4
    kr_sw = jnp.concatenate([kr[:, quarter:2 * quarter], kr[:, :quarter],
                             kr[:, 3 * quarter:], kr[:, 2 * quarter:3 * quarter]], axis=1)
    parts = [w[:, o[name]:o[name] + 256] for name in ("k_na", "v_na", "zF", "zB", "i_hg")]
    parts += [w[:, :MLA_KV_RANK], z(MLA_NOPE), kr, kr_sw]
    parts += [w[:, o[name]:o[name] + 256] for name in ("pool", "qa", "q_na", "q_hg", "g_cols")]
    mix_ref[0] = _bf(jnp.concatenate(parts, axis=1))


def _prep_w_in(w):
    depth, d, n = w.shape
    tr = 128
    n_gate = n - W_IN_COLS["gates"]
    return pl.pallas_call(
        _w_in_relayout_kernel,
        grid=(depth, d // tr),
        in_specs=[pl.BlockSpec((1, tr, n), lambda l, i: (l, i, 0))],
        out_specs=[pl.BlockSpec((1, tr, n_gate), lambda l, i: (l, i, 0)),
                   pl.BlockSpec((1, tr, D_MIX_PAD), lambda l, i: (l, i, 0))],
        out_shape=[jax.ShapeDtypeStruct((depth, d, n_gate), MXU_DTYPE),
                   jax.ShapeDtypeStruct((depth, d, D_MIX_PAD), MXU_DTYPE)],
        compiler_params=_cparams(("parallel", "parallel")),
        name="w_in_relayout",
    )(w)


def _prep_wq(w):
    r = w.shape[0]
    wh = w.reshape(r, N_HEADS, MLA_QK)
    pre = jnp.pad(wh, ((0, 0), (0, 0), (0, HEAD_PAD - MLA_QK)))
    sw = jnp.pad(wh[:, :, MLA_NOPE:][:, :, _rope_perm()], ((0, 0), (0, 0), (MLA_NOPE, HEAD_PAD - MLA_QK)))
    return _bf(jnp.concatenate([pre.reshape(r, -1), sw.reshape(r, -1)], axis=1))


def _prep_wkv(w):
    r = w.shape[0]
    wh = w.reshape(r, N_HEADS, MLA_NOPE + MLA_V)
    kpart = jnp.pad(wh[:, :, :MLA_NOPE], ((0, 0), (0, 0), (0, HEAD_PAD - MLA_NOPE)))
    return _bf(jnp.concatenate([kpart.reshape(r, -1), wh[:, :, MLA_NOPE:].reshape(r, -1)], axis=1))


def _pad_gain(g):
    plain = jnp.pad(g, (0, HEAD_PAD - MLA_QK))[None]
    sw = jnp.pad(g[MLA_NOPE:][_rope_perm()], (MLA_NOPE, HEAD_PAD - MLA_QK))[None]
    return plain, sw


def _rope_tables(n_tok):
    t = jnp.arange(n_tok)
    pos = jnp.stack([t // GRID_W, t % GRID_W], axis=-1).astype(jnp.float32)
    nf = MLA_ROPE // 4
    inv = ROPE_BASE ** (-jnp.arange(nf, dtype=jnp.float32) / nf)
    ang = pos[:, :, None] * inv
    cos, sin = jnp.cos(ang), jnp.sin(ang)
    cos_t = jnp.concatenate([cos, cos], axis=-1).reshape(n_tok, MLA_ROPE)
    sin_t = jnp.concatenate([-sin, sin], axis=-1).reshape(n_tok, MLA_ROPE)
    ones = jnp.ones((n_tok, MLA_NOPE), jnp.float32)
    zer = jnp.zeros((n_tok, HEAD_PAD - MLA_QK), jnp.float32)
    cos_p = jnp.concatenate([ones, cos_t, zer], axis=-1)
    sin_p = jnp.concatenate([0 * ones, sin_t, zer], axis=-1)
    return cos_p, sin_p


def _plain_tables(n_tok):
    cos_p = jnp.concatenate([jnp.ones((n_tok, MLA_QK), jnp.float32),
                             jnp.zeros((n_tok, HEAD_PAD - MLA_QK), jnp.float32)], axis=-1)
    return cos_p, jnp.zeros((n_tok, HEAD_PAD), jnp.float32)


def _head_indicator(dtype):
    i = np.arange(256) // 64
    return jnp.asarray((i[:, None] == i[None, :]).astype(np.float32), dtype=dtype)


def _na_bias_tables(rpb):
    qc = np.arange(GRID_W)[:, None]
    kc = np.arange(GRID_W)[None, :]
    cs = np.clip(qc - NA_WIN_C // 2, 0, GRID_W - NA_WIN_C)
    inwin = (kc >= cs) & (kc < cs + NA_WIN_C)
    dc = np.clip(kc - qc + NA_WIN_C - 1, 0, 2 * NA_WIN_C - 2)
    col_hot = (dc[:, :, None] == np.arange(2 * NA_WIN_C - 1)).astype(np.float32)
    p = np.arange(NA_WIN_R)[:, None, None]
    j = np.arange(NA_WIN_R)[None, :, None]
    row_hot = (np.arange(2 * NA_WIN_R - 1)[None, None, :] == j + NA_WIN_R - 1 - p).astype(np.float32)
    t = jnp.einsum('lhrd,pjr,qkd->lphqjk', rpb.astype(jnp.float32), row_hot, col_hot,
                   precision=lax.Precision.HIGHEST)
    t = jnp.where(inwin[None, None, None, :, None, :], t, NEG)
    return t.reshape(rpb.shape[0], NA_WIN_R, N_HEADS * GRID_W, NA_WIN_R * GRID_W)


def _hg_consts(tm):
    t = np.arange(tm)
    same = (t[:, None] // HG_SUB) == (t[None, :] // HG_SUB)
    tl = same & (t[None, :] <= t[:, None])
    tu = same & (t[None, :] >= t[:, None])
    sel = (np.arange(tm // HG_SUB)[:, None] == (t[None, :] // HG_SUB))
    f = lambda m: jnp.asarray(m.astype(np.float32), dtype=MXU_DTYPE)
    return f(tl), f(tu), f(same), f(sel)


def _block_diag(w):
    g, c, d = w.shape
    out = jnp.zeros((g * c, g * d), w.dtype)
    for i in range(g):
        out = out.at[i * c:(i + 1) * c, i * d:(i + 1) * d].set(w[i])
    return out


def kernel(x, c, ctx, c_ctx, ada_w, ada_b, norm1, norm2, w_in, pool_w, pool_scale, mla_q_norm, mla_wq_b, mla_kv_norm, mla_wkv_b, mla_gq, mla_gk, na_gq, na_gk, na_rpb, hg_lb, hg_norm, w_branch, w_out, w_ff1, w_ff2):
    B, L, D = x.shape
    Tc = ctx.shape[1]
    depth = ada_w.shape[0]
    f32 = jnp.float32
    assert L % HG_TILE == 0 and Tc % HG_TILE == 0 and B < 8

    lb_p = jax.nn.softmax(hg_lb.astype(f32), axis=0)
    lb_all = jnp.clip(jnp.cumsum(lb_p, axis=0)[:depth], LB_EPS, 1.0 - LB_EPS)

    cc = jnp.concatenate([c, c_ctx[None], jnp.zeros((8 - B - 1, D), f32)], axis=0)
    mods = _mods(cc, ada_w, ada_b)

    cos_l, sin_l = _rope_tables(L)
    cos_c, sin_c = _plain_tables(Tc)
    e_mat = _head_indicator(MXU_DTYPE)
    zero_state = jnp.zeros((B, 2, LANE, LANE), f32)
    zero_q = jnp.zeros((B, Tc, 256), f32)
    tl = min(512, L)
    t_big = min(1024, L)

    na_tbls = _na_bias_tables(na_rpb)
    w_gate, w_mix = _prep_w_in(w_in)
    w_br = _bf(w_branch)
    w_o = _bf(w_out)
    w1 = _bf(w_ff1)
    w2 = _bf(w_ff2)
    xl, xc = x, ctx
    for l in range(depth):
        ctx_out = l < depth - 1
        mods_l = mods[l, :B].reshape(B, 1, 6 * D)
        mods_c = mods[l, B].reshape(1, 1, 6 * D)
        g1n = norm1[l][None]
        g2n = norm2[l][None]
        wq = _prep_wq(mla_wq_b[l])
        wkv = _prep_wkv(mla_wkv_b[l])
        gq, gqsw = _pad_gain(mla_gq[l])
        gk, gksw = _pad_gain(mla_gk[l])
        qn = mla_q_norm[l][None]
        kvn = mla_kv_norm[l][None]
        na_q_gain = (jnp.tile(na_gq[l], N_HEADS) * NA_HD ** -0.5)[None]
        na_k_gain = jnp.tile(na_gk[l], N_HEADS)[None]
        pool_bd = _bf(_block_diag(pool_w[l]))
        pool_sc = pool_scale[l][None]
        hgn = jnp.tile(hg_norm[l], N_HEADS)[None]
        lbF = lb_all[l, 0][None]
        lbB = lb_all[l, 1][None]

        ul = _inproj(xl, g1n, mods_l, w_mix, l, 0, D_MIX_PAD, tl, D_MIX_PAD, f32)
        if ctx_out:
            uc = _inproj(xc, g1n, mods_c, w_mix, l, 0, D_MIX_PAD, Tc, MIX_TILE, f32)
        else:
            uc = _inproj(xc, g1n, mods_c, w_mix, l, 0, D_KV, Tc, D_KV, f32)
        cbase = 0

        q_mla, k_mla, v_mla = _mla_prep(ul, 0, cos_l, sin_l, qn, wq, kvn, wkv, gq, gqsw, gk, gksw, t_big)
        if ctx_out:
            q_mla_c, k_mla_c, v_mla_c = _mla_prep(uc, cbase, cos_c, sin_c, qn, wq, kvn, wkv, gq, gqsw, gk, gksw, Tc)
        else:
            k_mla_c, v_mla_c = _mla_kv_prep(uc, cbase, kvn, wkv, gk, Tc)
        b_l = _attention(q_mla, k_mla_c, v_mla_c, k_mla, v_mla, tl, Tc, tl)

        q_na, k_na, v_na = _na_prep(ul, COL["q_na"] // 256, ul, 0, na_q_gain, na_k_gain, e_mat, t_big)
        if ctx_out:
            q_na_c, k_na_c, v_na_c = _na_prep(uc, COL["q_na"] // 256, uc, cbase, na_q_gain, na_k_gain, e_mat, Tc)
        else:
            _, k_na_c, v_na_c = _na_prep(zero_q, 0, uc, cbase, na_q_gain, na_k_gain, e_mat, Tc)
        c_l = _na_attention(q_na, k_na, v_na, k_na_c, v_na_c, na_tbls, l, min(16, L // GRID_W))

        a_l = _pool(ul, 0, pool_bd, pool_sc, t_big)

        if ctx_out:
            hc = _hg_prep(uc, COL["q_hg"] // 256, uc, cbase, lbF, lbB, e_mat)
        else:
            hc = _hg_prep(zero_q, 0, uc, cbase, lbF, lbB, e_mat)
        oFc, oBc, sFc, sBc = _hg_scan(hc[0], hc[1], hc[2], hc[3], hc[4], hc[5], hc[6], zero_state, zero_state)
        hl = _hg_prep(ul, COL["q_hg"] // 256, ul, 0, lbF, lbB, e_mat)
        oFl, oBl, _, _ = _hg_scan(hl[0], hl[1], hl[2], hl[3], hl[4], hl[5], hl[6], sFc, sBc)

        xl_new = _merge(xl, a_l, b_l, c_l, oFl, oBl, hl[7], ul, g1n, w_gate, hgn, e_mat, w_br, w_o, l, mods_l, tl)

        if ctx_out:
            a_c = _pool(uc, cbase, pool_bd, pool_sc, Tc)
            b_c = _attention(q_mla_c, k_mla_c, v_mla_c, None, None, Tc, Tc, Tc)
            c_c = _ctx_na_attention(q_na_c, k_na_c, v_na_c)
            xc = _merge(xc, a_c, b_c, c_c, oFc, oBc, hc[7], uc, g1n, w_gate, hgn, e_mat, w_br, w_o, l, mods_c, Tc)
            xc = _ffn(xc, g2n, mods_c, w1, w2, l, Tc, 1024)

        xl = _ffn(xl_new, g2n, mods_l, w1, w2, l, tl, 2048)
    return xl
```
